```python
import jax, jax.numpy as jnp
from jax import lax
import numpy as np

D_MODEL = 1024
BATCH = 8
SEQ = 8192
DEPTH = 2

N_META = 16
D_MIX = D_MODEL
D_POOL = D_MIX // 4
POOL_WINDOWS = (2, 4, 8, 16)
POOL_GROUPS = len(POOL_WINDOWS)
POOL_GW = D_POOL // POOL_GROUPS
D_CONV = D_MIX // 4
CONV_K = 31
D_RNN = D_MIX - D_POOL - D_CONV
RG_HEADS = 8
RG_HD = D_RNN // RG_HEADS
RG_CONV_K = 4
RG_C = 8.0
D_IN = D_POOL + 2 * D_CONV + 2 * D_RNN
D_FF = 4 * D_MODEL
EPS = 1e-6

kernel_name = "hymba_pool_conformer_rglru_hybrid"


def _rmsnorm(x, g):
    xf = x.astype(jnp.float32)
    y = xf * lax.rsqrt(jnp.mean(xf * xf, axis=-1, keepdims=True) + EPS)
    return y.astype(x.dtype) * g


def _layernorm(x, g, b):
    xf = x.astype(jnp.float32)
    mu = jnp.mean(xf, axis=-1, keepdims=True)
    var = jnp.mean(jnp.square(xf - mu), axis=-1, keepdims=True)
    return ((xf - mu) * lax.rsqrt(var + EPS)).astype(x.dtype) * g + b


def _causal_depthwise_conv(x, w, b):
    K, C = w.shape
    y = lax.conv_general_dilated(
        x, w[:, None, :], window_strides=(1,), padding=[(K - 1, 0)],
        dimension_numbers=('NWC', 'WIO', 'NWC'), feature_group_count=C)
    return y + b


def _pool_mixer(u, w_grp, scale):
    B_, T, _ = u.shape
    uf = u.astype(jnp.float32)
    cs = jnp.cumsum(uf, axis=1)
    pos = jnp.arange(1, T + 1, dtype=jnp.int32)
    outs = []
    for g, w in enumerate(POOL_WINDOWS):
        sl = slice(g * POOL_GW, (g + 1) * POOL_GW)
        c = cs[..., sl]
        prev = jnp.pad(c, ((0, 0), (w, 0), (0, 0)))[:, :T]
        cnt = jnp.minimum(pos, w).astype(jnp.float32)[None, :, None]
        outs.append((c - prev) / cnt - uf[..., sl])
    pooled = jnp.stack(outs, axis=2).astype(u.dtype)
    mixed = jnp.einsum('btgc,gcd->btgd', pooled, w_grp).reshape(B_, T, D_POOL)
    return mixed * scale


def _conformer_conv(v, gt, w_dw, b_dw, ln_g, ln_b, w_pw):
    u = v * jax.nn.sigmoid(gt)
    u = _causal_depthwise_conv(u, w_dw, b_dw)
    u = jax.nn.silu(_layernorm(u, ln_g, ln_b))
    return u @ w_pw


def _linear_scan(a, b):
    def comb(l, r):
        return (l[0] * r[0], r[0] * l[1] + r[1])
    _, h = lax.associative_scan(comb, (a, b), axis=1)
    return h


def _rglru_branch(gate_in, x_in, conv_w, conv_b, w_a, b_a, w_x, b_x, lam):
    B_, T, _ = x_in.shape
    f32 = jnp.float32
    xc = _causal_depthwise_conv(x_in, conv_w, conv_b)
    xh = xc.reshape(B_, T, RG_HEADS, RG_HD)
    r = jax.nn.sigmoid((jnp.einsum('bthc,hcd->bthd', xh, w_a).reshape(B_, T, D_RNN) + b_a).astype(f32))
    i = jax.nn.sigmoid((jnp.einsum('bthc,hcd->bthd', xh, w_x).reshape(B_, T, D_RNN) + b_x).astype(f32))
    log_a = -RG_C * r * jax.nn.softplus(-lam.astype(f32))
    a = jnp.exp(log_a)
    mult = jnp.sqrt(-jnp.expm1(2.0 * log_a))
    h = _linear_scan(a, mult * (i * xc.astype(f32)))
    return (jax.nn.gelu(gate_in.astype(f32)) * h).astype(x_in.dtype)


def _fwd_setup_inputs(seed: int = 0) -> dict:
    key = jax.random.key(seed)
    ks = jax.random.split(key, 24)
    f32 = jnp.float32
    L = DEPTH

    def nrm(k, shape, scale):
        return jax.random.normal(k, shape, f32) * scale

    u = jax.random.uniform(ks[15], (L, D_RNN), f32, 0.9, 0.999)
    a0 = u ** (1.0 / RG_C)
    rg_lambda = jnp.log(a0) - jnp.log1p(-a0)
    return {
        "x": nrm(ks[0], (BATCH, SEQ, D_MODEL), 1.0),
        "meta_tokens": nrm(ks[1], (N_META, D_MODEL), 1.0),
        "mix_norm_g": 1.0 + nrm(ks[2], (L, D_MODEL), 0.05),
        "w_in": nrm(ks[3], (L, D_MODEL, D_IN), D_MODEL ** -0.5),
        "pool_w": nrm(ks[4], (L, POOL_GROUPS, POOL_GW, POOL_GW), POOL_GW ** -0.5),
        "pool_scale": 1.0 + nrm(ks[5], (L, D_POOL), 0.1),
        "convb_dw_w": nrm(ks[6], (L, CONV_K, D_CONV), CONV_K ** -0.5),
        "convb_dw_b": nrm(ks[7], (L, D_CONV), 0.02),
        "convb_ln_g": 1.0 + nrm(ks[8], (L, D_CONV), 0.05),
        "convb_ln_b": nrm(ks[9], (L, D_CONV), 0.02),
        "convb_pw_w": nrm(ks[10], (L, D_CONV, D_CONV), D_CONV ** -0.5),
        "rg_conv_w": nrm(ks[11], (L, RG_CONV_K, D_RNN), RG_CONV_K ** -0.5),
        "rg_conv_b": nrm(ks[12], (L, D_RNN), 0.02),
        "rg_w_a": nrm(ks[13], (L, RG_HEADS, RG_HD, RG_HD), RG_HD ** -0.5),
        "rg_b_a": nrm(ks[14], (L, D_RNN), 0.02),
        "rg_w_x": nrm(ks[16], (L, RG_HEADS, RG_HD, RG_HD), RG_HD ** -0.5),
        "rg_b_x": nrm(ks[17], (L, D_RNN), 0.02),
        "rg_lambda": rg_lambda,
        "w_out": nrm(ks[18], (L, D_MIX, D_MODEL), D_MIX ** -0.5),
        "mlp_norm_g": 1.0 + nrm(ks[19], (L, D_MODEL), 0.05),
        "w_up": nrm(ks[20], (L, D_MODEL, D_FF), D_MODEL ** -0.5),
        "w_down": nrm(ks[21], (L, D_FF, D_MODEL), D_FF ** -0.5),
        "final_norm_g": 1.0 + nrm(ks[22], (D_MODEL,), 0.05),
    }


def _fwd_reference(x, meta_tokens, mix_norm_g, w_in, pool_w, pool_scale, convb_dw_w, convb_dw_b,
              convb_ln_g, convb_ln_b, convb_pw_w, rg_conv_w, rg_conv_b, rg_w_a, rg_b_a,
              rg_w_x, rg_b_x, rg_lambda, w_out, mlp_norm_g, w_up, w_down, final_norm_g):
    B_ = x.shape[0]
    meta = jnp.broadcast_to(meta_tokens[None].astype(x.dtype), (B_, N_META, D_MODEL))
    h = jnp.concatenate([meta, x], axis=1)
    splits = [D_POOL, D_POOL + D_CONV, D_POOL + 2 * D_CONV, D_POOL + 2 * D_CONV + D_RNN]
    for l in range(DEPTH):
        u = _rmsnorm(h, mix_norm_g[l])
        p = u @ w_in[l]
        p_pool, p_bval, p_bgate, p_cgate, p_cx = jnp.split(p, splits, axis=-1)
        y_a = _pool_mixer(p_pool, pool_w[l], pool_scale[l])
        y_b = _conformer_conv(p_bval, p_bgate, convb_dw_w[l], convb_dw_b[l],
                              convb_ln_g[l], convb_ln_b[l], convb_pw_w[l])
        y_c = _rglru_branch(p_cgate, p_cx, rg_conv_w[l], rg_conv_b[l], rg_w_a[l], rg_b_a[l],
                            rg_w_x[l], rg_b_x[l], rg_lambda[l])
        y = jnp.concatenate([y_a, y_b, y_c], axis=-1)
        h = h + y @ w_out[l]
        u = _rmsnorm(h, mlp_norm_g[l])
        h = h + jnp.square(jax.nn.relu(u @ w_up[l])) @ w_down[l]
    h = _rmsnorm(h, final_norm_g)
    return h[:, N_META:]


import jax as _jax
import jax.numpy as _jnp

TWIN_FORMAT = 'train_step'
FWD_PARAMS = ['x', 'meta_tokens', 'mix_norm_g', 'w_in', 'pool_w', 'pool_scale', 'convb_dw_w', 'convb_dw_b', 'convb_ln_g', 'convb_ln_b', 'convb_pw_w', 'rg_conv_w', 'rg_conv_b', 'rg_w_a', 'rg_b_a', 'rg_w_x', 'rg_b_x', 'rg_lambda', 'w_out', 'mlp_norm_g', 'w_up', 'w_down', 'final_norm_g']
TWIN_WEIGHTS = ['meta_tokens', 'mix_norm_g', 'w_in', 'pool_w', 'pool_scale', 'convb_dw_w', 'convb_dw_b', 'convb_ln_g', 'convb_ln_b', 'convb_pw_w', 'rg_conv_w', 'rg_conv_b', 'rg_w_a', 'rg_b_a', 'rg_w_x', 'rg_b_x', 'rg_lambda', 'w_out', 'mlp_norm_g', 'w_up', 'w_down', 'final_norm_g']
TWIN_DIFF_INPUT = 'x'
TWIN_INPUTS = ['x', 'meta_tokens', 'mix_norm_g', 'w_in', 'pool_w', 'pool_scale', 'convb_dw_w', 'convb_dw_b', 'convb_ln_g', 'convb_ln_b', 'convb_pw_w', 'rg_conv_w', 'rg_conv_b', 'rg_w_a', 'rg_b_a', 'rg_w_x', 'rg_b_x', 'rg_lambda', 'w_out', 'mlp_norm_g', 'w_up', 'w_down', 'final_norm_g', 'loss_target', 'm_meta_tokens', 'm_mix_norm_g', 'm_w_in', 'm_pool_w', 'm_pool_scale', 'm_convb_dw_w', 'm_convb_dw_b', 'm_convb_ln_g', 'm_convb_ln_b', 'm_convb_pw_w', 'm_rg_conv_w', 'm_rg_conv_b', 'm_rg_w_a', 'm_rg_b_a', 'm_rg_w_x', 'm_rg_b_x', 'm_rg_lambda', 'm_w_out', 'm_mlp_norm_g', 'm_w_up', 'm_w_down', 'm_final_norm_g', 'v_meta_tokens', 'v_mix_norm_g', 'v_w_in', 'v_pool_w', 'v_pool_scale', 'v_convb_dw_w', 'v_convb_dw_b', 'v_convb_ln_g', 'v_convb_ln_b', 'v_convb_pw_w', 'v_rg_conv_w', 'v_rg_conv_b', 'v_rg_w_a', 'v_rg_b_a', 'v_rg_w_x', 'v_rg_b_x', 'v_rg_lambda', 'v_w_out', 'v_mlp_norm_g', 'v_w_up', 'v_w_down', 'v_final_norm_g']
TWIN_OUTPUTS = ['loss', 'grad_x', 'grad_meta_tokens', 'grad_mix_norm_g', 'grad_w_in', 'grad_pool_w', 'grad_pool_scale', 'grad_convb_dw_w', 'grad_convb_dw_b', 'grad_convb_ln_g', 'grad_convb_ln_b', 'grad_convb_pw_w', 'grad_rg_conv_w', 'grad_rg_conv_b', 'grad_rg_w_a', 'grad_rg_b_a', 'grad_rg_w_x', 'grad_rg_b_x', 'grad_rg_lambda', 'grad_w_out', 'grad_mlp_norm_g', 'grad_w_up', 'grad_w_down', 'grad_final_norm_g', 'delta_meta_tokens', 'delta_mix_norm_g', 'delta_w_in', 'delta_pool_w', 'delta_pool_scale', 'delta_convb_dw_w', 'delta_convb_dw_b', 'delta_convb_ln_g', 'delta_convb_ln_b', 'delta_convb_pw_w', 'delta_rg_conv_w', 'delta_rg_conv_b', 'delta_rg_w_a', 'delta_rg_b_a', 'delta_rg_w_x', 'delta_rg_b_x', 'delta_rg_lambda', 'delta_w_out', 'delta_mlp_norm_g', 'delta_w_up', 'delta_w_down', 'delta_final_norm_g', 'new_m_meta_tokens', 'new_m_mix_norm_g', 'new_m_w_in', 'new_m_pool_w', 'new_m_pool_scale', 'new_m_convb_dw_w', 'new_m_convb_dw_b', 'new_m_convb_ln_g', 'new_m_convb_ln_b', 'new_m_convb_pw_w', 'new_m_rg_conv_w', 'new_m_rg_conv_b', 'new_m_rg_w_a', 'new_m_rg_b_a', 'new_m_rg_w_x', 'new_m_rg_b_x', 'new_m_rg_lambda', 'new_m_w_out', 'new_m_mlp_norm_g', 'new_m_w_up', 'new_m_w_down', 'new_m_final_norm_g', 'new_v_meta_tokens', 'new_v_mix_norm_g', 'new_v_w_in', 'new_v_pool_w', 'new_v_pool_scale', 'new_v_convb_dw_w', 'new_v_convb_dw_b', 'new_v_convb_ln_g', 'new_v_convb_ln_b', 'new_v_convb_pw_w', 'new_v_rg_conv_w', 'new_v_rg_conv_b', 'new_v_rg_w_a', 'new_v_rg_b_a', 'new_v_rg_w_x', 'new_v_rg_b_x', 'new_v_rg_lambda', 'new_v_w_out', 'new_v_mlp_norm_g', 'new_v_w_up', 'new_v_w_down', 'new_v_final_norm_g']
TWIN_LEAF_KINDS = {'loss': 'loss', 'grad_x': 'grad_x', 'grad_meta_tokens': 'grad_w', 'grad_mix_norm_g': 'grad_w', 'grad_w_in': 'grad_w', 'grad_pool_w': 'grad_w', 'grad_pool_scale': 'grad_w', 'grad_convb_dw_w': 'grad_w', 'grad_convb_dw_b': 'grad_w', 'grad_convb_ln_g': 'grad_w', 'grad_convb_ln_b': 'grad_w', 'grad_convb_pw_w': 'grad_w', 'grad_rg_conv_w': 'grad_w', 'grad_rg_conv_b': 'grad_w', 'grad_rg_w_a': 'grad_w', 'grad_rg_b_a': 'grad_w', 'grad_rg_w_x': 'grad_w', 'grad_rg_b_x': 'grad_w', 'grad_rg_lambda': 'grad_w', 'grad_w_out': 'grad_w', 'grad_mlp_norm_g': 'grad_w', 'grad_w_up': 'grad_w', 'grad_w_down': 'grad_w', 'grad_final_norm_g': 'grad_w', 'delta_meta_tokens': 'delta_w', 'delta_mix_norm_g': 'delta_w', 'delta_w_in': 'delta_w', 'delta_pool_w': 'delta_w', 'delta_pool_scale': 'delta_w', 'delta_convb_dw_w': 'delta_w', 'delta_convb_dw_b': 'delta_w', 'delta_convb_ln_g': 'delta_w', 'delta_convb_ln_b': 'delta_w', 'delta_convb_pw_w': 'delta_w', 'delta_rg_conv_w': 'delta_w', 'delta_rg_conv_b': 'delta_w', 'delta_rg_w_a': 'delta_w', 'delta_rg_b_a': 'delta_w', 'delta_rg_w_x': 'delta_w', 'delta_rg_b_x': 'delta_w', 'delta_rg_lambda': 'delta_w', 'delta_w_out': 'delta_w', 'delta_mlp_norm_g': 'delta_w', 'delta_w_up': 'delta_w', 'delta_w_down': 'delta_w', 'delta_final_norm_g': 'delta_w', 'new_m_meta_tokens': 'new_m', 'new_m_mix_norm_g': 'new_m', 'new_m_w_in': 'new_m', 'new_m_pool_w': 'new_m', 'new_m_pool_scale': 'new_m', 'new_m_convb_dw_w': 'new_m', 'new_m_convb_dw_b': 'new_m', 'new_m_convb_ln_g': 'new_m', 'new_m_convb_ln_b': 'new_m', 'new_m_convb_pw_w': 'new_m', 'new_m_rg_conv_w': 'new_m', 'new_m_rg_conv_b': 'new_m', 'new_m_rg_w_a': 'new_m', 'new_m_rg_b_a': 'new_m', 'new_m_rg_w_x': 'new_m', 'new_m_rg_b_x': 'new_m', 'new_m_rg_lambda': 'new_m', 'new_m_w_out': 'new_m', 'new_m_mlp_norm_g': 'new_m', 'new_m_w_up': 'new_m', 'new_m_w_down': 'new_m', 'new_m_final_norm_g': 'new_m', 'new_v_meta_tokens': 'new_v', 'new_v_mix_norm_g': 'new_v', 'new_v_w_in': 'new_v', 'new_v_pool_w': 'new_v', 'new_v_pool_scale': 'new_v', 'new_v_convb_dw_w': 'new_v', 'new_v_convb_dw_b': 'new_v', 'new_v_convb_ln_g': 'new_v', 'new_v_convb_ln_b': 'new_v', 'new_v_convb_pw_w': 'new_v', 'new_v_rg_conv_w': 'new_v', 'new_v_rg_conv_b': 'new_v', 'new_v_rg_w_a': 'new_v', 'new_v_rg_b_a': 'new_v', 'new_v_rg_w_x': 'new_v', 'new_v_rg_b_x': 'new_v', 'new_v_rg_lambda': 'new_v', 'new_v_w_out': 'new_v', 'new_v_mlp_norm_g': 'new_v', 'new_v_w_up': 'new_v', 'new_v_w_down': 'new_v', 'new_v_final_norm_g': 'new_v'}


def _forward(args):
    return _fwd_reference(*[args[k] for k in FWD_PARAMS])


def _output_shape():
    def fwd():
        inp = _fwd_setup_inputs(0)
        return _fwd_reference(*[inp[k] for k in FWD_PARAMS])
    out = _jax.eval_shape(fwd)
    return out.shape, out.dtype

N_MICROBATCH = 1
ADAM_LR = 0.001
ADAM_B1 = 0.9
ADAM_B2 = 0.999
ADAM_EPS = 1e-08
ADAM_WD = 0.01
ADAM_STEP = 10
PER_EXAMPLE_BATCH_AXIS = {'x': 0, 'loss_target': 0}
SHARED_INPUTS = []
_WEIGHT_DTYPES = {'meta_tokens': _jnp.float32, 'mix_norm_g': _jnp.float32, 'w_in': _jnp.float32, 'pool_w': _jnp.float32, 'pool_scale': _jnp.float32, 'convb_dw_w': _jnp.float32, 'convb_dw_b': _jnp.float32, 'convb_ln_g': _jnp.float32, 'convb_ln_b': _jnp.float32, 'convb_pw_w': _jnp.float32, 'rg_conv_w': _jnp.float32, 'rg_conv_b': _jnp.float32, 'rg_w_a': _jnp.float32, 'rg_b_a': _jnp.float32, 'rg_w_x': _jnp.float32, 'rg_b_x': _jnp.float32, 'rg_lambda': _jnp.float32, 'w_out': _jnp.float32, 'mlp_norm_g': _jnp.float32, 'w_up': _jnp.float32, 'w_down': _jnp.float32, 'final_norm_g': _jnp.float32}
MOMENT_SCALE = {'meta_tokens': 7.455984e-03, 'mix_norm_g': 3.343802e-01, 'w_in': 2.416897e-01, 'pool_w': 1.809161e-01, 'pool_scale': 1.877363e-01, 'convb_dw_w': 1.489797e-01, 'convb_dw_b': 1.035846e+00, 'convb_ln_g': 3.712421e-01, 'convb_ln_b': 5.672112e-01, 'convb_pw_w': 2.295517e-01, 'rg_conv_w': 3.919452e-01, 'rg_conv_b': 2.363795e+00, 'rg_w_a': 8.501143e-02, 'rg_b_a': 1.059001e-01, 'rg_w_x': 1.718455e-01, 'rg_b_x': 1.745495e-01, 'rg_lambda': 2.561793e-01, 'w_out': 2.621295e-01, 'mlp_norm_g': 2.318382e-01, 'w_up': 1.172264e-01, 'w_down': 5.107277e-01, 'final_norm_g': 6.519035e+01}


def _to_microbatches(a, axis):
    t = _jnp.moveaxis(a, axis, 0)
    t = t.reshape((N_MICROBATCH, t.shape[0] // N_MICROBATCH) + t.shape[1:])
    return _jnp.moveaxis(t, 1, axis + 1)


def setup_inputs(seed: int = 0) -> dict:
    inp = _fwd_setup_inputs(seed)
    key = _jax.random.fold_in(_jax.random.key(seed), 7919)
    shape, _ = _output_shape()
    out = dict(inp)
    out["loss_target"] = _jax.random.normal(_jax.random.fold_in(key, 0), shape, _jnp.float32)
    for i, name in enumerate(TWIN_WEIGHTS):
        w = inp[name].astype(_jnp.float32)
        if MOMENT_SCALE is None:
            s = _jnp.sqrt(_jnp.mean(_jnp.square(w)) + 1e-30)
        else:
            s = MOMENT_SCALE[name]
        km, kv = _jax.random.split(_jax.random.fold_in(key, i + 1))
        out[name] = w
        out["m_" + name] = s * _jax.random.normal(km, w.shape, _jnp.float32)
        out["v_" + name] = (s * s) * _jax.random.uniform(kv, w.shape, _jnp.float32, 0.5, 1.5)
    if N_MICROBATCH > 1:
        for name, axis in PER_EXAMPLE_BATCH_AXIS.items():
            out[name] = _to_microbatches(out[name], axis)
    return {'x': out['x'], 'meta_tokens': out['meta_tokens'], 'mix_norm_g': out['mix_norm_g'], 'w_in': out['w_in'], 'pool_w': out['pool_w'], 'pool_scale': out['pool_scale'], 'convb_dw_w': out['convb_dw_w'], 'convb_dw_b': out['convb_dw_b'], 'convb_ln_g': out['convb_ln_g'], 'convb_ln_b': out['convb_ln_b'], 'convb_pw_w': out['convb_pw_w'], 'rg_conv_w': out['rg_conv_w'], 'rg_conv_b': out['rg_conv_b'], 'rg_w_a': out['rg_w_a'], 'rg_b_a': out['rg_b_a'], 'rg_w_x': out['rg_w_x'], 'rg_b_x': out['rg_b_x'], 'rg_lambda': out['rg_lambda'], 'w_out': out['w_out'], 'mlp_norm_g': out['mlp_norm_g'], 'w_up': out['w_up'], 'w_down': out['w_down'], 'final_norm_g': out['final_norm_g'], 'loss_target': out['loss_target'], 'm_meta_tokens': out['m_meta_tokens'], 'm_mix_norm_g': out['m_mix_norm_g'], 'm_w_in': out['m_w_in'], 'm_pool_w': out['m_pool_w'], 'm_pool_scale': out['m_pool_scale'], 'm_convb_dw_w': out['m_convb_dw_w'], 'm_convb_dw_b': out['m_convb_dw_b'], 'm_convb_ln_g': out['m_convb_ln_g'], 'm_convb_ln_b': out['m_convb_ln_b'], 'm_convb_pw_w': out['m_convb_pw_w'], 'm_rg_conv_w': out['m_rg_conv_w'], 'm_rg_conv_b': out['m_rg_conv_b'], 'm_rg_w_a': out['m_rg_w_a'], 'm_rg_b_a': out['m_rg_b_a'], 'm_rg_w_x': out['m_rg_w_x'], 'm_rg_b_x': out['m_rg_b_x'], 'm_rg_lambda': out['m_rg_lambda'], 'm_w_out': out['m_w_out'], 'm_mlp_norm_g': out['m_mlp_norm_g'], 'm_w_up': out['m_w_up'], 'm_w_down': out['m_w_down'], 'm_final_norm_g': out['m_final_norm_g'], 'v_meta_tokens': out['v_meta_tokens'], 'v_mix_norm_g': out['v_mix_norm_g'], 'v_w_in': out['v_w_in'], 'v_pool_w': out['v_pool_w'], 'v_pool_scale': out['v_pool_scale'], 'v_convb_dw_w': out['v_convb_dw_w'], 'v_convb_dw_b': out['v_convb_dw_b'], 'v_convb_ln_g': out['v_convb_ln_g'], 'v_convb_ln_b': out['v_convb_ln_b'], 'v_convb_pw_w': out['v_convb_pw_w'], 'v_rg_conv_w': out['v_rg_conv_w'], 'v_rg_conv_b': out['v_rg_conv_b'], 'v_rg_w_a': out['v_rg_w_a'], 'v_rg_b_a': out['v_rg_b_a'], 'v_rg_w_x': out['v_rg_w_x'], 'v_rg_b_x': out['v_rg_b_x'], 'v_rg_lambda': out['v_rg_lambda'], 'v_w_out': out['v_w_out'], 'v_mlp_norm_g': out['v_mlp_norm_g'], 'v_w_up': out['v_w_up'], 'v_w_down': out['v_w_down'], 'v_final_norm_g': out['v_final_norm_g']}


def _loss(weights, diff, rest, loss_target):
    with _jax.named_scope("forward"):
        args = {**rest, TWIN_DIFF_INPUT: diff, **{k: w.astype(_WEIGHT_DTYPES[k]) for k, w in weights.items()}}
        y = _forward(args)
    with _jax.named_scope("loss_head"):
        err = _jnp.square(y.astype(_jnp.float32) - loss_target)
        return 0.5 * _jnp.sum(_jnp.mean(err, axis=-1)) if err.ndim else 0.5 * err


def _adamw(w, g, m, v):
    m = ADAM_B1 * m + (1.0 - ADAM_B1) * g
    v = ADAM_B2 * v + (1.0 - ADAM_B2) * _jnp.square(g)
    m_hat = m / (1.0 - ADAM_B1 ** ADAM_STEP)
    v_hat = v / (1.0 - ADAM_B2 ** ADAM_STEP)
    delta = -ADAM_LR * (m_hat / (_jnp.sqrt(v_hat) + ADAM_EPS) + ADAM_WD * w)
    return delta, m, v


def reference(x, meta_tokens, mix_norm_g, w_in, pool_w, pool_scale, convb_dw_w, convb_dw_b, convb_ln_g, convb_ln_b, convb_pw_w, rg_conv_w, rg_conv_b, rg_w_a, rg_b_a, rg_w_x, rg_b_x, rg_lambda, w_out, mlp_norm_g, w_up, w_down, final_norm_g, loss_target, m_meta_tokens, m_mix_norm_g, m_w_in, m_pool_w, m_pool_scale, m_convb_dw_w, m_convb_dw_b, m_convb_ln_g, m_convb_ln_b, m_convb_pw_w, m_rg_conv_w, m_rg_conv_b, m_rg_w_a, m_rg_b_a, m_rg_w_x, m_rg_b_x, m_rg_lambda, m_w_out, m_mlp_norm_g, m_w_up, m_w_down, m_final_norm_g, v_meta_tokens, v_mix_norm_g, v_w_in, v_pool_w, v_pool_scale, v_convb_dw_w, v_convb_dw_b, v_convb_ln_g, v_convb_ln_b, v_convb_pw_w, v_rg_conv_w, v_rg_conv_b, v_rg_w_a, v_rg_b_a, v_rg_w_x, v_rg_b_x, v_rg_lambda, v_w_out, v_mlp_norm_g, v_w_up, v_w_down, v_final_norm_g):
    given = dict(x=x, meta_tokens=meta_tokens, mix_norm_g=mix_norm_g, w_in=w_in, pool_w=pool_w, pool_scale=pool_scale, convb_dw_w=convb_dw_w, convb_dw_b=convb_dw_b, convb_ln_g=convb_ln_g, convb_ln_b=convb_ln_b, convb_pw_w=convb_pw_w, rg_conv_w=rg_conv_w, rg_conv_b=rg_conv_b, rg_w_a=rg_w_a, rg_b_a=rg_b_a, rg_w_x=rg_w_x, rg_b_x=rg_b_x, rg_lambda=rg_lambda, w_out=w_out, mlp_norm_g=mlp_norm_g, w_up=w_up, w_down=w_down, final_norm_g=final_norm_g, loss_target=loss_target, m_meta_tokens=m_meta_tokens, m_mix_norm_g=m_mix_norm_g, m_w_in=m_w_in, m_pool_w=m_pool_w, m_pool_scale=m_pool_scale, m_convb_dw_w=m_convb_dw_w, m_convb_dw_b=m_convb_dw_b, m_convb_ln_g=m_convb_ln_g, m_convb_ln_b=m_convb_ln_b, m_convb_pw_w=m_convb_pw_w, m_rg_conv_w=m_rg_conv_w, m_rg_conv_b=m_rg_conv_b, m_rg_w_a=m_rg_w_a, m_rg_b_a=m_rg_b_a, m_rg_w_x=m_rg_w_x, m_rg_b_x=m_rg_b_x, m_rg_lambda=m_rg_lambda, m_w_out=m_w_out, m_mlp_norm_g=m_mlp_norm_g, m_w_up=m_w_up, m_w_down=m_w_down, m_final_norm_g=m_final_norm_g, v_meta_tokens=v_meta_tokens, v_mix_norm_g=v_mix_norm_g, v_w_in=v_w_in, v_pool_w=v_pool_w, v_pool_scale=v_pool_scale, v_convb_dw_w=v_convb_dw_w, v_convb_dw_b=v_convb_dw_b, v_convb_ln_g=v_convb_ln_g, v_convb_ln_b=v_convb_ln_b, v_convb_pw_w=v_convb_pw_w, v_rg_conv_w=v_rg_conv_w, v_rg_conv_b=v_rg_conv_b, v_rg_w_a=v_rg_w_a, v_rg_b_a=v_rg_b_a, v_rg_w_x=v_rg_w_x, v_rg_b_x=v_rg_b_x, v_rg_lambda=v_rg_lambda, v_w_out=v_w_out, v_mlp_norm_g=v_mlp_norm_g, v_w_up=v_w_up, v_w_down=v_w_down, v_final_norm_g=v_final_norm_g)
    weights = {n: given[n] for n in TWIN_WEIGHTS}
    shared = {n: given[n] for n in SHARED_INPUTS}
    per_example = {n: given[n] for n in ['x']}
    grad_fn = _jax.value_and_grad(_loss, argnums=(0, 1))

    def one_microbatch(ex, loss_target):
        ex = dict(ex)
        diff = ex.pop(TWIN_DIFF_INPUT)
        return grad_fn(weights, diff, {**shared, **ex}, loss_target)

    if N_MICROBATCH == 1:
        loss, (grad_w, grad_x) = one_microbatch(per_example, given["loss_target"])
    else:
        def body(carry, xs):
            loss_sum, grad_sum = carry
            l_k, (gw_k, gx_k) = one_microbatch(xs[0], xs[1])
            with _jax.named_scope("update"):
                return (loss_sum + l_k, _jax.tree.map(_jnp.add, grad_sum, gw_k)), gx_k

        init = (_jnp.zeros((), _jnp.float32), _jax.tree.map(_jnp.zeros_like, weights))
        (loss, grad_w), grad_x = _jax.lax.scan(body, init, (per_example, given["loss_target"]))
    with _jax.named_scope("update"):
        delta_w, new_m, new_v = {}, {}, {}
        for n in TWIN_WEIGHTS:
            delta_w[n], new_m[n], new_v[n] = _adamw(weights[n], grad_w[n], given["m_" + n], given["v_" + n])
    return (loss, grad_x, *[grad_w[n] for n in TWIN_WEIGHTS], *[delta_w[n] for n in TWIN_WEIGHTS],
            *[new_m[n] for n in TWIN_WEIGHTS], *[new_v[n] for n in TWIN_WEIGHTS])
```

```python
import functools

import jax
import jax.numpy as jnp
from jax import lax
from jax.experimental import pallas as pl
from jax.experimental.pallas import tpu as pltpu

F32 = jnp.float32
BF16 = jnp.bfloat16

N_DEV = 8
N_META = 16
DEPTH = 2
EPS = 1e-6
POOL_WINDOWS = (2, 4, 8, 16)
POOL_GW = 64
CONV_K = 31
RG_CONV_K = 4
RG_HEADS = 8
RG_HD = 64
RG_C = 8.0
D_POOL = 256
D_CONV = 256
D_RNN = 512

ADAM_LR = 0.001
ADAM_B1 = 0.9
ADAM_B2 = 0.999
ADAM_EPS = 1e-08
ADAM_WD = 0.01
ADAM_STEP = 10

TT = 432
HALO = 48
VMEM_LIMIT = 56 * 1024 * 1024

MESH = pl.DeviceIdType.MESH
ANY = pl.BlockSpec(memory_space=pl.ANY)


def _cparams(*sem):
    return pltpu.CompilerParams(dimension_semantics=sem, vmem_limit_bytes=VMEM_LIMIT)


def _full(shape):
    nd = len(shape)
    return pl.BlockSpec(shape, lambda *_: (0,) * nd)


def _layer(shape, l):
    nd = len(shape)
    return pl.BlockSpec((None,) + tuple(shape), lambda *_: (l,) + (0,) * nd)


def _dot(a, b):
    return jnp.dot(a, b, preferred_element_type=F32)


def _dot_nt(a, b):
    return lax.dot_general(a, b, (((1,), (1,)), ((), ())), preferred_element_type=F32)


def _dot_tn(a, b):
    return lax.dot_general(a, b, (((0,), (0,)), ((), ())), preferred_element_type=F32)


def _rms(h, g):
    r = lax.rsqrt(jnp.mean(h * h, axis=-1, keepdims=True) + EPS)
    return h * r, r


def _rms_bwd(du, xh, r, g):
    dxh = du * g
    return r * (dxh - xh * jnp.mean(dxh * xh, axis=-1, keepdims=True))


def _sigmoid(x):
    return 1.0 / (1.0 + jnp.exp(-x))


def _colsum(x):
    return jnp.sum(x, axis=0, keepdims=True)


def _rms_mm(h, g, w, l, ck, name):
    T, D = h.shape
    N = w.shape[-1]

    def body(h_ref, g_ref, w_ref, o_ref):
        xh, _ = _rms(h_ref[...], None)
        u = (xh * g_ref[l:l + 1, :]).astype(BF16)
        o_ref[...] = _dot(u, w_ref[...])

    return pl.pallas_call(
        body, name=name, grid=(N // ck, T // TT),
        in_specs=[pl.BlockSpec((TT, D), lambda j, i: (i, 0)), _full(g.shape),
                  pl.BlockSpec((None, D, ck), lambda j, i: (l, 0, j))],
        out_specs=pl.BlockSpec((TT, ck), lambda j, i: (i, j)),
        out_shape=jax.ShapeDtypeStruct((T, N), F32),
        compiler_params=_cparams("parallel", "parallel"))(h, g, w)


def _mm_res(res, a, w, l, ck, relu2, name):
    T, D = res.shape
    K = a.shape[-1]
    nk = K // ck

    def body(r_ref, a_ref, w_ref, o_ref):
        k = pl.program_id(1)

        @pl.when(k == 0)
        def _():
            o_ref[...] = r_ref[...]

        av = a_ref[...]
        if relu2:
            av = jnp.square(jnp.maximum(av, 0.0))
        o_ref[...] += _dot(av.astype(BF16), w_ref[...])

    return pl.pallas_call(
        body, name=name, grid=(T // TT, nk),
        in_specs=[pl.BlockSpec((TT, D), lambda i, k: (i, 0)), pl.BlockSpec((TT, ck), lambda i, k: (i, k)),
                  pl.BlockSpec((None, ck, D), lambda i, k: (l, k, 0))],
        out_specs=pl.BlockSpec((TT, D), lambda i, k: (i, 0)),
        out_shape=jax.ShapeDtypeStruct((T, D), F32),
        compiler_params=_cparams("parallel", "arbitrary"))(res, a, w)


def _final_loss(h, gf, tgt):
    T, D = h.shape

    def body(h_ref, g_ref, t_ref, dh_ref, dg_ref, loss_ref):
        i = pl.program_id(0)

        @pl.when(i == 0)
        def _():
            dg_ref[...] = jnp.zeros_like(dg_ref)
            loss_ref[...] = jnp.zeros_like(loss_ref)

        g = g_ref[...]
        xh, r = _rms(h_ref[...], None)
        row = i * TT + lax.broadcasted_iota(jnp.int32, (TT, 1), 0)
        keep = (row >= N_META).astype(F32)
        diff = (xh * g - t_ref[...]) * keep
        part = jnp.sum(jnp.sum(diff * diff, axis=1, keepdims=True), axis=0, keepdims=True)
        loss_ref[...] += (0.5 / D) * part
        dy = diff * (1.0 / D)
        dg_ref[...] += _colsum(dy * xh)
        dh_ref[...] = _rms_bwd(dy, xh, r, g)

    return pl.pallas_call(
        body, name="final_loss", grid=(T // TT,),
        in_specs=[pl.BlockSpec((TT, D), lambda i: (i, 0)), _full(gf.shape), pl.BlockSpec((TT, D), lambda i: (i, 0))],
        out_specs=[pl.BlockSpec((TT, D), lambda i: (i, 0)), _full((1, D)), _full((8, 128))],
        out_shape=[jax.ShapeDtypeStruct((T, D), F32), jax.ShapeDtypeStruct((1, D), F32),
                   jax.ShapeDtypeStruct((8, 128), F32)],
        compiler_params=_cparams("arbitrary"))(h, gf, tgt)


def _bwd_dz(dh, wdown, z, l, ck, name):
    T, D = dh.shape
    F = z.shape[-1]

    def body(d_ref, w_ref, z_ref, o_ref):
        dact = _dot_nt(d_ref[...].astype(BF16), w_ref[...])
        o_ref[...] = (dact * (2.0 * jnp.maximum(z_ref[...], 0.0))).astype(BF16)

    return pl.pallas_call(
        body, name=name, grid=(F // ck, T // TT),
        in_specs=[pl.BlockSpec((TT, D), lambda j, i: (i, 0)), pl.BlockSpec((None, ck, D), lambda j, i: (l, j, 0)),
                  pl.BlockSpec((TT, ck), lambda j, i: (i, j))],
        out_specs=pl.BlockSpec((TT, ck), lambda j, i: (i, j)),
        out_shape=jax.ShapeDtypeStruct((T, F), BF16),
        compiler_params=_cparams("parallel", "parallel"))(dh, wdown, z)


def _bwd_plain(dh, w, l, name):
    T, D = dh.shape
    K = w.shape[1]

    def body(d_ref, w_ref, o_ref):
        o_ref[...] = _dot_nt(d_ref[...].astype(BF16), w_ref[...])

    return pl.pallas_call(
        body, name=name, grid=(T // TT,),
        in_specs=[pl.BlockSpec((TT, D), lambda i: (i, 0)), _layer((K, D), l)],
        out_specs=pl.BlockSpec((TT, K), lambda i: (i, 0)),
        out_shape=jax.ShapeDtypeStruct((T, K), F32),
        compiler_params=_cparams("parallel"))(dh, w)


def _bwd_rms(dx, w, h, g, dh_in, l, ck, name):
    T, D = h.shape
    K = dx.shape[-1]
    nk = K // ck

    def body(x_ref, w_ref, h_ref, g_ref, di_ref, dh_ref, dg_ref, acc_ref):
        i, k = pl.program_id(0), pl.program_id(1)

        @pl.when((i == 0) & (k == 0))
        def _():
            dg_ref[...] = jnp.zeros_like(dg_ref)

        @pl.when(k == 0)
        def _():
            acc_ref[...] = jnp.zeros_like(acc_ref)

        acc_ref[...] += _dot_nt(x_ref[...], w_ref[...])

        @pl.when(k == nk - 1)
        def _():
            du = acc_ref[...]
            xh, r = _rms(h_ref[...], None)
            dg_ref[...] += _colsum(du * xh)
            dh_ref[...] = di_ref[...] + _rms_bwd(du, xh, r, g_ref[l:l + 1, :])

    return pl.pallas_call(
        body, name=name, grid=(T // TT, nk),
        in_specs=[pl.BlockSpec((TT, ck), lambda i, k: (i, k)), pl.BlockSpec((None, D, ck), lambda i, k: (l, 0, k)),
                  pl.BlockSpec((TT, D), lambda i, k: (i, 0)), _full(g.shape),
                  pl.BlockSpec((TT, D), lambda i, k: (i, 0))],
        out_specs=[pl.BlockSpec((TT, D), lambda i, k: (i, 0)), _full((1, D))],
        out_shape=[jax.ShapeDtypeStruct((T, D), F32), jax.ShapeDtypeStruct((1, D), F32)],
        scratch_shapes=[pltpu.VMEM((TT, D), F32)],
        compiler_params=_cparams("arbitrary", "arbitrary"))(dx, w, h, g, dh_in)


def _mm_tn(a, b, a_mode, g, l, chunk, ck, name):
    T = a.shape[0]
    Ka, Nb = a.shape[1], b.shape[1]
    ca, cb = (ck, Nb) if chunk == "a" else (Ka, ck)
    nj = (Ka // ck) if chunk == "a" else (Nb // ck)

    def body(*refs):
        if a_mode == "rms":
            a_ref, g_ref, b_ref, o_ref = refs
        else:
            a_ref, b_ref, o_ref = refs
        t = pl.program_id(1)

        @pl.when(t == 0)
        def _():
            o_ref[...] = jnp.zeros_like(o_ref)

        av = a_ref[...]
        if a_mode == "relu2":
            av = jnp.square(jnp.maximum(av, 0.0))
        elif a_mode == "rms":
            xh, _ = _rms(av, None)
            av = xh * g_ref[l:l + 1, :]
        o_ref[...] += _dot_tn(av.astype(BF16), b_ref[...].astype(BF16))

    if chunk == "a":
        a_spec = pl.BlockSpec((TT, ca), lambda j, t: (t, j))
        b_spec = pl.BlockSpec((TT, cb), lambda j, t: (t, 0))
        o_spec = pl.BlockSpec((ca, cb), lambda j, t: (j, 0))
    else:
        a_spec = pl.BlockSpec((TT, ca), lambda j, t: (t, 0))
        b_spec = pl.BlockSpec((TT, cb), lambda j, t: (t, j))
        o_spec = pl.BlockSpec((ca, cb), lambda j, t: (0, j))
    ins, specs = [a], [a_spec]
    if a_mode == "rms":
        ins.append(g)
        specs.append(_full(g.shape))
    ins.append(b)
    specs.append(b_spec)
    return pl.pallas_call(
        body, name=name, grid=(nj, T // TT), in_specs=specs, out_specs=o_spec,
        out_shape=jax.ShapeDtypeStruct((Ka, Nb), F32),
        compiler_params=_cparams("parallel", "arbitrary"))(*ins)


def _shift_down(x, k):
    return x if k == 0 else pltpu.roll(x, k, 0)


def _shift_up(x, k):
    return x if k == 0 else pltpu.roll(x, x.shape[0] - k, 0)


def _pool_sel(lane_grp, vals):
    return jnp.where(lane_grp == 0, vals[0], jnp.where(lane_grp == 1, vals[1], jnp.where(lane_grp == 2, vals[2], vals[3])))


def _pool_fwd(ue, pos):
    s = ue
    sums = []
    for k in (1, 2, 4, 8):
        s = s + _shift_down(s, k)
        sums.append(s[HALO:])
    grp = lax.broadcasted_iota(jnp.int32, (1, D_POOL), 1) // POOL_GW
    wsum = _pool_sel(grp, sums)
    width = _pool_sel(grp, [jnp.float32(w) for w in POOL_WINDOWS])
    cnt = jnp.minimum(pos.astype(F32), width)
    return wsum / cnt - ue[HALO:], cnt, grp


def _conv_fwd(xe, w_ref, nk):
    acc = None
    for k in range(nk):
        term = _shift_down(xe, nk - 1 - k)[HALO:] * w_ref(k)
        acc = term if acc is None else acc + term
    return acc


def _layernorm(cb):
    mu = jnp.mean(cb, axis=-1, keepdims=True)
    xc = cb - mu
    rstd = lax.rsqrt(jnp.mean(xc * xc, axis=-1, keepdims=True) + EPS)
    return xc * rstd, rstd


def _softplus_neg(lam):
    x = -lam
    e = jnp.exp(-jnp.abs(x))
    u = 1.0 + e
    d = u - 1.0
    log1p = jnp.where(d == 0.0, e, jnp.log(u) * (e / jnp.where(d == 0.0, 1.0, d)))
    return jnp.maximum(x, 0.0) + log1p


def _expm1(x):
    small = x * (1.0 + x * 0.5 * (1.0 + x * (1.0 / 3.0) * (1.0 + x * 0.25)))
    return jnp.where(jnp.abs(x) < 1e-2, small, jnp.exp(x) - 1.0)


def _gelu_parts(x):
    c0 = 0.7978845608028654
    inner = c0 * (x + 0.044715 * x * x * x)
    th = jnp.tanh(inner)
    gelu = 0.5 * x * (1.0 + th)
    dgelu = 0.5 * (1.0 + th) + 0.5 * x * (1.0 - th * th) * c0 * (1.0 + 3.0 * 0.044715 * x * x)
    return gelu, dgelu


def _rg_gates(xc, wa_ref, wx_ref, v512_ref):
    xb = xc.astype(BF16)
    ra = _sigmoid(_dot(xb, wa_ref[...]) + v512_ref[1:2, :])
    ri = _sigmoid(_dot(xb, wx_ref[...]) + v512_ref[2:3, :])
    sp = _softplus_neg(v512_ref[3:4, :])
    log_a = (-RG_C) * ra * sp
    a = jnp.exp(log_a)
    mult = jnp.sqrt(-_expm1(2.0 * log_a))
    return ra, ri, sp, a, mult


def _scan_down(a, b):
    n = a.shape[0]
    row = lax.broadcasted_iota(jnp.int32, (n, 1), 0)
    k = 1
    while k < n:
        ok = row >= k
        b = a * jnp.where(ok, _shift_down(b, k), 0.0) + b
        a = a * jnp.where(ok, _shift_down(a, k), 1.0)
        k *= 2
    return b, a


def _scan_up(c, g):
    n = c.shape[0]
    row = lax.broadcasted_iota(jnp.int32, (n, 1), 0)
    k = 1
    while k < n:
        ok = row < n - k
        g = c * jnp.where(ok, _shift_up(g, k), 0.0) + g
        c = c * jnp.where(ok, _shift_up(c, k), 1.0)
        k *= 2
    return g, c


def _mixer_specs(l, tile_of):
    r = TT // HALO
    return [
        pl.BlockSpec((TT, 1792), lambda i: (tile_of(i), 0)),
        pl.BlockSpec((HALO, 1792), lambda i: (jnp.maximum(tile_of(i) * r - 1, 0), 0)),
        _layer((256, 256), l), _layer((8, 256), l), _layer((32, 256), l), _layer((256, 256), l),
        _layer((512, 512), l), _layer((512, 512), l), _layer((8, 512), l),
    ]


def _mixer_fwd(p, wts, l):
    T = p.shape[0]

    def body(p_ref, ph_ref, wpool_ref, v256_ref, dww_ref, wpw_ref, wa_ref, wx_ref, v512_ref, y_ref, hr_ref, carry_ref):
        i = pl.program_id(0)

        @pl.when(i == 0)
        def _():
            carry_ref[...] = jnp.zeros_like(carry_ref)

        halo = jnp.where(i == 0, 0.0, ph_ref[...])
        e = jnp.concatenate([halo, p_ref[...]], axis=0)
        pos = i * TT + 1 + lax.broadcasted_iota(jnp.int32, (TT, 1), 0)
        pooled, _, _ = _pool_fwd(e[:, 0:256], pos)
        y_a = _dot(pooled.astype(BF16), wpool_ref[...]) * v256_ref[0:1, :]
        ub = e[:, 256:512] * _sigmoid(e[:, 512:768])
        cb = _conv_fwd(ub, lambda k: dww_ref[k:k + 1, :], CONV_K) + v256_ref[1:2, :]
        xhat, _ = _layernorm(cb)
        ln = xhat * v256_ref[2:3, :] + v256_ref[3:4, :]
        s = ln * _sigmoid(ln)
        y_b = _dot(s.astype(BF16), wpw_ref[...])
        xc = _conv_fwd(e[:, 1280:1792], lambda k: v512_ref[4 + k:5 + k, :], RG_CONV_K) + v512_ref[0:1, :]
        _, ri, _, a, mult = _rg_gates(xc, wa_ref, wx_ref, v512_ref)
        hloc, acum = _scan_down(a, mult * ri * xc)
        hr = hloc + acum * carry_ref[0:1, :]
        hr_ref[...] = hr
        carry_ref[0:1, :] = hr_ref[TT - 1:TT, :]
        gelu, _ = _gelu_parts(p_ref[:, 768:1280])
        y_ref[:, 0:256] = y_a.astype(BF16)
        y_ref[:, 256:512] = y_b.astype(BF16)
        y_ref[:, 512:1024] = (gelu * hr).astype(BF16)

    return pl.pallas_call(
        body, name=f"mixer_fwd{l}", grid=(T // TT,),
        in_specs=_mixer_specs(l, lambda i: i),
        out_specs=[pl.BlockSpec((TT, 1024), lambda i: (i, 0)), pl.BlockSpec((TT, 512), lambda i: (i, 0))],
        out_shape=[jax.ShapeDtypeStruct((T, 1024), BF16), jax.ShapeDtypeStruct((T, 512), F32)],
        scratch_shapes=[pltpu.VMEM((8, 512), F32)],
        compiler_params=_cparams("arbitrary"))(p, p, *wts)


def _mixer_bwd(p, hr, dy, wts, l):
    T = p.shape[0]
    nt = T // TT
    tile_of = lambda i: nt - 1 - i

    def body(p_ref, ph_ref, wpool_ref, v256_ref, dww_ref, wpw_ref, wa_ref, wx_ref, v512_ref, hr_ref, hrh_ref, dy_ref,
             dp_ref, g256_ref, g512_ref, gpool_ref, gpw_ref, gwa_ref, gwx_ref,
             q_c, dcb_c, dxc_c, ag_c):
        i = pl.program_id(0)
        j = nt - 1 - i

        @pl.when(i == 0)
        def _():
            for ref in (g256_ref, g512_ref, gpool_ref, gpw_ref, gwa_ref, gwx_ref, q_c, dcb_c, dxc_c, ag_c):
                ref[...] = jnp.zeros_like(ref)

        halo = jnp.where(j == 0, 0.0, ph_ref[...])
        e = jnp.concatenate([halo, p_ref[...]], axis=0)
        pos = j * TT + 1 + lax.broadcasted_iota(jnp.int32, (TT, 1), 0)
        row = lax.broadcasted_iota(jnp.int32, (TT, 1), 0)

        dy_a = dy_ref[:, 0:256]
        pooled, cnt, grp = _pool_fwd(e[:, 0:256], pos)
        scale = v256_ref[0:1, :]
        mixed = _dot(pooled.astype(BF16), wpool_ref[...])
        g256_ref[34:35, :] += _colsum(dy_a * mixed)
        dmixed = (dy_a * scale).astype(BF16)
        gpool_ref[...] += _dot_tn(pooled.astype(BF16), dmixed)
        dpooled = _dot_nt(dmixed, wpool_ref[...])
        q = dpooled / cnt
        s = jnp.concatenate([q, q_c[...]], axis=0)
        sums = []
        for k in (1, 2, 4, 8):
            s = s + _shift_up(s, k)
            sums.append(s[:TT])
        dp_ref[:, 0:256] = (_pool_sel(grp, sums) - dpooled).astype(BF16)
        q_c[...] = q[0:16]

        dy_b = dy_ref[:, 256:512].astype(BF16)
        sg = _sigmoid(e[:, 512:768])
        ub = e[:, 256:512] * sg
        cb = _conv_fwd(ub, lambda k: dww_ref[k:k + 1, :], CONV_K) + v256_ref[1:2, :]
        xhat, rstd = _layernorm(cb)
        ln_g = v256_ref[2:3, :]
        ln = xhat * ln_g + v256_ref[3:4, :]
        sig = _sigmoid(ln)
        sact = ln * sig
        gpw_ref[...] += _dot_tn(sact.astype(BF16), dy_b)
        dln = _dot_nt(dy_b, wpw_ref[...]) * (sig * (1.0 + ln * (1.0 - sig)))
        g256_ref[32:33, :] += _colsum(dln * xhat)
        g256_ref[33:34, :] += _colsum(dln)
        dxhat = dln * ln_g
        dcb = rstd * (dxhat - jnp.mean(dxhat, axis=-1, keepdims=True)
                      - xhat * jnp.mean(dxhat * xhat, axis=-1, keepdims=True))
        g256_ref[31:32, :] += _colsum(dcb)
        for k in range(CONV_K):
            g256_ref[k:k + 1, :] += _colsum(dcb * _shift_down(ub, CONV_K - 1 - k)[HALO:])
        ed = jnp.concatenate([dcb, dcb_c[...]], axis=0)
        dub = None
        for k in range(CONV_K):
            term = _shift_up(ed, CONV_K - 1 - k)[:TT] * dww_ref[k:k + 1, :]
            dub = term if dub is None else dub + term
        dcb_c[...] = dcb[0:32]
        sg_t = sg[HALO:]
        v_t = p_ref[:, 256:512]
        dp_ref[:, 256:512] = (dub * sg_t).astype(BF16)
        dp_ref[:, 512:768] = (dub * v_t * sg_t * (1.0 - sg_t)).astype(BF16)

        dy_c = dy_ref[:, 512:1024]
        cxe = e[:, 1280:1792]
        xc = _conv_fwd(cxe, lambda k: v512_ref[4 + k:5 + k, :], RG_CONV_K) + v512_ref[0:1, :]
        ra, ri, sp, a, mult = _rg_gates(xc, wa_ref, wx_ref, v512_ref)
        hrv = hr_ref[...]
        gelu, dgelu = _gelu_parts(p_ref[:, 768:1280])
        dp_ref[:, 768:1280] = (dy_c * hrv * dgelu).astype(BF16)
        dhr = dy_c * gelu
        coef = jnp.where(row < TT - 1, _shift_up(a, 1), 1.0)
        gloc, ccum = _scan_up(coef, dhr)
        gg = gloc + ccum * ag_c[0:1, :]
        ag_c[...] = (a * gg)[0:8]
        hr_before = jnp.where(j == 0, 0.0, hrh_ref[7:8, :])
        hr_prev = jnp.where(row >= 1, _shift_down(hrv, 1), hr_before)
        da = gg * hr_prev
        dmult = gg * ri * xc
        dri = gg * mult * xc
        dxc = gg * mult * ri
        dlog_a = da * a - dmult * (a * a / mult)
        g512_ref[3:4, :] += _colsum(dlog_a * ((-RG_C) * ra))
        dpre_a = dlog_a * ((-RG_C) * sp) * ra * (1.0 - ra)
        dpre_x = dri * ri * (1.0 - ri)
        g512_ref[1:2, :] += _colsum(dpre_a)
        g512_ref[2:3, :] += _colsum(dpre_x)
        xb = xc.astype(BF16)
        dpa_b, dpx_b = dpre_a.astype(BF16), dpre_x.astype(BF16)
        gwa_ref[...] += _dot_tn(xb, dpa_b)
        gwx_ref[...] += _dot_tn(xb, dpx_b)
        dxc = dxc + _dot_nt(dpa_b, wa_ref[...]) + _dot_nt(dpx_b, wx_ref[...])
        g512_ref[0:1, :] += _colsum(dxc)
        for k in range(RG_CONV_K):
            g512_ref[4 + k:5 + k, :] += _colsum(dxc * _shift_down(cxe, RG_CONV_K - 1 - k)[HALO:])
        ex = jnp.concatenate([dxc, dxc_c[...]], axis=0)
        dcx = None
        for k in range(RG_CONV_K):
            term = _shift_up(ex, RG_CONV_K - 1 - k)[:TT] * v512_ref[4 + k:5 + k, :]
            dcx = term if dcx is None else dcx + term
        dxc_c[...] = dxc[0:8]
        dp_ref[:, 1280:1792] = dcx.astype(BF16)

        @pl.when(i == nt - 1)
        def _():
            lam = v512_ref[3:4, :]
            g512_ref[3:4, :] = g512_ref[3:4, :] * (-_sigmoid(-lam))

    in_specs = _mixer_specs(l, tile_of) + [
        pl.BlockSpec((TT, 512), lambda i: (tile_of(i), 0)),
        pl.BlockSpec((8, 512), lambda i: (jnp.maximum(tile_of(i) * (TT // 8) - 1, 0), 0)),
        pl.BlockSpec((TT, 1024), lambda i: (tile_of(i), 0)),
    ]
    acc_shapes = [(40, 256), (8, 512), (256, 256), (256, 256), (512, 512), (512, 512)]
    return pl.pallas_call(
        body, name=f"mixer_bwd{l}", grid=(nt,),
        in_specs=in_specs,
        out_specs=[pl.BlockSpec((TT, 1792), lambda i: (tile_of(i), 0))] + [_full(s) for s in acc_shapes],
        out_shape=[jax.ShapeDtypeStruct((T, 1792), BF16)] + [jax.ShapeDtypeStruct(s, F32) for s in acc_shapes],
        scratch_shapes=[pltpu.VMEM((16, 256), F32), pltpu.VMEM((32, 256), F32), pltpu.VMEM((8, 512), F32),
                        pltpu.VMEM((8, 512), F32)],
        compiler_params=_cparams("arbitrary"))(p, p, *wts, hr, hr, dy)


def _mesh_pos():
    return lax.axis_index("x"), lax.axis_index("y"), lax.axis_index("c")


def _slot(d):
    return 4 * d[0] + 2 * d[1] + d[2]


def _allgather(items, name):
    n = len(items)

    def body(*refs):
        ins, outs = refs[:n], refs[n:2 * n]
        send_sems, recv_sems, local_sems = refs[2 * n:]
        x, y, c = _mesh_pos()
        me, sib = (x, y, c), (x, y, 1 - c)
        chips = [(1 - x, y), (x, 1 - y), (1 - x, 1 - y)]

        def copy(a, k, blk, to, from_input=False):
            region = items[a][2](outs[a], _slot(blk))
            return pltpu.make_async_remote_copy(
                src_ref=ins[a] if from_input else region, dst_ref=region,
                send_sem=send_sems.at[a * 7 + k], recv_sem=recv_sems.at[a * 7 + k],
                device_id=to, device_id_type=MESH)

        local = [pltpu.make_async_copy(ins[a], items[a][2](outs[a], _slot(me)), local_sems.at[a]) for a in range(n)]
        for cp in local:
            cp.start()
        first = []
        for a in range(n):
            first.append(copy(a, 0, me, sib, True))
            for j, chip in enumerate(chips):
                first.append(copy(a, 1 + j, me, (*chip, c), True))
        for cp in first:
            cp.start()
        passed = []
        for j, chip in enumerate(chips):
            for a in range(n):
                copy(a, 1 + j, (*chip, c), me).wait_recv()
                fwd = copy(a, 4 + j, (*chip, c), sib)
                fwd.start()
                passed.append(fwd)
        for a in range(n):
            copy(a, 0, sib, me).wait_recv()
            for j, chip in enumerate(chips):
                copy(a, 4 + j, (*chip, 1 - c), me).wait_recv()
        for cp in first + passed:
            cp.wait_send()
        for cp in local:
            cp.wait()

    return pl.pallas_call(
        body, name=name,
        in_specs=[ANY] * n, out_specs=[ANY] * n,
        out_shape=[jax.ShapeDtypeStruct(it[1], it[0].dtype) for it in items],
        scratch_shapes=[pltpu.SemaphoreType.DMA((7 * n,)), pltpu.SemaphoreType.DMA((7 * n,)),
                        pltpu.SemaphoreType.DMA((n,))],
    )(*[it[0] for it in items])


def _alltoall(pieces, recv_shapes, name):
    n = len(pieces)
    nr = len(recv_shapes)

    def body(*refs):
        ins, outs = refs[:n], refs[n:n + nr]
        send_sems, recv_sems, local_sems = refs[n + nr:]
        x, y, c = _mesh_pos()
        me = (x, y, c)
        peers = []
        for d in range(1, N_DEV):
            dx, dy, dc = (d >> 2) & 1, (d >> 1) & 1, d & 1
            peers.append((1 - x if dx else x, 1 - y if dy else y, 1 - c if dc else c))

        def copy(a, k, to):
            _, take, r, l = pieces[a]
            return pltpu.make_async_remote_copy(
                src_ref=take(ins[a], _slot(to)), dst_ref=outs[r].at[_slot(me), l],
                send_sem=send_sems.at[a * 7 + k], recv_sem=recv_sems.at[a * 7 + k],
                device_id=to, device_id_type=MESH)

        def arrival(a, k, frm):
            _, take, r, l = pieces[a]
            region = outs[r].at[_slot(frm), l]
            return pltpu.make_async_remote_copy(
                src_ref=region, dst_ref=region, send_sem=send_sems.at[a * 7 + k], recv_sem=recv_sems.at[a * 7 + k],
                device_id=frm, device_id_type=MESH)

        local = []
        for a in range(n):
            _, take, r, l = pieces[a]
            local.append(pltpu.make_async_copy(take(ins[a], _slot(me)), outs[r].at[_slot(me), l], local_sems.at[a]))
        for cp in local:
            cp.start()
        sends = [copy(a, k, to) for k, to in enumerate(peers) for a in range(n)]
        for cp in sends:
            cp.start()
        for k, frm in enumerate(peers):
            for a in range(n):
                arrival(a, k, frm).wait_recv()
        for cp in sends:
            cp.wait_send()
        for cp in local:
            cp.wait()

    return pl.pallas_call(
        body, name=name,
        in_specs=[ANY] * n, out_specs=[ANY] * nr,
        out_shape=[jax.ShapeDtypeStruct(s, F32) for s in recv_shapes],
        scratch_shapes=[pltpu.SemaphoreType.DMA((7 * n,)), pltpu.SemaphoreType.DMA((7 * n,)),
                        pltpu.SemaphoreType.DMA((n,))],
    )(*[pc[0] for pc in pieces])


def _adam(w, g, m, v):
    m = ADAM_B1 * m + (1.0 - ADAM_B1) * g
    v = ADAM_B2 * v + (1.0 - ADAM_B2) * (g * g)
    m_hat = m / (1.0 - ADAM_B1 ** ADAM_STEP)
    v_hat = v / (1.0 - ADAM_B2 ** ADAM_STEP)
    delta = -ADAM_LR * (m_hat / (jnp.sqrt(v_hat) + ADAM_EPS) + ADAM_WD * w)
    return delta, m, v


def _adamw_big(recv, w, m, v, tr, name):
    _, L, R, C = recv.shape

    def body(r_ref, w_ref, m_ref, v_ref, g_ref, d_ref, mo_ref, vo_ref):
        g = r_ref[0]
        for s in range(1, N_DEV):
            g = g + r_ref[s]
        g_ref[...] = g
        d_ref[...], mo_ref[...], vo_ref[...] = _adam(w_ref[...], g, m_ref[...], v_ref[...])

    blk = pl.BlockSpec((None, tr, C), lambda l, i: (l, i, 0))
    return pl.pallas_call(
        body, name=name, grid=(L, R // tr),
        in_specs=[pl.BlockSpec((N_DEV, None, tr, C), lambda l, i: (0, l, i, 0)), blk, blk, blk],
        out_specs=[blk] * 4,
        out_shape=[jax.ShapeDtypeStruct((L, R, C), F32)] * 4,
        compiler_params=_cparams("parallel", "parallel"))(recv, w, m, v)


def _adamw_plain(gs, ws, ms, vs, name):
    n = len(gs)

    def body(*refs):
        g_r, w_r, m_r, v_r = refs[:n], refs[n:2 * n], refs[2 * n:3 * n], refs[3 * n:4 * n]
        outs = refs[4 * n:]
        for a in range(n):
            d, mo, vo = _adam(w_r[a][...], g_r[a][...], m_r[a][...], v_r[a][...])
            outs[3 * a][...] = d
            outs[3 * a + 1][...] = mo
            outs[3 * a + 2][...] = vo

    shapes = []
    for a in range(n):
        shapes += [jax.ShapeDtypeStruct(ws[a].shape, F32)] * 3
    return pl.pallas_call(body, name=name, out_shape=shapes)(*gs, *ws, *ms, *vs)


_V256_ROWS = {"convb_dw_b": 31, "convb_ln_g": 32, "convb_ln_b": 33, "pool_scale": 34}
_V512_ROWS = {"rg_conv_b": 0, "rg_b_a": 1, "rg_b_x": 2, "rg_lambda": 3}
_SMALL_REPL = ["mix_norm_g", "mlp_norm_g", "final_norm_g", "pool_w", "pool_scale", "convb_dw_b", "convb_ln_g",
               "convb_ln_b", "rg_conv_b", "rg_w_a", "rg_b_a", "rg_w_x", "rg_b_x", "rg_lambda"]


def _adamw_small(gath, ws, ms, vs):
    n = len(_SMALL_REPL)
    sum_shapes = [g.shape[1:] for g in gath]

    def body(*refs):
        g_in = refs[:10]
        w_r, m_r, v_r = refs[10:10 + n], refs[10 + n:10 + 2 * n], refs[10 + 2 * n:10 + 3 * n]
        sums = refs[10 + 3 * n:20 + 3 * n]
        outs = refs[20 + 3 * n:]
        for a in range(10):
            acc = g_in[a][0]
            for s in range(1, N_DEV):
                acc = acc + g_in[a][s]
            sums[a][...] = acc
        s_mix, s_mlp, s_fin, s_pool, s_256, s_512, _, s_wa, s_wx, _ = sums

        def grad_of(name, idx):
            if name == "mix_norm_g":
                return s_mix[idx[0]:idx[0] + 1, :]
            if name == "mlp_norm_g":
                return s_mlp[idx[0]:idx[0] + 1, :]
            if name == "final_norm_g":
                return s_fin[...]
            if name in _V256_ROWS:
                r = _V256_ROWS[name]
                return s_256[idx[0], r:r + 1, :]
            if name in _V512_ROWS:
                r = _V512_ROWS[name]
                return s_512[idx[0], r:r + 1, :]
            src = {"pool_w": s_pool, "rg_w_a": s_wa, "rg_w_x": s_wx}[name]
            return src[idx[0], idx[1] * 64:(idx[1] + 1) * 64, :]

        for a, name in enumerate(_SMALL_REPL):
            shape = w_r[a].shape
            if name == "final_norm_g":
                parts = [((), (slice(None), slice(None)))]
            elif len(shape) == 2:
                parts = [((l,), (slice(l, l + 1), slice(None))) for l in range(shape[0])]
            else:
                parts = [((l, h), (l, h)) for l in range(shape[0]) for h in range(shape[1])]
            for idx, sel in parts:
                g = grad_of(name, idx)
                d, mo, vo = _adam(w_r[a][sel], g, m_r[a][sel], v_r[a][sel])
                outs[4 * a][sel] = g
                outs[4 * a + 1][sel] = d
                outs[4 * a + 2][sel] = mo
                outs[4 * a + 3][sel] = vo

    out_shape = [jax.ShapeDtypeStruct(s, F32) for s in sum_shapes]
    for a in range(n):
        out_shape += [jax.ShapeDtypeStruct(ws[a].shape, F32)] * 4
    res = pl.pallas_call(body, name="adamw_small", out_shape=out_shape,
                         compiler_params=pltpu.CompilerParams(vmem_limit_bytes=VMEM_LIMIT))(*gath, *ws, *ms, *vs)
    return res[:10], res[10:]


_WEIGHTS = ['meta_tokens', 'mix_norm_g', 'w_in', 'pool_w', 'pool_scale', 'convb_dw_w', 'convb_dw_b', 'convb_ln_g',
            'convb_ln_b', 'convb_pw_w', 'rg_conv_w', 'rg_conv_b', 'rg_w_a', 'rg_b_a', 'rg_w_x', 'rg_b_x', 'rg_lambda',
            'w_out', 'mlp_norm_g', 'w_up', 'w_down', 'final_norm_g']


def _block_diag(w):
    L, H, C, _ = w.shape
    eye = jnp.eye(H, dtype=w.dtype)
    return (w[:, :, :, None, :] * eye[None, :, None, :, None]).reshape(L, H * C, H * C)


def _diag_blocks(m, H):
    C = m.shape[0] // H
    return jnp.concatenate([m[h * C:(h + 1) * C, h * C:(h + 1) * C] for h in range(H)], axis=0)


def kernel(x, meta_tokens, mix_norm_g, w_in, pool_w, pool_scale, convb_dw_w, convb_dw_b, convb_ln_g, convb_ln_b, convb_pw_w, rg_conv_w, rg_conv_b, rg_w_a, rg_b_a, rg_w_x, rg_b_x, rg_lambda, w_out, mlp_norm_g, w_up, w_down, final_norm_g, loss_target, m_meta_tokens, m_mix_norm_g, m_w_in, m_pool_w, m_pool_scale, m_convb_dw_w, m_convb_dw_b, m_convb_ln_g, m_convb_ln_b, m_convb_pw_w, m_rg_conv_w, m_rg_conv_b, m_rg_w_a, m_rg_b_a, m_rg_w_x, m_rg_b_x, m_rg_lambda, m_w_out, m_mlp_norm_g, m_w_up, m_w_down, m_final_norm_g, v_meta_tokens, v_mix_norm_g, v_w_in, v_pool_w, v_pool_scale, v_convb_dw_w, v_convb_dw_b, v_convb_ln_g, v_convb_ln_b, v_convb_pw_w, v_rg_conv_w, v_rg_conv_b, v_rg_w_a, v_rg_b_a, v_rg_w_x, v_rg_b_x, v_rg_lambda, v_w_out, v_mlp_norm_g, v_w_up, v_w_down, v_final_norm_g):
    W = dict(meta_tokens=meta_tokens, mix_norm_g=mix_norm_g, w_in=w_in, pool_w=pool_w, pool_scale=pool_scale,
             convb_dw_w=convb_dw_w, convb_dw_b=convb_dw_b, convb_ln_g=convb_ln_g, convb_ln_b=convb_ln_b,
             convb_pw_w=convb_pw_w, rg_conv_w=rg_conv_w, rg_conv_b=rg_conv_b, rg_w_a=rg_w_a, rg_b_a=rg_b_a,
             rg_w_x=rg_w_x, rg_b_x=rg_b_x, rg_lambda=rg_lambda, w_out=w_out, mlp_norm_g=mlp_norm_g, w_up=w_up,
             w_down=w_down, final_norm_g=final_norm_g.reshape(1, -1))
    M = dict(meta_tokens=m_meta_tokens, mix_norm_g=m_mix_norm_g, w_in=m_w_in, pool_w=m_pool_w, pool_scale=m_pool_scale,
             convb_dw_w=m_convb_dw_w, convb_dw_b=m_convb_dw_b, convb_ln_g=m_convb_ln_g, convb_ln_b=m_convb_ln_b,
             convb_pw_w=m_convb_pw_w, rg_conv_w=m_rg_conv_w, rg_conv_b=m_rg_conv_b, rg_w_a=m_rg_w_a, rg_b_a=m_rg_b_a,
             rg_w_x=m_rg_w_x, rg_b_x=m_rg_b_x, rg_lambda=m_rg_lambda, w_out=m_w_out, mlp_norm_g=m_mlp_norm_g,
             w_up=m_w_up, w_down=m_w_down, final_norm_g=m_final_norm_g.reshape(1, -1))
    V = dict(meta_tokens=v_meta_tokens, mix_norm_g=v_mix_norm_g, w_in=v_w_in, pool_w=v_pool_w, pool_scale=v_pool_scale,
             convb_dw_w=v_convb_dw_w, convb_dw_b=v_convb_dw_b, convb_ln_g=v_convb_ln_g, convb_ln_b=v_convb_ln_b,
             convb_pw_w=v_convb_pw_w, rg_conv_w=v_rg_conv_w, rg_conv_b=v_rg_conv_b, rg_w_a=v_rg_w_a, rg_b_a=v_rg_b_a,
             rg_w_x=v_rg_w_x, rg_b_x=v_rg_b_x, rg_lambda=v_rg_lambda, w_out=v_w_out, mlp_norm_g=v_mlp_norm_g,
             w_up=v_w_up, w_down=v_w_down, final_norm_g=v_final_norm_g.reshape(1, -1))

    xs = x[0]
    S, D = xs.shape
    T = S + N_META
    assert T % TT == 0 and D == 1024
    L = DEPTH
    me = 4 * lax.axis_index("x") + 2 * lax.axis_index("y") + lax.axis_index("c")
    d_in = w_in.shape[-1] * N_DEV
    d_ff = w_up.shape[-1] * N_DEV
    c_in, c_ff = w_in.shape[-1], w_up.shape[-1]
    r_out, r_dn = w_out.shape[1], w_down.shape[1]

    by_slot = lambda ref, s: ref.at[s]
    gathered = _allgather([
        (w_in.astype(BF16), (N_DEV,) + w_in.shape, by_slot),
        (w_out.astype(BF16), (L, r_out * N_DEV, D), lambda ref, s: ref.at[:, pl.ds(s * r_out, r_out), :]),
        (w_up.astype(BF16), (L, D, d_ff), lambda ref, s: ref.at[:, :, pl.ds(s * c_ff, c_ff)]),
        (w_down.astype(BF16), (L, d_ff, D), lambda ref, s: ref.at[:, pl.ds(s * r_dn, r_dn), :]),
        (meta_tokens, (N_DEV,) + meta_tokens.shape, by_slot),
        (convb_dw_w, (N_DEV,) + convb_dw_w.shape, by_slot),
        (convb_pw_w, (N_DEV,) + convb_pw_w.shape, by_slot),
        (rg_conv_w, (N_DEV,) + rg_conv_w.shape, by_slot),
    ], "gather_weights")
    win_f = jnp.transpose(gathered[0], (1, 2, 0, 3)).reshape(L, D, d_in)
    wout_f, wup_f, wdown_f = gathered[1], gathered[2], gathered[3]
    meta_f = jnp.transpose(gathered[4], (1, 0, 2)).reshape(N_META, D)
    dww_f = jnp.transpose(gathered[5], (1, 2, 0, 3)).reshape(L, CONV_K, D_CONV)
    wpw_f = jnp.transpose(gathered[6], (1, 0, 2, 3)).reshape(L, D_CONV, D_CONV).astype(BF16)
    rgw_f = jnp.transpose(gathered[7], (1, 2, 0, 3)).reshape(L, RG_CONV_K, D_RNN)

    wpool_bd = _block_diag(pool_w).astype(BF16)
    wa_bd = _block_diag(rg_w_a).astype(BF16)
    wx_bd = _block_diag(rg_w_x).astype(BF16)
    zeros4 = jnp.zeros((L, 4, D_POOL), F32)
    v256 = jnp.concatenate([pool_scale[:, None], convb_dw_b[:, None], convb_ln_g[:, None], convb_ln_b[:, None], zeros4], axis=1)
    dww_p = jnp.concatenate([dww_f, jnp.zeros((L, 1, D_CONV), F32)], axis=1)
    v512 = jnp.concatenate([rg_conv_b[:, None], rg_b_a[:, None], rg_b_x[:, None], rg_lambda[:, None], rgw_f], axis=1)
    mix_w = (wpool_bd, v256, dww_p, wpw_f, wa_bd, wx_bd, v512)

    h = jnp.concatenate([meta_f, xs], axis=0)
    tgt = jnp.concatenate([jnp.zeros((N_META, D), F32), loss_target[0]], axis=0)
    saved = []
    for l in range(L):
        p = _rms_mm(h, mix_norm_g, win_f, l, d_in, f"in_proj{l}")
        y, hr = _mixer_fwd(p, mix_w, l)
        h2 = _mm_res(h, y, wout_f, l, y.shape[-1], False, f"out_proj{l}")
        z = _rms_mm(h2, mlp_norm_g, wup_f, l, 2048, f"mlp_up{l}")
        h3 = _mm_res(h2, z, wdown_f, l, 2048, True, f"mlp_down{l}")
        saved.append((h, p, y, hr, h2, z))
        h = h3

    dh, g_fin, loss_part = _final_loss(h, W["final_norm_g"], tgt)
    loss = lax.psum(loss_part[0, 0], ("x", "y", "c"))

    g_big = {}
    g_small = []
    for l in reversed(range(L)):
        h0, p, y, hr, h2, z = saved[l]
        dz = _bwd_dz(dh, wdown_f, z, l, 2048, f"mlp_down_bwd{l}")
        g_wdown = _mm_tn(z, dh, "relu2", None, l, "a", 1024, f"mlp_down_wgrad{l}")
        dh2, g_mlp = _bwd_rms(dz, wup_f, h2, mlp_norm_g, dh, l, 2048, f"mlp_up_bwd{l}")
        g_wup = _mm_tn(h2, dz, "rms", mlp_norm_g, l, "b", 1024, f"mlp_up_wgrad{l}")
        dy = _bwd_plain(dh2, wout_f, l, f"out_proj_bwd{l}")
        g_wout = _mm_tn(y, dh2, "plain", None, l, "b", D, f"out_proj_wgrad{l}")
        dp, g256, g512, gpool, gpw, gwa, gwx = _mixer_bwd(p, hr, dy, mix_w, l)
        dh, g_mix = _bwd_rms(dp, win_f, h0, mix_norm_g, dh2, l, d_in, f"in_proj_bwd{l}")
        g_win = _mm_tn(h0, dp, "rms", mix_norm_g, l, "b", d_in // 2, f"in_proj_wgrad{l}")
        g_big[l] = (g_win, g_wout, g_wup, g_wdown)
        g_small.append((g_mix, g_mlp, _diag_blocks(gpool, 4), g256, g512, gpw, _diag_blocks(gwa, RG_HEADS),
                        _diag_blocks(gwx, RG_HEADS)))
    g_small = g_small[::-1]
    grad_x = dh[N_META:][None]

    pieces = []
    for l in range(L):
        g_win, g_wout, g_wup, g_wdown = g_big[l]
        g_win_s = jnp.transpose(g_win.reshape(D, N_DEV, c_in), (1, 0, 2))
        pieces += [
            (g_win_s, by_slot, 0, l),
            (g_wout, lambda ref, s: ref.at[pl.ds(s * r_out, r_out), :], 1, l),
            (g_wup, lambda ref, s: ref.at[:, pl.ds(s * c_ff, c_ff)], 2, l),
            (g_wdown, lambda ref, s: ref.at[pl.ds(s * r_dn, r_dn), :], 3, l),
        ]
    recv = _alltoall(pieces, [(N_DEV, L, D, c_in), (N_DEV, L, r_out, D), (N_DEV, L, D, c_ff), (N_DEV, L, r_dn, D)],
                     "exchange_grads")
    out = {}
    for name, r, tr in (("w_in", recv[0], 256), ("w_out", recv[1], r_out), ("w_up", recv[2], 256), ("w_down", recv[3], 128)):
        out[name] = _adamw_big(r, W[name], M[name], V[name], tr, f"adamw_{name}")

    stack = lambda k: jnp.stack([g_small[l][k] for l in range(L)], axis=0)
    small_local = [stack(0).reshape(L, D), stack(1).reshape(L, D), g_fin, stack(2), stack(3), stack(4), stack(5),
                   stack(6), stack(7), dh[:N_META]]
    gath = _allgather([(g, (N_DEV,) + g.shape, by_slot) for g in small_local], "gather_small_grads")
    sums, small_out = _adamw_small(gath, [W[n] for n in _SMALL_REPL], [M[n] for n in _SMALL_REPL],
                                   [V[n] for n in _SMALL_REPL])
    for a, name in enumerate(_SMALL_REPL):
        out[name] = small_out[4 * a:4 * a + 4]
    s_256, s_512, s_pw, s_meta = sums[4], sums[5], sums[6], sums[9]
    g_shard = {
        "meta_tokens": lax.dynamic_slice_in_dim(s_meta, me * 128, 128, axis=1),
        "convb_dw_w": lax.dynamic_slice_in_dim(s_256[:, :CONV_K, :], me * 32, 32, axis=2),
        "convb_pw_w": lax.dynamic_slice_in_dim(s_pw, me * 32, 32, axis=1),
        "rg_conv_w": lax.dynamic_slice_in_dim(s_512[:, 4:8, :], me * 64, 64, axis=2),
    }
    names = list(g_shard)
    res = _adamw_plain([g_shard[n] for n in names], [W[n] for n in names], [M[n] for n in names],
                       [V[n] for n in names], "adamw_small_sharded")
    for a, name in enumerate(names):
        out[name] = (g_shard[name],) + tuple(res[3 * a:3 * a + 3])

    def fix(name, arr):
        return arr.reshape(-1) if name == "final_norm_g" else arr

    return (loss, grad_x,
            *[fix(n, out[n][0]) for n in _WEIGHTS], *[fix(n, out[n][1]) for n in _WEIGHTS],
            *[fix(n, out[n][2]) for n in _WEIGHTS], *[fix(n, out[n][3]) for n in _WEIGHTS])
```

```python
import functools
import operator

import jax
import jax.numpy as jnp
from jax import lax
from jax.experimental import pallas as pl
from jax.experimental.pallas import tpu as pltpu

F32 = jnp.float32
BF16 = jnp.bfloat16

N_DEV = 8
N_META = 16
DEPTH = 2
EPS = 1e-6
POOL_WINDOWS = (2, 4, 8, 16)
POOL_GW = 64
CONV_K = 31
RG_CONV_K = 4
RG_HEADS = 8
RG_HD = 64
RG_C = 8.0
D_POOL = 256
D_CONV = 256
D_RNN = 512

ADAM_LR = 0.001
ADAM_B1 = 0.9
ADAM_B2 = 0.999
ADAM_EPS = 1e-08
ADAM_WD = 0.01
ADAM_STEP = 10

TT = 432
HALO = 48
VMEM_LIMIT = 56 * 1024 * 1024

MESH = pl.DeviceIdType.MESH
ANY = pl.BlockSpec(memory_space=pl.ANY)


def _cparams(*sem):
    return pltpu.CompilerParams(dimension_semantics=sem, vmem_limit_bytes=VMEM_LIMIT)


def _full(shape):
    nd = len(shape)
    return pl.BlockSpec(shape, lambda *_: (0,) * nd)


def _layer(shape, l):
    nd = len(shape)
    return pl.BlockSpec((None,) + tuple(shape), lambda *_: (l,) + (0,) * nd)


def _dot(a, b):
    return jnp.dot(a, b, preferred_element_type=F32)


def _dot_nt(a, b):
    return lax.dot_general(a, b, (((1,), (1,)), ((), ())), preferred_element_type=F32)


def _dot_tn(a, b):
    return lax.dot_general(a, b, (((0,), (0,)), ((), ())), preferred_element_type=F32)


def _rms(h):
    r = lax.rsqrt(jnp.mean(h * h, axis=-1, keepdims=True) + EPS)
    return h * r, r


def _rms_bwd(du, xh, r, g):
    dxh = du * g
    return r * (dxh - xh * jnp.mean(dxh * xh, axis=-1, keepdims=True))


def _sigmoid(x):
    return 1.0 / (1.0 + jnp.exp(-x))


def _colsum(x):
    return jnp.sum(x, axis=0, keepdims=True)


def _mesh_pos():
    return lax.axis_index("x"), lax.axis_index("y"), lax.axis_index("c")


def _slot(d):
    return 4 * d[0] + 2 * d[1] + d[2]


class _Comm:
    def __init__(self):
        self.srcs, self.send, self.dst, self.land, self.out_shapes = [], [], [], [], []

    def add_out(self, shape, dtype):
        self.out_shapes.append(jax.ShapeDtypeStruct(shape, dtype))
        return len(self.out_shapes) - 1

    def add(self, src, send, dst, land):
        self.srcs.append(src)
        self.send.append(send)
        self.dst.append(dst)
        self.land.append(land)

    def gather(self, block, out_shape, place):
        self.add(block, lambda ref, s: ref, self.add_out(out_shape, block.dtype), place)

    def sem_shapes(self):
        n = len(self.srcs)
        return [pltpu.SemaphoreType.DMA((7 * n,)), pltpu.SemaphoreType.DMA((7 * n,)), pltpu.SemaphoreType.DMA((n,))]

    def _copies(self, cins, couts, send_sems, recv_sems, local_sems, with_arrivals=True):
        x, y, c = _mesh_pos()
        me = (x, y, c)
        sends, arrivals, local = [], [], []
        for a in range(len(self.srcs)):
            out = couts[self.dst[a]]
            mine = self.land[a](out, _slot(me))
            local.append(pltpu.make_async_copy(self.send[a](cins[a], _slot(me)), mine, local_sems.at[a]))
            for k in range(N_DEV - 1):
                d = k + 1
                peer = (1 - x if (d >> 2) & 1 else x, 1 - y if (d >> 1) & 1 else y, 1 - c if d & 1 else c)
                sems = dict(send_sem=send_sems.at[a * 7 + k], recv_sem=recv_sems.at[a * 7 + k],
                            device_id=peer, device_id_type=MESH)
                sends.append(pltpu.make_async_remote_copy(
                    src_ref=self.send[a](cins[a], _slot(peer)), dst_ref=mine, **sems))
                if with_arrivals:
                    theirs = self.land[a](out, _slot(peer))
                    arrivals.append(pltpu.make_async_remote_copy(src_ref=theirs, dst_ref=theirs, **sems))
        return sends, arrivals, local

    def start(self, *refs):
        sends, _, local = self._copies(*refs, with_arrivals=False)
        for cp in local + sends:
            cp.start()

    def wait(self, *refs):
        sends, arrivals, local = self._copies(*refs)
        for cp in arrivals:
            cp.wait_recv()
        for cp in sends:
            cp.wait_send()
        for cp in local:
            cp.wait()


def _call(body, *, name, grid, in_specs, out_specs, out_shape, operands, scratch_shapes=(), sem=None, comm=None):
    out_specs, out_shape = list(out_specs), list(out_shape)
    if comm is None:
        res = pl.pallas_call(body, name=name, grid=grid, in_specs=list(in_specs), out_specs=out_specs,
                             out_shape=out_shape, scratch_shapes=list(scratch_shapes),
                             compiler_params=_cparams(*sem))(*operands)
        return list(res), []
    n_in, n_out, n_scr = len(in_specs), len(out_specs), len(scratch_shapes)
    nc_in, nc_out = len(comm.srcs), len(comm.out_shapes)

    def hosted(*refs):
        ins, cins = refs[:n_in], refs[n_in:n_in + nc_in]
        o0 = n_in + nc_in
        outs, couts = refs[o0:o0 + n_out], refs[o0 + n_out:o0 + n_out + nc_out]
        s0 = o0 + n_out + nc_out
        scr, sems = refs[s0:s0 + n_scr], refs[s0 + n_scr:]
        ids = [pl.program_id(a) for a in range(len(grid))]
        first = functools.reduce(operator.and_, [i == 0 for i in ids])
        last = functools.reduce(operator.and_, [i == g - 1 for i, g in zip(ids, grid)])

        @pl.when(first)
        def _():
            comm.start(cins, couts, *sems)

        body(*ins, *outs, *scr)

        @pl.when(last)
        def _():
            comm.wait(cins, couts, *sems)

    res = pl.pallas_call(
        hosted, name=name, grid=grid, in_specs=list(in_specs) + [ANY] * nc_in,
        out_specs=out_specs + [ANY] * nc_out, out_shape=out_shape + comm.out_shapes,
        scratch_shapes=list(scratch_shapes) + comm.sem_shapes(),
        compiler_params=_cparams(*(["arbitrary"] * len(grid))))(*operands, *comm.srcs)
    return list(res[:n_out]), list(res[n_out:])


def _comm_only(comm, name):
    def body(*refs):
        n, m = len(comm.srcs), len(comm.out_shapes)
        comm.start(refs[:n], refs[n:n + m], *refs[n + m:])
        comm.wait(refs[:n], refs[n:n + m], *refs[n + m:])

    return list(pl.pallas_call(body, name=name, in_specs=[ANY] * len(comm.srcs), out_specs=[ANY] * len(comm.out_shapes),
                               out_shape=comm.out_shapes, scratch_shapes=comm.sem_shapes())(*comm.srcs))


def _allgather(items, name):
    n = len(items)

    def body(*refs):
        ins, outs = refs[:n], refs[n:2 * n]
        send_sems, recv_sems, local_sems = refs[2 * n:]
        x, y, c = _mesh_pos()
        me, sib = (x, y, c), (x, y, 1 - c)
        chips = [(1 - x, y), (x, 1 - y), (1 - x, 1 - y)]

        def copy(a, k, blk, to, from_input=False):
            region = items[a][2](outs[a], _slot(blk))
            return pltpu.make_async_remote_copy(
                src_ref=ins[a] if from_input else region, dst_ref=region,
                send_sem=send_sems.at[a * 7 + k], recv_sem=recv_sems.at[a * 7 + k],
                device_id=to, device_id_type=MESH)

        local = [pltpu.make_async_copy(ins[a], items[a][2](outs[a], _slot(me)), local_sems.at[a]) for a in range(n)]
        for cp in local:
            cp.start()
        first = []
        for a in range(n):
            first.append(copy(a, 0, me, sib, True))
            for j, chip in enumerate(chips):
                first.append(copy(a, 1 + j, me, (*chip, c), True))
        for cp in first:
            cp.start()
        passed = []
        for j, chip in enumerate(chips):
            for a in range(n):
                copy(a, 1 + j, (*chip, c), me).wait_recv()
                fwd = copy(a, 4 + j, (*chip, c), sib)
                fwd.start()
                passed.append(fwd)
        for a in range(n):
            copy(a, 0, sib, me).wait_recv()
            for j, chip in enumerate(chips):
                copy(a, 4 + j, (*chip, 1 - c), me).wait_recv()
        for cp in first + passed:
            cp.wait_send()
        for cp in local:
            cp.wait()

    return pl.pallas_call(
        body, name=name,
        in_specs=[ANY] * n, out_specs=[ANY] * n,
        out_shape=[jax.ShapeDtypeStruct(it[1], it[0].dtype) for it in items],
        scratch_shapes=[pltpu.SemaphoreType.DMA((7 * n,)), pltpu.SemaphoreType.DMA((7 * n,)),
                        pltpu.SemaphoreType.DMA((n,))],
    )(*[it[0] for it in items])


def _rms_mm(h, g, w, l, ck, name, comm=None):
    T, D = h.shape
    N = w.shape[-1]

    def body(h_ref, g_ref, w_ref, o_ref):
        xh, _ = _rms(h_ref[...])
        u = (xh * g_ref[l:l + 1, :]).astype(BF16)
        o_ref[...] = _dot(u, w_ref[...])

    return _call(
        body, name=name, grid=(N // ck, T // TT),
        in_specs=[pl.BlockSpec((TT, D), lambda j, i: (i, 0)), _full(g.shape), pl.BlockSpec((D, ck), lambda j, i: (0, j))],
        out_specs=[pl.BlockSpec((TT, ck), lambda j, i: (i, j))],
        out_shape=[jax.ShapeDtypeStruct((T, N), F32)],
        operands=(h, g, w), sem=("parallel", "parallel"), comm=comm)


def _mm_res(res, a, w, ck, relu2, name, comm=None):
    T, D = res.shape
    K = a.shape[-1]
    nk = K // ck

    def body(r_ref, a_ref, w_ref, o_ref):
        k = pl.program_id(1)

        @pl.when(k == 0)
        def _():
            o_ref[...] = r_ref[...]

        av = a_ref[...]
        if relu2:
            av = jnp.square(jnp.maximum(av, 0.0))
        o_ref[...] += _dot(av.astype(BF16), w_ref[...])

    return _call(
        body, name=name, grid=(T // TT, nk),
        in_specs=[pl.BlockSpec((TT, D), lambda i, k: (i, 0)), pl.BlockSpec((TT, ck), lambda i, k: (i, k)),
                  pl.BlockSpec((ck, D), lambda i, k: (k, 0))],
        out_specs=[pl.BlockSpec((TT, D), lambda i, k: (i, 0))],
        out_shape=[jax.ShapeDtypeStruct((T, D), F32)],
        operands=(res, a, w), sem=("parallel", "arbitrary"), comm=comm)


def _final_loss(h, gf, tgt):
    T, D = h.shape

    def body(h_ref, g_ref, t_ref, dh_ref, dg_ref, loss_ref):
        i = pl.program_id(0)

        @pl.when(i == 0)
        def _():
            dg_ref[...] = jnp.zeros_like(dg_ref)
            loss_ref[...] = jnp.zeros_like(loss_ref)

        g = g_ref[...]
        xh, r = _rms(h_ref[...])
        row = i * TT + lax.broadcasted_iota(jnp.int32, (TT, 1), 0)
        keep = (row >= N_META).astype(F32)
        diff = (xh * g - t_ref[...]) * keep
        part = jnp.sum(jnp.sum(diff * diff, axis=1, keepdims=True), axis=0, keepdims=True)
        loss_ref[...] += (0.5 / D) * part
        dy = diff * (1.0 / D)
        dg_ref[...] += _colsum(dy * xh)
        dh_ref[...] = _rms_bwd(dy, xh, r, g)

    return pl.pallas_call(
        body, name="final_loss", grid=(T // TT,),
        in_specs=[pl.BlockSpec((TT, D), lambda i: (i, 0)), _full(gf.shape), pl.BlockSpec((TT, D), lambda i: (i, 0))],
        out_specs=[pl.BlockSpec((TT, D), lambda i: (i, 0)), _full((1, D)), _full((8, 128))],
        out_shape=[jax.ShapeDtypeStruct((T, D), F32), jax.ShapeDtypeStruct((1, D), F32),
                   jax.ShapeDtypeStruct((8, 128), F32)],
        compiler_params=_cparams("arbitrary"))(h, gf, tgt)


def _bwd_dz(dh, wdown, z, ck, name, comm=None):
    T, D = dh.shape
    F = z.shape[-1]

    def body(d_ref, w_ref, z_ref, o_ref):
        dact = _dot_nt(d_ref[...].astype(BF16), w_ref[...])
        o_ref[...] = (dact * (2.0 * jnp.maximum(z_ref[...], 0.0))).astype(BF16)

    return _call(
        body, name=name, grid=(F // ck, T // TT),
        in_specs=[pl.BlockSpec((TT, D), lambda j, i: (i, 0)), pl.BlockSpec((ck, D), lambda j, i: (j, 0)),
                  pl.BlockSpec((TT, ck), lambda j, i: (i, j))],
        out_specs=[pl.BlockSpec((TT, ck), lambda j, i: (i, j))],
        out_shape=[jax.ShapeDtypeStruct((T, F), BF16)],
        operands=(dh, wdown, z), sem=("parallel", "parallel"), comm=comm)


def _bwd_plain(dh, w, name, comm=None):
    T, D = dh.shape
    K = w.shape[0]

    def body(d_ref, w_ref, o_ref):
        o_ref[...] = _dot_nt(d_ref[...].astype(BF16), w_ref[...])

    return _call(
        body, name=name, grid=(T // TT,),
        in_specs=[pl.BlockSpec((TT, D), lambda i: (i, 0)), _full((K, D))],
        out_specs=[pl.BlockSpec((TT, K), lambda i: (i, 0))],
        out_shape=[jax.ShapeDtypeStruct((T, K), F32)],
        operands=(dh, w), sem=("parallel",), comm=comm)


def _bwd_rms(dx, w, h, g, dh_in, l, ck, name, comm=None):
    T, D = h.shape
    K = dx.shape[-1]
    nk = K // ck

    def body(x_ref, w_ref, h_ref, g_ref, di_ref, dh_ref, dg_ref, acc_ref):
        i, k = pl.program_id(0), pl.program_id(1)

        @pl.when((i == 0) & (k == 0))
        def _():
            dg_ref[...] = jnp.zeros_like(dg_ref)

        @pl.when(k == 0)
        def _():
            acc_ref[...] = jnp.zeros_like(acc_ref)

        acc_ref[...] += _dot_nt(x_ref[...], w_ref[...])

        @pl.when(k == nk - 1)
        def _():
            du = acc_ref[...]
            xh, r = _rms(h_ref[...])
            dg_ref[...] += _colsum(du * xh)
            dh_ref[...] = di_ref[...] + _rms_bwd(du, xh, r, g_ref[l:l + 1, :])

    return _call(
        body, name=name, grid=(T // TT, nk),
        in_specs=[pl.BlockSpec((TT, ck), lambda i, k: (i, k)), pl.BlockSpec((D, ck), lambda i, k: (0, k)),
                  pl.BlockSpec((TT, D), lambda i, k: (i, 0)), _full(g.shape),
                  pl.BlockSpec((TT, D), lambda i, k: (i, 0))],
        out_specs=[pl.BlockSpec((TT, D), lambda i, k: (i, 0)), _full((1, D))],
        out_shape=[jax.ShapeDtypeStruct((T, D), F32), jax.ShapeDtypeStruct((1, D), F32)],
        scratch_shapes=[pltpu.VMEM((TT, D), F32)],
        operands=(dx, w, h, g, dh_in), sem=("arbitrary", "arbitrary"), comm=comm)


def _mm_tn(a, b, a_mode, g, l, chunk, ck, out_dtype, name, comm=None):
    T = a.shape[0]
    Ka, Nb = a.shape[1], b.shape[1]
    ca, cb = (ck, Nb) if chunk == "a" else (Ka, ck)
    nj = (Ka // ck) if chunk == "a" else (Nb // ck)
    nt = T // TT

    def body(*refs):
        if a_mode == "rms":
            a_ref, g_ref, b_ref, o_ref, acc_ref = refs
        else:
            a_ref, b_ref, o_ref, acc_ref = refs
        t = pl.program_id(1)

        @pl.when(t == 0)
        def _():
            acc_ref[...] = jnp.zeros_like(acc_ref)

        av = a_ref[...]
        if a_mode == "relu2":
            av = jnp.square(jnp.maximum(av, 0.0))
        elif a_mode == "rms":
            xh, _ = _rms(av)
            av = xh * g_ref[l:l + 1, :]
        acc_ref[...] += _dot_tn(av.astype(BF16), b_ref[...].astype(BF16))

        @pl.when(t == nt - 1)
        def _():
            o_ref[...] = acc_ref[...].astype(out_dtype)

    if chunk == "a":
        a_spec = pl.BlockSpec((TT, ca), lambda j, t: (t, j))
        b_spec = pl.BlockSpec((TT, cb), lambda j, t: (t, 0))
        o_spec = pl.BlockSpec((ca, cb), lambda j, t: (j, 0))
    else:
        a_spec = pl.BlockSpec((TT, ca), lambda j, t: (t, 0))
        b_spec = pl.BlockSpec((TT, cb), lambda j, t: (t, j))
        o_spec = pl.BlockSpec((ca, cb), lambda j, t: (0, j))
    ins, specs = [a], [a_spec]
    if a_mode == "rms":
        ins.append(g)
        specs.append(_full(g.shape))
    ins.append(b)
    specs.append(b_spec)
    return _call(
        body, name=name, grid=(nj, nt), in_specs=specs, out_specs=[o_spec],
        out_shape=[jax.ShapeDtypeStruct((Ka, Nb), out_dtype)], scratch_shapes=[pltpu.VMEM((ca, cb), F32)],
        operands=tuple(ins), sem=("parallel", "arbitrary"), comm=comm)


def _shift_down(x, k):
    return x if k == 0 else pltpu.roll(x, k, 0)


def _shift_up(x, k):
    return x if k == 0 else pltpu.roll(x, x.shape[0] - k, 0)


def _pool_sel(lane_grp, vals):
    return jnp.where(lane_grp == 0, vals[0], jnp.where(lane_grp == 1, vals[1], jnp.where(lane_grp == 2, vals[2], vals[3])))


def _pool_fwd(ue, pos):
    s = ue
    sums = []
    for k in (1, 2, 4, 8):
        s = s + _shift_down(s, k)
        sums.append(s[HALO:])
    grp = lax.broadcasted_iota(jnp.int32, (1, D_POOL), 1) // POOL_GW
    wsum = _pool_sel(grp, sums)
    width = _pool_sel(grp, [jnp.float32(w) for w in POOL_WINDOWS])
    cnt = jnp.minimum(pos.astype(F32), width)
    return wsum / cnt - ue[HALO:], cnt, grp


def _conv_fwd(xe, w_ref, nk):
    acc = None
    for k in range(nk):
        term = _shift_down(xe, nk - 1 - k)[HALO:] * w_ref(k)
        acc = term if acc is None else acc + term
    return acc


def _layernorm(cb):
    mu = jnp.mean(cb, axis=-1, keepdims=True)
    xc = cb - mu
    rstd = lax.rsqrt(jnp.mean(xc * xc, axis=-1, keepdims=True) + EPS)
    return xc * rstd, rstd


def _softplus_neg(lam):
    x = -lam
    e = jnp.exp(-jnp.abs(x))
    u = 1.0 + e
    d = u - 1.0
    log1p = jnp.where(d == 0.0, e, jnp.log(u) * (e / jnp.where(d == 0.0, 1.0, d)))
    return jnp.maximum(x, 0.0) + log1p


def _expm1(x):
    small = x * (1.0 + x * 0.5 * (1.0 + x * (1.0 / 3.0) * (1.0 + x * 0.25)))
    return jnp.where(jnp.abs(x) < 1e-2, small, jnp.exp(x) - 1.0)


def _gelu_parts(x):
    c0 = 0.7978845608028654
    inner = c0 * (x + 0.044715 * x * x * x)
    th = jnp.tanh(inner)
    gelu = 0.5 * x * (1.0 + th)
    dgelu = 0.5 * (1.0 + th) + 0.5 * x * (1.0 - th * th) * c0 * (1.0 + 3.0 * 0.044715 * x * x)
    return gelu, dgelu


def _rg_gates(xc, wa_ref, wx_ref, v512_ref):
    xb = xc.astype(BF16)
    ra = _sigmoid(_dot(xb, wa_ref[...]) + v512_ref[1:2, :])
    ri = _sigmoid(_dot(xb, wx_ref[...]) + v512_ref[2:3, :])
    sp = _softplus_neg(v512_ref[3:4, :])
    log_a = (-RG_C) * ra * sp
    a = jnp.exp(log_a)
    mult = jnp.sqrt(-_expm1(2.0 * log_a))
    return ra, ri, sp, a, mult


def _scan_down(a, b):
    n = a.shape[0]
    row = lax.broadcasted_iota(jnp.int32, (n, 1), 0)
    k = 1
    while k < n:
        ok = row >= k
        b = a * jnp.where(ok, _shift_down(b, k), 0.0) + b
        a = a * jnp.where(ok, _shift_down(a, k), 1.0)
        k *= 2
    return b, a


def _scan_up(c, g):
    n = c.shape[0]
    row = lax.broadcasted_iota(jnp.int32, (n, 1), 0)
    k = 1
    while k < n:
        ok = row < n - k
        g = c * jnp.where(ok, _shift_up(g, k), 0.0) + g
        c = c * jnp.where(ok, _shift_up(c, k), 1.0)
        k *= 2
    return g, c


def _mixer_specs(l, tile_of):
    r = TT // HALO
    return [
        pl.BlockSpec((TT, 1792), lambda i: (tile_of(i), 0)),
        pl.BlockSpec((HALO, 1792), lambda i: (jnp.maximum(tile_of(i) * r - 1, 0), 0)),
        _layer((256, 256), l), _layer((8, 256), l), _layer((32, 256), l), _layer((256, 256), l),
        _layer((512, 512), l), _layer((512, 512), l), _layer((8, 512), l),
    ]


def _mixer_fwd(p, wts, l, comm=None):
    T = p.shape[0]

    def body(p_ref, ph_ref, wpool_ref, v256_ref, dww_ref, wpw_ref, wa_ref, wx_ref, v512_ref, y_ref, hr_ref, carry_ref):
        i = pl.program_id(0)

        @pl.when(i == 0)
        def _():
            carry_ref[...] = jnp.zeros_like(carry_ref)

        halo = jnp.where(i == 0, 0.0, ph_ref[...])
        e = jnp.concatenate([halo, p_ref[...]], axis=0)
        pos = i * TT + 1 + lax.broadcasted_iota(jnp.int32, (TT, 1), 0)
        pooled, _, _ = _pool_fwd(e[:, 0:256], pos)
        y_a = _dot(pooled.astype(BF16), wpool_ref[...]) * v256_ref[0:1, :]
        ub = e[:, 256:512] * _sigmoid(e[:, 512:768])
        cb = _conv_fwd(ub, lambda k: dww_ref[k:k + 1, :], CONV_K) + v256_ref[1:2, :]
        xhat, _ = _layernorm(cb)
        ln = xhat * v256_ref[2:3, :] + v256_ref[3:4, :]
        s = ln * _sigmoid(ln)
        y_b = _dot(s.astype(BF16), wpw_ref[...])
        xc = _conv_fwd(e[:, 1280:1792], lambda k: v512_ref[4 + k:5 + k, :], RG_CONV_K) + v512_ref[0:1, :]
        _, ri, _, a, mult = _rg_gates(xc, wa_ref, wx_ref, v512_ref)
        hloc, acum = _scan_down(a, mult * ri * xc)
        hr = hloc + acum * carry_ref[0:1, :]
        hr_ref[...] = hr
        carry_ref[0:1, :] = hr_ref[TT - 1:TT, :]
        gelu, _ = _gelu_parts(p_ref[:, 768:1280])
        y_ref[:, 0:256] = y_a.astype(BF16)
        y_ref[:, 256:512] = y_b.astype(BF16)
        y_ref[:, 512:1024] = (gelu * hr).astype(BF16)

    return _call(
        body, name=f"mixer_fwd{l}", grid=(T // TT,),
        in_specs=_mixer_specs(l, lambda i: i),
        out_specs=[pl.BlockSpec((TT, 1024), lambda i: (i, 0)), pl.BlockSpec((TT, 512), lambda i: (i, 0))],
        out_shape=[jax.ShapeDtypeStruct((T, 1024), BF16), jax.ShapeDtypeStruct((T, 512), F32)],
        scratch_shapes=[pltpu.VMEM((8, 512), F32)],
        operands=(p, p, *wts), sem=("arbitrary",), comm=comm)


def _mixer_bwd(p, hr, dy, wts, l, comm=None):
    T = p.shape[0]
    nt = T // TT
    tile_of = lambda i: nt - 1 - i

    def body(p_ref, ph_ref, wpool_ref, v256_ref, dww_ref, wpw_ref, wa_ref, wx_ref, v512_ref, hr_ref, hrh_ref, dy_ref,
             dp_ref, g256_ref, g512_ref, gpool_ref, gpw_ref, gwa_ref, gwx_ref,
             q_c, dcb_c, dxc_c, ag_c):
        i = pl.program_id(0)
        j = nt - 1 - i

        @pl.when(i == 0)
        def _():
            for ref in (g256_ref, g512_ref, gpool_ref, gpw_ref, gwa_ref, gwx_ref, q_c, dcb_c, dxc_c, ag_c):
                ref[...] = jnp.zeros_like(ref)

        halo = jnp.where(j == 0, 0.0, ph_ref[...])
        e = jnp.concatenate([halo, p_ref[...]], axis=0)
        pos = j * TT + 1 + lax.broadcasted_iota(jnp.int32, (TT, 1), 0)
        row = lax.broadcasted_iota(jnp.int32, (TT, 1), 0)

        dy_a = dy_ref[:, 0:256]
        pooled, cnt, grp = _pool_fwd(e[:, 0:256], pos)
        scale = v256_ref[0:1, :]
        mixed = _dot(pooled.astype(BF16), wpool_ref[...])
        g256_ref[34:35, :] += _colsum(dy_a * mixed)
        dmixed = (dy_a * scale).astype(BF16)
        gpool_ref[...] += _dot_tn(pooled.astype(BF16), dmixed)
        dpooled = _dot_nt(dmixed, wpool_ref[...])
        q = dpooled / cnt
        s = jnp.concatenate([q, q_c[...]], axis=0)
        sums = []
        for k in (1, 2, 4, 8):
            s = s + _shift_up(s, k)
            sums.append(s[:TT])
        dp_ref[:, 0:256] = (_pool_sel(grp, sums) - dpooled).astype(BF16)
        q_c[...] = q[0:16]

        dy_b = dy_ref[:, 256:512].astype(BF16)
        sg = _sigmoid(e[:, 512:768])
        ub = e[:, 256:512] * sg
        cb = _conv_fwd(ub, lambda k: dww_ref[k:k + 1, :], CONV_K) + v256_ref[1:2, :]
        xhat, rstd = _layernorm(cb)
        ln_g = v256_ref[2:3, :]
        ln = xhat * ln_g + v256_ref[3:4, :]
        sig = _sigmoid(ln)
        sact = ln * sig
        gpw_ref[...] += _dot_tn(sact.astype(BF16), dy_b)
        dln = _dot_nt(dy_b, wpw_ref[...]) * (sig * (1.0 + ln * (1.0 - sig)))
        g256_ref[32:33, :] += _colsum(dln * xhat)
        g256_ref[33:34, :] += _colsum(dln)
        dxhat = dln * ln_g
        dcb = rstd * (dxhat - jnp.mean(dxhat, axis=-1, keepdims=True)
                      - xhat * jnp.mean(dxhat * xhat, axis=-1, keepdims=True))
        g256_ref[31:32, :] += _colsum(dcb)
        for k in range(CONV_K):
            g256_ref[k:k + 1, :] += _colsum(dcb * _shift_down(ub, CONV_K - 1 - k)[HALO:])
        ed = jnp.concatenate([dcb, dcb_c[...]], axis=0)
        dub = None
        for k in range(CONV_K):
            term = _shift_up(ed, CONV_K - 1 - k)[:TT] * dww_ref[k:k + 1, :]
            dub = term if dub is None else dub + term
        dcb_c[...] = dcb[0:32]
        sg_t = sg[HALO:]
        v_t = p_ref[:, 256:512]
        dp_ref[:, 256:512] = (dub * sg_t).astype(BF16)
        dp_ref[:, 512:768] = (dub * v_t * sg_t * (1.0 - sg_t)).astype(BF16)

        dy_c = dy_ref[:, 512:1024]
        cxe = e[:, 1280:1792]
        xc = _conv_fwd(cxe, lambda k: v512_ref[4 + k:5 + k, :], RG_CONV_K) + v512_ref[0:1, :]
        ra, ri, sp, a, mult = _rg_gates(xc, wa_ref, wx_ref, v512_ref)
        hrv = hr_ref[...]
        gelu, dgelu = _gelu_parts(p_ref[:, 768:1280])
        dp_ref[:, 768:1280] = (dy_c * hrv * dgelu).astype(BF16)
        dhr = dy_c * gelu
        coef = jnp.where(row < TT - 1, _shift_up(a, 1), 1.0)
        gloc, ccum = _scan_up(coef, dhr)
        gg = gloc + ccum * ag_c[0:1, :]
        ag_c[...] = (a * gg)[0:8]
        hr_before = jnp.where(j == 0, 0.0, hrh_ref[7:8, :])
        hr_prev = jnp.where(row >= 1, _shift_down(hrv, 1), hr_before)
        da = gg * hr_prev
        dmult = gg * ri * xc
        dri = gg * mult * xc
        dxc = gg * mult * ri
        dlog_a = da * a - dmult * (a * a / mult)
        g512_ref[3:4, :] += _colsum(dlog_a * ((-RG_C) * ra))
        dpre_a = dlog_a * ((-RG_C) * sp) * ra * (1.0 - ra)
        dpre_x = dri * ri * (1.0 - ri)
        g512_ref[1:2, :] += _colsum(dpre_a)
        g512_ref[2:3, :] += _colsum(dpre_x)
        xb = xc.astype(BF16)
        dpa_b, dpx_b = dpre_a.astype(BF16), dpre_x.astype(BF16)
        gwa_ref[...] += _dot_tn(xb, dpa_b)
        gwx_ref[...] += _dot_tn(xb, dpx_b)
        dxc = dxc + _dot_nt(dpa_b, wa_ref[...]) + _dot_nt(dpx_b, wx_ref[...])
        g512_ref[0:1, :] += _colsum(dxc)
        for k in range(RG_CONV_K):
            g512_ref[4 + k:5 + k, :] += _colsum(dxc * _shift_down(cxe, RG_CONV_K - 1 - k)[HALO:])
        ex = jnp.concatenate([dxc, dxc_c[...]], axis=0)
        dcx = None
        for k in range(RG_CONV_K):
            term = _shift_up(ex, RG_CONV_K - 1 - k)[:TT] * v512_ref[4 + k:5 + k, :]
            dcx = term if dcx is None else dcx + term
        dxc_c[...] = dxc[0:8]
        dp_ref[:, 1280:1792] = dcx.astype(BF16)

        @pl.when(i == nt - 1)
        def _():
            lam = v512_ref[3:4, :]
            g512_ref[3:4, :] = g512_ref[3:4, :] * (-_sigmoid(-lam))

    in_specs = _mixer_specs(l, tile_of) + [
        pl.BlockSpec((TT, 512), lambda i: (tile_of(i), 0)),
        pl.BlockSpec((8, 512), lambda i: (jnp.maximum(tile_of(i) * (TT // 8) - 1, 0), 0)),
        pl.BlockSpec((TT, 1024), lambda i: (tile_of(i), 0)),
    ]
    acc_shapes = [(40, 256), (8, 512), (256, 256), (256, 256), (512, 512), (512, 512)]
    return _call(
        body, name=f"mixer_bwd{l}", grid=(nt,),
        in_specs=in_specs,
        out_specs=[pl.BlockSpec((TT, 1792), lambda i: (tile_of(i), 0))] + [_full(s) for s in acc_shapes],
        out_shape=[jax.ShapeDtypeStruct((T, 1792), BF16)] + [jax.ShapeDtypeStruct(s, F32) for s in acc_shapes],
        scratch_shapes=[pltpu.VMEM((16, 256), F32), pltpu.VMEM((32, 256), F32), pltpu.VMEM((8, 512), F32),
                        pltpu.VMEM((8, 512), F32)],
        operands=(p, p, *wts, hr, hr, dy), sem=("arbitrary",), comm=comm)


def _adam(w, g, m, v):
    m = ADAM_B1 * m + (1.0 - ADAM_B1) * g
    v = ADAM_B2 * v + (1.0 - ADAM_B2) * (g * g)
    m_hat = m / (1.0 - ADAM_B1 ** ADAM_STEP)
    v_hat = v / (1.0 - ADAM_B2 ** ADAM_STEP)
    delta = -ADAM_LR * (m_hat / (jnp.sqrt(v_hat) + ADAM_EPS) + ADAM_WD * w)
    return delta, m, v


def _adamw_big(recvs, w, m, v, tr, name):
    L, R, C = w.shape

    def body(r0_ref, r1_ref, w_ref, m_ref, v_ref, g_ref, d_ref, mo_ref, vo_ref):
        l = pl.program_id(0)

        def total(r_ref):
            g = r_ref[0].astype(F32)
            for s in range(1, N_DEV):
                g = g + r_ref[s].astype(F32)
            g_ref[...] = g

        @pl.when(l == 0)
        def _():
            total(r0_ref)

        @pl.when(l == 1)
        def _():
            total(r1_ref)

        d_ref[...], mo_ref[...], vo_ref[...] = _adam(w_ref[...], g_ref[...], m_ref[...], v_ref[...])

    blk = pl.BlockSpec((None, tr, C), lambda l, i: (l, i, 0))
    return pl.pallas_call(
        body, name=name, grid=(L, R // tr),
        in_specs=[pl.BlockSpec((N_DEV, tr, C), lambda l, i: (0, i * (1 - l), 0)),
                  pl.BlockSpec((N_DEV, tr, C), lambda l, i: (0, i * l, 0)), blk, blk, blk],
        out_specs=[blk] * 4,
        out_shape=[jax.ShapeDtypeStruct((L, R, C), F32)] * 4,
        compiler_params=_cparams("arbitrary", "arbitrary"))(recvs[0], recvs[1], w, m, v)


def _adamw_plain(gs, ws, ms, vs, name):
    n = len(gs)

    def body(*refs):
        g_r, w_r, m_r, v_r = refs[:n], refs[n:2 * n], refs[2 * n:3 * n], refs[3 * n:4 * n]
        outs = refs[4 * n:]
        for a in range(n):
            d, mo, vo = _adam(w_r[a][...], g_r[a][...], m_r[a][...], v_r[a][...])
            outs[3 * a][...] = d
            outs[3 * a + 1][...] = mo
            outs[3 * a + 2][...] = vo

    shapes = []
    for a in range(n):
        shapes += [jax.ShapeDtypeStruct(ws[a].shape, F32)] * 3
    return pl.pallas_call(body, name=name, out_shape=shapes)(*gs, *ws, *ms, *vs)


_V256_ROWS = {"convb_dw_b": 31, "convb_ln_g": 32, "convb_ln_b": 33, "pool_scale": 34}
_V512_ROWS = {"rg_conv_b": 0, "rg_b_a": 1, "rg_b_x": 2, "rg_lambda": 3}
_SMALL_REPL = ["mix_norm_g", "mlp_norm_g", "final_norm_g", "pool_w", "pool_scale", "convb_dw_b", "convb_ln_g",
               "convb_ln_b", "rg_conv_b", "rg_w_a", "rg_b_a", "rg_w_x", "rg_b_x", "rg_lambda"]


def _adamw_small(gath, ws, ms, vs):
    n = len(_SMALL_REPL)
    sum_shapes = [g.shape[1:] for g in gath]

    def body(*refs):
        g_in = refs[:10]
        w_r, m_r, v_r = refs[10:10 + n], refs[10 + n:10 + 2 * n], refs[10 + 2 * n:10 + 3 * n]
        sums = refs[10 + 3 * n:20 + 3 * n]
        outs = refs[20 + 3 * n:]
        for a in range(10):
            acc = g_in[a][0]
            for s in range(1, N_DEV):
                acc = acc + g_in[a][s]
            sums[a][...] = acc
        s_mix, s_mlp, s_fin, s_pool, s_256, s_512, _, s_wa, s_wx, _ = sums

        def grad_of(name, idx):
            if name == "mix_norm_g":
                return s_mix[idx[0]:idx[0] + 1, :]
            if name == "mlp_norm_g":
                return s_mlp[idx[0]:idx[0] + 1, :]
            if name == "final_norm_g":
                return s_fin[...]
            if name in _V256_ROWS:
                r = _V256_ROWS[name]
                return s_256[idx[0], r:r + 1, :]
            if name in _V512_ROWS:
                r = _V512_ROWS[name]
                return s_512[idx[0], r:r + 1, :]
            src = {"pool_w": s_pool, "rg_w_a": s_wa, "rg_w_x": s_wx}[name]
            return src[idx[0], idx[1] * 64:(idx[1] + 1) * 64, :]

        for a, name in enumerate(_SMALL_REPL):
            shape = w_r[a].shape
            if name == "final_norm_g":
                parts = [((), (slice(None), slice(None)))]
            elif len(shape) == 2:
                parts = [((l,), (slice(l, l + 1), slice(None))) for l in range(shape[0])]
            else:
                parts = [((l, h), (l, h)) for l in range(shape[0]) for h in range(shape[1])]
            for idx, sel in parts:
                g = grad_of(name, idx)
                d, mo, vo = _adam(w_r[a][sel], g, m_r[a][sel], v_r[a][sel])
                outs[4 * a][sel] = g
                outs[4 * a + 1][sel] = d
                outs[4 * a + 2][sel] = mo
                outs[4 * a + 3][sel] = vo

    out_shape = [jax.ShapeDtypeStruct(s, F32) for s in sum_shapes]
    for a in range(n):
        out_shape += [jax.ShapeDtypeStruct(ws[a].shape, F32)] * 4
    res = pl.pallas_call(body, name="adamw_small", out_shape=out_shape,
                         compiler_params=pltpu.CompilerParams(vmem_limit_bytes=VMEM_LIMIT))(*gath, *ws, *ms, *vs)
    return res[:10], res[10:]


_WEIGHTS = ['meta_tokens', 'mix_norm_g', 'w_in', 'pool_w', 'pool_scale', 'convb_dw_w', 'convb_dw_b', 'convb_ln_g',
            'convb_ln_b', 'convb_pw_w', 'rg_conv_w', 'rg_conv_b', 'rg_w_a', 'rg_b_a', 'rg_w_x', 'rg_b_x', 'rg_lambda',
            'w_out', 'mlp_norm_g', 'w_up', 'w_down', 'final_norm_g']


def _block_diag(w):
    L, H, C, _ = w.shape
    eye = jnp.eye(H, dtype=w.dtype)
    return (w[:, :, :, None, :] * eye[None, :, None, :, None]).reshape(L, H * C, H * C)


def _diag_blocks(m, H):
    C = m.shape[0] // H
    return jnp.concatenate([m[h * C:(h + 1) * C, h * C:(h + 1) * C] for h in range(H)], axis=0)


def kernel(x, meta_tokens, mix_norm_g, w_in, pool_w, pool_scale, convb_dw_w, convb_dw_b, convb_ln_g, convb_ln_b, convb_pw_w, rg_conv_w, rg_conv_b, rg_w_a, rg_b_a, rg_w_x, rg_b_x, rg_lambda, w_out, mlp_norm_g, w_up, w_down, final_norm_g, loss_target, m_meta_tokens, m_mix_norm_g, m_w_in, m_pool_w, m_pool_scale, m_convb_dw_w, m_convb_dw_b, m_convb_ln_g, m_convb_ln_b, m_convb_pw_w, m_rg_conv_w, m_rg_conv_b, m_rg_w_a, m_rg_b_a, m_rg_w_x, m_rg_b_x, m_rg_lambda, m_w_out, m_mlp_norm_g, m_w_up, m_w_down, m_final_norm_g, v_meta_tokens, v_mix_norm_g, v_w_in, v_pool_w, v_pool_scale, v_convb_dw_w, v_convb_dw_b, v_convb_ln_g, v_convb_ln_b, v_convb_pw_w, v_rg_conv_w, v_rg_conv_b, v_rg_w_a, v_rg_b_a, v_rg_w_x, v_rg_b_x, v_rg_lambda, v_w_out, v_mlp_norm_g, v_w_up, v_w_down, v_final_norm_g):
    W = dict(meta_tokens=meta_tokens, mix_norm_g=mix_norm_g, w_in=w_in, pool_w=pool_w, pool_scale=pool_scale,
             convb_dw_w=convb_dw_w, convb_dw_b=convb_dw_b, convb_ln_g=convb_ln_g, convb_ln_b=convb_ln_b,
             convb_pw_w=convb_pw_w, rg_conv_w=rg_conv_w, rg_conv_b=rg_conv_b, rg_w_a=rg_w_a, rg_b_a=rg_b_a,
             rg_w_x=rg_w_x, rg_b_x=rg_b_x, rg_lambda=rg_lambda, w_out=w_out, mlp_norm_g=mlp_norm_g, w_up=w_up,
             w_down=w_down, final_norm_g=final_norm_g.reshape(1, -1))
    M = dict(meta_tokens=m_meta_tokens, mix_norm_g=m_mix_norm_g, w_in=m_w_in, pool_w=m_pool_w, pool_scale=m_pool_scale,
             convb_dw_w=m_convb_dw_w, convb_dw_b=m_convb_dw_b, convb_ln_g=m_convb_ln_g, convb_ln_b=m_convb_ln_b,
             convb_pw_w=m_convb_pw_w, rg_conv_w=m_rg_conv_w, rg_conv_b=m_rg_conv_b, rg_w_a=m_rg_w_a, rg_b_a=m_rg_b_a,
             rg_w_x=m_rg_w_x, rg_b_x=m_rg_b_x, rg_lambda=m_rg_lambda, w_out=m_w_out, mlp_norm_g=m_mlp_norm_g,
             w_up=m_w_up, w_down=m_w_down, final_norm_g=m_final_norm_g.reshape(1, -1))
    V = dict(meta_tokens=v_meta_tokens, mix_norm_g=v_mix_norm_g, w_in=v_w_in, pool_w=v_pool_w, pool_scale=v_pool_scale,
             convb_dw_w=v_convb_dw_w, convb_dw_b=v_convb_dw_b, convb_ln_g=v_convb_ln_g, convb_ln_b=v_convb_ln_b,
             convb_pw_w=v_convb_pw_w, rg_conv_w=v_rg_conv_w, rg_conv_b=v_rg_conv_b, rg_w_a=v_rg_w_a, rg_b_a=v_rg_b_a,
             rg_w_x=v_rg_w_x, rg_b_x=v_rg_b_x, rg_lambda=v_rg_lambda, w_out=v_w_out, mlp_norm_g=v_mlp_norm_g,
             w_up=v_w_up, w_down=v_w_down, final_norm_g=v_final_norm_g.reshape(1, -1))

    xs = x[0]
    S, D = xs.shape
    T = S + N_META
    assert T % TT == 0 and D == 1024
    L = DEPTH
    me = 4 * lax.axis_index("x") + 2 * lax.axis_index("y") + lax.axis_index("c")
    c_in, c_ff = w_in.shape[-1], w_up.shape[-1]
    d_in, d_ff = c_in * N_DEV, c_ff * N_DEV
    r_out, r_dn = w_out.shape[1], w_down.shape[1]
    by_slot = lambda ref, s: ref.at[s]
    rows_of = lambda n: (lambda ref, s: ref.at[pl.ds(s * n, n), :])
    cols_of = lambda n: (lambda ref, s: ref.at[:, pl.ds(s * n, n)])

    gathered = _allgather([
        (w_in.astype(BF16), (N_DEV,) + w_in.shape, by_slot),
        (meta_tokens, (N_DEV,) + meta_tokens.shape, by_slot),
        (convb_dw_w, (N_DEV,) + convb_dw_w.shape, by_slot),
        (convb_pw_w, (N_DEV,) + convb_pw_w.shape, by_slot),
        (rg_conv_w, (N_DEV,) + rg_conv_w.shape, by_slot),
    ], "gather_first")
    win_f = [jnp.transpose(gathered[0][:, l], (1, 0, 2)).reshape(D, d_in) for l in range(L)]
    meta_f = jnp.transpose(gathered[1], (1, 0, 2)).reshape(N_META, D)
    dww_f = jnp.transpose(gathered[2], (1, 2, 0, 3)).reshape(L, CONV_K, D_CONV)
    wpw_f = jnp.transpose(gathered[3], (1, 0, 2, 3)).reshape(L, D_CONV, D_CONV).astype(BF16)
    rgw_f = jnp.transpose(gathered[4], (1, 2, 0, 3)).reshape(L, RG_CONV_K, D_RNN)
    wout_b, wup_b, wdown_b = w_out.astype(BF16), w_up.astype(BF16), w_down.astype(BF16)

    def weight_gather(blocks):
        comm = _Comm()
        for blk, shape, place in blocks:
            comm.gather(blk, shape, place)
        return comm

    wpool_bd = _block_diag(pool_w).astype(BF16)
    wa_bd = _block_diag(rg_w_a).astype(BF16)
    wx_bd = _block_diag(rg_w_x).astype(BF16)
    zeros4 = jnp.zeros((L, 4, D_POOL), F32)
    v256 = jnp.concatenate([pool_scale[:, None], convb_dw_b[:, None], convb_ln_g[:, None], convb_ln_b[:, None], zeros4], axis=1)
    dww_p = jnp.concatenate([dww_f, jnp.zeros((L, 1, D_CONV), F32)], axis=1)
    v512 = jnp.concatenate([rg_conv_b[:, None], rg_b_a[:, None], rg_b_x[:, None], rg_lambda[:, None], rgw_f], axis=1)
    mix_w = (wpool_bd, v256, dww_p, wpw_f, wa_bd, wx_bd, v512)

    h = jnp.concatenate([meta_f, xs], axis=0)
    tgt = jnp.concatenate([jnp.zeros((N_META, D), F32), loss_target[0]], axis=0)
    wout_f, wup_f, wdown_f = [None] * L, [None] * L, [None] * L
    saved = []

    (p,), wout_f = _rms_mm(h, mix_norm_g, win_f[0], 0, d_in, "in_proj0",
                           weight_gather([(wout_b[l], (r_out * N_DEV, D), rows_of(r_out)) for l in range(L)]))
    (y, hr), (wup_f[0],) = _mixer_fwd(p, mix_w, 0, weight_gather([(wup_b[0], (D, d_ff), cols_of(c_ff))]))
    (h2,), _ = _mm_res(h, y, wout_f[0], y.shape[-1], False, "out_proj0")
    (z,), (wdown_f[0],) = _rms_mm(h2, mlp_norm_g, wup_f[0], 0, 2048, "mlp_up0",
                                  weight_gather([(wdown_b[0], (d_ff, D), rows_of(r_dn))]))
    (h3,), (wup_f[1],) = _mm_res(h2, z, wdown_f[0], 2048, True, "mlp_down0",
                                 weight_gather([(wup_b[1], (D, d_ff), cols_of(c_ff))]))
    saved.append((h, p, y, hr, h2, z))
    h = h3
    (p,), _ = _rms_mm(h, mix_norm_g, win_f[1], 1, d_in, "in_proj1")
    (y, hr), (wdown_f[1],) = _mixer_fwd(p, mix_w, 1, weight_gather([(wdown_b[1], (d_ff, D), rows_of(r_dn))]))
    (h2,), _ = _mm_res(h, y, wout_f[1], y.shape[-1], False, "out_proj1")
    (z,), _ = _rms_mm(h2, mlp_norm_g, wup_f[1], 1, 2048, "mlp_up1")
    (h3,), _ = _mm_res(h2, z, wdown_f[1], 2048, True, "mlp_down1")
    saved.append((h, p, y, hr, h2, z))

    dh, g_fin, loss_part = _final_loss(h3, W["final_norm_g"], tgt)
    loss = lax.psum(loss_part[0, 0], ("x", "y", "c"))

    def grad_exchange(pieces):
        comm = _Comm()
        for g, send, shape in pieces:
            comm.add(g, send, comm.add_out((N_DEV,) + shape, BF16), by_slot)
        return comm

    recv = {n: [None] * L for n in ("w_in", "w_out", "w_up", "w_down")}
    g_small = [None] * L
    pending_win = None
    for l in reversed(range(L)):
        h0, p, y, hr, h2, z = saved[l]
        comm = grad_exchange([(pending_win, by_slot, (D, c_in))]) if pending_win is not None else None
        (dz,), got = _bwd_dz(dh, wdown_f[l], z, 2048, f"mlp_down_bwd{l}", comm)
        if got:
            recv["w_in"][l + 1] = got[0]
        (g_wdown,), _ = _mm_tn(z, dh, "relu2", None, l, "a", 1024, BF16, f"mlp_down_wgrad{l}")
        (dh2, g_mlp), (recv["w_down"][l],) = _bwd_rms(
            dz, wup_f[l], h2, mlp_norm_g, dh, l, 2048, f"mlp_up_bwd{l}",
            grad_exchange([(g_wdown, rows_of(r_dn), (r_dn, D))]))
        (g_wup,), _ = _mm_tn(h2, dz, "rms", mlp_norm_g, l, "b", 1024, BF16, f"mlp_up_wgrad{l}")
        (dy,), _ = _bwd_plain(dh2, wout_f[l], f"out_proj_bwd{l}")
        (g_wout,), _ = _mm_tn(y, dh2, "plain", None, l, "b", D, BF16, f"out_proj_wgrad{l}")
        (dp, g256, g512, gpool, gpw, gwa, gwx), (recv["w_up"][l],) = _mixer_bwd(
            p, hr, dy, mix_w, l, grad_exchange([(g_wup, cols_of(c_ff), (D, c_ff))]))
        (g_win,), (recv["w_out"][l],) = _mm_tn(
            h0, dp, "rms", mix_norm_g, l, "b", d_in // 2, F32, f"in_proj_wgrad{l}",
            grad_exchange([(g_wout, rows_of(r_out), (r_out, D))]))
        (dh, g_mix), _ = _bwd_rms(dp, win_f[l], h0, mix_norm_g, dh2, l, d_in, f"in_proj_bwd{l}")
        pending_win = jnp.transpose(g_win.reshape(D, N_DEV, c_in), (1, 0, 2)).astype(BF16)
        g_small[l] = (g_mix, g_mlp, _diag_blocks(gpool, 4), g256, g512, gpw, _diag_blocks(gwa, RG_HEADS),
                      _diag_blocks(gwx, RG_HEADS))
    grad_x = dh[N_META:][None]

    stack = lambda k: jnp.stack([g_small[l][k] for l in range(L)], axis=0)
    small_local = [stack(0).reshape(L, D), stack(1).reshape(L, D), g_fin, stack(2), stack(3), stack(4), stack(5),
                   stack(6), stack(7), dh[:N_META]]
    tail = grad_exchange([(pending_win, by_slot, (D, c_in))])
    for g in small_local:
        tail.gather(g, (N_DEV,) + g.shape, by_slot)
    landed = _comm_only(tail, "exchange_tail")
    recv["w_in"][0], gath = landed[0], landed[1:]

    out = {}
    for name, tr in (("w_in", 256), ("w_out", r_out), ("w_up", 256), ("w_down", 128)):
        out[name] = _adamw_big(recv[name], W[name], M[name], V[name], tr, f"adamw_{name}")
    sums, small_out = _adamw_small(gath, [W[n] for n in _SMALL_REPL], [M[n] for n in _SMALL_REPL],
                                   [V[n] for n in _SMALL_REPL])
    for a, name in enumerate(_SMALL_REPL):
        out[name] = small_out[4 * a:4 * a + 4]
    s_256, s_512, s_pw, s_meta = sums[4], sums[5], sums[6], sums[9]
    g_shard = {
        "meta_tokens": lax.dynamic_slice_in_dim(s_meta, me * 128, 128, axis=1),
        "convb_dw_w": lax.dynamic_slice_in_dim(s_256[:, :CONV_K, :], me * 32, 32, axis=2),
        "convb_pw_w": lax.dynamic_slice_in_dim(s_pw, me * 32, 32, axis=1),
        "rg_conv_w": lax.dynamic_slice_in_dim(s_512[:, 4:8, :], me * 64, 64, axis=2),
    }
    names = list(g_shard)
    res = _adamw_plain([g_shard[n] for n in names], [W[n] for n in names], [M[n] for n in names],
                       [V[n] for n in names], "adamw_small_sharded")
    for a, name in enumerate(names):
        out[name] = (g_shard[name],) + tuple(res[3 * a:3 * a + 3])

    def fix(name, arr):
        return arr.reshape(-1) if name == "final_norm_g" else arr

    return (loss, grad_x,
            *[fix(n, out[n][0]) for n in _WEIGHTS], *[fix(n, out[n][1]) for n in _WEIGHTS],
            *[fix(n, out[n][2]) for n in _WEIGHTS], *[fix(n, out[n][3]) for n in _WEIGHTS])
```

```python
import functools
import operator

import jax
import jax.numpy as jnp
from jax import lax
from jax.experimental import pallas as pl
from jax.experimental.pallas import tpu as pltpu

F32 = jnp.float32
BF16 = jnp.bfloat16

N_DEV = 8
N_META = 16
DEPTH = 2
EPS = 1e-6
POOL_WINDOWS = (2, 4, 8, 16)
POOL_GW = 64
CONV_K = 31
RG_CONV_K = 4
RG_HEADS = 8
RG_HD = 64
RG_C = 8.0
D_POOL = 256
D_CONV = 256
D_RNN = 512

ADAM_LR = 0.001
ADAM_B1 = 0.9
ADAM_B2 = 0.999
ADAM_EPS = 1e-08
ADAM_WD = 0.01
ADAM_STEP = 10

TT = 432
TD = 912
HALO = 48
VMEM_LIMIT = 56 * 1024 * 1024

MESH = pl.DeviceIdType.MESH
ANY = pl.BlockSpec(memory_space=pl.ANY)


def _cparams(*sem):
    return pltpu.CompilerParams(dimension_semantics=sem, vmem_limit_bytes=VMEM_LIMIT)


def _full(shape):
    nd = len(shape)
    return pl.BlockSpec(shape, lambda *_: (0,) * nd)


def _layer(shape, l):
    nd = len(shape)
    return pl.BlockSpec((None,) + tuple(shape), lambda *_: (l,) + (0,) * nd)


def _dot(a, b):
    return jnp.dot(a, b, preferred_element_type=F32)


def _dot_nt(a, b):
    return lax.dot_general(a, b, (((1,), (1,)), ((), ())), preferred_element_type=F32)


def _dot_tn(a, b):
    return lax.dot_general(a, b, (((0,), (0,)), ((), ())), preferred_element_type=F32)


def _rms(h):
    r = lax.rsqrt(jnp.mean(h * h, axis=-1, keepdims=True) + EPS)
    return h * r, r


def _rms_bwd(du, xh, r, g):
    dxh = du * g
    return r * (dxh - xh * jnp.mean(dxh * xh, axis=-1, keepdims=True))


def _sigmoid(x):
    return 1.0 / (1.0 + jnp.exp(-x))


def _colsum(x):
    return jnp.sum(x, axis=0, keepdims=True)


def _mesh_pos():
    return lax.axis_index("x"), lax.axis_index("y"), lax.axis_index("c")


def _slot(d):
    return 4 * d[0] + 2 * d[1] + d[2]


class _Comm:
    def __init__(self):
        self.srcs, self.send, self.dst, self.land, self.out_shapes = [], [], [], [], []

    def add_out(self, shape, dtype):
        self.out_shapes.append(jax.ShapeDtypeStruct(shape, dtype))
        return len(self.out_shapes) - 1

    def add(self, src, send, dst, land):
        self.srcs.append(src)
        self.send.append(send)
        self.dst.append(dst)
        self.land.append(land)

    def gather(self, block, out_shape, place):
        self.add(block, lambda ref, s: ref, self.add_out(out_shape, block.dtype), place)

    def sem_shapes(self):
        n = len(self.srcs)
        return [pltpu.SemaphoreType.DMA((7 * n,)), pltpu.SemaphoreType.DMA((7 * n,)), pltpu.SemaphoreType.DMA((n,))]

    def _copies(self, cins, couts, send_sems, recv_sems, local_sems, with_arrivals=True):
        x, y, c = _mesh_pos()
        me = (x, y, c)
        sends, arrivals, local = [], [], []
        for a in range(len(self.srcs)):
            out = couts[self.dst[a]]
            mine = self.land[a](out, _slot(me))
            local.append(pltpu.make_async_copy(self.send[a](cins[a], _slot(me)), mine, local_sems.at[a]))
            for k in range(N_DEV - 1):
                d = k + 1
                peer = (1 - x if (d >> 2) & 1 else x, 1 - y if (d >> 1) & 1 else y, 1 - c if d & 1 else c)
                sems = dict(send_sem=send_sems.at[a * 7 + k], recv_sem=recv_sems.at[a * 7 + k],
                            device_id=peer, device_id_type=MESH)
                sends.append(pltpu.make_async_remote_copy(
                    src_ref=self.send[a](cins[a], _slot(peer)), dst_ref=mine, **sems))
                if with_arrivals:
                    theirs = self.land[a](out, _slot(peer))
                    arrivals.append(pltpu.make_async_remote_copy(src_ref=theirs, dst_ref=theirs, **sems))
        return sends, arrivals, local

    def start(self, *refs):
        sends, _, local = self._copies(*refs, with_arrivals=False)
        for cp in local + sends:
            cp.start()

    def wait(self, *refs):
        sends, arrivals, local = self._copies(*refs)
        for cp in arrivals:
            cp.wait_recv()
        for cp in sends:
            cp.wait_send()
        for cp in local:
            cp.wait()


def _call(body, *, name, grid, in_specs, out_specs, out_shape, operands, scratch_shapes=(), sem=None, comm=None):
    out_specs, out_shape = list(out_specs), list(out_shape)
    if comm is None:
        res = pl.pallas_call(body, name=name, grid=grid, in_specs=list(in_specs), out_specs=out_specs,
                             out_shape=out_shape, scratch_shapes=list(scratch_shapes),
                             compiler_params=_cparams(*sem))(*operands)
        return list(res), []
    n_in, n_out, n_scr = len(in_specs), len(out_specs), len(scratch_shapes)
    nc_in, nc_out = len(comm.srcs), len(comm.out_shapes)

    def hosted(*refs):
        ins, cins = refs[:n_in], refs[n_in:n_in + nc_in]
        o0 = n_in + nc_in
        outs, couts = refs[o0:o0 + n_out], refs[o0 + n_out:o0 + n_out + nc_out]
        s0 = o0 + n_out + nc_out
        scr, sems = refs[s0:s0 + n_scr], refs[s0 + n_scr:]
        ids = [pl.program_id(a) for a in range(len(grid))]
        first = functools.reduce(operator.and_, [i == 0 for i in ids])
        last = functools.reduce(operator.and_, [i == g - 1 for i, g in zip(ids, grid)])

        @pl.when(first)
        def _():
            comm.start(cins, couts, *sems)

        body(*ins, *outs, *scr)

        @pl.when(last)
        def _():
            comm.wait(cins, couts, *sems)

    res = pl.pallas_call(
        hosted, name=name, grid=grid, in_specs=list(in_specs) + [ANY] * nc_in,
        out_specs=out_specs + [ANY] * nc_out, out_shape=out_shape + comm.out_shapes,
        scratch_shapes=list(scratch_shapes) + comm.sem_shapes(),
        compiler_params=_cparams(*(["arbitrary"] * len(grid))))(*operands, *comm.srcs)
    return list(res[:n_out]), list(res[n_out:])


def _comm_only(comm, name):
    def body(*refs):
        n, m = len(comm.srcs), len(comm.out_shapes)
        comm.start(refs[:n], refs[n:n + m], *refs[n + m:])
        comm.wait(refs[:n], refs[n:n + m], *refs[n + m:])

    return list(pl.pallas_call(body, name=name, in_specs=[ANY] * len(comm.srcs), out_specs=[ANY] * len(comm.out_shapes),
                               out_shape=comm.out_shapes, scratch_shapes=comm.sem_shapes())(*comm.srcs))


def _allgather(items, name):
    n = len(items)

    def body(*refs):
        ins, outs = refs[:n], refs[n:2 * n]
        send_sems, recv_sems, local_sems = refs[2 * n:]
        x, y, c = _mesh_pos()
        me, sib = (x, y, c), (x, y, 1 - c)
        chips = [(1 - x, y), (x, 1 - y), (1 - x, 1 - y)]

        def copy(a, k, blk, to, from_input=False):
            region = items[a][2](outs[a], _slot(blk))
            return pltpu.make_async_remote_copy(
                src_ref=ins[a] if from_input else region, dst_ref=region,
                send_sem=send_sems.at[a * 7 + k], recv_sem=recv_sems.at[a * 7 + k],
                device_id=to, device_id_type=MESH)

        local = [pltpu.make_async_copy(ins[a], items[a][2](outs[a], _slot(me)), local_sems.at[a]) for a in range(n)]
        for cp in local:
            cp.start()
        first = []
        for a in range(n):
            first.append(copy(a, 0, me, sib, True))
            for j, chip in enumerate(chips):
                first.append(copy(a, 1 + j, me, (*chip, c), True))
        for cp in first:
            cp.start()
        passed = []
        for j, chip in enumerate(chips):
            for a in range(n):
                copy(a, 1 + j, (*chip, c), me).wait_recv()
                fwd = copy(a, 4 + j, (*chip, c), sib)
                fwd.start()
                passed.append(fwd)
        for a in range(n):
            copy(a, 0, sib, me).wait_recv()
            for j, chip in enumerate(chips):
                copy(a, 4 + j, (*chip, 1 - c), me).wait_recv()
        for cp in first + passed:
            cp.wait_send()
        for cp in local:
            cp.wait()

    return pl.pallas_call(
        body, name=name,
        in_specs=[ANY] * n, out_specs=[ANY] * n,
        out_shape=[jax.ShapeDtypeStruct(it[1], it[0].dtype) for it in items],
        scratch_shapes=[pltpu.SemaphoreType.DMA((7 * n,)), pltpu.SemaphoreType.DMA((7 * n,)),
                        pltpu.SemaphoreType.DMA((n,))],
    )(*[it[0] for it in items])


def _rms_mm(h, g, w, l, ck, name, comm=None, w_is_nk=False):
    T, D = h.shape
    N = w.shape[0] if w_is_nk else w.shape[1]

    def body(h_ref, g_ref, w_ref, o_ref):
        xh, _ = _rms(h_ref[...])
        u = (xh * g_ref[l:l + 1, :]).astype(BF16)
        o_ref[...] = _dot_nt(u, w_ref[...]) if w_is_nk else _dot(u, w_ref[...])

    w_spec = pl.BlockSpec((ck, D), lambda j, i: (j, 0)) if w_is_nk else pl.BlockSpec((D, ck), lambda j, i: (0, j))
    return _call(
        body, name=name, grid=(N // ck, T // TD),
        in_specs=[pl.BlockSpec((TD, D), lambda j, i: (i, 0)), _full(g.shape), w_spec],
        out_specs=[pl.BlockSpec((TD, ck), lambda j, i: (i, j))],
        out_shape=[jax.ShapeDtypeStruct((T, N), F32)],
        operands=(h, g, w), sem=("parallel", "parallel"), comm=comm)


def _mm_res(res, a, w, ck, relu2, name, comm=None):
    T, D = res.shape
    K = a.shape[-1]
    nk = K // ck

    def body(r_ref, a_ref, w_ref, o_ref):
        k = pl.program_id(1)

        @pl.when(k == 0)
        def _():
            o_ref[...] = r_ref[...]

        av = a_ref[...]
        if relu2:
            av = jnp.square(jnp.maximum(av, 0.0))
        o_ref[...] += _dot(av.astype(BF16), w_ref[...])

    return _call(
        body, name=name, grid=(T // TD, nk),
        in_specs=[pl.BlockSpec((TD, D), lambda i, k: (i, 0)), pl.BlockSpec((TD, ck), lambda i, k: (i, k)),
                  pl.BlockSpec((ck, D), lambda i, k: (k, 0))],
        out_specs=[pl.BlockSpec((TD, D), lambda i, k: (i, 0))],
        out_shape=[jax.ShapeDtypeStruct((T, D), F32)],
        operands=(res, a, w), sem=("parallel", "arbitrary"), comm=comm)


def _final_loss(h, gf, tgt):
    T, D = h.shape

    def body(h_ref, g_ref, t_ref, dh_ref, dg_ref, loss_ref):
        i = pl.program_id(0)

        @pl.when(i == 0)
        def _():
            dg_ref[...] = jnp.zeros_like(dg_ref)
            loss_ref[...] = jnp.zeros_like(loss_ref)

        g = g_ref[...]
        xh, r = _rms(h_ref[...])
        row = i * TD + lax.broadcasted_iota(jnp.int32, (TD, 1), 0)
        keep = (row >= N_META).astype(F32)
        diff = (xh * g - t_ref[...]) * keep
        part = jnp.sum(jnp.sum(diff * diff, axis=1, keepdims=True), axis=0, keepdims=True)
        loss_ref[...] += (0.5 / D) * part
        dy = diff * (1.0 / D)
        dg_ref[...] += _colsum(dy * xh)
        dh_ref[...] = _rms_bwd(dy, xh, r, g)

    return pl.pallas_call(
        body, name="final_loss", grid=(T // TD,),
        in_specs=[pl.BlockSpec((TD, D), lambda i: (i, 0)), _full(gf.shape), pl.BlockSpec((TD, D), lambda i: (i, 0))],
        out_specs=[pl.BlockSpec((TD, D), lambda i: (i, 0)), _full((1, D)), _full((8, 128))],
        out_shape=[jax.ShapeDtypeStruct((T, D), F32), jax.ShapeDtypeStruct((1, D), F32),
                   jax.ShapeDtypeStruct((8, 128), F32)],
        compiler_params=_cparams("arbitrary"))(h, gf, tgt)


def _bwd_dz(dh, wdown, z, ck, name, comm=None):
    T, D = dh.shape
    F = z.shape[-1]

    def body(d_ref, w_ref, z_ref, o_ref):
        dact = _dot_nt(d_ref[...].astype(BF16), w_ref[...])
        o_ref[...] = (dact * (2.0 * jnp.maximum(z_ref[...], 0.0))).astype(BF16)

    return _call(
        body, name=name, grid=(F // ck, T // TD),
        in_specs=[pl.BlockSpec((TD, D), lambda j, i: (i, 0)), pl.BlockSpec((ck, D), lambda j, i: (j, 0)),
                  pl.BlockSpec((TD, ck), lambda j, i: (i, j))],
        out_specs=[pl.BlockSpec((TD, ck), lambda j, i: (i, j))],
        out_shape=[jax.ShapeDtypeStruct((T, F), BF16)],
        operands=(dh, wdown, z), sem=("parallel", "parallel"), comm=comm)


def _bwd_plain(dh, w, name, comm=None):
    T, D = dh.shape
    K = w.shape[0]

    def body(d_ref, w_ref, o_ref):
        o_ref[...] = _dot_nt(d_ref[...].astype(BF16), w_ref[...])

    return _call(
        body, name=name, grid=(T // TD,),
        in_specs=[pl.BlockSpec((TD, D), lambda i: (i, 0)), _full((K, D))],
        out_specs=[pl.BlockSpec((TD, K), lambda i: (i, 0))],
        out_shape=[jax.ShapeDtypeStruct((T, K), F32)],
        operands=(dh, w), sem=("parallel",), comm=comm)


def _bwd_rms(dx, w, h, g, dh_in, l, ck, name, comm=None, w_is_kd=False):
    T, D = h.shape
    K = dx.shape[-1]
    nk = K // ck
    w_spec = pl.BlockSpec((ck, D), lambda i, k: (k, 0)) if w_is_kd else pl.BlockSpec((D, ck), lambda i, k: (0, k))

    def body(x_ref, w_ref, h_ref, g_ref, di_ref, dh_ref, dg_ref, acc_ref):
        i, k = pl.program_id(0), pl.program_id(1)

        @pl.when((i == 0) & (k == 0))
        def _():
            dg_ref[...] = jnp.zeros_like(dg_ref)

        @pl.when(k == 0)
        def _():
            acc_ref[...] = jnp.zeros_like(acc_ref)

        acc_ref[...] += _dot(x_ref[...], w_ref[...]) if w_is_kd else _dot_nt(x_ref[...], w_ref[...])

        @pl.when(k == nk - 1)
        def _():
            du = acc_ref[...]
            xh, r = _rms(h_ref[...])
            dg_ref[...] += _colsum(du * xh)
            dh_ref[...] = di_ref[...] + _rms_bwd(du, xh, r, g_ref[l:l + 1, :])

    return _call(
        body, name=name, grid=(T // TD, nk),
        in_specs=[pl.BlockSpec((TD, ck), lambda i, k: (i, k)), w_spec,
                  pl.BlockSpec((TD, D), lambda i, k: (i, 0)), _full(g.shape),
                  pl.BlockSpec((TD, D), lambda i, k: (i, 0))],
        out_specs=[pl.BlockSpec((TD, D), lambda i, k: (i, 0)), _full((1, D))],
        out_shape=[jax.ShapeDtypeStruct((T, D), F32), jax.ShapeDtypeStruct((1, D), F32)],
        scratch_shapes=[pltpu.VMEM((TD, D), F32)],
        operands=(dx, w, h, g, dh_in), sem=("arbitrary", "arbitrary"), comm=comm)


def _mm_tn(a, b, relu2, rms_on, g, l, chunk, ck, out_dtype, name, comm=None):
    T = a.shape[0]
    Ka, Nb = a.shape[1], b.shape[1]
    ca, cb = (ck, Nb) if chunk == "a" else (Ka, ck)
    nj = (Ka // ck) if chunk == "a" else (Nb // ck)
    nt = T // TD

    def body(*refs):
        if rms_on is not None:
            a_ref, g_ref, b_ref, o_ref, acc_ref = refs
        else:
            a_ref, b_ref, o_ref, acc_ref = refs
        t = pl.program_id(1)

        @pl.when(t == 0)
        def _():
            acc_ref[...] = jnp.zeros_like(acc_ref)

        av, bv = a_ref[...], b_ref[...]
        if relu2:
            av = jnp.square(jnp.maximum(av.astype(F32), 0.0))
        if rms_on == "a":
            av = _rms(av)[0] * g_ref[l:l + 1, :]
        elif rms_on == "b":
            bv = _rms(bv)[0] * g_ref[l:l + 1, :]
        acc_ref[...] += _dot_tn(av.astype(BF16), bv.astype(BF16))

        @pl.when(t == nt - 1)
        def _():
            o_ref[...] = acc_ref[...].astype(out_dtype)

    if chunk == "a":
        a_spec = pl.BlockSpec((TD, ca), lambda j, t: (t, j))
        b_spec = pl.BlockSpec((TD, cb), lambda j, t: (t, 0))
        o_spec = pl.BlockSpec((ca, cb), lambda j, t: (j, 0))
    else:
        a_spec = pl.BlockSpec((TD, ca), lambda j, t: (t, 0))
        b_spec = pl.BlockSpec((TD, cb), lambda j, t: (t, j))
        o_spec = pl.BlockSpec((ca, cb), lambda j, t: (0, j))
    ins, specs = [a], [a_spec]
    if rms_on is not None:
        ins.append(g)
        specs.append(_full(g.shape))
    ins.append(b)
    specs.append(b_spec)
    return _call(
        body, name=name, grid=(nj, nt), in_specs=specs, out_specs=[o_spec],
        out_shape=[jax.ShapeDtypeStruct((Ka, Nb), out_dtype)], scratch_shapes=[pltpu.VMEM((ca, cb), F32)],
        operands=tuple(ins), sem=("parallel", "arbitrary"), comm=comm)


def _shift_down(x, k):
    return x if k == 0 else pltpu.roll(x, k, 0)


def _shift_up(x, k):
    return x if k == 0 else pltpu.roll(x, x.shape[0] - k, 0)


def _pool_sel(lane_grp, vals):
    return jnp.where(lane_grp == 0, vals[0], jnp.where(lane_grp == 1, vals[1], jnp.where(lane_grp == 2, vals[2], vals[3])))


def _pool_fwd(ue, pos):
    s = ue
    sums = []
    for k in (1, 2, 4, 8):
        s = s + _shift_down(s, k)
        sums.append(s[HALO:])
    grp = lax.broadcasted_iota(jnp.int32, (1, D_POOL), 1) // POOL_GW
    wsum = _pool_sel(grp, sums)
    width = _pool_sel(grp, [jnp.float32(w) for w in POOL_WINDOWS])
    cnt = jnp.minimum(pos.astype(F32), width)
    return wsum / cnt - ue[HALO:], cnt, grp


def _conv_fwd(xe, w_ref, nk):
    acc = None
    for k in range(nk):
        term = _shift_down(xe, nk - 1 - k)[HALO:] * w_ref(k)
        acc = term if acc is None else acc + term
    return acc


def _layernorm(cb):
    mu = jnp.mean(cb, axis=-1, keepdims=True)
    xc = cb - mu
    rstd = lax.rsqrt(jnp.mean(xc * xc, axis=-1, keepdims=True) + EPS)
    return xc * rstd, rstd


def _softplus_neg(lam):
    x = -lam
    e = jnp.exp(-jnp.abs(x))
    u = 1.0 + e
    d = u - 1.0
    log1p = jnp.where(d == 0.0, e, jnp.log(u) * (e / jnp.where(d == 0.0, 1.0, d)))
    return jnp.maximum(x, 0.0) + log1p


def _expm1(x):
    small = x * (1.0 + x * 0.5 * (1.0 + x * (1.0 / 3.0) * (1.0 + x * 0.25)))
    return jnp.where(jnp.abs(x) < 1e-2, small, jnp.exp(x) - 1.0)


def _gelu_parts(x):
    c0 = 0.7978845608028654
    inner = c0 * (x + 0.044715 * x * x * x)
    th = jnp.tanh(inner)
    gelu = 0.5 * x * (1.0 + th)
    dgelu = 0.5 * (1.0 + th) + 0.5 * x * (1.0 - th * th) * c0 * (1.0 + 3.0 * 0.044715 * x * x)
    return gelu, dgelu


def _rg_gates(xc, wa_ref, wx_ref, v512_ref):
    xb = xc.astype(BF16)
    ra = _sigmoid(_dot(xb, wa_ref[...]) + v512_ref[1:2, :])
    ri = _sigmoid(_dot(xb, wx_ref[...]) + v512_ref[2:3, :])
    sp = _softplus_neg(v512_ref[3:4, :])
    log_a = (-RG_C) * ra * sp
    a = jnp.exp(log_a)
    mult = jnp.sqrt(-_expm1(2.0 * log_a))
    return ra, ri, sp, a, mult


def _scan_down(a, b):
    n = a.shape[0]
    row = lax.broadcasted_iota(jnp.int32, (n, 1), 0)
    k = 1
    while k < n:
        ok = row >= k
        b = a * jnp.where(ok, _shift_down(b, k), 0.0) + b
        a = a * jnp.where(ok, _shift_down(a, k), 1.0)
        k *= 2
    return b, a


def _scan_up(c, g):
    n = c.shape[0]
    row = lax.broadcasted_iota(jnp.int32, (n, 1), 0)
    k = 1
    while k < n:
        ok = row < n - k
        g = c * jnp.where(ok, _shift_up(g, k), 0.0) + g
        c = c * jnp.where(ok, _shift_up(c, k), 1.0)
        k *= 2
    return g, c


def _mixer_specs(l, tile_of):
    r = TT // HALO
    return [
        pl.BlockSpec((TT, 1792), lambda i: (tile_of(i), 0)),
        pl.BlockSpec((HALO, 1792), lambda i: (jnp.maximum(tile_of(i) * r - 1, 0), 0)),
        _layer((256, 256), l), _layer((8, 256), l), _layer((32, 256), l), _layer((256, 256), l),
        _layer((512, 512), l), _layer((512, 512), l), _layer((8, 512), l),
    ]


def _mixer_fwd(p, wts, l, comm=None):
    T = p.shape[0]

    def body(p_ref, ph_ref, wpool_ref, v256_ref, dww_ref, wpw_ref, wa_ref, wx_ref, v512_ref, y_ref, hr_ref, carry_ref):
        i = pl.program_id(0)

        @pl.when(i == 0)
        def _():
            carry_ref[...] = jnp.zeros_like(carry_ref)

        halo = jnp.where(i == 0, 0.0, ph_ref[...])
        e = jnp.concatenate([halo, p_ref[...]], axis=0)
        pos = i * TT + 1 + lax.broadcasted_iota(jnp.int32, (TT, 1), 0)
        pooled, _, _ = _pool_fwd(e[:, 0:256], pos)
        y_a = _dot(pooled.astype(BF16), wpool_ref[...]) * v256_ref[0:1, :]
        ub = e[:, 256:512] * _sigmoid(e[:, 512:768])
        cb = _conv_fwd(ub, lambda k: dww_ref[k:k + 1, :], CONV_K) + v256_ref[1:2, :]
        xhat, _ = _layernorm(cb)
        ln = xhat * v256_ref[2:3, :] + v256_ref[3:4, :]
        s = ln * _sigmoid(ln)
        y_b = _dot(s.astype(BF16), wpw_ref[...])
        xc = _conv_fwd(e[:, 1280:1792], lambda k: v512_ref[4 + k:5 + k, :], RG_CONV_K) + v512_ref[0:1, :]
        _, ri, _, a, mult = _rg_gates(xc, wa_ref, wx_ref, v512_ref)
        hloc, acum = _scan_down(a, mult * ri * xc)
        hr = hloc + acum * carry_ref[0:1, :]
        hr_ref[...] = hr
        carry_ref[0:1, :] = hr_ref[TT - 1:TT, :]
        gelu, _ = _gelu_parts(p_ref[:, 768:1280])
        y_ref[:, 0:256] = y_a.astype(BF16)
        y_ref[:, 256:512] = y_b.astype(BF16)
        y_ref[:, 512:1024] = (gelu * hr).astype(BF16)

    return _call(
        body, name=f"mixer_fwd{l}", grid=(T // TT,),
        in_specs=_mixer_specs(l, lambda i: i),
        out_specs=[pl.BlockSpec((TT, 1024), lambda i: (i, 0)), pl.BlockSpec((TT, 512), lambda i: (i, 0))],
        out_shape=[jax.ShapeDtypeStruct((T, 1024), BF16), jax.ShapeDtypeStruct((T, 512), F32)],
        scratch_shapes=[pltpu.VMEM((8, 512), F32)],
        operands=(p, p, *wts), sem=("arbitrary",), comm=comm)


def _mixer_bwd(p, hr, dy, wts, l, comm=None):
    T = p.shape[0]
    nt = T // TT
    tile_of = lambda i: nt - 1 - i

    def body(p_ref, ph_ref, wpool_ref, v256_ref, dww_ref, wpw_ref, wa_ref, wx_ref, v512_ref, hr_ref, hrh_ref, dy_ref,
             dp_ref, g256_ref, g512_ref, gpool_ref, gpw_ref, gwa_ref, gwx_ref,
             q_c, dcb_c, dxc_c, ag_c):
        i = pl.program_id(0)
        j = nt - 1 - i

        @pl.when(i == 0)
        def _():
            for ref in (g256_ref, g512_ref, gpool_ref, gpw_ref, gwa_ref, gwx_ref, q_c, dcb_c, dxc_c, ag_c):
                ref[...] = jnp.zeros_like(ref)

        halo = jnp.where(j == 0, 0.0, ph_ref[...])
        e = jnp.concatenate([halo, p_ref[...]], axis=0)
        pos = j * TT + 1 + lax.broadcasted_iota(jnp.int32, (TT, 1), 0)
        row = lax.broadcasted_iota(jnp.int32, (TT, 1), 0)

        dy_a = dy_ref[:, 0:256]
        pooled, cnt, grp = _pool_fwd(e[:, 0:256], pos)
        scale = v256_ref[0:1, :]
        mixed = _dot(pooled.astype(BF16), wpool_ref[...])
        g256_ref[34:35, :] += _colsum(dy_a * mixed)
        dmixed = (dy_a * scale).astype(BF16)
        gpool_ref[...] += _dot_tn(pooled.astype(BF16), dmixed)
        dpooled = _dot_nt(dmixed, wpool_ref[...])
        q = dpooled / cnt
        s = jnp.concatenate([q, q_c[...]], axis=0)
        sums = []
        for k in (1, 2, 4, 8):
            s = s + _shift_up(s, k)
            sums.append(s[:TT])
        dp_ref[:, 0:256] = (_pool_sel(grp, sums) - dpooled).astype(BF16)
        q_c[...] = q[0:16]

        dy_b = dy_ref[:, 256:512].astype(BF16)
        sg = _sigmoid(e[:, 512:768])
        ub = e[:, 256:512] * sg
        cb = _conv_fwd(ub, lambda k: dww_ref[k:k + 1, :], CONV_K) + v256_ref[1:2, :]
        xhat, rstd = _layernorm(cb)
        ln_g = v256_ref[2:3, :]
        ln = xhat * ln_g + v256_ref[3:4, :]
        sig = _sigmoid(ln)
        sact = ln * sig
        gpw_ref[...] += _dot_tn(sact.astype(BF16), dy_b)
        dln = _dot_nt(dy_b, wpw_ref[...]) * (sig * (1.0 + ln * (1.0 - sig)))
        g256_ref[32:33, :] += _colsum(dln * xhat)
        g256_ref[33:34, :] += _colsum(dln)
        dxhat = dln * ln_g
        dcb = rstd * (dxhat - jnp.mean(dxhat, axis=-1, keepdims=True)
                      - xhat * jnp.mean(dxhat * xhat, axis=-1, keepdims=True))
        g256_ref[31:32, :] += _colsum(dcb)
        for k in range(CONV_K):
            g256_ref[k:k + 1, :] += _colsum(dcb * _shift_down(ub, CONV_K - 1 - k)[HALO:])
        ed = jnp.concatenate([dcb, dcb_c[...]], axis=0)
        dub = None
        for k in range(CONV_K):
            term = _shift_up(ed, CONV_K - 1 - k)[:TT] * dww_ref[k:k + 1, :]
            dub = term if dub is None else dub + term
        dcb_c[...] = dcb[0:32]
        sg_t = sg[HALO:]
        v_t = p_ref[:, 256:512]
        dp_ref[:, 256:512] = (dub * sg_t).astype(BF16)
        dp_ref[:, 512:768] = (dub * v_t * sg_t * (1.0 - sg_t)).astype(BF16)

        dy_c = dy_ref[:, 512:1024]
        cxe = e[:, 1280:1792]
        xc = _conv_fwd(cxe, lambda k: v512_ref[4 + k:5 + k, :], RG_CONV_K) + v512_ref[0:1, :]
        ra, ri, sp, a, mult = _rg_gates(xc, wa_ref, wx_ref, v512_ref)
        hrv = hr_ref[...]
        gelu, dgelu = _gelu_parts(p_ref[:, 768:1280])
        dp_ref[:, 768:1280] = (dy_c * hrv * dgelu).astype(BF16)
        dhr = dy_c * gelu
        coef = jnp.where(row < TT - 1, _shift_up(a, 1), 1.0)
        gloc, ccum = _scan_up(coef, dhr)
        gg = gloc + ccum * ag_c[0:1, :]
        ag_c[...] = (a * gg)[0:8]
        hr_before = jnp.where(j == 0, 0.0, hrh_ref[7:8, :])
        hr_prev = jnp.where(row >= 1, _shift_down(hrv, 1), hr_before)
        da = gg * hr_prev
        dmult = gg * ri * xc
        dri = gg * mult * xc
        dxc = gg * mult * ri
        dlog_a = da * a - dmult * (a * a / mult)
        g512_ref[3:4, :] += _colsum(dlog_a * ((-RG_C) * ra))
        dpre_a = dlog_a * ((-RG_C) * sp) * ra * (1.0 - ra)
        dpre_x = dri * ri * (1.0 - ri)
        g512_ref[1:2, :] += _colsum(dpre_a)
        g512_ref[2:3, :] += _colsum(dpre_x)
        xb = xc.astype(BF16)
        dpa_b, dpx_b = dpre_a.astype(BF16), dpre_x.astype(BF16)
        gwa_ref[...] += _dot_tn(xb, dpa_b)
        gwx_ref[...] += _dot_tn(xb, dpx_b)
        dxc = dxc + _dot_nt(dpa_b, wa_ref[...]) + _dot_nt(dpx_b, wx_ref[...])
        g512_ref[0:1, :] += _colsum(dxc)
        for k in range(RG_CONV_K):
            g512_ref[4 + k:5 + k, :] += _colsum(dxc * _shift_down(cxe, RG_CONV_K - 1 - k)[HALO:])
        ex = jnp.concatenate([dxc, dxc_c[...]], axis=0)
        dcx = None
        for k in range(RG_CONV_K):
            term = _shift_up(ex, RG_CONV_K - 1 - k)[:TT] * v512_ref[4 + k:5 + k, :]
            dcx = term if dcx is None else dcx + term
        dxc_c[...] = dxc[0:8]
        dp_ref[:, 1280:1792] = dcx.astype(BF16)

        @pl.when(i == nt - 1)
        def _():
            lam = v512_ref[3:4, :]
            g512_ref[3:4, :] = g512_ref[3:4, :] * (-_sigmoid(-lam))

    in_specs = _mixer_specs(l, tile_of) + [
        pl.BlockSpec((TT, 512), lambda i: (tile_of(i), 0)),
        pl.BlockSpec((8, 512), lambda i: (jnp.maximum(tile_of(i) * (TT // 8) - 1, 0), 0)),
        pl.BlockSpec((TT, 1024), lambda i: (tile_of(i), 0)),
    ]
    acc_shapes = [(40, 256), (8, 512), (256, 256), (256, 256), (512, 512), (512, 512)]
    return _call(
        body, name=f"mixer_bwd{l}", grid=(nt,),
        in_specs=in_specs,
        out_specs=[pl.BlockSpec((TT, 1792), lambda i: (tile_of(i), 0))] + [_full(s) for s in acc_shapes],
        out_shape=[jax.ShapeDtypeStruct((T, 1792), BF16)] + [jax.ShapeDtypeStruct(s, F32) for s in acc_shapes],
        scratch_shapes=[pltpu.VMEM((16, 256), F32), pltpu.VMEM((32, 256), F32), pltpu.VMEM((8, 512), F32),
                        pltpu.VMEM((8, 512), F32)],
        operands=(p, p, *wts, hr, hr, dy), sem=("arbitrary",), comm=comm)


def _adam(w, g, m, v):
    m = ADAM_B1 * m + (1.0 - ADAM_B1) * g
    v = ADAM_B2 * v + (1.0 - ADAM_B2) * (g * g)
    m_hat = m / (1.0 - ADAM_B1 ** ADAM_STEP)
    v_hat = v / (1.0 - ADAM_B2 ** ADAM_STEP)
    delta = -ADAM_LR * (m_hat / (jnp.sqrt(v_hat) + ADAM_EPS) + ADAM_WD * w)
    return delta, m, v


def _adamw_big(recvs, w, m, v, tr, name):
    L, R, C = w.shape

    def body(r0_ref, r1_ref, w_ref, m_ref, v_ref, g_ref, d_ref, mo_ref, vo_ref):
        l = pl.program_id(0)

        def total(r_ref):
            g = r_ref[0].astype(F32)
            for s in range(1, N_DEV):
                g = g + r_ref[s].astype(F32)
            g_ref[...] = g

        @pl.when(l == 0)
        def _():
            total(r0_ref)

        @pl.when(l == 1)
        def _():
            total(r1_ref)

        d_ref[...], mo_ref[...], vo_ref[...] = _adam(w_ref[...], g_ref[...], m_ref[...], v_ref[...])

    blk = pl.BlockSpec((None, tr, C), lambda l, i: (l, i, 0))
    return pl.pallas_call(
        body, name=name, grid=(L, R // tr),
        in_specs=[pl.BlockSpec((N_DEV, tr, C), lambda l, i: (0, i * (1 - l), 0)),
                  pl.BlockSpec((N_DEV, tr, C), lambda l, i: (0, i * l, 0)), blk, blk, blk],
        out_specs=[blk] * 4,
        out_shape=[jax.ShapeDtypeStruct((L, R, C), F32)] * 4,
        compiler_params=_cparams("arbitrary", "arbitrary"))(recvs[0], recvs[1], w, m, v)


def _sum_slots(recvs, name):
    L = len(recvs)
    _, R, C = recvs[0].shape

    def body(*refs):
        o_ref = refs[L]
        for l in range(L):
            acc = refs[l][0].astype(F32)
            for s in range(1, N_DEV):
                acc = acc + refs[l][s].astype(F32)
            o_ref[l] = acc

    return pl.pallas_call(body, name=name, out_shape=jax.ShapeDtypeStruct((L, R, C), F32))(*recvs)


def _adamw_plain(gs, ws, ms, vs, name):
    n = len(gs)

    def body(*refs):
        g_r, w_r, m_r, v_r = refs[:n], refs[n:2 * n], refs[2 * n:3 * n], refs[3 * n:4 * n]
        outs = refs[4 * n:]
        for a in range(n):
            d, mo, vo = _adam(w_r[a][...], g_r[a][...], m_r[a][...], v_r[a][...])
            outs[3 * a][...] = d
            outs[3 * a + 1][...] = mo
            outs[3 * a + 2][...] = vo

    shapes = []
    for a in range(n):
        shapes += [jax.ShapeDtypeStruct(ws[a].shape, F32)] * 3
    return pl.pallas_call(body, name=name, out_shape=shapes,
                          compiler_params=pltpu.CompilerParams(vmem_limit_bytes=VMEM_LIMIT))(*gs, *ws, *ms, *vs)


_V256_ROWS = {"convb_dw_b": 31, "convb_ln_g": 32, "convb_ln_b": 33, "pool_scale": 34}
_V512_ROWS = {"rg_conv_b": 0, "rg_b_a": 1, "rg_b_x": 2, "rg_lambda": 3}
_SMALL_REPL = ["mix_norm_g", "mlp_norm_g", "final_norm_g", "pool_w", "pool_scale", "convb_dw_b", "convb_ln_g",
               "convb_ln_b", "rg_conv_b", "rg_w_a", "rg_b_a", "rg_w_x", "rg_b_x", "rg_lambda"]


def _adamw_small(gath1, gath0, gath_tail, ws, ms, vs):
    n = len(_SMALL_REPL)
    n1, n0, nt = len(gath1), len(gath0), len(gath_tail)
    ng = n1 + n0 + nt
    L = DEPTH
    per_layer = [g.shape[1:] for g in gath1[:8]]
    sum_shapes = ([(L,) + s[1:] for s in per_layer[:2]] + [gath1[8].shape[1:]] + [(L,) + s for s in per_layer[2:]]
                  + [gath_tail[1].shape[1:]])

    def body(*refs):
        in1, in0, in_t = refs[:n1], refs[n1:n1 + n0], refs[n1 + n0:ng]
        w_r, m_r, v_r = refs[ng:ng + n], refs[ng + n:ng + 2 * n], refs[ng + 2 * n:ng + 3 * n]
        sums = refs[ng + 3 * n:ng + 3 * n + 10]
        outs = refs[ng + 3 * n + 10:]

        def total(ref):
            acc = ref[0]
            for s in range(1, N_DEV):
                acc = acc + ref[s]
            return acc

        s_mix, s_mlp, s_fin, s_pool, s_256, s_512, s_pw, s_wa, s_wx, s_meta = sums
        s_mix[1:2, :] = total(in1[0])
        s_mix[0:1, :] = total(in_t[0])
        s_mlp[1:2, :] = total(in1[1])
        s_mlp[0:1, :] = total(in0[0])
        s_fin[...] = total(in1[8])
        s_meta[...] = total(in_t[1])
        for k, dst in enumerate((s_pool, s_256, s_512, s_pw, s_wa, s_wx)):
            dst[1] = total(in1[2 + k])
            dst[0] = total(in0[1 + k])

        def grad_of(name, idx):
            if name == "mix_norm_g":
                return s_mix[idx[0]:idx[0] + 1, :]
            if name == "mlp_norm_g":
                return s_mlp[idx[0]:idx[0] + 1, :]
            if name == "final_norm_g":
                return s_fin[...]
            if name in _V256_ROWS:
                r = _V256_ROWS[name]
                return s_256[idx[0], r:r + 1, :]
            if name in _V512_ROWS:
                r = _V512_ROWS[name]
                return s_512[idx[0], r:r + 1, :]
            src = {"pool_w": s_pool, "rg_w_a": s_wa, "rg_w_x": s_wx}[name]
            return src[idx[0], idx[1] * 64:(idx[1] + 1) * 64, :]

        for a, name in enumerate(_SMALL_REPL):
            shape = w_r[a].shape
            if name == "final_norm_g":
                parts = [((), (slice(None), slice(None)))]
            elif len(shape) == 2:
                parts = [((l,), (slice(l, l + 1), slice(None))) for l in range(shape[0])]
            else:
                parts = [((l, h), (l, h)) for l in range(shape[0]) for h in range(shape[1])]
            for idx, sel in parts:
                g = grad_of(name, idx)
                d, mo, vo = _adam(w_r[a][sel], g, m_r[a][sel], v_r[a][sel])
                outs[4 * a][sel] = g
                outs[4 * a + 1][sel] = d
                outs[4 * a + 2][sel] = mo
                outs[4 * a + 3][sel] = vo

    out_shape = [jax.ShapeDtypeStruct(s, F32) for s in sum_shapes]
    for a in range(n):
        out_shape += [jax.ShapeDtypeStruct(ws[a].shape, F32)] * 4
    res = pl.pallas_call(body, name="adamw_small", out_shape=out_shape,
                         compiler_params=pltpu.CompilerParams(vmem_limit_bytes=VMEM_LIMIT))(
                             *gath1, *gath0, *gath_tail, *ws, *ms, *vs)
    return res[:10], res[10:]


_WEIGHTS = ['meta_tokens', 'mix_norm_g', 'w_in', 'pool_w', 'pool_scale', 'convb_dw_w', 'convb_dw_b', 'convb_ln_g',
            'convb_ln_b', 'convb_pw_w', 'rg_conv_w', 'rg_conv_b', 'rg_w_a', 'rg_b_a', 'rg_w_x', 'rg_b_x', 'rg_lambda',
            'w_out', 'mlp_norm_g', 'w_up', 'w_down', 'final_norm_g']


def _block_diag(w):
    L, H, C, _ = w.shape
    eye = jnp.eye(H, dtype=w.dtype)
    return (w[:, :, :, None, :] * eye[None, :, None, :, None]).reshape(L, H * C, H * C)


def _diag_blocks(m, H):
    C = m.shape[0] // H
    return jnp.concatenate([m[h * C:(h + 1) * C, h * C:(h + 1) * C] for h in range(H)], axis=0)


def kernel(x, meta_tokens, mix_norm_g, w_in, pool_w, pool_scale, convb_dw_w, convb_dw_b, convb_ln_g, convb_ln_b, convb_pw_w, rg_conv_w, rg_conv_b, rg_w_a, rg_b_a, rg_w_x, rg_b_x, rg_lambda, w_out, mlp_norm_g, w_up, w_down, final_norm_g, loss_target, m_meta_tokens, m_mix_norm_g, m_w_in, m_pool_w, m_pool_scale, m_convb_dw_w, m_convb_dw_b, m_convb_ln_g, m_convb_ln_b, m_convb_pw_w, m_rg_conv_w, m_rg_conv_b, m_rg_w_a, m_rg_b_a, m_rg_w_x, m_rg_b_x, m_rg_lambda, m_w_out, m_mlp_norm_g, m_w_up, m_w_down, m_final_norm_g, v_meta_tokens, v_mix_norm_g, v_w_in, v_pool_w, v_pool_scale, v_convb_dw_w, v_convb_dw_b, v_convb_ln_g, v_convb_ln_b, v_convb_pw_w, v_rg_conv_w, v_rg_conv_b, v_rg_w_a, v_rg_b_a, v_rg_w_x, v_rg_b_x, v_rg_lambda, v_w_out, v_mlp_norm_g, v_w_up, v_w_down, v_final_norm_g):
    W = dict(meta_tokens=meta_tokens, mix_norm_g=mix_norm_g, w_in=w_in, pool_w=pool_w, pool_scale=pool_scale,
             convb_dw_w=convb_dw_w, convb_dw_b=convb_dw_b, convb_ln_g=convb_ln_g, convb_ln_b=convb_ln_b,
             convb_pw_w=convb_pw_w, rg_conv_w=rg_conv_w, rg_conv_b=rg_conv_b, rg_w_a=rg_w_a, rg_b_a=rg_b_a,
             rg_w_x=rg_w_x, rg_b_x=rg_b_x, rg_lambda=rg_lambda, w_out=w_out, mlp_norm_g=mlp_norm_g, w_up=w_up,
             w_down=w_down, final_norm_g=final_norm_g.reshape(1, -1))
    M = dict(meta_tokens=m_meta_tokens, mix_norm_g=m_mix_norm_g, w_in=m_w_in, pool_w=m_pool_w, pool_scale=m_pool_scale,
             convb_dw_w=m_convb_dw_w, convb_dw_b=m_convb_dw_b, convb_ln_g=m_convb_ln_g, convb_ln_b=m_convb_ln_b,
             convb_pw_w=m_convb_pw_w, rg_conv_w=m_rg_conv_w, rg_conv_b=m_rg_conv_b, rg_w_a=m_rg_w_a, rg_b_a=m_rg_b_a,
             rg_w_x=m_rg_w_x, rg_b_x=m_rg_b_x, rg_lambda=m_rg_lambda, w_out=m_w_out, mlp_norm_g=m_mlp_norm_g,
             w_up=m_w_up, w_down=m_w_down, final_norm_g=m_final_norm_g.reshape(1, -1))
    V = dict(meta_tokens=v_meta_tokens, mix_norm_g=v_mix_norm_g, w_in=v_w_in, pool_w=v_pool_w, pool_scale=v_pool_scale,
             convb_dw_w=v_convb_dw_w, convb_dw_b=v_convb_dw_b, convb_ln_g=v_convb_ln_g, convb_ln_b=v_convb_ln_b,
             convb_pw_w=v_convb_pw_w, rg_conv_w=v_rg_conv_w, rg_conv_b=v_rg_conv_b, rg_w_a=v_rg_w_a, rg_b_a=v_rg_b_a,
             rg_w_x=v_rg_w_x, rg_b_x=v_rg_b_x, rg_lambda=v_rg_lambda, w_out=v_w_out, mlp_norm_g=v_mlp_norm_g,
             w_up=v_w_up, w_down=v_w_down, final_norm_g=v_final_norm_g.reshape(1, -1))

    xs = x[0]
    S, D = xs.shape
    T = S + N_META
    assert T % TT == 0 and D == 1024
    L = DEPTH
    me = 4 * lax.axis_index("x") + 2 * lax.axis_index("y") + lax.axis_index("c")
    c_in, c_ff = w_in.shape[-1], w_up.shape[-1]
    d_in, d_ff = c_in * N_DEV, c_ff * N_DEV
    r_out, r_dn = w_out.shape[1], w_down.shape[1]
    by_slot = lambda ref, s: ref.at[s]
    rows_of = lambda n: (lambda ref, s: ref.at[pl.ds(s * n, n), :])
    cols_of = lambda n: (lambda ref, s: ref.at[:, pl.ds(s * n, n)])

    win_t = jnp.transpose(w_in, (0, 2, 1)).astype(BF16)
    gathered = _allgather([
        (win_t[0], (d_in, D), rows_of(c_in)),
        (meta_tokens, (N_DEV,) + meta_tokens.shape, by_slot),
        (convb_dw_w, (N_DEV,) + convb_dw_w.shape, by_slot),
        (convb_pw_w, (N_DEV,) + convb_pw_w.shape, by_slot),
        (rg_conv_w, (N_DEV,) + rg_conv_w.shape, by_slot),
    ], "gather_first")
    win_f = [gathered[0], None]
    meta_f = jnp.transpose(gathered[1], (1, 0, 2)).reshape(N_META, D)
    dww_f = jnp.transpose(gathered[2], (1, 2, 0, 3)).reshape(L, CONV_K, D_CONV)
    wpw_f = jnp.transpose(gathered[3], (1, 0, 2, 3)).reshape(L, D_CONV, D_CONV).astype(BF16)
    rgw_f = jnp.transpose(gathered[4], (1, 2, 0, 3)).reshape(L, RG_CONV_K, D_RNN)
    wout_b, wup_b, wdown_b = w_out.astype(BF16), w_up.astype(BF16), w_down.astype(BF16)

    def weight_gather(blocks):
        comm = _Comm()
        for blk, shape, place in blocks:
            comm.gather(blk, shape, place)
        return comm

    wpool_bd = _block_diag(pool_w).astype(BF16)
    wa_bd = _block_diag(rg_w_a).astype(BF16)
    wx_bd = _block_diag(rg_w_x).astype(BF16)
    zeros4 = jnp.zeros((L, 4, D_POOL), F32)
    v256 = jnp.concatenate([pool_scale[:, None], convb_dw_b[:, None], convb_ln_g[:, None], convb_ln_b[:, None], zeros4], axis=1)
    dww_p = jnp.concatenate([dww_f, jnp.zeros((L, 1, D_CONV), F32)], axis=1)
    v512 = jnp.concatenate([rg_conv_b[:, None], rg_b_a[:, None], rg_b_x[:, None], rg_lambda[:, None], rgw_f], axis=1)
    mix_w = (wpool_bd, v256, dww_p, wpw_f, wa_bd, wx_bd, v512)

    h = jnp.concatenate([meta_f, xs], axis=0)
    tgt = jnp.concatenate([jnp.zeros((N_META, D), F32), loss_target[0]], axis=0)
    wout_f, wup_f, wdown_f = [None] * L, [None] * L, [None] * L
    saved = []

    (p,), wout_f = _rms_mm(h, mix_norm_g, win_f[0], 0, d_in, "in_proj0",
                           weight_gather([(wout_b[l], (r_out * N_DEV, D), rows_of(r_out)) for l in range(L)]),
                           w_is_nk=True)
    (y, hr), (wup_f[0],) = _mixer_fwd(p, mix_w, 0, weight_gather([(wup_b[0], (D, d_ff), cols_of(c_ff))]))
    (h2,), (win_f[1],) = _mm_res(h, y, wout_f[0], y.shape[-1], False, "out_proj0",
                                 weight_gather([(win_t[1], (d_in, D), rows_of(c_in))]))
    (z,), (wdown_f[0],) = _rms_mm(h2, mlp_norm_g, wup_f[0], 0, 2048, "mlp_up0",
                                  weight_gather([(wdown_b[0], (d_ff, D), rows_of(r_dn))]))
    (h3,), (wup_f[1],) = _mm_res(h2, z, wdown_f[0], 2048, True, "mlp_down0",
                                 weight_gather([(wup_b[1], (D, d_ff), cols_of(c_ff))]))
    saved.append((h, p, y, hr, h2, z))
    h = h3
    (p,), _ = _rms_mm(h, mix_norm_g, win_f[1], 1, d_in, "in_proj1", w_is_nk=True)
    (y, hr), (wdown_f[1],) = _mixer_fwd(p, mix_w, 1, weight_gather([(wdown_b[1], (d_ff, D), rows_of(r_dn))]))
    (h2,), _ = _mm_res(h, y, wout_f[1], y.shape[-1], False, "out_proj1")
    (z,), _ = _rms_mm(h2, mlp_norm_g, wup_f[1], 1, 2048, "mlp_up1")
    (h3,), _ = _mm_res(h2, z, wdown_f[1], 2048, True, "mlp_down1")
    saved.append((h, p, y, hr, h2, z))

    dh, g_fin, loss_part = _final_loss(h3, W["final_norm_g"], tgt)
    loss = lax.psum(loss_part[0, 0], ("x", "y", "c"))

    def grad_exchange(pieces):
        comm = _Comm()
        for g, send, shape in pieces:
            comm.add(g, send, comm.add_out((N_DEV,) + shape, BF16), by_slot)
        return comm

    def small_gather(comm, grads):
        for g in grads:
            comm.gather(g, (N_DEV,) + g.shape, by_slot)
        return comm

    recv = {n: [None] * L for n in ("w_in", "w_out", "w_up", "w_down")}
    small1 = gath1 = gath0 = None
    for l in reversed(range(L)):
        h0, p, y, hr, h2, z = saved[l]
        comm = small_gather(_Comm(), small1) if l == 0 else None
        (dz,), got = _bwd_dz(dh, wdown_f[l], z, 2048, f"mlp_down_bwd{l}", comm)
        if l == 0:
            gath1 = got
        (g_wdown,), _ = _mm_tn(z, dh, True, None, None, l, "a", 1024, BF16, f"mlp_down_wgrad{l}")
        (dh2, g_mlp), (recv["w_down"][l],) = _bwd_rms(
            dz, wup_f[l], h2, mlp_norm_g, dh, l, 2048, f"mlp_up_bwd{l}",
            grad_exchange([(g_wdown, rows_of(r_dn), (r_dn, D))]))
        (g_wup,), _ = _mm_tn(h2, dz, False, "a", mlp_norm_g, l, "b", 1024, BF16, f"mlp_up_wgrad{l}")
        (dy,), _ = _bwd_plain(dh2, wout_f[l], f"out_proj_bwd{l}")
        (g_wout,), _ = _mm_tn(y, dh2, False, None, None, l, "b", D, BF16, f"out_proj_wgrad{l}")
        (dp, g256, g512, gpool, gpw, gwa, gwx), (recv["w_up"][l],) = _mixer_bwd(
            p, hr, dy, mix_w, l, grad_exchange([(g_wup, cols_of(c_ff), (D, c_ff))]))
        smalls = [g_mlp, _diag_blocks(gpool, 4), g256, g512, gpw, _diag_blocks(gwa, RG_HEADS), _diag_blocks(gwx, RG_HEADS)]
        comm = grad_exchange([(g_wout, rows_of(r_out), (r_out, D))])
        if l == 0:
            small_gather(comm, smalls)
        (g_win_t,), got = _mm_tn(dp, h0, False, "b", mix_norm_g, l, "a", d_in // 2, BF16, f"in_proj_wgrad{l}", comm)
        recv["w_out"][l] = got[0]
        if l == 0:
            gath0 = got[1:]
        (dh, g_mix), (recv["w_in"][l],) = _bwd_rms(
            dp, win_f[l], h0, mix_norm_g, dh2, l, d_in, f"in_proj_bwd{l}",
            grad_exchange([(g_win_t, rows_of(c_in), (c_in, D))]), w_is_kd=True)
        if l == 1:
            small1 = [g_mix] + smalls + [g_fin]
    grad_x = dh[N_META:][None]

    gath_tail = _comm_only(small_gather(_Comm(), [g_mix, dh[:N_META]]), "exchange_tail")

    out = {}
    for name, tr in (("w_out", r_out), ("w_up", 256), ("w_down", 128)):
        out[name] = _adamw_big(recv[name], W[name], M[name], V[name], tr, f"adamw_{name}")
    g_win = jnp.transpose(_sum_slots(recv["w_in"], "sum_w_in"), (0, 2, 1))
    out["w_in"] = (g_win,) + tuple(_adamw_plain([g_win], [W["w_in"]], [M["w_in"]], [V["w_in"]], "adamw_w_in"))
    sums, small_out = _adamw_small(gath1, gath0, gath_tail, [W[n] for n in _SMALL_REPL], [M[n] for n in _SMALL_REPL],
                                   [V[n] for n in _SMALL_REPL])
    for a, name in enumerate(_SMALL_REPL):
        out[name] = small_out[4 * a:4 * a + 4]
    s_256, s_512, s_pw, s_meta = sums[4], sums[5], sums[6], sums[9]
    g_shard = {
        "meta_tokens": lax.dynamic_slice_in_dim(s_meta, me * 128, 128, axis=1),
        "convb_dw_w": lax.dynamic_slice_in_dim(s_256[:, :CONV_K, :], me * 32, 32, axis=2),
        "convb_pw_w": lax.dynamic_slice_in_dim(s_pw, me * 32, 32, axis=1),
        "rg_conv_w": lax.dynamic_slice_in_dim(s_512[:, 4:8, :], me * 64, 64, axis=2),
    }
    names = list(g_shard)
    res = _adamw_plain([g_shard[n] for n in names], [W[n] for n in names], [M[n] for n in names],
                       [V[n] for n in names], "adamw_small_sharded")
    for a, name in enumerate(names):
        out[name] = (g_shard[name],) + tuple(res[3 * a:3 * a + 3])

    def fix(name, arr):
        return arr.reshape(-1) if name == "final_norm_g" else arr

    return (loss, grad_x,
            *[fix(n, out[n][0]) for n in _WEIGHTS], *[fix(n, out[n][1]) for n in _WEIGHTS],
            *[fix(n, out[n][2]) for n in _WEIGHTS], *[fix(n, out[n][3]) for n in _WEIGHTS])
```

```python
import functools
import operator

import jax
import jax.numpy as jnp
from jax import lax
from jax.experimental import pallas as pl
from jax.experimental.pallas import tpu as pltpu

F32 = jnp.float32
BF16 = jnp.bfloat16

N_DEV = 8
N_META = 16
DEPTH = 2
EPS = 1e-6
POOL_WINDOWS = (2, 4, 8, 16)
POOL_GW = 64
CONV_K = 31
RG_CONV_K = 4
RG_HEADS = 8
RG_HD = 64
RG_C = 8.0
D_POOL = 256
D_CONV = 256
D_RNN = 512

ADAM_LR = 0.001
ADAM_B1 = 0.9
ADAM_B2 = 0.999
ADAM_EPS = 1e-08
ADAM_WD = 0.01
ADAM_STEP = 10

TT = 432
TD = 912
HALO = 48
VMEM_LIMIT = 56 * 1024 * 1024

MESH = pl.DeviceIdType.MESH
ANY = pl.BlockSpec(memory_space=pl.ANY)


def _cparams(*sem):
    return pltpu.CompilerParams(dimension_semantics=sem, vmem_limit_bytes=VMEM_LIMIT)


def _full(shape):
    nd = len(shape)
    return pl.BlockSpec(shape, lambda *_: (0,) * nd)


def _layer(shape, l):
    nd = len(shape)
    return pl.BlockSpec((None,) + tuple(shape), lambda *_: (l,) + (0,) * nd)


def _dot(a, b):
    return jnp.dot(a, b, preferred_element_type=F32)


def _dot_nt(a, b):
    return lax.dot_general(a, b, (((1,), (1,)), ((), ())), preferred_element_type=F32)


def _dot_tn(a, b):
    return lax.dot_general(a, b, (((0,), (0,)), ((), ())), preferred_element_type=F32)


def _rms(h):
    r = lax.rsqrt(jnp.mean(h * h, axis=-1, keepdims=True) + EPS)
    return h * r, r


def _rms_bwd(du, xh, r, g):
    dxh = du * g
    return r * (dxh - xh * jnp.mean(dxh * xh, axis=-1, keepdims=True))


def _sigmoid(x):
    return 1.0 / (1.0 + jnp.exp(-x))


def _colsum(x):
    return jnp.sum(x, axis=0, keepdims=True)


def _mesh_pos():
    return lax.axis_index("x"), lax.axis_index("y"), lax.axis_index("c")


def _slot(d):
    return 4 * d[0] + 2 * d[1] + d[2]


class _Comm:
    def __init__(self):
        self.srcs, self.send, self.dst, self.land, self.out_shapes = [], [], [], [], []

    def add_out(self, shape, dtype):
        self.out_shapes.append(jax.ShapeDtypeStruct(shape, dtype))
        return len(self.out_shapes) - 1

    def add(self, src, send, dst, land):
        self.srcs.append(src)
        self.send.append(send)
        self.dst.append(dst)
        self.land.append(land)

    def gather(self, block, out_shape, place):
        self.add(block, lambda ref, s: ref, self.add_out(out_shape, block.dtype), place)

    def sem_shapes(self):
        n = len(self.srcs)
        return [pltpu.SemaphoreType.DMA((7 * n,)), pltpu.SemaphoreType.DMA((7 * n,)), pltpu.SemaphoreType.DMA((n,))]

    def _copies(self, cins, couts, send_sems, recv_sems, local_sems, with_arrivals=True):
        x, y, c = _mesh_pos()
        me = (x, y, c)
        sends, arrivals, local = [], [], []
        for a in range(len(self.srcs)):
            out = couts[self.dst[a]]
            mine = self.land[a](out, _slot(me))
            local.append(pltpu.make_async_copy(self.send[a](cins[a], _slot(me)), mine, local_sems.at[a]))
            for k in range(N_DEV - 1):
                d = k + 1
                peer = (1 - x if (d >> 2) & 1 else x, 1 - y if (d >> 1) & 1 else y, 1 - c if d & 1 else c)
                sems = dict(send_sem=send_sems.at[a * 7 + k], recv_sem=recv_sems.at[a * 7 + k],
                            device_id=peer, device_id_type=MESH)
                sends.append(pltpu.make_async_remote_copy(
                    src_ref=self.send[a](cins[a], _slot(peer)), dst_ref=mine, **sems))
                if with_arrivals:
                    theirs = self.land[a](out, _slot(peer))
                    arrivals.append(pltpu.make_async_remote_copy(src_ref=theirs, dst_ref=theirs, **sems))
        return sends, arrivals, local

    def start(self, *refs):
        sends, _, local = self._copies(*refs, with_arrivals=False)
        for cp in local + sends:
            cp.start()

    def wait(self, *refs):
        sends, arrivals, local = self._copies(*refs)
        for cp in arrivals:
            cp.wait_recv()
        for cp in sends:
            cp.wait_send()
        for cp in local:
            cp.wait()


def _call(body, *, name, grid, in_specs, out_specs, out_shape, operands, scratch_shapes=(), sem=None, comm=None):
    out_specs, out_shape = list(out_specs), list(out_shape)
    if comm is None:
        res = pl.pallas_call(body, name=name, grid=grid, in_specs=list(in_specs), out_specs=out_specs,
                             out_shape=out_shape, scratch_shapes=list(scratch_shapes),
                             compiler_params=_cparams(*sem))(*operands)
        return list(res), []
    n_in, n_out, n_scr = len(in_specs), len(out_specs), len(scratch_shapes)
    nc_in, nc_out = len(comm.srcs), len(comm.out_shapes)

    def hosted(*refs):
        ins, cins = refs[:n_in], refs[n_in:n_in + nc_in]
        o0 = n_in + nc_in
        outs, couts = refs[o0:o0 + n_out], refs[o0 + n_out:o0 + n_out + nc_out]
        s0 = o0 + n_out + nc_out
        scr, sems = refs[s0:s0 + n_scr], refs[s0 + n_scr:]
        ids = [pl.program_id(a) for a in range(len(grid))]
        first = functools.reduce(operator.and_, [i == 0 for i in ids])
        last = functools.reduce(operator.and_, [i == g - 1 for i, g in zip(ids, grid)])

        @pl.when(first)
        def _():
            comm.start(cins, couts, *sems)

        body(*ins, *outs, *scr)

        @pl.when(last)
        def _():
            comm.wait(cins, couts, *sems)

    res = pl.pallas_call(
        hosted, name=name, grid=grid, in_specs=list(in_specs) + [ANY] * nc_in,
        out_specs=out_specs + [ANY] * nc_out, out_shape=out_shape + comm.out_shapes,
        scratch_shapes=list(scratch_shapes) + comm.sem_shapes(),
        compiler_params=_cparams(*(["arbitrary"] * len(grid))))(*operands, *comm.srcs)
    return list(res[:n_out]), list(res[n_out:])


def _comm_only(comm, name):
    def body(*refs):
        n, m = len(comm.srcs), len(comm.out_shapes)
        comm.start(refs[:n], refs[n:n + m], *refs[n + m:])
        comm.wait(refs[:n], refs[n:n + m], *refs[n + m:])

    return list(pl.pallas_call(body, name=name, in_specs=[ANY] * len(comm.srcs), out_specs=[ANY] * len(comm.out_shapes),
                               out_shape=comm.out_shapes, scratch_shapes=comm.sem_shapes())(*comm.srcs))


def _allgather(items, name):
    n = len(items)

    def body(*refs):
        ins, outs = refs[:n], refs[n:2 * n]
        send_sems, recv_sems, local_sems = refs[2 * n:]
        x, y, c = _mesh_pos()
        me, sib = (x, y, c), (x, y, 1 - c)
        chips = [(1 - x, y), (x, 1 - y), (1 - x, 1 - y)]

        def copy(a, k, blk, to, from_input=False):
            region = items[a][2](outs[a], _slot(blk))
            return pltpu.make_async_remote_copy(
                src_ref=ins[a] if from_input else region, dst_ref=region,
                send_sem=send_sems.at[a * 7 + k], recv_sem=recv_sems.at[a * 7 + k],
                device_id=to, device_id_type=MESH)

        local = [pltpu.make_async_copy(ins[a], items[a][2](outs[a], _slot(me)), local_sems.at[a]) for a in range(n)]
        for cp in local:
            cp.start()
        first = []
        for a in range(n):
            first.append(copy(a, 0, me, sib, True))
            for j, chip in enumerate(chips):
                first.append(copy(a, 1 + j, me, (*chip, c), True))
        for cp in first:
            cp.start()
        passed = []
        for j, chip in enumerate(chips):
            for a in range(n):
                copy(a, 1 + j, (*chip, c), me).wait_recv()
                fwd = copy(a, 4 + j, (*chip, c), sib)
                fwd.start()
                passed.append(fwd)
        for a in range(n):
            copy(a, 0, sib, me).wait_recv()
            for j, chip in enumerate(chips):
                copy(a, 4 + j, (*chip, 1 - c), me).wait_recv()
        for cp in first + passed:
            cp.wait_send()
        for cp in local:
            cp.wait()

    return pl.pallas_call(
        body, name=name,
        in_specs=[ANY] * n, out_specs=[ANY] * n,
        out_shape=[jax.ShapeDtypeStruct(it[1], it[0].dtype) for it in items],
        scratch_shapes=[pltpu.SemaphoreType.DMA((7 * n,)), pltpu.SemaphoreType.DMA((7 * n,)),
                        pltpu.SemaphoreType.DMA((n,))],
    )(*[it[0] for it in items])


def _rms_mm(h, g, w, l, ck, name, comm=None, w_is_nk=False, out_dtype=F32):
    T, D = h.shape
    N = w.shape[0] if w_is_nk else w.shape[1]
    normed = h.dtype == BF16

    def body(h_ref, g_ref, w_ref, o_ref):
        if normed:
            u = h_ref[...]
        else:
            u = (_rms(h_ref[...])[0] * g_ref[l:l + 1, :]).astype(BF16)
        res = _dot_nt(u, w_ref[...]) if w_is_nk else _dot(u, w_ref[...])
        o_ref[...] = res.astype(out_dtype)

    w_spec = pl.BlockSpec((ck, D), lambda j, i: (j, 0)) if w_is_nk else pl.BlockSpec((D, ck), lambda j, i: (0, j))
    return _call(
        body, name=name, grid=(N // ck, T // TD),
        in_specs=[pl.BlockSpec((TD, D), lambda j, i: (i, 0)), _full(g.shape), w_spec],
        out_specs=[pl.BlockSpec((TD, ck), lambda j, i: (i, j))],
        out_shape=[jax.ShapeDtypeStruct((T, N), out_dtype)],
        operands=(h, g, w), sem=("parallel", "parallel"), comm=comm)


def _mm_res(res, a, w, ck, relu2, name, comm=None, norm=None):
    T, D = res.shape
    K = a.shape[-1]
    nk = K // ck

    def body(*refs):
        if norm is None:
            r_ref, a_ref, w_ref, o_ref = refs
        else:
            r_ref, a_ref, w_ref, g_ref, o_ref, u_ref = refs
        k = pl.program_id(1)

        @pl.when(k == 0)
        def _():
            o_ref[...] = r_ref[...]

        av = a_ref[...]
        if relu2:
            av = jnp.square(jnp.maximum(av.astype(F32), 0.0))
        o_ref[...] += _dot(av.astype(BF16), w_ref[...])

        if norm is not None:
            @pl.when(k == nk - 1)
            def _():
                u_ref[...] = (_rms(o_ref[...])[0] * g_ref[norm[1]:norm[1] + 1, :]).astype(BF16)

    row = pl.BlockSpec((TD, D), lambda i, k: (i, 0))
    in_specs = [row, pl.BlockSpec((TD, ck), lambda i, k: (i, k)), pl.BlockSpec((ck, D), lambda i, k: (k, 0))]
    operands = (res, a, w)
    out_specs, out_shape = [row], [jax.ShapeDtypeStruct((T, D), F32)]
    if norm is not None:
        in_specs.append(_full(norm[0].shape))
        operands += (norm[0],)
        out_specs.append(row)
        out_shape.append(jax.ShapeDtypeStruct((T, D), BF16))
    return _call(
        body, name=name, grid=(T // TD, nk), in_specs=in_specs, out_specs=out_specs, out_shape=out_shape,
        operands=operands, sem=("parallel", "arbitrary"), comm=comm)


def _final_loss(h, gf, tgt):
    T, D = h.shape

    def body(h_ref, g_ref, t_ref, dh_ref, dhb_ref, dg_ref, loss_ref):
        i = pl.program_id(0)

        @pl.when(i == 0)
        def _():
            dg_ref[...] = jnp.zeros_like(dg_ref)
            loss_ref[...] = jnp.zeros_like(loss_ref)

        g = g_ref[...]
        xh, r = _rms(h_ref[...])
        row = i * TD + lax.broadcasted_iota(jnp.int32, (TD, 1), 0)
        keep = (row >= N_META).astype(F32)
        diff = (xh * g - t_ref[...]) * keep
        part = jnp.sum(jnp.sum(diff * diff, axis=1, keepdims=True), axis=0, keepdims=True)
        loss_ref[...] += (0.5 / D) * part
        dy = diff * (1.0 / D)
        dg_ref[...] += _colsum(dy * xh)
        dh = _rms_bwd(dy, xh, r, g)
        dh_ref[...] = dh
        dhb_ref[...] = dh.astype(BF16)

    row = pl.BlockSpec((TD, D), lambda i: (i, 0))
    return pl.pallas_call(
        body, name="final_loss", grid=(T // TD,),
        in_specs=[row, _full(gf.shape), row],
        out_specs=[row, row, _full((1, D)), _full((8, 128))],
        out_shape=[jax.ShapeDtypeStruct((T, D), F32), jax.ShapeDtypeStruct((T, D), BF16),
                   jax.ShapeDtypeStruct((1, D), F32), jax.ShapeDtypeStruct((8, 128), F32)],
        compiler_params=_cparams("arbitrary"))(h, gf, tgt)


def _bwd_dz(dh, wdown, z, ck, name, comm=None):
    T, D = dh.shape
    F = z.shape[-1]

    def body(d_ref, w_ref, z_ref, o_ref):
        dact = _dot_nt(d_ref[...], w_ref[...])
        o_ref[...] = (dact * (2.0 * jnp.maximum(z_ref[...].astype(F32), 0.0))).astype(BF16)

    return _call(
        body, name=name, grid=(F // ck, T // TD),
        in_specs=[pl.BlockSpec((TD, D), lambda j, i: (i, 0)), pl.BlockSpec((ck, D), lambda j, i: (j, 0)),
                  pl.BlockSpec((TD, ck), lambda j, i: (i, j))],
        out_specs=[pl.BlockSpec((TD, ck), lambda j, i: (i, j))],
        out_shape=[jax.ShapeDtypeStruct((T, F), BF16)],
        operands=(dh, wdown, z), sem=("parallel", "parallel"), comm=comm)


def _bwd_plain(dh, w, name, comm=None):
    T, D = dh.shape
    K = w.shape[0]

    def body(d_ref, w_ref, o_ref):
        o_ref[...] = _dot_nt(d_ref[...], w_ref[...])

    return _call(
        body, name=name, grid=(T // TD,),
        in_specs=[pl.BlockSpec((TD, D), lambda i: (i, 0)), _full((K, D))],
        out_specs=[pl.BlockSpec((TD, K), lambda i: (i, 0))],
        out_shape=[jax.ShapeDtypeStruct((T, K), F32)],
        operands=(dh, w), sem=("parallel",), comm=comm)


def _bwd_rms(dx, w, h, g, dh_in, l, ck, name, comm=None, w_is_kd=False):
    T, D = h.shape
    K = dx.shape[-1]
    nk = K // ck
    w_spec = pl.BlockSpec((ck, D), lambda i, k: (k, 0)) if w_is_kd else pl.BlockSpec((D, ck), lambda i, k: (0, k))

    def body(x_ref, w_ref, h_ref, g_ref, di_ref, dh_ref, dhb_ref, dg_ref, acc_ref):
        i, k = pl.program_id(0), pl.program_id(1)

        @pl.when((i == 0) & (k == 0))
        def _():
            dg_ref[...] = jnp.zeros_like(dg_ref)

        @pl.when(k == 0)
        def _():
            acc_ref[...] = jnp.zeros_like(acc_ref)

        acc_ref[...] += _dot(x_ref[...], w_ref[...]) if w_is_kd else _dot_nt(x_ref[...], w_ref[...])

        @pl.when(k == nk - 1)
        def _():
            du = acc_ref[...]
            xh, r = _rms(h_ref[...])
            dg_ref[...] += _colsum(du * xh)
            dh = di_ref[...] + _rms_bwd(du, xh, r, g_ref[l:l + 1, :])
            dh_ref[...] = dh
            dhb_ref[...] = dh.astype(BF16)

    row = pl.BlockSpec((TD, D), lambda i, k: (i, 0))
    return _call(
        body, name=name, grid=(T // TD, nk),
        in_specs=[pl.BlockSpec((TD, ck), lambda i, k: (i, k)), w_spec, row, _full(g.shape), row],
        out_specs=[row, row, _full((1, D))],
        out_shape=[jax.ShapeDtypeStruct((T, D), F32), jax.ShapeDtypeStruct((T, D), BF16),
                   jax.ShapeDtypeStruct((1, D), F32)],
        scratch_shapes=[pltpu.VMEM((TD, D), F32)],
        operands=(dx, w, h, g, dh_in), sem=("arbitrary", "arbitrary"), comm=comm)


def _mm_tn(a, b, relu2, rms_on, g, l, chunk, ck, out_dtype, name, comm=None):
    T = a.shape[0]
    Ka, Nb = a.shape[1], b.shape[1]
    ca, cb = (ck, Nb) if chunk == "a" else (Ka, ck)
    nj = (Ka // ck) if chunk == "a" else (Nb // ck)
    nt = T // TD

    def body(*refs):
        if rms_on is not None:
            a_ref, g_ref, b_ref, o_ref, acc_ref = refs
        else:
            a_ref, b_ref, o_ref, acc_ref = refs
        t = pl.program_id(1)

        @pl.when(t == 0)
        def _():
            acc_ref[...] = jnp.zeros_like(acc_ref)

        av, bv = a_ref[...], b_ref[...]
        if relu2:
            av = jnp.square(jnp.maximum(av.astype(F32), 0.0))
        if rms_on == "a":
            av = _rms(av)[0] * g_ref[l:l + 1, :]
        elif rms_on == "b":
            bv = _rms(bv)[0] * g_ref[l:l + 1, :]
        acc_ref[...] += _dot_tn(av.astype(BF16), bv.astype(BF16))

        @pl.when(t == nt - 1)
        def _():
            o_ref[...] = acc_ref[...].astype(out_dtype)

    if chunk == "a":
        a_spec = pl.BlockSpec((TD, ca), lambda j, t: (t, j))
        b_spec = pl.BlockSpec((TD, cb), lambda j, t: (t, 0))
        o_spec = pl.BlockSpec((ca, cb), lambda j, t: (j, 0))
    else:
        a_spec = pl.BlockSpec((TD, ca), lambda j, t: (t, 0))
        b_spec = pl.BlockSpec((TD, cb), lambda j, t: (t, j))
        o_spec = pl.BlockSpec((ca, cb), lambda j, t: (0, j))
    ins, specs = [a], [a_spec]
    if rms_on is not None:
        ins.append(g)
        specs.append(_full(g.shape))
    ins.append(b)
    specs.append(b_spec)
    return _call(
        body, name=name, grid=(nj, nt), in_specs=specs, out_specs=[o_spec],
        out_shape=[jax.ShapeDtypeStruct((Ka, Nb), out_dtype)], scratch_shapes=[pltpu.VMEM((ca, cb), F32)],
        operands=tuple(ins), sem=("parallel", "arbitrary"), comm=comm)


def _shift_down(x, k):
    return x if k == 0 else pltpu.roll(x, k, 0)


def _shift_up(x, k):
    return x if k == 0 else pltpu.roll(x, x.shape[0] - k, 0)


def _pool_sel(lane_grp, vals):
    return jnp.where(lane_grp == 0, vals[0], jnp.where(lane_grp == 1, vals[1], jnp.where(lane_grp == 2, vals[2], vals[3])))


def _pool_fwd(ue, pos):
    s = ue
    sums = []
    for k in (1, 2, 4, 8):
        s = s + _shift_down(s, k)
        sums.append(s[HALO:])
    grp = lax.broadcasted_iota(jnp.int32, (1, D_POOL), 1) // POOL_GW
    wsum = _pool_sel(grp, sums)
    width = _pool_sel(grp, [jnp.float32(w) for w in POOL_WINDOWS])
    cnt = jnp.minimum(pos.astype(F32), width)
    return wsum / cnt - ue[HALO:], cnt, grp


def _conv_fwd(xe, w_ref, nk):
    acc = None
    for k in range(nk):
        term = _shift_down(xe, nk - 1 - k)[HALO:] * w_ref(k)
        acc = term if acc is None else acc + term
    return acc


def _layernorm(cb):
    mu = jnp.mean(cb, axis=-1, keepdims=True)
    xc = cb - mu
    rstd = lax.rsqrt(jnp.mean(xc * xc, axis=-1, keepdims=True) + EPS)
    return xc * rstd, rstd


def _softplus_neg(lam):
    x = -lam
    e = jnp.exp(-jnp.abs(x))
    u = 1.0 + e
    d = u - 1.0
    log1p = jnp.where(d == 0.0, e, jnp.log(u) * (e / jnp.where(d == 0.0, 1.0, d)))
    return jnp.maximum(x, 0.0) + log1p


def _expm1(x):
    small = x * (1.0 + x * 0.5 * (1.0 + x * (1.0 / 3.0) * (1.0 + x * 0.25)))
    return jnp.where(jnp.abs(x) < 1e-2, small, jnp.exp(x) - 1.0)


def _gelu_parts(x):
    c0 = 0.7978845608028654
    inner = c0 * (x + 0.044715 * x * x * x)
    th = jnp.tanh(inner)
    gelu = 0.5 * x * (1.0 + th)
    dgelu = 0.5 * (1.0 + th) + 0.5 * x * (1.0 - th * th) * c0 * (1.0 + 3.0 * 0.044715 * x * x)
    return gelu, dgelu


def _rg_gates(xc, wa_ref, wx_ref, v512_ref):
    xb = xc.astype(BF16)
    ra = _sigmoid(_dot(xb, wa_ref[...]) + v512_ref[1:2, :])
    ri = _sigmoid(_dot(xb, wx_ref[...]) + v512_ref[2:3, :])
    sp = _softplus_neg(v512_ref[3:4, :])
    log_a = (-RG_C) * ra * sp
    a = jnp.exp(log_a)
    mult = jnp.sqrt(-_expm1(2.0 * log_a))
    return ra, ri, sp, a, mult


def _scan_down(a, b):
    n = a.shape[0]
    row = lax.broadcasted_iota(jnp.int32, (n, 1), 0)
    k = 1
    while k < n:
        ok = row >= k
        b = a * jnp.where(ok, _shift_down(b, k), 0.0) + b
        a = a * jnp.where(ok, _shift_down(a, k), 1.0)
        k *= 2
    return b, a


def _scan_up(c, g):
    n = c.shape[0]
    row = lax.broadcasted_iota(jnp.int32, (n, 1), 0)
    k = 1
    while k < n:
        ok = row < n - k
        g = c * jnp.where(ok, _shift_up(g, k), 0.0) + g
        c = c * jnp.where(ok, _shift_up(c, k), 1.0)
        k *= 2
    return g, c


def _mixer_specs(l, tile_of):
    r = TT // HALO
    return [
        pl.BlockSpec((TT, 1792), lambda i: (tile_of(i), 0)),
        pl.BlockSpec((HALO, 1792), lambda i: (jnp.maximum(tile_of(i) * r - 1, 0), 0)),
        _layer((256, 256), l), _layer((8, 256), l), _layer((32, 256), l), _layer((256, 256), l),
        _layer((512, 512), l), _layer((512, 512), l), _layer((8, 512), l),
    ]


def _mixer_fwd(p, wts, l, comm=None):
    T = p.shape[0]

    def body(p_ref, ph_ref, wpool_ref, v256_ref, dww_ref, wpw_ref, wa_ref, wx_ref, v512_ref, y_ref, hr_ref, carry_ref):
        i = pl.program_id(0)

        @pl.when(i == 0)
        def _():
            carry_ref[...] = jnp.zeros_like(carry_ref)

        halo = jnp.where(i == 0, 0.0, ph_ref[...])
        e = jnp.concatenate([halo, p_ref[...]], axis=0)
        pos = i * TT + 1 + lax.broadcasted_iota(jnp.int32, (TT, 1), 0)
        pooled, _, _ = _pool_fwd(e[:, 0:256], pos)
        y_a = _dot(pooled.astype(BF16), wpool_ref[...]) * v256_ref[0:1, :]
        ub = e[:, 256:512] * _sigmoid(e[:, 512:768])
        cb = _conv_fwd(ub, lambda k: dww_ref[k:k + 1, :], CONV_K) + v256_ref[1:2, :]
        xhat, _ = _layernorm(cb)
        ln = xhat * v256_ref[2:3, :] + v256_ref[3:4, :]
        s = ln * _sigmoid(ln)
        y_b = _dot(s.astype(BF16), wpw_ref[...])
        xc = _conv_fwd(e[:, 1280:1792], lambda k: v512_ref[4 + k:5 + k, :], RG_CONV_K) + v512_ref[0:1, :]
        _, ri, _, a, mult = _rg_gates(xc, wa_ref, wx_ref, v512_ref)
        hloc, acum = _scan_down(a, mult * ri * xc)
        hr = hloc + acum * carry_ref[0:1, :]
        hr_ref[...] = hr
        carry_ref[0:1, :] = hr_ref[TT - 1:TT, :]
        gelu, _ = _gelu_parts(p_ref[:, 768:1280])
        y_ref[:, 0:256] = y_a.astype(BF16)
        y_ref[:, 256:512] = y_b.astype(BF16)
        y_ref[:, 512:1024] = (gelu * hr).astype(BF16)

    return _call(
        body, name=f"mixer_fwd{l}", grid=(T // TT,),
        in_specs=_mixer_specs(l, lambda i: i),
        out_specs=[pl.BlockSpec((TT, 1024), lambda i: (i, 0)), pl.BlockSpec((TT, 512), lambda i: (i, 0))],
        out_shape=[jax.ShapeDtypeStruct((T, 1024), BF16), jax.ShapeDtypeStruct((T, 512), F32)],
        scratch_shapes=[pltpu.VMEM((8, 512), F32)],
        operands=(p, p, *wts), sem=("arbitrary",), comm=comm)


def _mixer_bwd(p, hr, dy, wts, l, comm=None):
    T = p.shape[0]
    nt = T // TT
    tile_of = lambda i: nt - 1 - i

    def body(p_ref, ph_ref, wpool_ref, v256_ref, dww_ref, wpw_ref, wa_ref, wx_ref, v512_ref, hr_ref, hrh_ref, dy_ref,
             dp_ref, g256_ref, g512_ref, gpool_ref, gpw_ref, gwa_ref, gwx_ref,
             q_c, dcb_c, dxc_c, ag_c):
        i = pl.program_id(0)
        j = nt - 1 - i

        @pl.when(i == 0)
        def _():
            for ref in (g256_ref, g512_ref, gpool_ref, gpw_ref, gwa_ref, gwx_ref, q_c, dcb_c, dxc_c, ag_c):
                ref[...] = jnp.zeros_like(ref)

        halo = jnp.where(j == 0, 0.0, ph_ref[...])
        e = jnp.concatenate([halo, p_ref[...]], axis=0)
        pos = j * TT + 1 + lax.broadcasted_iota(jnp.int32, (TT, 1), 0)
        row = lax.broadcasted_iota(jnp.int32, (TT, 1), 0)

        dy_a = dy_ref[:, 0:256]
        pooled, cnt, grp = _pool_fwd(e[:, 0:256], pos)
        scale = v256_ref[0:1, :]
        mixed = _dot(pooled.astype(BF16), wpool_ref[...])
        g256_ref[34:35, :] += _colsum(dy_a * mixed)
        dmixed = (dy_a * scale).astype(BF16)
        gpool_ref[...] += _dot_tn(pooled.astype(BF16), dmixed)
        dpooled = _dot_nt(dmixed, wpool_ref[...])
        q = dpooled / cnt
        s = jnp.concatenate([q, q_c[...]], axis=0)
        sums = []
        for k in (1, 2, 4, 8):
            s = s + _shift_up(s, k)
            sums.append(s[:TT])
        dp_ref[:, 0:256] = (_pool_sel(grp, sums) - dpooled).astype(BF16)
        q_c[...] = q[0:16]

        dy_b = dy_ref[:, 256:512].astype(BF16)
        sg = _sigmoid(e[:, 512:768])
        ub = e[:, 256:512] * sg
        cb = _conv_fwd(ub, lambda k: dww_ref[k:k + 1, :], CONV_K) + v256_ref[1:2, :]
        xhat, rstd = _layernorm(cb)
        ln_g = v256_ref[2:3, :]
        ln = xhat * ln_g + v256_ref[3:4, :]
        sig = _sigmoid(ln)
        sact = ln * sig
        gpw_ref[...] += _dot_tn(sact.astype(BF16), dy_b)
        dln = _dot_nt(dy_b, wpw_ref[...]) * (sig * (1.0 + ln * (1.0 - sig)))
        g256_ref[32:33, :] += _colsum(dln * xhat)
        g256_ref[33:34, :] += _colsum(dln)
        dxhat = dln * ln_g
        dcb = rstd * (dxhat - jnp.mean(dxhat, axis=-1, keepdims=True)
                      - xhat * jnp.mean(dxhat * xhat, axis=-1, keepdims=True))
        g256_ref[31:32, :] += _colsum(dcb)
        for k in range(CONV_K):
            g256_ref[k:k + 1, :] += _colsum(dcb * _shift_down(ub, CONV_K - 1 - k)[HALO:])
        ed = jnp.concatenate([dcb, dcb_c[...]], axis=0)
        dub = None
        for k in range(CONV_K):
            term = _shift_up(ed, CONV_K - 1 - k)[:TT] * dww_ref[k:k + 1, :]
            dub = term if dub is None else dub + term
        dcb_c[...] = dcb[0:32]
        sg_t = sg[HALO:]
        v_t = p_ref[:, 256:512]
        dp_ref[:, 256:512] = (dub * sg_t).astype(BF16)
        dp_ref[:, 512:768] = (dub * v_t * sg_t * (1.0 - sg_t)).astype(BF16)

        dy_c = dy_ref[:, 512:1024]
        cxe = e[:, 1280:1792]
        xc = _conv_fwd(cxe, lambda k: v512_ref[4 + k:5 + k, :], RG_CONV_K) + v512_ref[0:1, :]
        ra, ri, sp, a, mult = _rg_gates(xc, wa_ref, wx_ref, v512_ref)
        hrv = hr_ref[...]
        gelu, dgelu = _gelu_parts(p_ref[:, 768:1280])
        dp_ref[:, 768:1280] = (dy_c * hrv * dgelu).astype(BF16)
        dhr = dy_c * gelu
        coef = jnp.where(row < TT - 1, _shift_up(a, 1), 1.0)
        gloc, ccum = _scan_up(coef, dhr)
        gg = gloc + ccum * ag_c[0:1, :]
        ag_c[...] = (a * gg)[0:8]
        hr_before = jnp.where(j == 0, 0.0, hrh_ref[7:8, :])
        hr_prev = jnp.where(row >= 1, _shift_down(hrv, 1), hr_before)
        da = gg * hr_prev
        dmult = gg * ri * xc
        dri = gg * mult * xc
        dxc = gg * mult * ri
        dlog_a = da * a - dmult * (a * a / mult)
        g512_ref[3:4, :] += _colsum(dlog_a * ((-RG_C) * ra))
        dpre_a = dlog_a * ((-RG_C) * sp) * ra * (1.0 - ra)
        dpre_x = dri * ri * (1.0 - ri)
        g512_ref[1:2, :] += _colsum(dpre_a)
        g512_ref[2:3, :] += _colsum(dpre_x)
        xb = xc.astype(BF16)
        dpa_b, dpx_b = dpre_a.astype(BF16), dpre_x.astype(BF16)
        gwa_ref[...] += _dot_tn(xb, dpa_b)
        gwx_ref[...] += _dot_tn(xb, dpx_b)
        dxc = dxc + _dot_nt(dpa_b, wa_ref[...]) + _dot_nt(dpx_b, wx_ref[...])
        g512_ref[0:1, :] += _colsum(dxc)
        for k in range(RG_CONV_K):
            g512_ref[4 + k:5 + k, :] += _colsum(dxc * _shift_down(cxe, RG_CONV_K - 1 - k)[HALO:])
        ex = jnp.concatenate([dxc, dxc_c[...]], axis=0)
        dcx = None
        for k in range(RG_CONV_K):
            term = _shift_up(ex, RG_CONV_K - 1 - k)[:TT] * v512_ref[4 + k:5 + k, :]
            dcx = term if dcx is None else dcx + term
        dxc_c[...] = dxc[0:8]
        dp_ref[:, 1280:1792] = dcx.astype(BF16)

        @pl.when(i == nt - 1)
        def _():
            lam = v512_ref[3:4, :]
            g512_ref[3:4, :] = g512_ref[3:4, :] * (-_sigmoid(-lam))

    in_specs = _mixer_specs(l, tile_of) + [
        pl.BlockSpec((TT, 512), lambda i: (tile_of(i), 0)),
        pl.BlockSpec((8, 512), lambda i: (jnp.maximum(tile_of(i) * (TT // 8) - 1, 0), 0)),
        pl.BlockSpec((TT, 1024), lambda i: (tile_of(i), 0)),
    ]
    acc_shapes = [(40, 256), (8, 512), (256, 256), (256, 256), (512, 512), (512, 512)]
    return _call(
        body, name=f"mixer_bwd{l}", grid=(nt,),
        in_specs=in_specs,
        out_specs=[pl.BlockSpec((TT, 1792), lambda i: (tile_of(i), 0))] + [_full(s) for s in acc_shapes],
        out_shape=[jax.ShapeDtypeStruct((T, 1792), BF16)] + [jax.ShapeDtypeStruct(s, F32) for s in acc_shapes],
        scratch_shapes=[pltpu.VMEM((16, 256), F32), pltpu.VMEM((32, 256), F32), pltpu.VMEM((8, 512), F32),
                        pltpu.VMEM((8, 512), F32)],
        operands=(p, p, *wts, hr, hr, dy), sem=("arbitrary",), comm=comm)


def _adam(w, g, m, v):
    m = ADAM_B1 * m + (1.0 - ADAM_B1) * g
    v = ADAM_B2 * v + (1.0 - ADAM_B2) * (g * g)
    m_hat = m / (1.0 - ADAM_B1 ** ADAM_STEP)
    v_hat = v / (1.0 - ADAM_B2 ** ADAM_STEP)
    delta = -ADAM_LR * (m_hat / (jnp.sqrt(v_hat) + ADAM_EPS) + ADAM_WD * w)
    return delta, m, v


def _adamw_big(recvs, w, m, v, tr, name):
    L, R, C = w.shape

    def body(r0_ref, r1_ref, w_ref, m_ref, v_ref, g_ref, d_ref, mo_ref, vo_ref):
        l = pl.program_id(0)

        def total(r_ref):
            g = r_ref[0].astype(F32)
            for s in range(1, N_DEV):
                g = g + r_ref[s].astype(F32)
            g_ref[...] = g

        @pl.when(l == 0)
        def _():
            total(r0_ref)

        @pl.when(l == 1)
        def _():
            total(r1_ref)

        d_ref[...], mo_ref[...], vo_ref[...] = _adam(w_ref[...], g_ref[...], m_ref[...], v_ref[...])

    blk = pl.BlockSpec((None, tr, C), lambda l, i: (l, i, 0))
    return pl.pallas_call(
        body, name=name, grid=(L, R // tr),
        in_specs=[pl.BlockSpec((N_DEV, tr, C), lambda l, i: (0, i * (1 - l), 0)),
                  pl.BlockSpec((N_DEV, tr, C), lambda l, i: (0, i * l, 0)), blk, blk, blk],
        out_specs=[blk] * 4,
        out_shape=[jax.ShapeDtypeStruct((L, R, C), F32)] * 4,
        compiler_params=_cparams("arbitrary", "arbitrary"))(recvs[0], recvs[1], w, m, v)


def _sum_slots(recvs, name):
    L = len(recvs)
    _, R, C = recvs[0].shape

    def body(*refs):
        o_ref = refs[L]
        for l in range(L):
            acc = refs[l][0].astype(F32)
            for s in range(1, N_DEV):
                acc = acc + refs[l][s].astype(F32)
            o_ref[l] = acc

    return pl.pallas_call(body, name=name, out_shape=jax.ShapeDtypeStruct((L, R, C), F32))(*recvs)


def _adamw_plain(gs, ws, ms, vs, name):
    n = len(gs)

    def body(*refs):
        g_r, w_r, m_r, v_r = refs[:n], refs[n:2 * n], refs[2 * n:3 * n], refs[3 * n:4 * n]
        outs = refs[4 * n:]
        for a in range(n):
            d, mo, vo = _adam(w_r[a][...], g_r[a][...], m_r[a][...], v_r[a][...])
            outs[3 * a][...] = d
            outs[3 * a + 1][...] = mo
            outs[3 * a + 2][...] = vo

    shapes = []
    for a in range(n):
        shapes += [jax.ShapeDtypeStruct(ws[a].shape, F32)] * 3
    return pl.pallas_call(body, name=name, out_shape=shapes,
                          compiler_params=pltpu.CompilerParams(vmem_limit_bytes=VMEM_LIMIT))(*gs, *ws, *ms, *vs)


_V256_ROWS = {"convb_dw_b": 31, "convb_ln_g": 32, "convb_ln_b": 33, "pool_scale": 34}
_V512_ROWS = {"rg_conv_b": 0, "rg_b_a": 1, "rg_b_x": 2, "rg_lambda": 3}
_SMALL_REPL = ["mix_norm_g", "mlp_norm_g", "final_norm_g", "pool_w", "pool_scale", "convb_dw_b", "convb_ln_g",
               "convb_ln_b", "rg_conv_b", "rg_w_a", "rg_b_a", "rg_w_x", "rg_b_x", "rg_lambda"]


def _adamw_small(gath1, gath0, gath_tail, ws, ms, vs):
    n = len(_SMALL_REPL)
    n1, n0, nt = len(gath1), len(gath0), len(gath_tail)
    ng = n1 + n0 + nt
    L = DEPTH
    per_layer = [g.shape[1:] for g in gath1[:8]]
    sum_shapes = ([(L,) + s[1:] for s in per_layer[:2]] + [gath1[8].shape[1:]] + [(L,) + s for s in per_layer[2:]]
                  + [gath_tail[1].shape[1:]])

    def body(*refs):
        in1, in0, in_t = refs[:n1], refs[n1:n1 + n0], refs[n1 + n0:ng]
        w_r, m_r, v_r = refs[ng:ng + n], refs[ng + n:ng + 2 * n], refs[ng + 2 * n:ng + 3 * n]
        sums = refs[ng + 3 * n:ng + 3 * n + 10]
        outs = refs[ng + 3 * n + 10:]

        def total(ref):
            acc = ref[0]
            for s in range(1, N_DEV):
                acc = acc + ref[s]
            return acc

        s_mix, s_mlp, s_fin, s_pool, s_256, s_512, s_pw, s_wa, s_wx, s_meta = sums
        s_mix[1:2, :] = total(in1[0])
        s_mix[0:1, :] = total(in_t[0])
        s_mlp[1:2, :] = total(in1[1])
        s_mlp[0:1, :] = total(in0[0])
        s_fin[...] = total(in1[8])
        s_meta[...] = total(in_t[1])
        for k, dst in enumerate((s_pool, s_256, s_512, s_pw, s_wa, s_wx)):
            dst[1] = total(in1[2 + k])
            dst[0] = total(in0[1 + k])

        def grad_of(name, idx):
            if name == "mix_norm_g":
                return s_mix[idx[0]:idx[0] + 1, :]
            if name == "mlp_norm_g":
                return s_mlp[idx[0]:idx[0] + 1, :]
            if name == "final_norm_g":
                return s_fin[...]
            if name in _V256_ROWS:
                r = _V256_ROWS[name]
                return s_256[idx[0], r:r + 1, :]
            if name in _V512_ROWS:
                r = _V512_ROWS[name]
                return s_512[idx[0], r:r + 1, :]
            src = {"pool_w": s_pool, "rg_w_a": s_wa, "rg_w_x": s_wx}[name]
            return src[idx[0], idx[1] * 64:(idx[1] + 1) * 64, :]

        for a, name in enumerate(_SMALL_REPL):
            shape = w_r[a].shape
            if name == "final_norm_g":
                parts = [((), (slice(None), slice(None)))]
            elif len(shape) == 2:
                parts = [((l,), (slice(l, l + 1), slice(None))) for l in range(shape[0])]
            else:
                parts = [((l, h), (l, h)) for l in range(shape[0]) for h in range(shape[1])]
            for idx, sel in parts:
                g = grad_of(name, idx)
                d, mo, vo = _adam(w_r[a][sel], g, m_r[a][sel], v_r[a][sel])
                outs[4 * a][sel] = g
                outs[4 * a + 1][sel] = d
                outs[4 * a + 2][sel] = mo
                outs[4 * a + 3][sel] = vo

    out_shape = [jax.ShapeDtypeStruct(s, F32) for s in sum_shapes]
    for a in range(n):
        out_shape += [jax.ShapeDtypeStruct(ws[a].shape, F32)] * 4
    res = pl.pallas_call(body, name="adamw_small", out_shape=out_shape,
                         compiler_params=pltpu.CompilerParams(vmem_limit_bytes=VMEM_LIMIT))(
                             *gath1, *gath0, *gath_tail, *ws, *ms, *vs)
    return res[:10], res[10:]


_WEIGHTS = ['meta_tokens', 'mix_norm_g', 'w_in', 'pool_w', 'pool_scale', 'convb_dw_w', 'convb_dw_b', 'convb_ln_g',
            'convb_ln_b', 'convb_pw_w', 'rg_conv_w', 'rg_conv_b', 'rg_w_a', 'rg_b_a', 'rg_w_x', 'rg_b_x', 'rg_lambda',
            'w_out', 'mlp_norm_g', 'w_up', 'w_down', 'final_norm_g']


def _block_diag(w):
    L, H, C, _ = w.shape
    eye = jnp.eye(H, dtype=w.dtype)
    return (w[:, :, :, None, :] * eye[None, :, None, :, None]).reshape(L, H * C, H * C)


def _diag_blocks(m, H):
    C = m.shape[0] // H
    return jnp.concatenate([m[h * C:(h + 1) * C, h * C:(h + 1) * C] for h in range(H)], axis=0)


def kernel(x, meta_tokens, mix_norm_g, w_in, pool_w, pool_scale, convb_dw_w, convb_dw_b, convb_ln_g, convb_ln_b, convb_pw_w, rg_conv_w, rg_conv_b, rg_w_a, rg_b_a, rg_w_x, rg_b_x, rg_lambda, w_out, mlp_norm_g, w_up, w_down, final_norm_g, loss_target, m_meta_tokens, m_mix_norm_g, m_w_in, m_pool_w, m_pool_scale, m_convb_dw_w, m_convb_dw_b, m_convb_ln_g, m_convb_ln_b, m_convb_pw_w, m_rg_conv_w, m_rg_conv_b, m_rg_w_a, m_rg_b_a, m_rg_w_x, m_rg_b_x, m_rg_lambda, m_w_out, m_mlp_norm_g, m_w_up, m_w_down, m_final_norm_g, v_meta_tokens, v_mix_norm_g, v_w_in, v_pool_w, v_pool_scale, v_convb_dw_w, v_convb_dw_b, v_convb_ln_g, v_convb_ln_b, v_convb_pw_w, v_rg_conv_w, v_rg_conv_b, v_rg_w_a, v_rg_b_a, v_rg_w_x, v_rg_b_x, v_rg_lambda, v_w_out, v_mlp_norm_g, v_w_up, v_w_down, v_final_norm_g):
    W = dict(meta_tokens=meta_tokens, mix_norm_g=mix_norm_g, w_in=w_in, pool_w=pool_w, pool_scale=pool_scale,
             convb_dw_w=convb_dw_w, convb_dw_b=convb_dw_b, convb_ln_g=convb_ln_g, convb_ln_b=convb_ln_b,
             convb_pw_w=convb_pw_w, rg_conv_w=rg_conv_w, rg_conv_b=rg_conv_b, rg_w_a=rg_w_a, rg_b_a=rg_b_a,
             rg_w_x=rg_w_x, rg_b_x=rg_b_x, rg_lambda=rg_lambda, w_out=w_out, mlp_norm_g=mlp_norm_g, w_up=w_up,
             w_down=w_down, final_norm_g=final_norm_g.reshape(1, -1))
    M = dict(meta_tokens=m_meta_tokens, mix_norm_g=m_mix_norm_g, w_in=m_w_in, pool_w=m_pool_w, pool_scale=m_pool_scale,
             convb_dw_w=m_convb_dw_w, convb_dw_b=m_convb_dw_b, convb_ln_g=m_convb_ln_g, convb_ln_b=m_convb_ln_b,
             convb_pw_w=m_convb_pw_w, rg_conv_w=m_rg_conv_w, rg_conv_b=m_rg_conv_b, rg_w_a=m_rg_w_a, rg_b_a=m_rg_b_a,
             rg_w_x=m_rg_w_x, rg_b_x=m_rg_b_x, rg_lambda=m_rg_lambda, w_out=m_w_out, mlp_norm_g=m_mlp_norm_g,
             w_up=m_w_up, w_down=m_w_down, final_norm_g=m_final_norm_g.reshape(1, -1))
    V = dict(meta_tokens=v_meta_tokens, mix_norm_g=v_mix_norm_g, w_in=v_w_in, pool_w=v_pool_w, pool_scale=v_pool_scale,
             convb_dw_w=v_convb_dw_w, convb_dw_b=v_convb_dw_b, convb_ln_g=v_convb_ln_g, convb_ln_b=v_convb_ln_b,
             convb_pw_w=v_convb_pw_w, rg_conv_w=v_rg_conv_w, rg_conv_b=v_rg_conv_b, rg_w_a=v_rg_w_a, rg_b_a=v_rg_b_a,
             rg_w_x=v_rg_w_x, rg_b_x=v_rg_b_x, rg_lambda=v_rg_lambda, w_out=v_w_out, mlp_norm_g=v_mlp_norm_g,
             w_up=v_w_up, w_down=v_w_down, final_norm_g=v_final_norm_g.reshape(1, -1))

    xs = x[0]
    S, D = xs.shape
    T = S + N_META
    assert T % TT == 0 and D == 1024
    L = DEPTH
    me = 4 * lax.axis_index("x") + 2 * lax.axis_index("y") + lax.axis_index("c")
    c_in, c_ff = w_in.shape[-1], w_up.shape[-1]
    d_in, d_ff = c_in * N_DEV, c_ff * N_DEV
    r_out, r_dn = w_out.shape[1], w_down.shape[1]
    by_slot = lambda ref, s: ref.at[s]
    rows_of = lambda n: (lambda ref, s: ref.at[pl.ds(s * n, n), :])
    cols_of = lambda n: (lambda ref, s: ref.at[:, pl.ds(s * n, n)])

    win_t = jnp.transpose(w_in, (0, 2, 1)).astype(BF16)
    gathered = _allgather([
        (win_t[0], (d_in, D), rows_of(c_in)),
        (meta_tokens, (N_DEV,) + meta_tokens.shape, by_slot),
        (convb_dw_w, (N_DEV,) + convb_dw_w.shape, by_slot),
        (convb_pw_w, (N_DEV,) + convb_pw_w.shape, by_slot),
        (rg_conv_w, (N_DEV,) + rg_conv_w.shape, by_slot),
    ], "gather_first")
    win_f = [gathered[0], None]
    meta_f = jnp.transpose(gathered[1], (1, 0, 2)).reshape(N_META, D)
    dww_f = jnp.transpose(gathered[2], (1, 2, 0, 3)).reshape(L, CONV_K, D_CONV)
    wpw_f = jnp.transpose(gathered[3], (1, 0, 2, 3)).reshape(L, D_CONV, D_CONV).astype(BF16)
    rgw_f = jnp.transpose(gathered[4], (1, 2, 0, 3)).reshape(L, RG_CONV_K, D_RNN)
    wout_b, wup_b, wdown_b = w_out.astype(BF16), w_up.astype(BF16), w_down.astype(BF16)

    def weight_gather(blocks):
        comm = _Comm()
        for blk, shape, place in blocks:
            comm.gather(blk, shape, place)
        return comm

    wpool_bd = _block_diag(pool_w).astype(BF16)
    wa_bd = _block_diag(rg_w_a).astype(BF16)
    wx_bd = _block_diag(rg_w_x).astype(BF16)
    zeros4 = jnp.zeros((L, 4, D_POOL), F32)
    v256 = jnp.concatenate([pool_scale[:, None], convb_dw_b[:, None], convb_ln_g[:, None], convb_ln_b[:, None], zeros4], axis=1)
    dww_p = jnp.concatenate([dww_f, jnp.zeros((L, 1, D_CONV), F32)], axis=1)
    v512 = jnp.concatenate([rg_conv_b[:, None], rg_b_a[:, None], rg_b_x[:, None], rg_lambda[:, None], rgw_f], axis=1)
    mix_w = (wpool_bd, v256, dww_p, wpw_f, wa_bd, wx_bd, v512)

    h = jnp.concatenate([meta_f, xs], axis=0)
    tgt = jnp.concatenate([jnp.zeros((N_META, D), F32), loss_target[0]], axis=0)
    wout_f, wup_f, wdown_f = [None] * L, [None] * L, [None] * L
    saved = []

    (p,), wout_f = _rms_mm(h, mix_norm_g, win_f[0], 0, d_in, "in_proj0",
                           weight_gather([(wout_b[l], (r_out * N_DEV, D), rows_of(r_out)) for l in range(L)]),
                           w_is_nk=True)
    (y, hr), (wup_f[0],) = _mixer_fwd(p, mix_w, 0, weight_gather([(wup_b[0], (D, d_ff), cols_of(c_ff))]))
    (h2, u2), (win_f[1],) = _mm_res(h, y, wout_f[0], y.shape[-1], False, "out_proj0",
                                    weight_gather([(win_t[1], (d_in, D), rows_of(c_in))]), norm=(mlp_norm_g, 0))
    (z,), (wdown_f[0],) = _rms_mm(u2, mlp_norm_g, wup_f[0], 0, 2048, "mlp_up0",
                                  weight_gather([(wdown_b[0], (d_ff, D), rows_of(r_dn))]), out_dtype=BF16)
    (h3, u1), (wup_f[1],) = _mm_res(h2, z, wdown_f[0], 2048, True, "mlp_down0",
                                    weight_gather([(wup_b[1], (D, d_ff), cols_of(c_ff))]), norm=(mix_norm_g, 1))
    saved.append((h, None, p, y, hr, h2, u2, z))
    h = h3
    (p,), _ = _rms_mm(u1, mix_norm_g, win_f[1], 1, d_in, "in_proj1", w_is_nk=True)
    (y, hr), (wdown_f[1],) = _mixer_fwd(p, mix_w, 1, weight_gather([(wdown_b[1], (d_ff, D), rows_of(r_dn))]))
    (h2, u2), _ = _mm_res(h, y, wout_f[1], y.shape[-1], False, "out_proj1", norm=(mlp_norm_g, 1))
    (z,), _ = _rms_mm(u2, mlp_norm_g, wup_f[1], 1, 2048, "mlp_up1", out_dtype=BF16)
    (h3,), _ = _mm_res(h2, z, wdown_f[1], 2048, True, "mlp_down1")
    saved.append((h, u1, p, y, hr, h2, u2, z))

    dh, dhb, g_fin, loss_part = _final_loss(h3, W["final_norm_g"], tgt)
    loss = lax.psum(loss_part[0, 0], ("x", "y", "c"))

    def grad_exchange(pieces):
        comm = _Comm()
        for g, send, shape in pieces:
            comm.add(g, send, comm.add_out((N_DEV,) + shape, BF16), by_slot)
        return comm

    def small_gather(comm, grads):
        for g in grads:
            comm.gather(g, (N_DEV,) + g.shape, by_slot)
        return comm

    recv = {n: [None] * L for n in ("w_in", "w_out", "w_up", "w_down")}
    small1 = gath1 = gath0 = None
    for l in reversed(range(L)):
        h0, u1, p, y, hr, h2, u2, z = saved[l]
        comm = small_gather(_Comm(), small1) if l == 0 else None
        (dz,), got = _bwd_dz(dhb, wdown_f[l], z, 2048, f"mlp_down_bwd{l}", comm)
        if l == 0:
            gath1 = got
        (g_wdown,), _ = _mm_tn(z, dhb, True, None, None, l, "a", 1024, BF16, f"mlp_down_wgrad{l}")
        (dh2, dh2b, g_mlp), (recv["w_down"][l],) = _bwd_rms(
            dz, wup_f[l], h2, mlp_norm_g, dh, l, 2048, f"mlp_up_bwd{l}",
            grad_exchange([(g_wdown, rows_of(r_dn), (r_dn, D))]))
        (g_wup,), _ = _mm_tn(u2, dz, False, None, None, l, "b", 1024, BF16, f"mlp_up_wgrad{l}")
        (dy,), _ = _bwd_plain(dh2b, wout_f[l], f"out_proj_bwd{l}")
        (g_wout,), _ = _mm_tn(y, dh2b, False, None, None, l, "b", D, BF16, f"out_proj_wgrad{l}")
        (dp, g256, g512, gpool, gpw, gwa, gwx), (recv["w_up"][l], recv["w_out"][l]) = _mixer_bwd(
            p, hr, dy, mix_w, l, grad_exchange([(g_wup, cols_of(c_ff), (D, c_ff)),
                                                (g_wout, rows_of(r_out), (r_out, D))]))
        smalls = [g_mlp, _diag_blocks(gpool, 4), g256, g512, gpw, _diag_blocks(gwa, RG_HEADS), _diag_blocks(gwx, RG_HEADS)]
        comm = small_gather(_Comm(), smalls[:5]) if l == 0 else None
        if u1 is None:
            (g_win_t,), got_a = _mm_tn(dp, h0, False, "b", mix_norm_g, l, "a", d_in // 2, BF16, f"in_proj_wgrad{l}", comm)
        else:
            (g_win_t,), got_a = _mm_tn(dp, u1, False, None, None, l, "a", d_in // 2, BF16, f"in_proj_wgrad{l}", comm)
        comm = grad_exchange([(g_win_t, rows_of(c_in), (c_in, D))])
        if l == 0:
            small_gather(comm, smalls[5:])
        (dh, dhb, g_mix), got_b = _bwd_rms(
            dp, win_f[l], h0, mix_norm_g, dh2, l, d_in, f"in_proj_bwd{l}", comm, w_is_kd=True)
        recv["w_in"][l] = got_b[0]
        if l == 0:
            gath0 = got_a + got_b[1:]
        if l == 1:
            small1 = [g_mix] + smalls + [g_fin]
    grad_x = dh[N_META:][None]

    gath_tail = _comm_only(small_gather(_Comm(), [g_mix, dh[:N_META]]), "exchange_tail")

    out = {}
    for name, tr in (("w_out", r_out), ("w_up", 256), ("w_down", 128)):
        out[name] = _adamw_big(recv[name], W[name], M[name], V[name], tr, f"adamw_{name}")
    g_win = jnp.transpose(_sum_slots(recv["w_in"], "sum_w_in"), (0, 2, 1))
    out["w_in"] = (g_win,) + tuple(_adamw_plain([g_win], [W["w_in"]], [M["w_in"]], [V["w_in"]], "adamw_w_in"))
    sums, small_out = _adamw_small(gath1, gath0, gath_tail, [W[n] for n in _SMALL_REPL], [M[n] for n in _SMALL_REPL],
                                   [V[n] for n in _SMALL_REPL])
    for a, name in enumerate(_SMALL_REPL):
        out[name] = small_out[4 * a:4 * a + 4]
    s_256, s_512, s_pw, s_meta = sums[4], sums[5], sums[6], sums[9]
    g_shard = {
        "meta_tokens": lax.dynamic_slice_in_dim(s_meta, me * 128, 128, axis=1),
        "convb_dw_w": lax.dynamic_slice_in_dim(s_256[:, :CONV_K, :], me * 32, 32, axis=2),
        "convb_pw_w": lax.dynamic_slice_in_dim(s_pw, me * 32, 32, axis=1),
        "rg_conv_w": lax.dynamic_slice_in_dim(s_512[:, 4:8, :], me * 64, 64, axis=2),
    }
    names = list(g_shard)
    res = _adamw_plain([g_shard[n] for n in names], [W[n] for n in names], [M[n] for n in names],
                       [V[n] for n in names], "adamw_small_sharded")
    for a, name in enumerate(names):
        out[name] = (g_shard[name],) + tuple(res[3 * a:3 * a + 3])

    def fix(name, arr):
        return arr.reshape(-1) if name == "final_norm_g" else arr

    return (loss, grad_x,
            *[fix(n, out[n][0]) for n in _WEIGHTS], *[fix(n, out[n][1]) for n in _WEIGHTS],
            *[fix(n, out[n][2]) for n in _WEIGHTS], *[fix(n, out[n][3]) for n in _WEIGHTS])
```

```python
import functools
import operator

import jax
import jax.numpy as jnp
from jax import lax
from jax.experimental import pallas as pl
from jax.experimental.pallas import tpu as pltpu

F32 = jnp.float32
BF16 = jnp.bfloat16

N_DEV = 8
N_META = 16
DEPTH = 2
EPS = 1e-6
POOL_WINDOWS = (2, 4, 8, 16)
POOL_GW = 64
CONV_K = 31
RG_CONV_K = 4
RG_HEADS = 8
RG_HD = 64
RG_C = 8.0
D_POOL = 256
D_CONV = 256
D_RNN = 512

ADAM_LR = 0.001
ADAM_B1 = 0.9
ADAM_B2 = 0.999
ADAM_EPS = 1e-08
ADAM_WD = 0.01
ADAM_STEP = 10

TT = 432
TD = 912
TD_BIG = 2736
HALO = 48
VMEM_LIMIT = 56 * 1024 * 1024

MESH = pl.DeviceIdType.MESH
ANY = pl.BlockSpec(memory_space=pl.ANY)


def _cparams(*sem):
    return pltpu.CompilerParams(dimension_semantics=sem, vmem_limit_bytes=VMEM_LIMIT)


def _full(shape):
    nd = len(shape)
    return pl.BlockSpec(shape, lambda *_: (0,) * nd)


def _layer(shape, l):
    nd = len(shape)
    return pl.BlockSpec((None,) + tuple(shape), lambda *_: (l,) + (0,) * nd)


def _dot(a, b):
    return jnp.dot(a, b, preferred_element_type=F32)


def _dot_nt(a, b):
    return lax.dot_general(a, b, (((1,), (1,)), ((), ())), preferred_element_type=F32)


def _dot_tn(a, b):
    return lax.dot_general(a, b, (((0,), (0,)), ((), ())), preferred_element_type=F32)


def _rms(h):
    r = lax.rsqrt(jnp.mean(h * h, axis=-1, keepdims=True) + EPS)
    return h * r, r


def _rms_bwd(du, xh, r, g):
    dxh = du * g
    return r * (dxh - xh * jnp.mean(dxh * xh, axis=-1, keepdims=True))


def _sigmoid(x):
    return 1.0 / (1.0 + jnp.exp(-x))


def _colsum(x):
    return jnp.sum(x, axis=0, keepdims=True)


def _mesh_pos():
    return lax.axis_index("x"), lax.axis_index("y"), lax.axis_index("c")


def _slot(d):
    return 4 * d[0] + 2 * d[1] + d[2]


class _Comm:
    def __init__(self):
        self.srcs, self.send, self.dst, self.land, self.out_shapes = [], [], [], [], []

    def add_out(self, shape, dtype):
        self.out_shapes.append(jax.ShapeDtypeStruct(shape, dtype))
        return len(self.out_shapes) - 1

    def add(self, src, send, dst, land):
        self.srcs.append(src)
        self.send.append(send)
        self.dst.append(dst)
        self.land.append(land)

    def gather(self, block, out_shape, place):
        self.add(block, lambda ref, s: ref, self.add_out(out_shape, block.dtype), place)

    def sem_shapes(self):
        n = len(self.srcs)
        return [pltpu.SemaphoreType.DMA((7 * n,)), pltpu.SemaphoreType.DMA((7 * n,)), pltpu.SemaphoreType.DMA((n,))]

    def _copies(self, cins, couts, send_sems, recv_sems, local_sems, with_arrivals=True):
        x, y, c = _mesh_pos()
        me = (x, y, c)
        sends, arrivals, local = [], [], []
        for a in range(len(self.srcs)):
            out = couts[self.dst[a]]
            mine = self.land[a](out, _slot(me))
            local.append(pltpu.make_async_copy(self.send[a](cins[a], _slot(me)), mine, local_sems.at[a]))
            for k in range(N_DEV - 1):
                d = k + 1
                peer = (1 - x if (d >> 2) & 1 else x, 1 - y if (d >> 1) & 1 else y, 1 - c if d & 1 else c)
                sems = dict(send_sem=send_sems.at[a * 7 + k], recv_sem=recv_sems.at[a * 7 + k],
                            device_id=peer, device_id_type=MESH)
                sends.append(pltpu.make_async_remote_copy(
                    src_ref=self.send[a](cins[a], _slot(peer)), dst_ref=mine, **sems))
                if with_arrivals:
                    theirs = self.land[a](out, _slot(peer))
                    arrivals.append(pltpu.make_async_remote_copy(src_ref=theirs, dst_ref=theirs, **sems))
        return sends, arrivals, local

    def start(self, *refs):
        sends, _, local = self._copies(*refs, with_arrivals=False)
        for cp in local + sends:
            cp.start()

    def wait(self, *refs):
        sends, arrivals, local = self._copies(*refs)
        for cp in arrivals:
            cp.wait_recv()
        for cp in sends:
            cp.wait_send()
        for cp in local:
            cp.wait()


def _call(body, *, name, grid, in_specs, out_specs, out_shape, operands, scratch_shapes=(), sem=None, comm=None):
    out_specs, out_shape = list(out_specs), list(out_shape)
    if comm is None:
        res = pl.pallas_call(body, name=name, grid=grid, in_specs=list(in_specs), out_specs=out_specs,
                             out_shape=out_shape, scratch_shapes=list(scratch_shapes),
                             compiler_params=_cparams(*sem))(*operands)
        return list(res), []
    n_in, n_out, n_scr = len(in_specs), len(out_specs), len(scratch_shapes)
    nc_in, nc_out = len(comm.srcs), len(comm.out_shapes)

    def hosted(*refs):
        ins, cins = refs[:n_in], refs[n_in:n_in + nc_in]
        o0 = n_in + nc_in
        outs, couts = refs[o0:o0 + n_out], refs[o0 + n_out:o0 + n_out + nc_out]
        s0 = o0 + n_out + nc_out
        scr, sems = refs[s0:s0 + n_scr], refs[s0 + n_scr:]
        ids = [pl.program_id(a) for a in range(len(grid))]
        first = functools.reduce(operator.and_, [i == 0 for i in ids])
        last = functools.reduce(operator.and_, [i == g - 1 for i, g in zip(ids, grid)])

        @pl.when(first)
        def _():
            comm.start(cins, couts, *sems)

        body(*ins, *outs, *scr)

        @pl.when(last)
        def _():
            comm.wait(cins, couts, *sems)

    res = pl.pallas_call(
        hosted, name=name, grid=grid, in_specs=list(in_specs) + [ANY] * nc_in,
        out_specs=out_specs + [ANY] * nc_out, out_shape=out_shape + comm.out_shapes,
        scratch_shapes=list(scratch_shapes) + comm.sem_shapes(),
        compiler_params=_cparams(*(["arbitrary"] * len(grid))))(*operands, *comm.srcs)
    return list(res[:n_out]), list(res[n_out:])


def _comm_only(comm, name):
    def body(*refs):
        n, m = len(comm.srcs), len(comm.out_shapes)
        comm.start(refs[:n], refs[n:n + m], *refs[n + m:])
        comm.wait(refs[:n], refs[n:n + m], *refs[n + m:])

    return list(pl.pallas_call(body, name=name, in_specs=[ANY] * len(comm.srcs), out_specs=[ANY] * len(comm.out_shapes),
                               out_shape=comm.out_shapes, scratch_shapes=comm.sem_shapes())(*comm.srcs))


def _allgather(items, name):
    n = len(items)

    def body(*refs):
        ins, outs = refs[:n], refs[n:2 * n]
        send_sems, recv_sems, local_sems = refs[2 * n:]
        x, y, c = _mesh_pos()
        me, sib = (x, y, c), (x, y, 1 - c)
        chips = [(1 - x, y), (x, 1 - y), (1 - x, 1 - y)]

        def copy(a, k, blk, to, from_input=False):
            region = items[a][2](outs[a], _slot(blk))
            return pltpu.make_async_remote_copy(
                src_ref=ins[a] if from_input else region, dst_ref=region,
                send_sem=send_sems.at[a * 7 + k], recv_sem=recv_sems.at[a * 7 + k],
                device_id=to, device_id_type=MESH)

        local = [pltpu.make_async_copy(ins[a], items[a][2](outs[a], _slot(me)), local_sems.at[a]) for a in range(n)]
        for cp in local:
            cp.start()
        first = []
        for a in range(n):
            first.append(copy(a, 0, me, sib, True))
            for j, chip in enumerate(chips):
                first.append(copy(a, 1 + j, me, (*chip, c), True))
        for cp in first:
            cp.start()
        passed = []
        for j, chip in enumerate(chips):
            for a in range(n):
                copy(a, 1 + j, (*chip, c), me).wait_recv()
                fwd = copy(a, 4 + j, (*chip, c), sib)
                fwd.start()
                passed.append(fwd)
        for a in range(n):
            copy(a, 0, sib, me).wait_recv()
            for j, chip in enumerate(chips):
                copy(a, 4 + j, (*chip, 1 - c), me).wait_recv()
        for cp in first + passed:
            cp.wait_send()
        for cp in local:
            cp.wait()

    return pl.pallas_call(
        body, name=name,
        in_specs=[ANY] * n, out_specs=[ANY] * n,
        out_shape=[jax.ShapeDtypeStruct(it[1], it[0].dtype) for it in items],
        scratch_shapes=[pltpu.SemaphoreType.DMA((7 * n,)), pltpu.SemaphoreType.DMA((7 * n,)),
                        pltpu.SemaphoreType.DMA((n,))],
    )(*[it[0] for it in items])


def _rms_mm(h, g, w, l, ck, name, comm=None, w_is_nk=False, out_dtype=F32):
    T, D = h.shape
    N = w.shape[0] if w_is_nk else w.shape[1]
    normed = h.dtype == BF16

    def body(h_ref, g_ref, w_ref, o_ref):
        if normed:
            u = h_ref[...]
        else:
            u = (_rms(h_ref[...])[0] * g_ref[l:l + 1, :]).astype(BF16)
        res = _dot_nt(u, w_ref[...]) if w_is_nk else _dot(u, w_ref[...])
        o_ref[...] = res.astype(out_dtype)

    w_spec = pl.BlockSpec((ck, D), lambda j, i: (j, 0)) if w_is_nk else pl.BlockSpec((D, ck), lambda j, i: (0, j))
    td = TD_BIG if normed and out_dtype == BF16 else TD
    return _call(
        body, name=name, grid=(N // ck, T // td),
        in_specs=[pl.BlockSpec((td, D), lambda j, i: (i, 0)), _full(g.shape), w_spec],
        out_specs=[pl.BlockSpec((td, ck), lambda j, i: (i, j))],
        out_shape=[jax.ShapeDtypeStruct((T, N), out_dtype)],
        operands=(h, g, w), sem=("parallel", "parallel"), comm=comm)


def _mm_res(res, a, w, ck, relu2, name, comm=None, norm=None):
    T, D = res.shape
    K = a.shape[-1]
    nk = K // ck

    def body(*refs):
        if norm is None:
            r_ref, a_ref, w_ref, o_ref = refs
        else:
            r_ref, a_ref, w_ref, g_ref, o_ref, u_ref = refs
        k = pl.program_id(1)

        @pl.when(k == 0)
        def _():
            o_ref[...] = r_ref[...]

        av = a_ref[...]
        if relu2:
            av = jnp.square(jnp.maximum(av.astype(F32), 0.0))
        o_ref[...] += _dot(av.astype(BF16), w_ref[...])

        if norm is not None:
            @pl.when(k == nk - 1)
            def _():
                u_ref[...] = (_rms(o_ref[...])[0] * g_ref[norm[1]:norm[1] + 1, :]).astype(BF16)

    row = pl.BlockSpec((TD, D), lambda i, k: (i, 0))
    in_specs = [row, pl.BlockSpec((TD, ck), lambda i, k: (i, k)), pl.BlockSpec((ck, D), lambda i, k: (k, 0))]
    operands = (res, a, w)
    out_specs, out_shape = [row], [jax.ShapeDtypeStruct((T, D), F32)]
    if norm is not None:
        in_specs.append(_full(norm[0].shape))
        operands += (norm[0],)
        out_specs.append(row)
        out_shape.append(jax.ShapeDtypeStruct((T, D), BF16))
    return _call(
        body, name=name, grid=(T // TD, nk), in_specs=in_specs, out_specs=out_specs, out_shape=out_shape,
        operands=operands, sem=("parallel", "arbitrary"), comm=comm)


def _final_loss(h, gf, tgt):
    T, D = h.shape

    def body(h_ref, g_ref, t_ref, dh_ref, dhb_ref, dg_ref, loss_ref):
        i = pl.program_id(0)

        @pl.when(i == 0)
        def _():
            dg_ref[...] = jnp.zeros_like(dg_ref)
            loss_ref[...] = jnp.zeros_like(loss_ref)

        g = g_ref[...]
        xh, r = _rms(h_ref[...])
        row = i * TD + lax.broadcasted_iota(jnp.int32, (TD, 1), 0)
        keep = (row >= N_META).astype(F32)
        diff = (xh * g - t_ref[...]) * keep
        part = jnp.sum(jnp.sum(diff * diff, axis=1, keepdims=True), axis=0, keepdims=True)
        loss_ref[...] += (0.5 / D) * part
        dy = diff * (1.0 / D)
        dg_ref[...] += _colsum(dy * xh)
        dh = _rms_bwd(dy, xh, r, g)
        dh_ref[...] = dh
        dhb_ref[...] = dh.astype(BF16)

    row = pl.BlockSpec((TD, D), lambda i: (i, 0))
    return pl.pallas_call(
        body, name="final_loss", grid=(T // TD,),
        in_specs=[row, _full(gf.shape), row],
        out_specs=[row, row, _full((1, D)), _full((8, 128))],
        out_shape=[jax.ShapeDtypeStruct((T, D), F32), jax.ShapeDtypeStruct((T, D), BF16),
                   jax.ShapeDtypeStruct((1, D), F32), jax.ShapeDtypeStruct((8, 128), F32)],
        compiler_params=_cparams("arbitrary"))(h, gf, tgt)


def _bwd_dz(dh, wdown, z, ck, name, comm=None):
    T, D = dh.shape
    F = z.shape[-1]

    def body(d_ref, w_ref, z_ref, o_ref):
        dact = _dot_nt(d_ref[...], w_ref[...])
        o_ref[...] = (dact * (2.0 * jnp.maximum(z_ref[...].astype(F32), 0.0))).astype(BF16)

    return _call(
        body, name=name, grid=(F // ck, T // TD_BIG),
        in_specs=[pl.BlockSpec((TD_BIG, D), lambda j, i: (i, 0)), pl.BlockSpec((ck, D), lambda j, i: (j, 0)),
                  pl.BlockSpec((TD_BIG, ck), lambda j, i: (i, j))],
        out_specs=[pl.BlockSpec((TD_BIG, ck), lambda j, i: (i, j))],
        out_shape=[jax.ShapeDtypeStruct((T, F), BF16)],
        operands=(dh, wdown, z), sem=("parallel", "parallel"), comm=comm)


def _bwd_plain(dh, w, name, comm=None):
    T, D = dh.shape
    K = w.shape[0]

    def body(d_ref, w_ref, o_ref):
        o_ref[...] = _dot_nt(d_ref[...], w_ref[...])

    return _call(
        body, name=name, grid=(T // TD,),
        in_specs=[pl.BlockSpec((TD, D), lambda i: (i, 0)), _full((K, D))],
        out_specs=[pl.BlockSpec((TD, K), lambda i: (i, 0))],
        out_shape=[jax.ShapeDtypeStruct((T, K), F32)],
        operands=(dh, w), sem=("parallel",), comm=comm)


def _bwd_rms(dx, w, h, g, dh_in, l, ck, name, comm=None, w_is_kd=False):
    T, D = h.shape
    K = dx.shape[-1]
    nk = K // ck
    w_spec = pl.BlockSpec((ck, D), lambda i, k: (k, 0)) if w_is_kd else pl.BlockSpec((D, ck), lambda i, k: (0, k))

    def body(x_ref, w_ref, h_ref, g_ref, di_ref, dh_ref, dhb_ref, dg_ref, acc_ref):
        i, k = pl.program_id(0), pl.program_id(1)

        @pl.when((i == 0) & (k == 0))
        def _():
            dg_ref[...] = jnp.zeros_like(dg_ref)

        @pl.when(k == 0)
        def _():
            acc_ref[...] = jnp.zeros_like(acc_ref)

        acc_ref[...] += _dot(x_ref[...], w_ref[...]) if w_is_kd else _dot_nt(x_ref[...], w_ref[...])

        @pl.when(k == nk - 1)
        def _():
            du = acc_ref[...]
            xh, r = _rms(h_ref[...])
            dg_ref[...] += _colsum(du * xh)
            dh = di_ref[...] + _rms_bwd(du, xh, r, g_ref[l:l + 1, :])
            dh_ref[...] = dh
            dhb_ref[...] = dh.astype(BF16)

    row = pl.BlockSpec((TD, D), lambda i, k: (i, 0))
    return _call(
        body, name=name, grid=(T // TD, nk),
        in_specs=[pl.BlockSpec((TD, ck), lambda i, k: (i, k)), w_spec, row, _full(g.shape), row],
        out_specs=[row, row, _full((1, D))],
        out_shape=[jax.ShapeDtypeStruct((T, D), F32), jax.ShapeDtypeStruct((T, D), BF16),
                   jax.ShapeDtypeStruct((1, D), F32)],
        scratch_shapes=[pltpu.VMEM((TD, D), F32)],
        operands=(dx, w, h, g, dh_in), sem=("arbitrary", "arbitrary"), comm=comm)


def _mm_tn(a, b, relu2, rms_on, g, l, chunk, ck, out_dtype, name, comm=None):
    T = a.shape[0]
    Ka, Nb = a.shape[1], b.shape[1]
    ca, cb = (ck, Nb) if chunk == "a" else (Ka, ck)
    nj = (Ka // ck) if chunk == "a" else (Nb // ck)
    td = TD_BIG if a.dtype == BF16 and b.dtype == BF16 else TD
    nt = T // td

    def body(*refs):
        if rms_on is not None:
            a_ref, g_ref, b_ref, o_ref, acc_ref = refs
        else:
            a_ref, b_ref, o_ref, acc_ref = refs
        t = pl.program_id(1)

        @pl.when(t == 0)
        def _():
            acc_ref[...] = jnp.zeros_like(acc_ref)

        av, bv = a_ref[...], b_ref[...]
        if relu2:
            av = jnp.square(jnp.maximum(av.astype(F32), 0.0))
        if rms_on == "a":
            av = _rms(av)[0] * g_ref[l:l + 1, :]
        elif rms_on == "b":
            bv = _rms(bv)[0] * g_ref[l:l + 1, :]
        acc_ref[...] += _dot_tn(av.astype(BF16), bv.astype(BF16))

        @pl.when(t == nt - 1)
        def _():
            o_ref[...] = acc_ref[...].astype(out_dtype)

    if chunk == "a":
        a_spec = pl.BlockSpec((td, ca), lambda j, t: (t, j))
        b_spec = pl.BlockSpec((td, cb), lambda j, t: (t, 0))
        o_spec = pl.BlockSpec((ca, cb), lambda j, t: (j, 0))
    else:
        a_spec = pl.BlockSpec((td, ca), lambda j, t: (t, 0))
        b_spec = pl.BlockSpec((td, cb), lambda j, t: (t, j))
        o_spec = pl.BlockSpec((ca, cb), lambda j, t: (0, j))
    ins, specs = [a], [a_spec]
    if rms_on is not None:
        ins.append(g)
        specs.append(_full(g.shape))
    ins.append(b)
    specs.append(b_spec)
    return _call(
        body, name=name, grid=(nj, nt), in_specs=specs, out_specs=[o_spec],
        out_shape=[jax.ShapeDtypeStruct((Ka, Nb), out_dtype)], scratch_shapes=[pltpu.VMEM((ca, cb), F32)],
        operands=tuple(ins), sem=("parallel", "arbitrary"), comm=comm)


def _shift_down(x, k):
    return x if k == 0 else pltpu.roll(x, k, 0)


def _shift_up(x, k):
    return x if k == 0 else pltpu.roll(x, x.shape[0] - k, 0)


def _pool_sel(lane_grp, vals):
    return jnp.where(lane_grp == 0, vals[0], jnp.where(lane_grp == 1, vals[1], jnp.where(lane_grp == 2, vals[2], vals[3])))


def _pool_fwd(ue, pos):
    s = ue
    sums = []
    for k in (1, 2, 4, 8):
        s = s + _shift_down(s, k)
        sums.append(s[HALO:])
    grp = lax.broadcasted_iota(jnp.int32, (1, D_POOL), 1) // POOL_GW
    wsum = _pool_sel(grp, sums)
    width = _pool_sel(grp, [jnp.float32(w) for w in POOL_WINDOWS])
    cnt = jnp.minimum(pos.astype(F32), width)
    return wsum / cnt - ue[HALO:], cnt, grp


def _conv_fwd(xe, w_ref, nk):
    if isinstance(xe, jax.Array):
        assert nk <= 8
        window = lambda q: xe[HALO - 8:]
    else:
        window = lambda q: xe[pl.ds(HALO - 8 - 8 * q, TT + 8), :]
    return _taps(window, w_ref, nk, _shift_down)[8:]


def _taps(window, w_ref, nk, shift):
    acc = None
    for r in range(min(8, nk)):
        inner = None
        for q in range((nk - 1 - r) // 8 + 1):
            term = window(q) * w_ref(nk - 1 - 8 * q - r)
            inner = term if inner is None else inner + term
        inner = shift(inner, r)
        acc = inner if acc is None else acc + inner
    return acc


def _layernorm(cb):
    mu = jnp.mean(cb, axis=-1, keepdims=True)
    xc = cb - mu
    rstd = lax.rsqrt(jnp.mean(xc * xc, axis=-1, keepdims=True) + EPS)
    return xc * rstd, rstd


def _softplus_neg(lam):
    x = -lam
    e = jnp.exp(-jnp.abs(x))
    u = 1.0 + e
    d = u - 1.0
    log1p = jnp.where(d == 0.0, e, jnp.log(u) * (e / jnp.where(d == 0.0, 1.0, d)))
    return jnp.maximum(x, 0.0) + log1p


def _expm1(x):
    small = x * (1.0 + x * 0.5 * (1.0 + x * (1.0 / 3.0) * (1.0 + x * 0.25)))
    return jnp.where(jnp.abs(x) < 1e-2, small, jnp.exp(x) - 1.0)


def _gelu_parts(x):
    c0 = 0.7978845608028654
    inner = c0 * (x + 0.044715 * x * x * x)
    th = jnp.tanh(inner)
    gelu = 0.5 * x * (1.0 + th)
    dgelu = 0.5 * (1.0 + th) + 0.5 * x * (1.0 - th * th) * c0 * (1.0 + 3.0 * 0.044715 * x * x)
    return gelu, dgelu


def _rg_gates(xc, wa_ref, wx_ref, v512_ref):
    xb = xc.astype(BF16)
    ra = _sigmoid(_dot(xb, wa_ref[...]) + v512_ref[1:2, :])
    ri = _sigmoid(_dot(xb, wx_ref[...]) + v512_ref[2:3, :])
    sp = _softplus_neg(v512_ref[3:4, :])
    log_a = (-RG_C) * ra * sp
    a = jnp.exp(log_a)
    mult = jnp.sqrt(-_expm1(2.0 * log_a))
    return ra, ri, sp, a, mult


def _scan_down(a, b):
    n = a.shape[0]
    row = lax.broadcasted_iota(jnp.int32, (n, 1), 0)
    k = 1
    while k < n:
        ok = row >= k
        b = a * jnp.where(ok, _shift_down(b, k), 0.0) + b
        a = a * jnp.where(ok, _shift_down(a, k), 1.0)
        k *= 2
    return b, a


def _scan_up(c, g):
    n = c.shape[0]
    row = lax.broadcasted_iota(jnp.int32, (n, 1), 0)
    k = 1
    while k < n:
        ok = row < n - k
        g = c * jnp.where(ok, _shift_up(g, k), 0.0) + g
        c = c * jnp.where(ok, _shift_up(c, k), 1.0)
        k *= 2
    return g, c


def _mixer_specs(l, tile_of):
    r = TT // HALO
    return [
        pl.BlockSpec((TT, 1792), lambda i: (tile_of(i), 0)),
        pl.BlockSpec((HALO, 1792), lambda i: (jnp.maximum(tile_of(i) * r - 1, 0), 0)),
        _layer((256, 256), l), _layer((8, 256), l), _layer((32, 256), l), _layer((256, 256), l),
        _layer((512, 512), l), _layer((512, 512), l), _layer((8, 512), l),
    ]


def _mixer_fwd(p, wts, l, comm=None):
    T = p.shape[0]

    def body(p_ref, ph_ref, wpool_ref, v256_ref, dww_ref, wpw_ref, wa_ref, wx_ref, v512_ref, y_ref, hr_ref, cb_ref,
             carry_ref, ub_s):
        i = pl.program_id(0)

        @pl.when(i == 0)
        def _():
            carry_ref[...] = jnp.zeros_like(carry_ref)

        halo = jnp.where(i == 0, 0.0, ph_ref[...])
        e = jnp.concatenate([halo, p_ref[...]], axis=0)
        pos = i * TT + 1 + lax.broadcasted_iota(jnp.int32, (TT, 1), 0)
        pooled, _, _ = _pool_fwd(e[:, 0:256], pos)
        y_a = _dot(pooled.astype(BF16), wpool_ref[...]) * v256_ref[0:1, :]
        ub_s[...] = e[:, 256:512] * _sigmoid(e[:, 512:768])
        cb = _conv_fwd(ub_s, lambda k: dww_ref[k:k + 1, :], CONV_K) + v256_ref[1:2, :]
        cb_ref[...] = cb
        xhat, _ = _layernorm(cb)
        ln = xhat * v256_ref[2:3, :] + v256_ref[3:4, :]
        s = ln * _sigmoid(ln)
        y_b = _dot(s.astype(BF16), wpw_ref[...])
        xc = _conv_fwd(e[:, 1280:1792], lambda k: v512_ref[4 + k:5 + k, :], RG_CONV_K) + v512_ref[0:1, :]
        _, ri, _, a, mult = _rg_gates(xc, wa_ref, wx_ref, v512_ref)
        hloc, acum = _scan_down(a, mult * ri * xc)
        hr = hloc + acum * carry_ref[0:1, :]
        hr_ref[...] = hr
        carry_ref[0:1, :] = hr_ref[TT - 1:TT, :]
        gelu, _ = _gelu_parts(p_ref[:, 768:1280])
        y_ref[:, 0:256] = y_a.astype(BF16)
        y_ref[:, 256:512] = y_b.astype(BF16)
        y_ref[:, 512:1024] = (gelu * hr).astype(BF16)

    return _call(
        body, name=f"mixer_fwd{l}", grid=(T // TT,),
        in_specs=_mixer_specs(l, lambda i: i),
        out_specs=[pl.BlockSpec((TT, 1024), lambda i: (i, 0)), pl.BlockSpec((TT, 512), lambda i: (i, 0)),
                   pl.BlockSpec((TT, 256), lambda i: (i, 0))],
        out_shape=[jax.ShapeDtypeStruct((T, 1024), BF16), jax.ShapeDtypeStruct((T, 512), F32),
                   jax.ShapeDtypeStruct((T, 256), F32)],
        scratch_shapes=[pltpu.VMEM((8, 512), F32), pltpu.VMEM((HALO + TT, D_CONV), F32)],
        operands=(p, p, *wts), sem=("arbitrary",), comm=comm)


def _mixer_bwd(p, hr, cb, dy, wts, l, comm=None):
    T = p.shape[0]
    nt = T // TT
    tile_of = lambda i: nt - 1 - i

    def body(p_ref, ph_ref, wpool_ref, v256_ref, dww_ref, wpw_ref, wa_ref, wx_ref, v512_ref, hr_ref, hrh_ref, cb_ref,
             dy_ref, dp_ref, g256_ref, g512_ref, gpool_ref, gpw_ref, gwa_ref, gwx_ref,
             q_c, dcb_c, dxc_c, ag_c, ub_s, ed_s):
        i = pl.program_id(0)
        j = nt - 1 - i

        @pl.when(i == 0)
        def _():
            for ref in (g256_ref, g512_ref, gpool_ref, gpw_ref, gwa_ref, gwx_ref, q_c, dcb_c, dxc_c, ag_c):
                ref[...] = jnp.zeros_like(ref)

        halo = jnp.where(j == 0, 0.0, ph_ref[...])
        e = jnp.concatenate([halo, p_ref[...]], axis=0)
        pos = j * TT + 1 + lax.broadcasted_iota(jnp.int32, (TT, 1), 0)
        row = lax.broadcasted_iota(jnp.int32, (TT, 1), 0)

        dy_a = dy_ref[:, 0:256]
        pooled, cnt, grp = _pool_fwd(e[:, 0:256], pos)
        scale = v256_ref[0:1, :]
        mixed = _dot(pooled.astype(BF16), wpool_ref[...])
        g256_ref[34:35, :] += _colsum(dy_a * mixed)
        dmixed = (dy_a * scale).astype(BF16)
        gpool_ref[...] += _dot_tn(pooled.astype(BF16), dmixed)
        dpooled = _dot_nt(dmixed, wpool_ref[...])
        q = dpooled / cnt
        s = jnp.concatenate([q, q_c[...]], axis=0)
        sums = []
        for k in (1, 2, 4, 8):
            s = s + _shift_up(s, k)
            sums.append(s[:TT])
        dp_ref[:, 0:256] = (_pool_sel(grp, sums) - dpooled).astype(BF16)
        q_c[...] = q[0:16]

        dy_b = dy_ref[:, 256:512].astype(BF16)
        sg = _sigmoid(e[:, 512:768])
        ub_s[...] = e[:, 256:512] * sg
        xhat, rstd = _layernorm(cb_ref[...])
        ln_g = v256_ref[2:3, :]
        ln = xhat * ln_g + v256_ref[3:4, :]
        sig = _sigmoid(ln)
        sact = ln * sig
        gpw_ref[...] += _dot_tn(sact.astype(BF16), dy_b)
        dln = _dot_nt(dy_b, wpw_ref[...]) * (sig * (1.0 + ln * (1.0 - sig)))
        g256_ref[32:33, :] += _colsum(dln * xhat)
        g256_ref[33:34, :] += _colsum(dln)
        dxhat = dln * ln_g
        dcb = rstd * (dxhat - jnp.mean(dxhat, axis=-1, keepdims=True)
                      - xhat * jnp.mean(dxhat * xhat, axis=-1, keepdims=True))
        g256_ref[31:32, :] += _colsum(dcb)
        pad8 = jnp.zeros((8, D_CONV), F32)
        dcb_ext = jnp.concatenate([pad8, dcb, pad8], axis=0)
        for r in range(8):
            d_r = _shift_up(dcb_ext, r)[0:TT + 8]
            for q in range((CONV_K - 1 - r) // 8 + 1):
                k = CONV_K - 1 - 8 * q - r
                g256_ref[k:k + 1, :] += _colsum(d_r * ub_s[pl.ds(HALO - 8 - 8 * q, TT + 8), :])
        ed_s[0:TT, :] = dcb
        ed_s[TT:TT + 32, :] = dcb_c[...]
        dub = _taps(lambda q: ed_s[pl.ds(8 * q, TT + 8), :], lambda k: dww_ref[k:k + 1, :], CONV_K, _shift_up)[:TT]
        dcb_c[...] = dcb[0:32]
        sg_t = sg[HALO:]
        v_t = p_ref[:, 256:512]
        dp_ref[:, 256:512] = (dub * sg_t).astype(BF16)
        dp_ref[:, 512:768] = (dub * v_t * sg_t * (1.0 - sg_t)).astype(BF16)

        dy_c = dy_ref[:, 512:1024]
        cxe = e[:, 1280:1792]
        xc = _conv_fwd(cxe, lambda k: v512_ref[4 + k:5 + k, :], RG_CONV_K) + v512_ref[0:1, :]
        ra, ri, sp, a, mult = _rg_gates(xc, wa_ref, wx_ref, v512_ref)
        hrv = hr_ref[...]
        gelu, dgelu = _gelu_parts(p_ref[:, 768:1280])
        dp_ref[:, 768:1280] = (dy_c * hrv * dgelu).astype(BF16)
        dhr = dy_c * gelu
        coef = jnp.where(row < TT - 1, _shift_up(a, 1), 1.0)
        gloc, ccum = _scan_up(coef, dhr)
        gg = gloc + ccum * ag_c[0:1, :]
        ag_c[...] = (a * gg)[0:8]
        hr_before = jnp.where(j == 0, 0.0, hrh_ref[7:8, :])
        hr_prev = jnp.where(row >= 1, _shift_down(hrv, 1), hr_before)
        da = gg * hr_prev
        dmult = gg * ri * xc
        dri = gg * mult * xc
        dxc = gg * mult * ri
        dlog_a = da * a - dmult * (a * a / mult)
        g512_ref[3:4, :] += _colsum(dlog_a * ((-RG_C) * ra))
        dpre_a = dlog_a * ((-RG_C) * sp) * ra * (1.0 - ra)
        dpre_x = dri * ri * (1.0 - ri)
        g512_ref[1:2, :] += _colsum(dpre_a)
        g512_ref[2:3, :] += _colsum(dpre_x)
        xb = xc.astype(BF16)
        dpa_b, dpx_b = dpre_a.astype(BF16), dpre_x.astype(BF16)
        gwa_ref[...] += _dot_tn(xb, dpa_b)
        gwx_ref[...] += _dot_tn(xb, dpx_b)
        dxc = dxc + _dot_nt(dpa_b, wa_ref[...]) + _dot_nt(dpx_b, wx_ref[...])
        g512_ref[0:1, :] += _colsum(dxc)
        for k in range(RG_CONV_K):
            g512_ref[4 + k:5 + k, :] += _colsum(dxc * _shift_down(cxe, RG_CONV_K - 1 - k)[HALO:])
        ex = jnp.concatenate([dxc, dxc_c[...]], axis=0)
        dcx = _taps(lambda q: ex, lambda k: v512_ref[4 + k:5 + k, :], RG_CONV_K, _shift_up)[:TT]
        dxc_c[...] = dxc[0:8]
        dp_ref[:, 1280:1792] = dcx.astype(BF16)

        @pl.when(i == nt - 1)
        def _():
            lam = v512_ref[3:4, :]
            g512_ref[3:4, :] = g512_ref[3:4, :] * (-_sigmoid(-lam))

    in_specs = _mixer_specs(l, tile_of) + [
        pl.BlockSpec((TT, 512), lambda i: (tile_of(i), 0)),
        pl.BlockSpec((8, 512), lambda i: (jnp.maximum(tile_of(i) * (TT // 8) - 1, 0), 0)),
        pl.BlockSpec((TT, 256), lambda i: (tile_of(i), 0)),
        pl.BlockSpec((TT, 1024), lambda i: (tile_of(i), 0)),
    ]
    acc_shapes = [(40, 256), (8, 512), (256, 256), (256, 256), (512, 512), (512, 512)]
    return _call(
        body, name=f"mixer_bwd{l}", grid=(nt,),
        in_specs=in_specs,
        out_specs=[pl.BlockSpec((TT, 1792), lambda i: (tile_of(i), 0))] + [_full(s) for s in acc_shapes],
        out_shape=[jax.ShapeDtypeStruct((T, 1792), BF16)] + [jax.ShapeDtypeStruct(s, F32) for s in acc_shapes],
        scratch_shapes=[pltpu.VMEM((16, 256), F32), pltpu.VMEM((32, 256), F32), pltpu.VMEM((8, 512), F32),
                        pltpu.VMEM((8, 512), F32), pltpu.VMEM((HALO + TT, D_CONV), F32),
                        pltpu.VMEM((TT + 32, D_CONV), F32)],
        operands=(p, p, *wts, hr, hr, cb, dy), sem=("arbitrary",), comm=comm)


def _adam(w, g, m, v):
    m = ADAM_B1 * m + (1.0 - ADAM_B1) * g
    v = ADAM_B2 * v + (1.0 - ADAM_B2) * (g * g)
    m_hat = m / (1.0 - ADAM_B1 ** ADAM_STEP)
    v_hat = v / (1.0 - ADAM_B2 ** ADAM_STEP)
    delta = -ADAM_LR * (m_hat / (jnp.sqrt(v_hat) + ADAM_EPS) + ADAM_WD * w)
    return delta, m, v


def _adamw_big(recvs, w, m, v, tr, name):
    L, R, C = w.shape

    def body(r0_ref, r1_ref, w_ref, m_ref, v_ref, g_ref, d_ref, mo_ref, vo_ref):
        l = pl.program_id(0)

        def total(r_ref):
            g = r_ref[0].astype(F32)
            for s in range(1, N_DEV):
                g = g + r_ref[s].astype(F32)
            g_ref[...] = g

        @pl.when(l == 0)
        def _():
            total(r0_ref)

        @pl.when(l == 1)
        def _():
            total(r1_ref)

        d_ref[...], mo_ref[...], vo_ref[...] = _adam(w_ref[...], g_ref[...], m_ref[...], v_ref[...])

    blk = pl.BlockSpec((None, tr, C), lambda l, i: (l, i, 0))
    return pl.pallas_call(
        body, name=name, grid=(L, R // tr),
        in_specs=[pl.BlockSpec((N_DEV, tr, C), lambda l, i: (0, i * (1 - l), 0)),
                  pl.BlockSpec((N_DEV, tr, C), lambda l, i: (0, i * l, 0)), blk, blk, blk],
        out_specs=[blk] * 4,
        out_shape=[jax.ShapeDtypeStruct((L, R, C), F32)] * 4,
        compiler_params=_cparams("arbitrary", "arbitrary"))(recvs[0], recvs[1], w, m, v)


def _sum_slots(recvs, name):
    L = len(recvs)
    _, R, C = recvs[0].shape

    def body(*refs):
        o_ref = refs[L]
        for l in range(L):
            acc = refs[l][0].astype(F32)
            for s in range(1, N_DEV):
                acc = acc + refs[l][s].astype(F32)
            o_ref[l] = acc

    return pl.pallas_call(body, name=name, out_shape=jax.ShapeDtypeStruct((L, R, C), F32))(*recvs)


def _adamw_plain(gs, ws, ms, vs, name):
    n = len(gs)

    def body(*refs):
        g_r, w_r, m_r, v_r = refs[:n], refs[n:2 * n], refs[2 * n:3 * n], refs[3 * n:4 * n]
        outs = refs[4 * n:]
        for a in range(n):
            d, mo, vo = _adam(w_r[a][...], g_r[a][...], m_r[a][...], v_r[a][...])
            outs[3 * a][...] = d
            outs[3 * a + 1][...] = mo
            outs[3 * a + 2][...] = vo

    shapes = []
    for a in range(n):
        shapes += [jax.ShapeDtypeStruct(ws[a].shape, F32)] * 3
    return pl.pallas_call(body, name=name, out_shape=shapes,
                          compiler_params=pltpu.CompilerParams(vmem_limit_bytes=VMEM_LIMIT))(*gs, *ws, *ms, *vs)


_V256_ROWS = {"convb_dw_b": 31, "convb_ln_g": 32, "convb_ln_b": 33, "pool_scale": 34}
_V512_ROWS = {"rg_conv_b": 0, "rg_b_a": 1, "rg_b_x": 2, "rg_lambda": 3}
_SMALL_REPL = ["mix_norm_g", "mlp_norm_g", "final_norm_g", "pool_w", "pool_scale", "convb_dw_b", "convb_ln_g",
               "convb_ln_b", "rg_conv_b", "rg_w_a", "rg_b_a", "rg_w_x", "rg_b_x", "rg_lambda"]


def _adamw_small(gath1, gath0, gath_tail, ws, ms, vs):
    n = len(_SMALL_REPL)
    n1, n0, nt = len(gath1), len(gath0), len(gath_tail)
    ng = n1 + n0 + nt
    L = DEPTH
    per_layer = [g.shape[1:] for g in gath1[:8]]
    sum_shapes = ([(L,) + s[1:] for s in per_layer[:2]] + [gath1[8].shape[1:]] + [(L,) + s for s in per_layer[2:]]
                  + [gath_tail[1].shape[1:]])

    def body(*refs):
        in1, in0, in_t = refs[:n1], refs[n1:n1 + n0], refs[n1 + n0:ng]
        w_r, m_r, v_r = refs[ng:ng + n], refs[ng + n:ng + 2 * n], refs[ng + 2 * n:ng + 3 * n]
        sums = refs[ng + 3 * n:ng + 3 * n + 10]
        outs = refs[ng + 3 * n + 10:]

        def total(ref):
            acc = ref[0]
            for s in range(1, N_DEV):
                acc = acc + ref[s]
            return acc

        s_mix, s_mlp, s_fin, s_pool, s_256, s_512, s_pw, s_wa, s_wx, s_meta = sums
        s_mix[1:2, :] = total(in1[0])
        s_mix[0:1, :] = total(in_t[0])
        s_mlp[1:2, :] = total(in1[1])
        s_mlp[0:1, :] = total(in0[0])
        s_fin[...] = total(in1[8])
        s_meta[...] = total(in_t[1])
        for k, dst in enumerate((s_pool, s_256, s_512, s_pw, s_wa, s_wx)):
            dst[1] = total(in1[2 + k])
            dst[0] = total(in0[1 + k])

        def grad_of(name, idx):
            if name == "mix_norm_g":
                return s_mix[idx[0]:idx[0] + 1, :]
            if name == "mlp_norm_g":
                return s_mlp[idx[0]:idx[0] + 1, :]
            if name == "final_norm_g":
                return s_fin[...]
            if name in _V256_ROWS:
                r = _V256_ROWS[name]
                return s_256[idx[0], r:r + 1, :]
            if name in _V512_ROWS:
                r = _V512_ROWS[name]
                return s_512[idx[0], r:r + 1, :]
            src = {"pool_w": s_pool, "rg_w_a": s_wa, "rg_w_x": s_wx}[name]
            return src[idx[0], idx[1] * 64:(idx[1] + 1) * 64, :]

        for a, name in enumerate(_SMALL_REPL):
            shape = w_r[a].shape
            if name == "final_norm_g":
                parts = [((), (slice(None), slice(None)))]
            elif len(shape) == 2:
                parts = [((l,), (slice(l, l + 1), slice(None))) for l in range(shape[0])]
            else:
                parts = [((l, h), (l, h)) for l in range(shape[0]) for h in range(shape[1])]
            for idx, sel in parts:
                g = grad_of(name, idx)
                d, mo, vo = _adam(w_r[a][sel], g, m_r[a][sel], v_r[a][sel])
                outs[4 * a][sel] = g
                outs[4 * a + 1][sel] = d
                outs[4 * a + 2][sel] = mo
                outs[4 * a + 3][sel] = vo

    out_shape = [jax.ShapeDtypeStruct(s, F32) for s in sum_shapes]
    for a in range(n):
        out_shape += [jax.ShapeDtypeStruct(ws[a].shape, F32)] * 4
    res = pl.pallas_call(body, name="adamw_small", out_shape=out_shape,
                         compiler_params=pltpu.CompilerParams(vmem_limit_bytes=VMEM_LIMIT))(
                             *gath1, *gath0, *gath_tail, *ws, *ms, *vs)
    return res[:10], res[10:]


_WEIGHTS = ['meta_tokens', 'mix_norm_g', 'w_in', 'pool_w', 'pool_scale', 'convb_dw_w', 'convb_dw_b', 'convb_ln_g',
            'convb_ln_b', 'convb_pw_w', 'rg_conv_w', 'rg_conv_b', 'rg_w_a', 'rg_b_a', 'rg_w_x', 'rg_b_x', 'rg_lambda',
            'w_out', 'mlp_norm_g', 'w_up', 'w_down', 'final_norm_g']


def _block_diag(w):
    L, H, C, _ = w.shape
    eye = jnp.eye(H, dtype=w.dtype)
    return (w[:, :, :, None, :] * eye[None, :, None, :, None]).reshape(L, H * C, H * C)


def _diag_blocks(m, H):
    C = m.shape[0] // H
    return jnp.concatenate([m[h * C:(h + 1) * C, h * C:(h + 1) * C] for h in range(H)], axis=0)


def kernel(x, meta_tokens, mix_norm_g, w_in, pool_w, pool_scale, convb_dw_w, convb_dw_b, convb_ln_g, convb_ln_b, convb_pw_w, rg_conv_w, rg_conv_b, rg_w_a, rg_b_a, rg_w_x, rg_b_x, rg_lambda, w_out, mlp_norm_g, w_up, w_down, final_norm_g, loss_target, m_meta_tokens, m_mix_norm_g, m_w_in, m_pool_w, m_pool_scale, m_convb_dw_w, m_convb_dw_b, m_convb_ln_g, m_convb_ln_b, m_convb_pw_w, m_rg_conv_w, m_rg_conv_b, m_rg_w_a, m_rg_b_a, m_rg_w_x, m_rg_b_x, m_rg_lambda, m_w_out, m_mlp_norm_g, m_w_up, m_w_down, m_final_norm_g, v_meta_tokens, v_mix_norm_g, v_w_in, v_pool_w, v_pool_scale, v_convb_dw_w, v_convb_dw_b, v_convb_ln_g, v_convb_ln_b, v_convb_pw_w, v_rg_conv_w, v_rg_conv_b, v_rg_w_a, v_rg_b_a, v_rg_w_x, v_rg_b_x, v_rg_lambda, v_w_out, v_mlp_norm_g, v_w_up, v_w_down, v_final_norm_g):
    W = dict(meta_tokens=meta_tokens, mix_norm_g=mix_norm_g, w_in=w_in, pool_w=pool_w, pool_scale=pool_scale,
             convb_dw_w=convb_dw_w, convb_dw_b=convb_dw_b, convb_ln_g=convb_ln_g, convb_ln_b=convb_ln_b,
             convb_pw_w=convb_pw_w, rg_conv_w=rg_conv_w, rg_conv_b=rg_conv_b, rg_w_a=rg_w_a, rg_b_a=rg_b_a,
             rg_w_x=rg_w_x, rg_b_x=rg_b_x, rg_lambda=rg_lambda, w_out=w_out, mlp_norm_g=mlp_norm_g, w_up=w_up,
             w_down=w_down, final_norm_g=final_norm_g.reshape(1, -1))
    M = dict(meta_tokens=m_meta_tokens, mix_norm_g=m_mix_norm_g, w_in=m_w_in, pool_w=m_pool_w, pool_scale=m_pool_scale,
             convb_dw_w=m_convb_dw_w, convb_dw_b=m_convb_dw_b, convb_ln_g=m_convb_ln_g, convb_ln_b=m_convb_ln_b,
             convb_pw_w=m_convb_pw_w, rg_conv_w=m_rg_conv_w, rg_conv_b=m_rg_conv_b, rg_w_a=m_rg_w_a, rg_b_a=m_rg_b_a,
             rg_w_x=m_rg_w_x, rg_b_x=m_rg_b_x, rg_lambda=m_rg_lambda, w_out=m_w_out, mlp_norm_g=m_mlp_norm_g,
             w_up=m_w_up, w_down=m_w_down, final_norm_g=m_final_norm_g.reshape(1, -1))
    V = dict(meta_tokens=v_meta_tokens, mix_norm_g=v_mix_norm_g, w_in=v_w_in, pool_w=v_pool_w, pool_scale=v_pool_scale,
             convb_dw_w=v_convb_dw_w, convb_dw_b=v_convb_dw_b, convb_ln_g=v_convb_ln_g, convb_ln_b=v_convb_ln_b,
             convb_pw_w=v_convb_pw_w, rg_conv_w=v_rg_conv_w, rg_conv_b=v_rg_conv_b, rg_w_a=v_rg_w_a, rg_b_a=v_rg_b_a,
             rg_w_x=v_rg_w_x, rg_b_x=v_rg_b_x, rg_lambda=v_rg_lambda, w_out=v_w_out, mlp_norm_g=v_mlp_norm_g,
             w_up=v_w_up, w_down=v_w_down, final_norm_g=v_final_norm_g.reshape(1, -1))

    xs = x[0]
    S, D = xs.shape
    T = S + N_META
    assert T % TT == 0 and D == 1024
    L = DEPTH
    me = 4 * lax.axis_index("x") + 2 * lax.axis_index("y") + lax.axis_index("c")
    c_in, c_ff = w_in.shape[-1], w_up.shape[-1]
    d_in, d_ff = c_in * N_DEV, c_ff * N_DEV
    r_out, r_dn = w_out.shape[1], w_down.shape[1]
    by_slot = lambda ref, s: ref.at[s]
    rows_of = lambda n: (lambda ref, s: ref.at[pl.ds(s * n, n), :])
    cols_of = lambda n: (lambda ref, s: ref.at[:, pl.ds(s * n, n)])

    win_t = jnp.transpose(w_in, (0, 2, 1)).astype(BF16)
    gathered = _allgather([
        (win_t[0], (d_in, D), rows_of(c_in)),
        (meta_tokens, (N_DEV,) + meta_tokens.shape, by_slot),
        (convb_dw_w, (N_DEV,) + convb_dw_w.shape, by_slot),
        (convb_pw_w, (N_DEV,) + convb_pw_w.shape, by_slot),
        (rg_conv_w, (N_DEV,) + rg_conv_w.shape, by_slot),
    ], "gather_first")
    win_f = [gathered[0], None]
    meta_f = jnp.transpose(gathered[1], (1, 0, 2)).reshape(N_META, D)
    dww_f = jnp.transpose(gathered[2], (1, 2, 0, 3)).reshape(L, CONV_K, D_CONV)
    wpw_f = jnp.transpose(gathered[3], (1, 0, 2, 3)).reshape(L, D_CONV, D_CONV).astype(BF16)
    rgw_f = jnp.transpose(gathered[4], (1, 2, 0, 3)).reshape(L, RG_CONV_K, D_RNN)
    wout_b, wup_b, wdown_b = w_out.astype(BF16), w_up.astype(BF16), w_down.astype(BF16)

    def weight_gather(blocks):
        comm = _Comm()
        for blk, shape, place in blocks:
            comm.gather(blk, shape, place)
        return comm

    wpool_bd = _block_diag(pool_w).astype(BF16)
    wa_bd = _block_diag(rg_w_a).astype(BF16)
    wx_bd = _block_diag(rg_w_x).astype(BF16)
    zeros4 = jnp.zeros((L, 4, D_POOL), F32)
    v256 = jnp.concatenate([pool_scale[:, None], convb_dw_b[:, None], convb_ln_g[:, None], convb_ln_b[:, None], zeros4], axis=1)
    dww_p = jnp.concatenate([dww_f, jnp.zeros((L, 1, D_CONV), F32)], axis=1)
    v512 = jnp.concatenate([rg_conv_b[:, None], rg_b_a[:, None], rg_b_x[:, None], rg_lambda[:, None], rgw_f], axis=1)
    mix_w = (wpool_bd, v256, dww_p, wpw_f, wa_bd, wx_bd, v512)

    h = jnp.concatenate([meta_f, xs], axis=0)
    tgt = jnp.concatenate([jnp.zeros((N_META, D), F32), loss_target[0]], axis=0)
    wout_f, wup_f, wdown_f = [None] * L, [None] * L, [None] * L
    saved = []

    (p,), wout_f = _rms_mm(h, mix_norm_g, win_f[0], 0, d_in, "in_proj0",
                           weight_gather([(wout_b[l], (r_out * N_DEV, D), rows_of(r_out)) for l in range(L)]),
                           w_is_nk=True)
    (y, hr, cb), (wup_f[0],) = _mixer_fwd(p, mix_w, 0, weight_gather([(wup_b[0], (D, d_ff), cols_of(c_ff))]))
    (h2, u2), (win_f[1],) = _mm_res(h, y, wout_f[0], y.shape[-1], False, "out_proj0",
                                    weight_gather([(win_t[1], (d_in, D), rows_of(c_in))]), norm=(mlp_norm_g, 0))
    (z,), (wdown_f[0],) = _rms_mm(u2, mlp_norm_g, wup_f[0], 0, 1024, "mlp_up0",
                                  weight_gather([(wdown_b[0], (d_ff, D), rows_of(r_dn))]), out_dtype=BF16)
    (h3, u1), (wup_f[1],) = _mm_res(h2, z, wdown_f[0], 2048, True, "mlp_down0",
                                    weight_gather([(wup_b[1], (D, d_ff), cols_of(c_ff))]), norm=(mix_norm_g, 1))
    saved.append((h, None, p, y, hr, cb, h2, u2, z))
    h = h3
    (p,), _ = _rms_mm(u1, mix_norm_g, win_f[1], 1, d_in, "in_proj1", w_is_nk=True)
    (y, hr, cb), (wdown_f[1],) = _mixer_fwd(p, mix_w, 1, weight_gather([(wdown_b[1], (d_ff, D), rows_of(r_dn))]))
    (h2, u2), _ = _mm_res(h, y, wout_f[1], y.shape[-1], False, "out_proj1", norm=(mlp_norm_g, 1))
    (z,), _ = _rms_mm(u2, mlp_norm_g, wup_f[1], 1, 1024, "mlp_up1", out_dtype=BF16)
    (h3,), _ = _mm_res(h2, z, wdown_f[1], 2048, True, "mlp_down1")
    saved.append((h, u1, p, y, hr, cb, h2, u2, z))

    dh, dhb, g_fin, loss_part = _final_loss(h3, W["final_norm_g"], tgt)
    loss = lax.psum(loss_part[0, 0], ("x", "y", "c"))

    def grad_exchange(pieces):
        comm = _Comm()
        for g, send, shape in pieces:
            comm.add(g, send, comm.add_out((N_DEV,) + shape, BF16), by_slot)
        return comm

    def small_gather(comm, grads):
        for g in grads:
            comm.gather(g, (N_DEV,) + g.shape, by_slot)
        return comm

    recv = {n: [None] * L for n in ("w_in", "w_out", "w_up", "w_down")}
    small1 = gath1 = gath0 = None
    for l in reversed(range(L)):
        h0, u1, p, y, hr, cb, h2, u2, z = saved[l]
        comm = small_gather(_Comm(), small1) if l == 0 else None
        (dz,), got = _bwd_dz(dhb, wdown_f[l], z, 512, f"mlp_down_bwd{l}", comm)
        if l == 0:
            gath1 = got
        (g_wdown,), _ = _mm_tn(z, dhb, True, None, None, l, "a", 512, BF16, f"mlp_down_wgrad{l}")
        (dh2, dh2b, g_mlp), (recv["w_down"][l],) = _bwd_rms(
            dz, wup_f[l], h2, mlp_norm_g, dh, l, 2048, f"mlp_up_bwd{l}",
            grad_exchange([(g_wdown, rows_of(r_dn), (r_dn, D))]))
        (g_wup,), _ = _mm_tn(u2, dz, False, None, None, l, "b", 1024, BF16, f"mlp_up_wgrad{l}")
        (dy,), _ = _bwd_plain(dh2b, wout_f[l], f"out_proj_bwd{l}")
        (g_wout,), _ = _mm_tn(y, dh2b, False, None, None, l, "b", D, BF16, f"out_proj_wgrad{l}")
        (dp, g256, g512, gpool, gpw, gwa, gwx), (recv["w_up"][l], recv["w_out"][l]) = _mixer_bwd(
            p, hr, cb, dy, mix_w, l, grad_exchange([(g_wup, cols_of(c_ff), (D, c_ff)),
                                                    (g_wout, rows_of(r_out), (r_out, D))]))
        smalls = [g_mlp, _diag_blocks(gpool, 4), g256, g512, gpw, _diag_blocks(gwa, RG_HEADS), _diag_blocks(gwx, RG_HEADS)]
        comm = small_gather(_Comm(), smalls[:5]) if l == 0 else None
        if u1 is None:
            (g_win_t,), got_a = _mm_tn(dp, h0, False, "b", mix_norm_g, l, "a", d_in // 2, BF16, f"in_proj_wgrad{l}", comm)
        else:
            (g_win_t,), got_a = _mm_tn(dp, u1, False, None, None, l, "a", d_in // 2, BF16, f"in_proj_wgrad{l}", comm)
        comm = grad_exchange([(g_win_t, rows_of(c_in), (c_in, D))])
        if l == 0:
            small_gather(comm, smalls[5:])
        (dh, dhb, g_mix), got_b = _bwd_rms(
            dp, win_f[l], h0, mix_norm_g, dh2, l, d_in, f"in_proj_bwd{l}", comm, w_is_kd=True)
        recv["w_in"][l] = got_b[0]
        if l == 0:
            gath0 = got_a + got_b[1:]
        if l == 1:
            small1 = [g_mix] + smalls + [g_fin]
    grad_x = dh[N_META:][None]

    gath_tail = _comm_only(small_gather(_Comm(), [g_mix, dh[:N_META]]), "exchange_tail")

    out = {}
    for name, tr in (("w_out", r_out), ("w_up", 256), ("w_down", 128)):
        out[name] = _adamw_big(recv[name], W[name], M[name], V[name], tr, f"adamw_{name}")
    g_win = jnp.transpose(_sum_slots(recv["w_in"], "sum_w_in"), (0, 2, 1))
    out["w_in"] = (g_win,) + tuple(_adamw_plain([g_win], [W["w_in"]], [M["w_in"]], [V["w_in"]], "adamw_w_in"))
    sums, small_out = _adamw_small(gath1, gath0, gath_tail, [W[n] for n in _SMALL_REPL], [M[n] for n in _SMALL_REPL],
                                   [V[n] for n in _SMALL_REPL])
    for a, name in enumerate(_SMALL_REPL):
        out[name] = small_out[4 * a:4 * a + 4]
    s_256, s_512, s_pw, s_meta = sums[4], sums[5], sums[6], sums[9]
    g_shard = {
        "meta_tokens": lax.dynamic_slice_in_dim(s_meta, me * 128, 128, axis=1),
        "convb_dw_w": lax.dynamic_slice_in_dim(s_256[:, :CONV_K, :], me * 32, 32, axis=2),
        "convb_pw_w": lax.dynamic_slice_in_dim(s_pw, me * 32, 32, axis=1),
        "rg_conv_w": lax.dynamic_slice_in_dim(s_512[:, 4:8, :], me * 64, 64, axis=2),
    }
    names = list(g_shard)
    res = _adamw_plain([g_shard[n] for n in names], [W[n] for n in names], [M[n] for n in names],
                       [V[n] for n in names], "adamw_small_sharded")
    for a, name in enumerate(names):
        out[name] = (g_shard[name],) + tuple(res[3 * a:3 * a + 3])

    def fix(name, arr):
        return arr.reshape(-1) if name == "final_norm_g" else arr

    return (loss, grad_x,
            *[fix(n, out[n][0]) for n in _WEIGHTS], *[fix(n, out[n][1]) for n in _WEIGHTS],
            *[fix(n, out[n][2]) for n in _WEIGHTS], *[fix(n, out[n][3]) for n in _WEIGHTS])
```

```python
import functools
import operator

import jax
import jax.numpy as jnp
from jax import lax
from jax.experimental import pallas as pl
from jax.experimental.pallas import tpu as pltpu

F32 = jnp.float32
BF16 = jnp.bfloat16

N_DEV = 8
N_META = 16
DEPTH = 2
EPS = 1e-6
POOL_WINDOWS = (2, 4, 8, 16)
POOL_GW = 64
CONV_K = 31
RG_CONV_K = 4
RG_HEADS = 8
RG_HD = 64
RG_C = 8.0
D_POOL = 256
D_CONV = 256
D_RNN = 512

ADAM_LR = 0.001
ADAM_B1 = 0.9
ADAM_B2 = 0.999
ADAM_EPS = 1e-08
ADAM_WD = 0.01
ADAM_STEP = 10

TT = 432
TD = 912
TD_BIG = 2736
HALO = 48
VMEM_LIMIT = 56 * 1024 * 1024

MESH = pl.DeviceIdType.MESH
ANY = pl.BlockSpec(memory_space=pl.ANY)


def _cparams(*sem):
    return pltpu.CompilerParams(dimension_semantics=sem, vmem_limit_bytes=VMEM_LIMIT)


def _full(shape):
    nd = len(shape)
    return pl.BlockSpec(shape, lambda *_: (0,) * nd)


def _layer(shape, l):
    nd = len(shape)
    return pl.BlockSpec((None,) + tuple(shape), lambda *_: (l,) + (0,) * nd)


def _dot(a, b):
    return jnp.dot(a, b, preferred_element_type=F32)


def _dot_nt(a, b):
    return lax.dot_general(a, b, (((1,), (1,)), ((), ())), preferred_element_type=F32)


def _dot_tn(a, b):
    return lax.dot_general(a, b, (((0,), (0,)), ((), ())), preferred_element_type=F32)


def _rms(h):
    r = lax.rsqrt(jnp.mean(h * h, axis=-1, keepdims=True) + EPS)
    return h * r, r


def _rms_bwd(du, xh, r, g):
    dxh = du * g
    return r * (dxh - xh * jnp.mean(dxh * xh, axis=-1, keepdims=True))


def _sigmoid(x):
    return 1.0 / (1.0 + jnp.exp(-x))


def _colsum(x):
    return jnp.sum(x, axis=0, keepdims=True)


def _mesh_pos():
    return lax.axis_index("x"), lax.axis_index("y"), lax.axis_index("c")


def _slot(d):
    return 4 * d[0] + 2 * d[1] + d[2]


class _Comm:
    def __init__(self):
        self.srcs, self.send, self.dst, self.land, self.out_shapes = [], [], [], [], []

    def add_out(self, shape, dtype):
        self.out_shapes.append(jax.ShapeDtypeStruct(shape, dtype))
        return len(self.out_shapes) - 1

    def add(self, src, send, dst, land):
        self.srcs.append(src)
        self.send.append(send)
        self.dst.append(dst)
        self.land.append(land)

    def gather(self, block, out_shape, place):
        self.add(block, lambda ref, s: ref, self.add_out(out_shape, block.dtype), place)

    def sem_shapes(self):
        n = len(self.srcs)
        return [pltpu.SemaphoreType.DMA((7 * n,)), pltpu.SemaphoreType.DMA((7 * n,)), pltpu.SemaphoreType.DMA((n,))]

    def _copies(self, cins, couts, send_sems, recv_sems, local_sems, with_arrivals=True):
        x, y, c = _mesh_pos()
        me = (x, y, c)
        sends, arrivals, local = [], [], []
        for a in range(len(self.srcs)):
            out = couts[self.dst[a]]
            mine = self.land[a](out, _slot(me))
            local.append(pltpu.make_async_copy(self.send[a](cins[a], _slot(me)), mine, local_sems.at[a]))
            for k in range(N_DEV - 1):
                d = k + 1
                peer = (1 - x if (d >> 2) & 1 else x, 1 - y if (d >> 1) & 1 else y, 1 - c if d & 1 else c)
                sems = dict(send_sem=send_sems.at[a * 7 + k], recv_sem=recv_sems.at[a * 7 + k],
                            device_id=peer, device_id_type=MESH)
                sends.append(pltpu.make_async_remote_copy(
                    src_ref=self.send[a](cins[a], _slot(peer)), dst_ref=mine, **sems))
                if with_arrivals:
                    theirs = self.land[a](out, _slot(peer))
                    arrivals.append(pltpu.make_async_remote_copy(src_ref=theirs, dst_ref=theirs, **sems))
        return sends, arrivals, local

    def start(self, *refs):
        sends, _, local = self._copies(*refs, with_arrivals=False)
        for cp in local + sends:
            cp.start()

    def wait(self, *refs):
        sends, arrivals, local = self._copies(*refs)
        for cp in arrivals:
            cp.wait_recv()
        for cp in sends:
            cp.wait_send()
        for cp in local:
            cp.wait()


def _call(body, *, name, grid, in_specs, out_specs, out_shape, operands, scratch_shapes=(), sem=None, comm=None):
    out_specs, out_shape = list(out_specs), list(out_shape)
    if comm is None:
        res = pl.pallas_call(body, name=name, grid=grid, in_specs=list(in_specs), out_specs=out_specs,
                             out_shape=out_shape, scratch_shapes=list(scratch_shapes),
                             compiler_params=_cparams(*sem))(*operands)
        return list(res), []
    n_in, n_out, n_scr = len(in_specs), len(out_specs), len(scratch_shapes)
    nc_in, nc_out = len(comm.srcs), len(comm.out_shapes)

    def hosted(*refs):
        ins, cins = refs[:n_in], refs[n_in:n_in + nc_in]
        o0 = n_in + nc_in
        outs, couts = refs[o0:o0 + n_out], refs[o0 + n_out:o0 + n_out + nc_out]
        s0 = o0 + n_out + nc_out
        scr, sems = refs[s0:s0 + n_scr], refs[s0 + n_scr:]
        ids = [pl.program_id(a) for a in range(len(grid))]
        first = functools.reduce(operator.and_, [i == 0 for i in ids])
        last = functools.reduce(operator.and_, [i == g - 1 for i, g in zip(ids, grid)])

        @pl.when(first)
        def _():
            comm.start(cins, couts, *sems)

        body(*ins, *outs, *scr)

        @pl.when(last)
        def _():
            comm.wait(cins, couts, *sems)

    res = pl.pallas_call(
        hosted, name=name, grid=grid, in_specs=list(in_specs) + [ANY] * nc_in,
        out_specs=out_specs + [ANY] * nc_out, out_shape=out_shape + comm.out_shapes,
        scratch_shapes=list(scratch_shapes) + comm.sem_shapes(),
        compiler_params=_cparams(*(["arbitrary"] * len(grid))))(*operands, *comm.srcs)
    return list(res[:n_out]), list(res[n_out:])


def _comm_only(comm, name):
    def body(*refs):
        n, m = len(comm.srcs), len(comm.out_shapes)
        comm.start(refs[:n], refs[n:n + m], *refs[n + m:])
        comm.wait(refs[:n], refs[n:n + m], *refs[n + m:])

    return list(pl.pallas_call(body, name=name, in_specs=[ANY] * len(comm.srcs), out_specs=[ANY] * len(comm.out_shapes),
                               out_shape=comm.out_shapes, scratch_shapes=comm.sem_shapes())(*comm.srcs))


def _allgather(items, name):
    n = len(items)

    def body(*refs):
        ins, outs = refs[:n], refs[n:2 * n]
        send_sems, recv_sems, local_sems = refs[2 * n:]
        x, y, c = _mesh_pos()
        me, sib = (x, y, c), (x, y, 1 - c)
        chips = [(1 - x, y), (x, 1 - y), (1 - x, 1 - y)]

        def copy(a, k, blk, to, from_input=False):
            region = items[a][2](outs[a], _slot(blk))
            return pltpu.make_async_remote_copy(
                src_ref=ins[a] if from_input else region, dst_ref=region,
                send_sem=send_sems.at[a * 7 + k], recv_sem=recv_sems.at[a * 7 + k],
                device_id=to, device_id_type=MESH)

        local = [pltpu.make_async_copy(ins[a], items[a][2](outs[a], _slot(me)), local_sems.at[a]) for a in range(n)]
        for cp in local:
            cp.start()
        first = []
        for a in range(n):
            first.append(copy(a, 0, me, sib, True))
            for j, chip in enumerate(chips):
                first.append(copy(a, 1 + j, me, (*chip, c), True))
        for cp in first:
            cp.start()
        passed = []
        for j, chip in enumerate(chips):
            for a in range(n):
                copy(a, 1 + j, (*chip, c), me).wait_recv()
                fwd = copy(a, 4 + j, (*chip, c), sib)
                fwd.start()
                passed.append(fwd)
        for a in range(n):
            copy(a, 0, sib, me).wait_recv()
            for j, chip in enumerate(chips):
                copy(a, 4 + j, (*chip, 1 - c), me).wait_recv()
        for cp in first + passed:
            cp.wait_send()
        for cp in local:
            cp.wait()

    return pl.pallas_call(
        body, name=name,
        in_specs=[ANY] * n, out_specs=[ANY] * n,
        out_shape=[jax.ShapeDtypeStruct(it[1], it[0].dtype) for it in items],
        scratch_shapes=[pltpu.SemaphoreType.DMA((7 * n,)), pltpu.SemaphoreType.DMA((7 * n,)),
                        pltpu.SemaphoreType.DMA((n,))],
    )(*[it[0] for it in items])


def _rms_mm(h, g, w, l, ck, name, comm=None, w_is_nk=False, out_dtype=F32):
    T, D = h.shape
    N = w.shape[0] if w_is_nk else w.shape[1]
    normed = h.dtype == BF16

    def body(h_ref, g_ref, w_ref, o_ref):
        if normed:
            u = h_ref[...]
        else:
            u = (_rms(h_ref[...])[0] * g_ref[l:l + 1, :]).astype(BF16)
        res = _dot_nt(u, w_ref[...]) if w_is_nk else _dot(u, w_ref[...])
        o_ref[...] = res.astype(out_dtype)

    w_spec = pl.BlockSpec((ck, D), lambda j, i: (j, 0)) if w_is_nk else pl.BlockSpec((D, ck), lambda j, i: (0, j))
    td = TD
    return _call(
        body, name=name, grid=(N // ck, T // td),
        in_specs=[pl.BlockSpec((td, D), lambda j, i: (i, 0)), _full(g.shape), w_spec],
        out_specs=[pl.BlockSpec((td, ck), lambda j, i: (i, j))],
        out_shape=[jax.ShapeDtypeStruct((T, N), out_dtype)],
        operands=(h, g, w), sem=("parallel", "parallel"), comm=comm)


def _mm_res(res, a, w, ck, relu2, name, comm=None, norm=None):
    T, D = res.shape
    K = a.shape[-1]
    nk = K // ck

    def body(*refs):
        if norm is None:
            r_ref, a_ref, w_ref, o_ref = refs
        else:
            r_ref, a_ref, w_ref, g_ref, o_ref, u_ref = refs
        k = pl.program_id(1)

        @pl.when(k == 0)
        def _():
            o_ref[...] = r_ref[...]

        av = a_ref[...]
        if relu2:
            av = jnp.square(jnp.maximum(av.astype(F32), 0.0))
        o_ref[...] += _dot(av.astype(BF16), w_ref[...])

        if norm is not None:
            @pl.when(k == nk - 1)
            def _():
                u_ref[...] = (_rms(o_ref[...])[0] * g_ref[norm[1]:norm[1] + 1, :]).astype(BF16)

    row = pl.BlockSpec((TD, D), lambda i, k: (i, 0))
    in_specs = [row, pl.BlockSpec((TD, ck), lambda i, k: (i, k)), pl.BlockSpec((ck, D), lambda i, k: (k, 0))]
    operands = (res, a, w)
    out_specs, out_shape = [row], [jax.ShapeDtypeStruct((T, D), F32)]
    if norm is not None:
        in_specs.append(_full(norm[0].shape))
        operands += (norm[0],)
        out_specs.append(row)
        out_shape.append(jax.ShapeDtypeStruct((T, D), BF16))
    return _call(
        body, name=name, grid=(T // TD, nk), in_specs=in_specs, out_specs=out_specs, out_shape=out_shape,
        operands=operands, sem=("parallel", "arbitrary"), comm=comm)


def _final_loss(h, gf, tgt):
    T, D = h.shape

    def body(h_ref, g_ref, t_ref, dh_ref, dhb_ref, dg_ref, loss_ref):
        i = pl.program_id(0)

        @pl.when(i == 0)
        def _():
            dg_ref[...] = jnp.zeros_like(dg_ref)
            loss_ref[...] = jnp.zeros_like(loss_ref)

        g = g_ref[...]
        xh, r = _rms(h_ref[...])
        row = i * TD + lax.broadcasted_iota(jnp.int32, (TD, 1), 0)
        keep = (row >= N_META).astype(F32)
        diff = (xh * g - t_ref[...]) * keep
        part = jnp.sum(jnp.sum(diff * diff, axis=1, keepdims=True), axis=0, keepdims=True)
        loss_ref[...] += (0.5 / D) * part
        dy = diff * (1.0 / D)
        dg_ref[...] += _colsum(dy * xh)
        dh = _rms_bwd(dy, xh, r, g)
        dh_ref[...] = dh
        dhb_ref[...] = dh.astype(BF16)

    row = pl.BlockSpec((TD, D), lambda i: (i, 0))
    return pl.pallas_call(
        body, name="final_loss", grid=(T // TD,),
        in_specs=[row, _full(gf.shape), row],
        out_specs=[row, row, _full((1, D)), _full((8, 128))],
        out_shape=[jax.ShapeDtypeStruct((T, D), F32), jax.ShapeDtypeStruct((T, D), BF16),
                   jax.ShapeDtypeStruct((1, D), F32), jax.ShapeDtypeStruct((8, 128), F32)],
        compiler_params=_cparams("arbitrary"))(h, gf, tgt)


def _bwd_dz(dh, wdown, z, ck, name, comm=None):
    T, D = dh.shape
    F = z.shape[-1]

    def body(d_ref, w_ref, z_ref, o_ref):
        dact = _dot_nt(d_ref[...], w_ref[...])
        o_ref[...] = (dact * (2.0 * jnp.maximum(z_ref[...].astype(F32), 0.0))).astype(BF16)

    return _call(
        body, name=name, grid=(F // ck, T // TD),
        in_specs=[pl.BlockSpec((TD, D), lambda j, i: (i, 0)), pl.BlockSpec((ck, D), lambda j, i: (j, 0)),
                  pl.BlockSpec((TD, ck), lambda j, i: (i, j))],
        out_specs=[pl.BlockSpec((TD, ck), lambda j, i: (i, j))],
        out_shape=[jax.ShapeDtypeStruct((T, F), BF16)],
        operands=(dh, wdown, z), sem=("parallel", "parallel"), comm=comm)


def _bwd_plain(dh, w, name, comm=None):
    T, D = dh.shape
    K = w.shape[0]

    def body(d_ref, w_ref, o_ref):
        o_ref[...] = _dot_nt(d_ref[...], w_ref[...])

    return _call(
        body, name=name, grid=(T // TD,),
        in_specs=[pl.BlockSpec((TD, D), lambda i: (i, 0)), _full((K, D))],
        out_specs=[pl.BlockSpec((TD, K), lambda i: (i, 0))],
        out_shape=[jax.ShapeDtypeStruct((T, K), F32)],
        operands=(dh, w), sem=("parallel",), comm=comm)


def _bwd_rms(dx, w, h, g, dh_in, l, ck, name, comm=None, w_is_kd=False):
    T, D = h.shape
    K = dx.shape[-1]
    nk = K // ck
    w_spec = pl.BlockSpec((ck, D), lambda i, k: (k, 0)) if w_is_kd else pl.BlockSpec((D, ck), lambda i, k: (0, k))

    def body(x_ref, w_ref, h_ref, g_ref, di_ref, dh_ref, dhb_ref, dg_ref, acc_ref):
        i, k = pl.program_id(0), pl.program_id(1)

        @pl.when((i == 0) & (k == 0))
        def _():
            dg_ref[...] = jnp.zeros_like(dg_ref)

        @pl.when(k == 0)
        def _():
            acc_ref[...] = jnp.zeros_like(acc_ref)

        acc_ref[...] += _dot(x_ref[...], w_ref[...]) if w_is_kd else _dot_nt(x_ref[...], w_ref[...])

        @pl.when(k == nk - 1)
        def _():
            du = acc_ref[...]
            xh, r = _rms(h_ref[...])
            dg_ref[...] += _colsum(du * xh)
            dh = di_ref[...] + _rms_bwd(du, xh, r, g_ref[l:l + 1, :])
            dh_ref[...] = dh
            dhb_ref[...] = dh.astype(BF16)

    row = pl.BlockSpec((TD, D), lambda i, k: (i, 0))
    return _call(
        body, name=name, grid=(T // TD, nk),
        in_specs=[pl.BlockSpec((TD, ck), lambda i, k: (i, k)), w_spec, row, _full(g.shape), row],
        out_specs=[row, row, _full((1, D))],
        out_shape=[jax.ShapeDtypeStruct((T, D), F32), jax.ShapeDtypeStruct((T, D), BF16),
                   jax.ShapeDtypeStruct((1, D), F32)],
        scratch_shapes=[pltpu.VMEM((TD, D), F32)],
        operands=(dx, w, h, g, dh_in), sem=("arbitrary", "arbitrary"), comm=comm)


def _mm_tn(a, b, relu2, rms_on, g, l, chunk, ck, out_dtype, name, comm=None):
    T = a.shape[0]
    Ka, Nb = a.shape[1], b.shape[1]
    ca, cb = (ck, Nb) if chunk == "a" else (Ka, ck)
    nj = (Ka // ck) if chunk == "a" else (Nb // ck)
    td = TD_BIG if a.dtype == BF16 and b.dtype == BF16 else TD
    nt = T // td

    def body(*refs):
        if rms_on is not None:
            a_ref, g_ref, b_ref, o_ref, acc_ref = refs
        else:
            a_ref, b_ref, o_ref, acc_ref = refs
        t = pl.program_id(1)

        @pl.when(t == 0)
        def _():
            acc_ref[...] = jnp.zeros_like(acc_ref)

        av, bv = a_ref[...], b_ref[...]
        if relu2:
            av = jnp.square(jnp.maximum(av.astype(F32), 0.0))
        if rms_on == "a":
            av = _rms(av)[0] * g_ref[l:l + 1, :]
        elif rms_on == "b":
            bv = _rms(bv)[0] * g_ref[l:l + 1, :]
        acc_ref[...] += _dot_tn(av.astype(BF16), bv.astype(BF16))

        @pl.when(t == nt - 1)
        def _():
            o_ref[...] = acc_ref[...].astype(out_dtype)

    if chunk == "a":
        a_spec = pl.BlockSpec((td, ca), lambda j, t: (t, j))
        b_spec = pl.BlockSpec((td, cb), lambda j, t: (t, 0))
        o_spec = pl.BlockSpec((ca, cb), lambda j, t: (j, 0))
    else:
        a_spec = pl.BlockSpec((td, ca), lambda j, t: (t, 0))
        b_spec = pl.BlockSpec((td, cb), lambda j, t: (t, j))
        o_spec = pl.BlockSpec((ca, cb), lambda j, t: (0, j))
    ins, specs = [a], [a_spec]
    if rms_on is not None:
        ins.append(g)
        specs.append(_full(g.shape))
    ins.append(b)
    specs.append(b_spec)
    return _call(
        body, name=name, grid=(nj, nt), in_specs=specs, out_specs=[o_spec],
        out_shape=[jax.ShapeDtypeStruct((Ka, Nb), out_dtype)], scratch_shapes=[pltpu.VMEM((ca, cb), F32)],
        operands=tuple(ins), sem=("parallel", "arbitrary"), comm=comm)


def _shift_down(x, k):
    return x if k == 0 else pltpu.roll(x, k, 0)


def _shift_up(x, k):
    return x if k == 0 else pltpu.roll(x, x.shape[0] - k, 0)


def _pool_sel(lane_grp, vals):
    return jnp.where(lane_grp == 0, vals[0], jnp.where(lane_grp == 1, vals[1], jnp.where(lane_grp == 2, vals[2], vals[3])))


def _pool_fwd(ue, pos):
    s = ue
    sums = []
    for k in (1, 2, 4, 8):
        s = s + _shift_down(s, k)
        sums.append(s[HALO:])
    grp = lax.broadcasted_iota(jnp.int32, (1, D_POOL), 1) // POOL_GW
    wsum = _pool_sel(grp, sums)
    width = _pool_sel(grp, [jnp.float32(w) for w in POOL_WINDOWS])
    cnt = jnp.minimum(pos.astype(F32), width)
    return wsum / cnt - ue[HALO:], cnt, grp


def _conv_fwd(xe, w_ref, nk):
    if isinstance(xe, jax.Array):
        assert nk <= 8
        window = lambda q: xe[HALO - 8:]
    else:
        window = lambda q: xe[pl.ds(HALO - 8 - 8 * q, TT + 8), :]
    return _taps(window, w_ref, nk, _shift_down)[8:]


def _taps(window, w_ref, nk, shift):
    acc = None
    for r in range(min(8, nk)):
        inner = None
        for q in range((nk - 1 - r) // 8 + 1):
            term = window(q) * w_ref(nk - 1 - 8 * q - r)
            inner = term if inner is None else inner + term
        inner = shift(inner, r)
        acc = inner if acc is None else acc + inner
    return acc


def _layernorm(cb):
    mu = jnp.mean(cb, axis=-1, keepdims=True)
    xc = cb - mu
    rstd = lax.rsqrt(jnp.mean(xc * xc, axis=-1, keepdims=True) + EPS)
    return xc * rstd, rstd


def _softplus_neg(lam):
    x = -lam
    e = jnp.exp(-jnp.abs(x))
    u = 1.0 + e
    d = u - 1.0
    log1p = jnp.where(d == 0.0, e, jnp.log(u) * (e / jnp.where(d == 0.0, 1.0, d)))
    return jnp.maximum(x, 0.0) + log1p


def _expm1(x):
    small = x * (1.0 + x * 0.5 * (1.0 + x * (1.0 / 3.0) * (1.0 + x * 0.25)))
    return jnp.where(jnp.abs(x) < 1e-2, small, jnp.exp(x) - 1.0)


def _gelu_parts(x):
    c0 = 0.7978845608028654
    inner = c0 * (x + 0.044715 * x * x * x)
    th = jnp.tanh(inner)
    gelu = 0.5 * x * (1.0 + th)
    dgelu = 0.5 * (1.0 + th) + 0.5 * x * (1.0 - th * th) * c0 * (1.0 + 3.0 * 0.044715 * x * x)
    return gelu, dgelu


def _rg_gates(xc, wa_ref, wx_ref, v512_ref):
    xb = xc.astype(BF16)
    ra = _sigmoid(_dot(xb, wa_ref[...]) + v512_ref[1:2, :])
    ri = _sigmoid(_dot(xb, wx_ref[...]) + v512_ref[2:3, :])
    sp = _softplus_neg(v512_ref[3:4, :])
    log_a = (-RG_C) * ra * sp
    a = jnp.exp(log_a)
    mult = jnp.sqrt(-_expm1(2.0 * log_a))
    return ra, ri, sp, a, mult


def _scan_down(a, b):
    n = a.shape[0]
    row = lax.broadcasted_iota(jnp.int32, (n, 1), 0)
    k = 1
    while k < n:
        ok = row >= k
        b = a * jnp.where(ok, _shift_down(b, k), 0.0) + b
        a = a * jnp.where(ok, _shift_down(a, k), 1.0)
        k *= 2
    return b, a


def _scan_up(c, g):
    n = c.shape[0]
    row = lax.broadcasted_iota(jnp.int32, (n, 1), 0)
    k = 1
    while k < n:
        ok = row < n - k
        g = c * jnp.where(ok, _shift_up(g, k), 0.0) + g
        c = c * jnp.where(ok, _shift_up(c, k), 1.0)
        k *= 2
    return g, c


def _mixer_specs(l, tile_of):
    r = TT // HALO
    return [
        pl.BlockSpec((TT, 1792), lambda i: (tile_of(i), 0)),
        pl.BlockSpec((HALO, 1792), lambda i: (jnp.maximum(tile_of(i) * r - 1, 0), 0)),
        _layer((256, 256), l), _layer((8, 256), l), _layer((32, 256), l), _layer((256, 256), l),
        _layer((512, 512), l), _layer((512, 512), l), _layer((8, 512), l),
    ]


def _mixer_fwd(p, wts, l, comm=None):
    T = p.shape[0]

    def body(p_ref, ph_ref, wpool_ref, v256_ref, dww_ref, wpw_ref, wa_ref, wx_ref, v512_ref, y_ref, hr_ref, cb_ref,
             carry_ref, ub_s):
        i = pl.program_id(0)

        @pl.when(i == 0)
        def _():
            carry_ref[...] = jnp.zeros_like(carry_ref)

        halo = jnp.where(i == 0, 0.0, ph_ref[...])
        e = jnp.concatenate([halo, p_ref[...]], axis=0)
        pos = i * TT + 1 + lax.broadcasted_iota(jnp.int32, (TT, 1), 0)
        pooled, _, _ = _pool_fwd(e[:, 0:256], pos)
        y_a = _dot(pooled.astype(BF16), wpool_ref[...]) * v256_ref[0:1, :]
        ub_s[...] = e[:, 256:512] * _sigmoid(e[:, 512:768])
        cb = _conv_fwd(ub_s, lambda k: dww_ref[k:k + 1, :], CONV_K) + v256_ref[1:2, :]
        cb_ref[...] = cb
        xhat, _ = _layernorm(cb)
        ln = xhat * v256_ref[2:3, :] + v256_ref[3:4, :]
        s = ln * _sigmoid(ln)
        y_b = _dot(s.astype(BF16), wpw_ref[...])
        xc = _conv_fwd(e[:, 1280:1792], lambda k: v512_ref[4 + k:5 + k, :], RG_CONV_K) + v512_ref[0:1, :]
        _, ri, _, a, mult = _rg_gates(xc, wa_ref, wx_ref, v512_ref)
        hloc, acum = _scan_down(a, mult * ri * xc)
        hr = hloc + acum * carry_ref[0:1, :]
        hr_ref[...] = hr
        carry_ref[0:1, :] = hr_ref[TT - 1:TT, :]
        gelu, _ = _gelu_parts(p_ref[:, 768:1280])
        y_ref[:, 0:256] = y_a.astype(BF16)
        y_ref[:, 256:512] = y_b.astype(BF16)
        y_ref[:, 512:1024] = (gelu * hr).astype(BF16)

    return _call(
        body, name=f"mixer_fwd{l}", grid=(T // TT,),
        in_specs=_mixer_specs(l, lambda i: i),
        out_specs=[pl.BlockSpec((TT, 1024), lambda i: (i, 0)), pl.BlockSpec((TT, 512), lambda i: (i, 0)),
                   pl.BlockSpec((TT, 256), lambda i: (i, 0))],
        out_shape=[jax.ShapeDtypeStruct((T, 1024), BF16), jax.ShapeDtypeStruct((T, 512), F32),
                   jax.ShapeDtypeStruct((T, 256), F32)],
        scratch_shapes=[pltpu.VMEM((8, 512), F32), pltpu.VMEM((HALO + TT, D_CONV), F32)],
        operands=(p, p, *wts), sem=("arbitrary",), comm=comm)


def _mixer_bwd(p, hr, cb, dy, wts, l, comm=None):
    T = p.shape[0]
    nt = T // TT
    tile_of = lambda i: nt - 1 - i

    def body(p_ref, ph_ref, wpool_ref, v256_ref, dww_ref, wpw_ref, wa_ref, wx_ref, v512_ref, hr_ref, hrh_ref, cb_ref,
             dy_ref, dp_ref, g256_ref, g512_ref, gpool_ref, gpw_ref, gwa_ref, gwx_ref,
             q_c, dcb_c, dxc_c, ag_c, ub_s, ed_s):
        i = pl.program_id(0)
        j = nt - 1 - i

        @pl.when(i == 0)
        def _():
            for ref in (g256_ref, g512_ref, gpool_ref, gpw_ref, gwa_ref, gwx_ref, q_c, dcb_c, dxc_c, ag_c):
                ref[...] = jnp.zeros_like(ref)

        halo = jnp.where(j == 0, 0.0, ph_ref[...])
        e = jnp.concatenate([halo, p_ref[...]], axis=0)
        pos = j * TT + 1 + lax.broadcasted_iota(jnp.int32, (TT, 1), 0)
        row = lax.broadcasted_iota(jnp.int32, (TT, 1), 0)

        dy_a = dy_ref[:, 0:256]
        pooled, cnt, grp = _pool_fwd(e[:, 0:256], pos)
        scale = v256_ref[0:1, :]
        mixed = _dot(pooled.astype(BF16), wpool_ref[...])
        g256_ref[34:35, :] += _colsum(dy_a * mixed)
        dmixed = (dy_a * scale).astype(BF16)
        gpool_ref[...] += _dot_tn(pooled.astype(BF16), dmixed)
        dpooled = _dot_nt(dmixed, wpool_ref[...])
        q = dpooled / cnt
        s = jnp.concatenate([q, q_c[...]], axis=0)
        sums = []
        for k in (1, 2, 4, 8):
            s = s + _shift_up(s, k)
            sums.append(s[:TT])
        dp_ref[:, 0:256] = (_pool_sel(grp, sums) - dpooled).astype(BF16)
        q_c[...] = q[0:16]

        dy_b = dy_ref[:, 256:512].astype(BF16)
        sg = _sigmoid(e[:, 512:768])
        ub_s[...] = e[:, 256:512] * sg
        xhat, rstd = _layernorm(cb_ref[...])
        ln_g = v256_ref[2:3, :]
        ln = xhat * ln_g + v256_ref[3:4, :]
        sig = _sigmoid(ln)
        sact = ln * sig
        gpw_ref[...] += _dot_tn(sact.astype(BF16), dy_b)
        dln = _dot_nt(dy_b, wpw_ref[...]) * (sig * (1.0 + ln * (1.0 - sig)))
        g256_ref[32:33, :] += _colsum(dln * xhat)
        g256_ref[33:34, :] += _colsum(dln)
        dxhat = dln * ln_g
        dcb = rstd * (dxhat - jnp.mean(dxhat, axis=-1, keepdims=True)
                      - xhat * jnp.mean(dxhat * xhat, axis=-1, keepdims=True))
        g256_ref[31:32, :] += _colsum(dcb)
        pad8 = jnp.zeros((8, D_CONV), F32)
        dcb_ext = jnp.concatenate([pad8, dcb, pad8], axis=0)
        for r in range(8):
            d_r = _shift_up(dcb_ext, r)[0:TT + 8]
            for q in range((CONV_K - 1 - r) // 8 + 1):
                k = CONV_K - 1 - 8 * q - r
                g256_ref[k:k + 1, :] += _colsum(d_r * ub_s[pl.ds(HALO - 8 - 8 * q, TT + 8), :])
        ed_s[0:TT, :] = dcb
        ed_s[TT:TT + 32, :] = dcb_c[...]
        dub = _taps(lambda q: ed_s[pl.ds(8 * q, TT + 8), :], lambda k: dww_ref[k:k + 1, :], CONV_K, _shift_up)[:TT]
        dcb_c[...] = dcb[0:32]
        sg_t = sg[HALO:]
        v_t = p_ref[:, 256:512]
        dp_ref[:, 256:512] = (dub * sg_t).astype(BF16)
        dp_ref[:, 512:768] = (dub * v_t * sg_t * (1.0 - sg_t)).astype(BF16)

        dy_c = dy_ref[:, 512:1024]
        cxe = e[:, 1280:1792]
        xc = _conv_fwd(cxe, lambda k: v512_ref[4 + k:5 + k, :], RG_CONV_K) + v512_ref[0:1, :]
        ra, ri, sp, a, mult = _rg_gates(xc, wa_ref, wx_ref, v512_ref)
        hrv = hr_ref[...]
        gelu, dgelu = _gelu_parts(p_ref[:, 768:1280])
        dp_ref[:, 768:1280] = (dy_c * hrv * dgelu).astype(BF16)
        dhr = dy_c * gelu
        coef = jnp.where(row < TT - 1, _shift_up(a, 1), 1.0)
        gloc, ccum = _scan_up(coef, dhr)
        gg = gloc + ccum * ag_c[0:1, :]
        ag_c[...] = (a * gg)[0:8]
        hr_before = jnp.where(j == 0, 0.0, hrh_ref[7:8, :])
        hr_prev = jnp.where(row >= 1, _shift_down(hrv, 1), hr_before)
        da = gg * hr_prev
        dmult = gg * ri * xc
        dri = gg * mult * xc
        dxc = gg * mult * ri
        dlog_a = da * a - dmult * (a * a / mult)
        g512_ref[3:4, :] += _colsum(dlog_a * ((-RG_C) * ra))
        dpre_a = dlog_a * ((-RG_C) * sp) * ra * (1.0 - ra)
        dpre_x = dri * ri * (1.0 - ri)
        g512_ref[1:2, :] += _colsum(dpre_a)
        g512_ref[2:3, :] += _colsum(dpre_x)
        xb = xc.astype(BF16)
        dpa_b, dpx_b = dpre_a.astype(BF16), dpre_x.astype(BF16)
        gwa_ref[...] += _dot_tn(xb, dpa_b)
        gwx_ref[...] += _dot_tn(xb, dpx_b)
        dxc = dxc + _dot_nt(dpa_b, wa_ref[...]) + _dot_nt(dpx_b, wx_ref[...])
        g512_ref[0:1, :] += _colsum(dxc)
        for k in range(RG_CONV_K):
            g512_ref[4 + k:5 + k, :] += _colsum(dxc * _shift_down(cxe, RG_CONV_K - 1 - k)[HALO:])
        ex = jnp.concatenate([dxc, dxc_c[...]], axis=0)
        dcx = _taps(lambda q: ex, lambda k: v512_ref[4 + k:5 + k, :], RG_CONV_K, _shift_up)[:TT]
        dxc_c[...] = dxc[0:8]
        dp_ref[:, 1280:1792] = dcx.astype(BF16)

        @pl.when(i == nt - 1)
        def _():
            lam = v512_ref[3:4, :]
            g512_ref[3:4, :] = g512_ref[3:4, :] * (-_sigmoid(-lam))

    in_specs = _mixer_specs(l, tile_of) + [
        pl.BlockSpec((TT, 512), lambda i: (tile_of(i), 0)),
        pl.BlockSpec((8, 512), lambda i: (jnp.maximum(tile_of(i) * (TT // 8) - 1, 0), 0)),
        pl.BlockSpec((TT, 256), lambda i: (tile_of(i), 0)),
        pl.BlockSpec((TT, 1024), lambda i: (tile_of(i), 0)),
    ]
    acc_shapes = [(40, 256), (8, 512), (256, 256), (256, 256), (512, 512), (512, 512)]
    return _call(
        body, name=f"mixer_bwd{l}", grid=(nt,),
        in_specs=in_specs,
        out_specs=[pl.BlockSpec((TT, 1792), lambda i: (tile_of(i), 0))] + [_full(s) for s in acc_shapes],
        out_shape=[jax.ShapeDtypeStruct((T, 1792), BF16)] + [jax.ShapeDtypeStruct(s, F32) for s in acc_shapes],
        scratch_shapes=[pltpu.VMEM((16, 256), F32), pltpu.VMEM((32, 256), F32), pltpu.VMEM((8, 512), F32),
                        pltpu.VMEM((8, 512), F32), pltpu.VMEM((HALO + TT, D_CONV), F32),
                        pltpu.VMEM((TT + 32, D_CONV), F32)],
        operands=(p, p, *wts, hr, hr, cb, dy), sem=("arbitrary",), comm=comm)


def _adam(w, g, m, v):
    m = ADAM_B1 * m + (1.0 - ADAM_B1) * g
    v = ADAM_B2 * v + (1.0 - ADAM_B2) * (g * g)
    m_hat = m / (1.0 - ADAM_B1 ** ADAM_STEP)
    v_hat = v / (1.0 - ADAM_B2 ** ADAM_STEP)
    delta = -ADAM_LR * (m_hat / (jnp.sqrt(v_hat) + ADAM_EPS) + ADAM_WD * w)
    return delta, m, v


def _adamw_big(recvs, w, m, v, tr, name):
    L, R, C = w.shape

    def body(r0_ref, r1_ref, w_ref, m_ref, v_ref, g_ref, d_ref, mo_ref, vo_ref):
        l = pl.program_id(0)

        def total(r_ref):
            g = r_ref[0].astype(F32)
            for s in range(1, N_DEV):
                g = g + r_ref[s].astype(F32)
            g_ref[...] = g

        @pl.when(l == 0)
        def _():
            total(r0_ref)

        @pl.when(l == 1)
        def _():
            total(r1_ref)

        d_ref[...], mo_ref[...], vo_ref[...] = _adam(w_ref[...], g_ref[...], m_ref[...], v_ref[...])

    blk = pl.BlockSpec((None, tr, C), lambda l, i: (l, i, 0))
    return pl.pallas_call(
        body, name=name, grid=(L, R // tr),
        in_specs=[pl.BlockSpec((N_DEV, tr, C), lambda l, i: (0, i * (1 - l), 0)),
                  pl.BlockSpec((N_DEV, tr, C), lambda l, i: (0, i * l, 0)), blk, blk, blk],
        out_specs=[blk] * 4,
        out_shape=[jax.ShapeDtypeStruct((L, R, C), F32)] * 4,
        compiler_params=_cparams("arbitrary", "arbitrary"))(recvs[0], recvs[1], w, m, v)


def _sum_slots(recvs, name):
    L = len(recvs)
    _, R, C = recvs[0].shape

    def body(*refs):
        o_ref = refs[L]
        for l in range(L):
            acc = refs[l][0].astype(F32)
            for s in range(1, N_DEV):
                acc = acc + refs[l][s].astype(F32)
            o_ref[l] = acc

    return pl.pallas_call(body, name=name, out_shape=jax.ShapeDtypeStruct((L, R, C), F32))(*recvs)


def _adamw_plain(gs, ws, ms, vs, name):
    n = len(gs)

    def body(*refs):
        g_r, w_r, m_r, v_r = refs[:n], refs[n:2 * n], refs[2 * n:3 * n], refs[3 * n:4 * n]
        outs = refs[4 * n:]
        for a in range(n):
            d, mo, vo = _adam(w_r[a][...], g_r[a][...], m_r[a][...], v_r[a][...])
            outs[3 * a][...] = d
            outs[3 * a + 1][...] = mo
            outs[3 * a + 2][...] = vo

    shapes = []
    for a in range(n):
        shapes += [jax.ShapeDtypeStruct(ws[a].shape, F32)] * 3
    return pl.pallas_call(body, name=name, out_shape=shapes,
                          compiler_params=pltpu.CompilerParams(vmem_limit_bytes=VMEM_LIMIT))(*gs, *ws, *ms, *vs)


_V256_ROWS = {"convb_dw_b": 31, "convb_ln_g": 32, "convb_ln_b": 33, "pool_scale": 34}
_V512_ROWS = {"rg_conv_b": 0, "rg_b_a": 1, "rg_b_x": 2, "rg_lambda": 3}
_SMALL_REPL = ["mix_norm_g", "mlp_norm_g", "final_norm_g", "pool_w", "pool_scale", "convb_dw_b", "convb_ln_g",
               "convb_ln_b", "rg_conv_b", "rg_w_a", "rg_b_a", "rg_w_x", "rg_b_x", "rg_lambda"]


def _adamw_small(gath1, gath0, gath_tail, ws, ms, vs):
    n = len(_SMALL_REPL)
    n1, n0, nt = len(gath1), len(gath0), len(gath_tail)
    ng = n1 + n0 + nt
    L = DEPTH
    per_layer = [g.shape[1:] for g in gath1[:8]]
    sum_shapes = ([(L,) + s[1:] for s in per_layer[:2]] + [gath1[8].shape[1:]] + [(L,) + s for s in per_layer[2:]]
                  + [gath_tail[1].shape[1:]])

    def body(*refs):
        in1, in0, in_t = refs[:n1], refs[n1:n1 + n0], refs[n1 + n0:ng]
        w_r, m_r, v_r = refs[ng:ng + n], refs[ng + n:ng + 2 * n], refs[ng + 2 * n:ng + 3 * n]
        sums = refs[ng + 3 * n:ng + 3 * n + 10]
        outs = refs[ng + 3 * n + 10:]

        def total(ref):
            acc = ref[0]
            for s in range(1, N_DEV):
                acc = acc + ref[s]
            return acc

        s_mix, s_mlp, s_fin, s_pool, s_256, s_512, s_pw, s_wa, s_wx, s_meta = sums
        s_mix[1:2, :] = total(in1[0])
        s_mix[0:1, :] = total(in_t[0])
        s_mlp[1:2, :] = total(in1[1])
        s_mlp[0:1, :] = total(in0[0])
        s_fin[...] = total(in1[8])
        s_meta[...] = total(in_t[1])
        for k, dst in enumerate((s_pool, s_256, s_512, s_pw, s_wa, s_wx)):
            dst[1] = total(in1[2 + k])
            dst[0] = total(in0[1 + k])

        def grad_of(name, idx):
            if name == "mix_norm_g":
                return s_mix[idx[0]:idx[0] + 1, :]
            if name == "mlp_norm_g":
                return s_mlp[idx[0]:idx[0] + 1, :]
            if name == "final_norm_g":
                return s_fin[...]
            if name in _V256_ROWS:
                r = _V256_ROWS[name]
                return s_256[idx[0], r:r + 1, :]
            if name in _V512_ROWS:
                r = _V512_ROWS[name]
                return s_512[idx[0], r:r + 1, :]
            src = {"pool_w": s_pool, "rg_w_a": s_wa, "rg_w_x": s_wx}[name]
            return src[idx[0], :, idx[1] * 64:(idx[1] + 1) * 64]

        for a, name in enumerate(_SMALL_REPL):
            shape = w_r[a].shape
            if name == "final_norm_g":
                parts = [((), (slice(None), slice(None)))]
            elif len(shape) == 2:
                parts = [((l,), (slice(l, l + 1), slice(None))) for l in range(shape[0])]
            else:
                parts = [((l, h), (l, h)) for l in range(shape[0]) for h in range(shape[1])]
            for idx, sel in parts:
                g = grad_of(name, idx)
                d, mo, vo = _adam(w_r[a][sel], g, m_r[a][sel], v_r[a][sel])
                outs[4 * a][sel] = g
                outs[4 * a + 1][sel] = d
                outs[4 * a + 2][sel] = mo
                outs[4 * a + 3][sel] = vo

    out_shape = [jax.ShapeDtypeStruct(s, F32) for s in sum_shapes]
    for a in range(n):
        out_shape += [jax.ShapeDtypeStruct(ws[a].shape, F32)] * 4
    res = pl.pallas_call(body, name="adamw_small", out_shape=out_shape,
                         compiler_params=pltpu.CompilerParams(vmem_limit_bytes=VMEM_LIMIT))(
                             *gath1, *gath0, *gath_tail, *ws, *ms, *vs)
    return res[:10], res[10:]


_WEIGHTS = ['meta_tokens', 'mix_norm_g', 'w_in', 'pool_w', 'pool_scale', 'convb_dw_w', 'convb_dw_b', 'convb_ln_g',
            'convb_ln_b', 'convb_pw_w', 'rg_conv_w', 'rg_conv_b', 'rg_w_a', 'rg_b_a', 'rg_w_x', 'rg_b_x', 'rg_lambda',
            'w_out', 'mlp_norm_g', 'w_up', 'w_down', 'final_norm_g']


def _block_diag(w):
    L, H, C, _ = w.shape
    eye = jnp.eye(H, dtype=w.dtype)
    return (w[:, :, :, None, :] * eye[None, :, None, :, None]).reshape(L, H * C, H * C)


def _diag_blocks(m, H):
    C = m.shape[0] // H
    return jnp.concatenate([m[h * C:(h + 1) * C, h * C:(h + 1) * C] for h in range(H)], axis=1)


def kernel(x, meta_tokens, mix_norm_g, w_in, pool_w, pool_scale, convb_dw_w, convb_dw_b, convb_ln_g, convb_ln_b, convb_pw_w, rg_conv_w, rg_conv_b, rg_w_a, rg_b_a, rg_w_x, rg_b_x, rg_lambda, w_out, mlp_norm_g, w_up, w_down, final_norm_g, loss_target, m_meta_tokens, m_mix_norm_g, m_w_in, m_pool_w, m_pool_scale, m_convb_dw_w, m_convb_dw_b, m_convb_ln_g, m_convb_ln_b, m_convb_pw_w, m_rg_conv_w, m_rg_conv_b, m_rg_w_a, m_rg_b_a, m_rg_w_x, m_rg_b_x, m_rg_lambda, m_w_out, m_mlp_norm_g, m_w_up, m_w_down, m_final_norm_g, v_meta_tokens, v_mix_norm_g, v_w_in, v_pool_w, v_pool_scale, v_convb_dw_w, v_convb_dw_b, v_convb_ln_g, v_convb_ln_b, v_convb_pw_w, v_rg_conv_w, v_rg_conv_b, v_rg_w_a, v_rg_b_a, v_rg_w_x, v_rg_b_x, v_rg_lambda, v_w_out, v_mlp_norm_g, v_w_up, v_w_down, v_final_norm_g):
    W = dict(meta_tokens=meta_tokens, mix_norm_g=mix_norm_g, w_in=w_in, pool_w=pool_w, pool_scale=pool_scale,
             convb_dw_w=convb_dw_w, convb_dw_b=convb_dw_b, convb_ln_g=convb_ln_g, convb_ln_b=convb_ln_b,
             convb_pw_w=convb_pw_w, rg_conv_w=rg_conv_w, rg_conv_b=rg_conv_b, rg_w_a=rg_w_a, rg_b_a=rg_b_a,
             rg_w_x=rg_w_x, rg_b_x=rg_b_x, rg_lambda=rg_lambda, w_out=w_out, mlp_norm_g=mlp_norm_g, w_up=w_up,
             w_down=w_down, final_norm_g=final_norm_g.reshape(1, -1))
    M = dict(meta_tokens=m_meta_tokens, mix_norm_g=m_mix_norm_g, w_in=m_w_in, pool_w=m_pool_w, pool_scale=m_pool_scale,
             convb_dw_w=m_convb_dw_w, convb_dw_b=m_convb_dw_b, convb_ln_g=m_convb_ln_g, convb_ln_b=m_convb_ln_b,
             convb_pw_w=m_convb_pw_w, rg_conv_w=m_rg_conv_w, rg_conv_b=m_rg_conv_b, rg_w_a=m_rg_w_a, rg_b_a=m_rg_b_a,
             rg_w_x=m_rg_w_x, rg_b_x=m_rg_b_x, rg_lambda=m_rg_lambda, w_out=m_w_out, mlp_norm_g=m_mlp_norm_g,
             w_up=m_w_up, w_down=m_w_down, final_norm_g=m_final_norm_g.reshape(1, -1))
    V = dict(meta_tokens=v_meta_tokens, mix_norm_g=v_mix_norm_g, w_in=v_w_in, pool_w=v_pool_w, pool_scale=v_pool_scale,
             convb_dw_w=v_convb_dw_w, convb_dw_b=v_convb_dw_b, convb_ln_g=v_convb_ln_g, convb_ln_b=v_convb_ln_b,
             convb_pw_w=v_convb_pw_w, rg_conv_w=v_rg_conv_w, rg_conv_b=v_rg_conv_b, rg_w_a=v_rg_w_a, rg_b_a=v_rg_b_a,
             rg_w_x=v_rg_w_x, rg_b_x=v_rg_b_x, rg_lambda=v_rg_lambda, w_out=v_w_out, mlp_norm_g=v_mlp_norm_g,
             w_up=v_w_up, w_down=v_w_down, final_norm_g=v_final_norm_g.reshape(1, -1))

    xs = x[0]
    S, D = xs.shape
    T = S + N_META
    assert T % TT == 0 and D == 1024
    L = DEPTH
    me = 4 * lax.axis_index("x") + 2 * lax.axis_index("y") + lax.axis_index("c")
    c_in, c_ff = w_in.shape[-1], w_up.shape[-1]
    d_in, d_ff = c_in * N_DEV, c_ff * N_DEV
    r_out, r_dn = w_out.shape[1], w_down.shape[1]
    by_slot = lambda ref, s: ref.at[s]
    rows_of = lambda n: (lambda ref, s: ref.at[pl.ds(s * n, n), :])
    cols_of = lambda n: (lambda ref, s: ref.at[:, pl.ds(s * n, n)])

    win_t = jnp.transpose(w_in, (0, 2, 1)).astype(BF16)
    gathered = _allgather([
        (win_t[0], (d_in, D), rows_of(c_in)),
        (meta_tokens, (N_DEV,) + meta_tokens.shape, by_slot),
        (convb_dw_w, (N_DEV,) + convb_dw_w.shape, by_slot),
        (convb_pw_w, (N_DEV,) + convb_pw_w.shape, by_slot),
        (rg_conv_w, (N_DEV,) + rg_conv_w.shape, by_slot),
    ], "gather_first")
    win_f = [gathered[0], None]
    meta_f = jnp.transpose(gathered[1], (1, 0, 2)).reshape(N_META, D)
    dww_f = jnp.transpose(gathered[2], (1, 2, 0, 3)).reshape(L, CONV_K, D_CONV)
    wpw_f = jnp.transpose(gathered[3], (1, 0, 2, 3)).reshape(L, D_CONV, D_CONV).astype(BF16)
    rgw_f = jnp.transpose(gathered[4], (1, 2, 0, 3)).reshape(L, RG_CONV_K, D_RNN)
    wout_b, wup_b, wdown_b = w_out.astype(BF16), w_up.astype(BF16), w_down.astype(BF16)

    def weight_gather(blocks):
        comm = _Comm()
        for blk, shape, place in blocks:
            comm.gather(blk, shape, place)
        return comm

    wpool_bd = _block_diag(pool_w).astype(BF16)
    wa_bd = _block_diag(rg_w_a).astype(BF16)
    wx_bd = _block_diag(rg_w_x).astype(BF16)
    zeros4 = jnp.zeros((L, 4, D_POOL), F32)
    v256 = jnp.concatenate([pool_scale[:, None], convb_dw_b[:, None], convb_ln_g[:, None], convb_ln_b[:, None], zeros4], axis=1)
    dww_p = jnp.concatenate([dww_f, jnp.zeros((L, 1, D_CONV), F32)], axis=1)
    v512 = jnp.concatenate([rg_conv_b[:, None], rg_b_a[:, None], rg_b_x[:, None], rg_lambda[:, None], rgw_f], axis=1)
    mix_w = (wpool_bd, v256, dww_p, wpw_f, wa_bd, wx_bd, v512)

    h = jnp.concatenate([meta_f, xs], axis=0)
    tgt = jnp.concatenate([jnp.zeros((N_META, D), F32), loss_target[0]], axis=0)
    wout_f, wup_f, wdown_f = [None] * L, [None] * L, [None] * L
    saved = []

    (p,), _ = _rms_mm(h, mix_norm_g, win_f[0], 0, d_in, "in_proj0", w_is_nk=True)
    (y, hr, cb), (wup_f[0], wout_f[0], wout_f[1], win_f[1]) = _mixer_fwd(p, mix_w, 0, weight_gather(
        [(wup_b[0], (D, d_ff), cols_of(c_ff))]
        + [(wout_b[l], (r_out * N_DEV, D), rows_of(r_out)) for l in range(L)]
        + [(win_t[1], (d_in, D), rows_of(c_in))]))
    (h2, u2), _ = _mm_res(h, y, wout_f[0], y.shape[-1], False, "out_proj0", norm=(mlp_norm_g, 0))
    (z,), (wdown_f[0],) = _rms_mm(u2, mlp_norm_g, wup_f[0], 0, 2048, "mlp_up0",
                                  weight_gather([(wdown_b[0], (d_ff, D), rows_of(r_dn))]), out_dtype=BF16)
    (h3, u1), _ = _mm_res(h2, z, wdown_f[0], 2048, True, "mlp_down0", norm=(mix_norm_g, 1))
    saved.append((h, None, p, y, hr, cb, h2, u2, z))
    h = h3
    (p,), _ = _rms_mm(u1, mix_norm_g, win_f[1], 1, d_in, "in_proj1", w_is_nk=True)
    (y, hr, cb), (wdown_f[1], wup_f[1]) = _mixer_fwd(p, mix_w, 1, weight_gather(
        [(wdown_b[1], (d_ff, D), rows_of(r_dn)), (wup_b[1], (D, d_ff), cols_of(c_ff))]))
    (h2, u2), _ = _mm_res(h, y, wout_f[1], y.shape[-1], False, "out_proj1", norm=(mlp_norm_g, 1))
    (z,), _ = _rms_mm(u2, mlp_norm_g, wup_f[1], 1, 2048, "mlp_up1", out_dtype=BF16)
    (h3,), _ = _mm_res(h2, z, wdown_f[1], 2048, True, "mlp_down1")
    saved.append((h, u1, p, y, hr, cb, h2, u2, z))

    dh, dhb, g_fin, loss_part = _final_loss(h3, W["final_norm_g"], tgt)
    loss = lax.psum(loss_part[0, 0], ("x", "y", "c"))

    def grad_exchange(pieces):
        comm = _Comm()
        for g, send, shape in pieces:
            comm.add(g, send, comm.add_out((N_DEV,) + shape, BF16), by_slot)
        return comm

    def small_gather(comm, grads):
        for g in grads:
            comm.gather(g, (N_DEV,) + g.shape, by_slot)
        return comm

    recv = {n: [None] * L for n in ("w_in", "w_out", "w_up", "w_down")}
    small1 = gath1 = gath0 = g_win_t1 = None
    for l in reversed(range(L)):
        h0, u1, p, y, hr, cb, h2, u2, z = saved[l]
        (dz,), _ = _bwd_dz(dhb, wdown_f[l], z, 2048, f"mlp_down_bwd{l}")
        (g_wdown,), _ = _mm_tn(z, dhb, True, None, None, l, "a", 512, BF16, f"mlp_down_wgrad{l}")
        comm = grad_exchange([(g_wdown, rows_of(r_dn), (r_dn, D))]) if l == 0 else None
        (dh2, dh2b, g_mlp), got = _bwd_rms(dz, wup_f[l], h2, mlp_norm_g, dh, l, 2048, f"mlp_up_bwd{l}", comm)
        if l == 0:
            recv["w_down"][0] = got[0]
        (g_wup,), _ = _mm_tn(u2, dz, False, None, None, l, "b", 1024, BF16, f"mlp_up_wgrad{l}")
        (dy,), _ = _bwd_plain(dh2b, wout_f[l], f"out_proj_bwd{l}")
        (g_wout,), _ = _mm_tn(y, dh2b, False, None, None, l, "b", D, BF16, f"out_proj_wgrad{l}")
        comm = grad_exchange([(g_wup, cols_of(c_ff), (D, c_ff)), (g_wout, rows_of(r_out), (r_out, D))]
                             + ([(g_wdown, rows_of(r_dn), (r_dn, D))] if l == 1 else
                                [(g_win_t1, rows_of(c_in), (c_in, D))]))
        if l == 0:
            small_gather(comm, small1)
        (dp, g256, g512, gpool, gpw, gwa, gwx), got = _mixer_bwd(p, hr, cb, dy, mix_w, l, comm)
        recv["w_up"][l], recv["w_out"][l] = got[0], got[1]
        if l == 1:
            recv["w_down"][1] = got[2]
        else:
            recv["w_in"][1], gath1 = got[2], got[3:]
        smalls = [g_mlp, _diag_blocks(gpool, 4), g256, g512, gpw, _diag_blocks(gwa, RG_HEADS), _diag_blocks(gwx, RG_HEADS)]
        comm = small_gather(_Comm(), smalls[:5]) if l == 0 else None
        if u1 is None:
            (g_win_t,), got_a = _mm_tn(dp, h0, False, "b", mix_norm_g, l, "a", d_in // 2, BF16, f"in_proj_wgrad{l}", comm)
        else:
            (g_win_t,), got_a = _mm_tn(dp, u1, False, None, None, l, "a", d_in // 2, BF16, f"in_proj_wgrad{l}", comm)
        comm = small_gather(grad_exchange([(g_win_t, rows_of(c_in), (c_in, D))]), smalls[5:]) if l == 0 else None
        (dh, dhb, g_mix), got_b = _bwd_rms(
            dp, win_f[l], h0, mix_norm_g, dh2, l, d_in, f"in_proj_bwd{l}", comm, w_is_kd=True)
        if l == 0:
            recv["w_in"][0] = got_b[0]
            gath0 = got_a + got_b[1:]
        if l == 1:
            g_win_t1 = g_win_t
            small1 = [g_mix] + smalls + [g_fin]
    grad_x = dh[N_META:][None]

    gath_tail = _comm_only(small_gather(_Comm(), [g_mix, dh[:N_META]]), "exchange_tail")

    out = {}
    for name, tr in (("w_out", r_out), ("w_up", 256), ("w_down", 128)):
        out[name] = _adamw_big(recv[name], W[name], M[name], V[name], tr, f"adamw_{name}")
    g_win = jnp.transpose(_sum_slots(recv["w_in"], "sum_w_in"), (0, 2, 1))
    out["w_in"] = (g_win,) + tuple(_adamw_plain([g_win], [W["w_in"]], [M["w_in"]], [V["w_in"]], "adamw_w_in"))
    sums, small_out = _adamw_small(gath1, gath0, gath_tail, [W[n] for n in _SMALL_REPL], [M[n] for n in _SMALL_REPL],
                                   [V[n] for n in _SMALL_REPL])
    for a, name in enumerate(_SMALL_REPL):
        out[name] = small_out[4 * a:4 * a + 4]
    s_256, s_512, s_pw, s_meta = sums[4], sums[5], sums[6], sums[9]
    g_shard = {
        "meta_tokens": lax.dynamic_slice_in_dim(s_meta, me * 128, 128, axis=1),
        "convb_dw_w": lax.dynamic_slice_in_dim(s_256[:, :CONV_K, :], me * 32, 32, axis=2),
        "convb_pw_w": lax.dynamic_slice_in_dim(s_pw, me * 32, 32, axis=1),
        "rg_conv_w": lax.dynamic_slice_in_dim(s_512[:, 4:8, :], me * 64, 64, axis=2),
    }
    names = list(g_shard)
    res = _adamw_plain([g_shard[n] for n in names], [W[n] for n in names], [M[n] for n in names],
                       [V[n] for n in names], "adamw_small_sharded")
    for a, name in enumerate(names):
        out[name] = (g_shard[name],) + tuple(res[3 * a:3 * a + 3])

    def fix(name, arr):
        return arr.reshape(-1) if name == "final_norm_g" else arr

    return (loss, grad_x,
            *[fix(n, out[n][0]) for n in _WEIGHTS], *[fix(n, out[n][1]) for n in _WEIGHTS],
            *[fix(n, out[n][2]) for n in _WEIGHTS], *[fix(n, out[n][3]) for n in _WEIGHTS])
```

```python
import functools
import operator

import jax
import jax.numpy as jnp
from jax import lax
from jax.experimental import pallas as pl
from jax.experimental.pallas import tpu as pltpu

F32 = jnp.float32
BF16 = jnp.bfloat16

N_DEV = 8
N_META = 16
DEPTH = 2
EPS = 1e-6
POOL_WINDOWS = (2, 4, 8, 16)
POOL_GW = 64
CONV_K = 31
RG_CONV_K = 4
RG_HEADS = 8
RG_HD = 64
RG_C = 8.0
D_POOL = 256
D_CONV = 256
D_RNN = 512

ADAM_LR = 0.001
ADAM_B1 = 0.9
ADAM_B2 = 0.999
ADAM_EPS = 1e-08
ADAM_WD = 0.01
ADAM_STEP = 10

TT = 432
TD = 912
TD_BIG = 2736
HALO = 48
VMEM_LIMIT = 56 * 1024 * 1024

MESH = pl.DeviceIdType.MESH
ANY = pl.BlockSpec(memory_space=pl.ANY)


def _cparams(*sem):
    return pltpu.CompilerParams(dimension_semantics=sem, vmem_limit_bytes=VMEM_LIMIT)


def _full(shape):
    nd = len(shape)
    return pl.BlockSpec(shape, lambda *_: (0,) * nd)


def _layer(shape, l):
    nd = len(shape)
    return pl.BlockSpec((None,) + tuple(shape), lambda *_: (l,) + (0,) * nd)


def _dot(a, b):
    return jnp.dot(a, b, preferred_element_type=F32)


def _dot_nt(a, b):
    return lax.dot_general(a, b, (((1,), (1,)), ((), ())), preferred_element_type=F32)


def _dot_tn(a, b):
    return lax.dot_general(a, b, (((0,), (0,)), ((), ())), preferred_element_type=F32)


def _rms(h):
    r = lax.rsqrt(jnp.mean(h * h, axis=-1, keepdims=True) + EPS)
    return h * r, r


def _rms_bwd(du, xh, r, g):
    dxh = du * g
    return r * (dxh - xh * jnp.mean(dxh * xh, axis=-1, keepdims=True))


def _sigmoid(x):
    return 1.0 / (1.0 + jnp.exp(-x))


def _colsum(x):
    return jnp.sum(x, axis=0, keepdims=True)


def _mesh_pos():
    return lax.axis_index("x"), lax.axis_index("y"), lax.axis_index("c")


def _slot(d):
    return 4 * d[0] + 2 * d[1] + d[2]


class _Comm:
    def __init__(self):
        self.srcs, self.send, self.dst, self.land, self.out_shapes = [], [], [], [], []

    def add_out(self, shape, dtype):
        self.out_shapes.append(jax.ShapeDtypeStruct(shape, dtype))
        return len(self.out_shapes) - 1

    def add(self, src, send, dst, land):
        self.srcs.append(src)
        self.send.append(send)
        self.dst.append(dst)
        self.land.append(land)

    def gather(self, block, out_shape, place):
        self.add(block, lambda ref, s: ref, self.add_out(out_shape, block.dtype), place)

    def sem_shapes(self):
        n = len(self.srcs)
        return [pltpu.SemaphoreType.DMA((7 * n,)), pltpu.SemaphoreType.DMA((7 * n,)), pltpu.SemaphoreType.DMA((n,))]

    def _copies(self, cins, couts, send_sems, recv_sems, local_sems, with_arrivals=True):
        x, y, c = _mesh_pos()
        me = (x, y, c)
        sends, arrivals, local = [], [], []
        for a in range(len(self.srcs)):
            out = couts[self.dst[a]]
            mine = self.land[a](out, _slot(me))
            local.append(pltpu.make_async_copy(self.send[a](cins[a], _slot(me)), mine, local_sems.at[a]))
            for k in range(N_DEV - 1):
                d = k + 1
                peer = (1 - x if (d >> 2) & 1 else x, 1 - y if (d >> 1) & 1 else y, 1 - c if d & 1 else c)
                sems = dict(send_sem=send_sems.at[a * 7 + k], recv_sem=recv_sems.at[a * 7 + k],
                            device_id=peer, device_id_type=MESH)
                sends.append(pltpu.make_async_remote_copy(
                    src_ref=self.send[a](cins[a], _slot(peer)), dst_ref=mine, **sems))
                if with_arrivals:
                    theirs = self.land[a](out, _slot(peer))
                    arrivals.append(pltpu.make_async_remote_copy(src_ref=theirs, dst_ref=theirs, **sems))
        return sends, arrivals, local

    def start(self, *refs):
        sends, _, local = self._copies(*refs, with_arrivals=False)
        for cp in local + sends:
            cp.start()

    def wait(self, *refs):
        sends, arrivals, local = self._copies(*refs)
        for cp in arrivals:
            cp.wait_recv()
        for cp in sends:
            cp.wait_send()
        for cp in local:
            cp.wait()


def _call(body, *, name, grid, in_specs, out_specs, out_shape, operands, scratch_shapes=(), sem=None, comm=None):
    out_specs, out_shape = list(out_specs), list(out_shape)
    if comm is None:
        res = pl.pallas_call(body, name=name, grid=grid, in_specs=list(in_specs), out_specs=out_specs,
                             out_shape=out_shape, scratch_shapes=list(scratch_shapes),
                             compiler_params=_cparams(*sem))(*operands)
        return list(res), []
    n_in, n_out, n_scr = len(in_specs), len(out_specs), len(scratch_shapes)
    nc_in, nc_out = len(comm.srcs), len(comm.out_shapes)

    def hosted(*refs):
        ins, cins = refs[:n_in], refs[n_in:n_in + nc_in]
        o0 = n_in + nc_in
        outs, couts = refs[o0:o0 + n_out], refs[o0 + n_out:o0 + n_out + nc_out]
        s0 = o0 + n_out + nc_out
        scr, sems = refs[s0:s0 + n_scr], refs[s0 + n_scr:]
        ids = [pl.program_id(a) for a in range(len(grid))]
        first = functools.reduce(operator.and_, [i == 0 for i in ids])
        last = functools.reduce(operator.and_, [i == g - 1 for i, g in zip(ids, grid)])

        @pl.when(first)
        def _():
            comm.start(cins, couts, *sems)

        body(*ins, *outs, *scr)

        @pl.when(last)
        def _():
            comm.wait(cins, couts, *sems)

    res = pl.pallas_call(
        hosted, name=name, grid=grid, in_specs=list(in_specs) + [ANY] * nc_in,
        out_specs=out_specs + [ANY] * nc_out, out_shape=out_shape + comm.out_shapes,
        scratch_shapes=list(scratch_shapes) + comm.sem_shapes(),
        compiler_params=_cparams(*(["arbitrary"] * len(grid))))(*operands, *comm.srcs)
    return list(res[:n_out]), list(res[n_out:])


def _comm_only(comm, name):
    def body(*refs):
        n, m = len(comm.srcs), len(comm.out_shapes)
        comm.start(refs[:n], refs[n:n + m], *refs[n + m:])
        comm.wait(refs[:n], refs[n:n + m], *refs[n + m:])

    return list(pl.pallas_call(body, name=name, in_specs=[ANY] * len(comm.srcs), out_specs=[ANY] * len(comm.out_shapes),
                               out_shape=comm.out_shapes, scratch_shapes=comm.sem_shapes())(*comm.srcs))


def _allgather(items, name):
    n = len(items)

    def body(*refs):
        ins, outs = refs[:n], refs[n:2 * n]
        send_sems, recv_sems, local_sems = refs[2 * n:]
        x, y, c = _mesh_pos()
        me, sib = (x, y, c), (x, y, 1 - c)
        chips = [(1 - x, y), (x, 1 - y), (1 - x, 1 - y)]

        def copy(a, k, blk, to, from_input=False):
            region = items[a][2](outs[a], _slot(blk))
            return pltpu.make_async_remote_copy(
                src_ref=ins[a] if from_input else region, dst_ref=region,
                send_sem=send_sems.at[a * 7 + k], recv_sem=recv_sems.at[a * 7 + k],
                device_id=to, device_id_type=MESH)

        local = [pltpu.make_async_copy(ins[a], items[a][2](outs[a], _slot(me)), local_sems.at[a]) for a in range(n)]
        for cp in local:
            cp.start()
        first = []
        for a in range(n):
            first.append(copy(a, 0, me, sib, True))
            for j, chip in enumerate(chips):
                first.append(copy(a, 1 + j, me, (*chip, c), True))
        for cp in first:
            cp.start()
        passed = []
        for j, chip in enumerate(chips):
            for a in range(n):
                copy(a, 1 + j, (*chip, c), me).wait_recv()
                fwd = copy(a, 4 + j, (*chip, c), sib)
                fwd.start()
                passed.append(fwd)
        for a in range(n):
            copy(a, 0, sib, me).wait_recv()
            for j, chip in enumerate(chips):
                copy(a, 4 + j, (*chip, 1 - c), me).wait_recv()
        for cp in first + passed:
            cp.wait_send()
        for cp in local:
            cp.wait()

    return pl.pallas_call(
        body, name=name,
        in_specs=[ANY] * n, out_specs=[ANY] * n,
        out_shape=[jax.ShapeDtypeStruct(it[1], it[0].dtype) for it in items],
        scratch_shapes=[pltpu.SemaphoreType.DMA((7 * n,)), pltpu.SemaphoreType.DMA((7 * n,)),
                        pltpu.SemaphoreType.DMA((n,))],
    )(*[it[0] for it in items])


def _rms_mm(h, g, w, l, ck, name, comm=None, w_is_nk=False, out_dtype=F32):
    T, D = h.shape
    N = w.shape[0] if w_is_nk else w.shape[1]
    normed = h.dtype == BF16

    def body(h_ref, g_ref, w_ref, o_ref):
        if normed:
            u = h_ref[...]
        else:
            u = (_rms(h_ref[...])[0] * g_ref[l:l + 1, :]).astype(BF16)
        res = _dot_nt(u, w_ref[...]) if w_is_nk else _dot(u, w_ref[...])
        o_ref[...] = res.astype(out_dtype)

    w_spec = pl.BlockSpec((ck, D), lambda j, i: (j, 0)) if w_is_nk else pl.BlockSpec((D, ck), lambda j, i: (0, j))
    td = TD
    return _call(
        body, name=name, grid=(N // ck, T // td),
        in_specs=[pl.BlockSpec((td, D), lambda j, i: (i, 0)), _full(g.shape), w_spec],
        out_specs=[pl.BlockSpec((td, ck), lambda j, i: (i, j))],
        out_shape=[jax.ShapeDtypeStruct((T, N), out_dtype)],
        operands=(h, g, w), sem=("parallel", "parallel"), comm=comm)


def _mm_res(res, a, w, ck, relu2, name, comm=None, norm=None):
    T, D = res.shape
    K = a.shape[-1]
    nk = K // ck

    def body(*refs):
        if norm is None:
            r_ref, a_ref, w_ref, o_ref = refs
        else:
            r_ref, a_ref, w_ref, g_ref, o_ref, u_ref = refs
        k = pl.program_id(1)

        @pl.when(k == 0)
        def _():
            o_ref[...] = r_ref[...]

        av = a_ref[...]
        if relu2:
            av = jnp.square(jnp.maximum(av.astype(F32), 0.0))
        o_ref[...] += _dot(av.astype(BF16), w_ref[...])

        if norm is not None:
            @pl.when(k == nk - 1)
            def _():
                u_ref[...] = (_rms(o_ref[...])[0] * g_ref[norm[1]:norm[1] + 1, :]).astype(BF16)

    row = pl.BlockSpec((TD, D), lambda i, k: (i, 0))
    in_specs = [row, pl.BlockSpec((TD, ck), lambda i, k: (i, k)), pl.BlockSpec((ck, D), lambda i, k: (k, 0))]
    operands = (res, a, w)
    out_specs, out_shape = [row], [jax.ShapeDtypeStruct((T, D), F32)]
    if norm is not None:
        in_specs.append(_full(norm[0].shape))
        operands += (norm[0],)
        out_specs.append(row)
        out_shape.append(jax.ShapeDtypeStruct((T, D), BF16))
    return _call(
        body, name=name, grid=(T // TD, nk), in_specs=in_specs, out_specs=out_specs, out_shape=out_shape,
        operands=operands, sem=("parallel", "arbitrary"), comm=comm)


def _final_loss(h, gf, tgt):
    T, D = h.shape

    def body(h_ref, g_ref, t_ref, dh_ref, dhb_ref, dg_ref, loss_ref):
        i = pl.program_id(0)

        @pl.when(i == 0)
        def _():
            dg_ref[...] = jnp.zeros_like(dg_ref)
            loss_ref[...] = jnp.zeros_like(loss_ref)

        g = g_ref[...]
        xh, r = _rms(h_ref[...])
        row = i * TD + lax.broadcasted_iota(jnp.int32, (TD, 1), 0)
        keep = (row >= N_META).astype(F32)
        diff = (xh * g - t_ref[...]) * keep
        part = jnp.sum(jnp.sum(diff * diff, axis=1, keepdims=True), axis=0, keepdims=True)
        loss_ref[...] += (0.5 / D) * part
        dy = diff * (1.0 / D)
        dg_ref[...] += _colsum(dy * xh)
        dh = _rms_bwd(dy, xh, r, g)
        dh_ref[...] = dh
        dhb_ref[...] = dh.astype(BF16)

    row = pl.BlockSpec((TD, D), lambda i: (i, 0))
    return pl.pallas_call(
        body, name="final_loss", grid=(T // TD,),
        in_specs=[row, _full(gf.shape), row],
        out_specs=[row, row, _full((1, D)), _full((8, 128))],
        out_shape=[jax.ShapeDtypeStruct((T, D), F32), jax.ShapeDtypeStruct((T, D), BF16),
                   jax.ShapeDtypeStruct((1, D), F32), jax.ShapeDtypeStruct((8, 128), F32)],
        compiler_params=_cparams("arbitrary"))(h, gf, tgt)


def _bwd_dz(dh, wdown, z, ck, name, comm=None):
    T, D = dh.shape
    F = z.shape[-1]

    def body(d_ref, w_ref, z_ref, o_ref):
        dact = _dot_nt(d_ref[...], w_ref[...])
        o_ref[...] = (dact * (2.0 * jnp.maximum(z_ref[...].astype(F32), 0.0))).astype(BF16)

    return _call(
        body, name=name, grid=(F // ck, T // TD),
        in_specs=[pl.BlockSpec((TD, D), lambda j, i: (i, 0)), pl.BlockSpec((ck, D), lambda j, i: (j, 0)),
                  pl.BlockSpec((TD, ck), lambda j, i: (i, j))],
        out_specs=[pl.BlockSpec((TD, ck), lambda j, i: (i, j))],
        out_shape=[jax.ShapeDtypeStruct((T, F), BF16)],
        operands=(dh, wdown, z), sem=("parallel", "parallel"), comm=comm)


def _bwd_plain(dh, w, name, comm=None):
    T, D = dh.shape
    K = w.shape[0]

    def body(d_ref, w_ref, o_ref):
        o_ref[...] = _dot_nt(d_ref[...], w_ref[...])

    return _call(
        body, name=name, grid=(T // TD,),
        in_specs=[pl.BlockSpec((TD, D), lambda i: (i, 0)), _full((K, D))],
        out_specs=[pl.BlockSpec((TD, K), lambda i: (i, 0))],
        out_shape=[jax.ShapeDtypeStruct((T, K), F32)],
        operands=(dh, w), sem=("parallel",), comm=comm)


def _bwd_rms(dx, w, h, g, dh_in, l, ck, name, comm=None, w_is_kd=False, emit_bf16=True):
    T, D = h.shape
    K = dx.shape[-1]
    nk = K // ck
    w_spec = pl.BlockSpec((ck, D), lambda i, k: (k, 0)) if w_is_kd else pl.BlockSpec((D, ck), lambda i, k: (0, k))

    def body(x_ref, w_ref, h_ref, g_ref, di_ref, *rest):
        dh_ref, dg_ref = rest[0], rest[2 if emit_bf16 else 1]
        i, k = pl.program_id(0), pl.program_id(1)

        @pl.when((i == 0) & (k == 0))
        def _():
            dg_ref[...] = jnp.zeros_like(dg_ref)

        part = _dot(x_ref[...], w_ref[...]) if w_is_kd else _dot_nt(x_ref[...], w_ref[...])

        def finish(du):
            xh, r = _rms(h_ref[...])
            dg_ref[...] += _colsum(du * xh)
            dh = di_ref[...] + _rms_bwd(du, xh, r, g_ref[l:l + 1, :])
            dh_ref[...] = dh
            if emit_bf16:
                rest[1][...] = dh.astype(BF16)

        if nk == 1:
            finish(part)
        else:
            acc_ref = rest[-1]

            @pl.when(k == 0)
            def _():
                acc_ref[...] = part

            @pl.when(k > 0)
            def _():
                acc_ref[...] += part

            @pl.when(k == nk - 1)
            def _():
                finish(acc_ref[...])

    row = pl.BlockSpec((TD, D), lambda i, k: (i, 0))
    out_specs = [row] + ([row] if emit_bf16 else []) + [_full((1, D))]
    out_shape = ([jax.ShapeDtypeStruct((T, D), F32)] + ([jax.ShapeDtypeStruct((T, D), BF16)] if emit_bf16 else [])
                 + [jax.ShapeDtypeStruct((1, D), F32)])
    return _call(
        body, name=name, grid=(T // TD, nk),
        in_specs=[pl.BlockSpec((TD, ck), lambda i, k: (i, k)), w_spec, row, _full(g.shape), row],
        out_specs=out_specs, out_shape=out_shape,
        scratch_shapes=[pltpu.VMEM((TD, D), F32)] if nk > 1 else [],
        operands=(dx, w, h, g, dh_in), sem=("arbitrary", "arbitrary"), comm=comm)


def _mm_tn(a, b, relu2, rms_on, g, l, chunk, ck, out_dtype, name, comm=None):
    T = a.shape[0]
    Ka, Nb = a.shape[1], b.shape[1]
    ca, cb = (ck, Nb) if chunk == "a" else (Ka, ck)
    nj = (Ka // ck) if chunk == "a" else (Nb // ck)
    td = TD_BIG if a.dtype == BF16 and b.dtype == BF16 else TD
    nt = T // td

    def body(*refs):
        if rms_on is not None:
            a_ref, g_ref, b_ref, o_ref, acc_ref = refs
        else:
            a_ref, b_ref, o_ref, acc_ref = refs
        t = pl.program_id(1)

        @pl.when(t == 0)
        def _():
            acc_ref[...] = jnp.zeros_like(acc_ref)

        av, bv = a_ref[...], b_ref[...]
        if relu2:
            av = jnp.square(jnp.maximum(av.astype(F32), 0.0))
        if rms_on == "a":
            av = _rms(av)[0] * g_ref[l:l + 1, :]
        elif rms_on == "b":
            bv = _rms(bv)[0] * g_ref[l:l + 1, :]
        acc_ref[...] += _dot_tn(av.astype(BF16), bv.astype(BF16))

        @pl.when(t == nt - 1)
        def _():
            o_ref[...] = acc_ref[...].astype(out_dtype)

    if chunk == "a":
        a_spec = pl.BlockSpec((td, ca), lambda j, t: (t, j))
        b_spec = pl.BlockSpec((td, cb), lambda j, t: (t, 0))
        o_spec = pl.BlockSpec((ca, cb), lambda j, t: (j, 0))
    else:
        a_spec = pl.BlockSpec((td, ca), lambda j, t: (t, 0))
        b_spec = pl.BlockSpec((td, cb), lambda j, t: (t, j))
        o_spec = pl.BlockSpec((ca, cb), lambda j, t: (0, j))
    ins, specs = [a], [a_spec]
    if rms_on is not None:
        ins.append(g)
        specs.append(_full(g.shape))
    ins.append(b)
    specs.append(b_spec)
    return _call(
        body, name=name, grid=(nj, nt), in_specs=specs, out_specs=[o_spec],
        out_shape=[jax.ShapeDtypeStruct((Ka, Nb), out_dtype)], scratch_shapes=[pltpu.VMEM((ca, cb), F32)],
        operands=tuple(ins), sem=("parallel", "arbitrary"), comm=comm)


def _shift_down(x, k):
    return x if k == 0 else pltpu.roll(x, k, 0)


def _shift_up(x, k):
    return x if k == 0 else pltpu.roll(x, x.shape[0] - k, 0)


def _pool_sel(lane_grp, vals):
    return jnp.where(lane_grp == 0, vals[0], jnp.where(lane_grp == 1, vals[1], jnp.where(lane_grp == 2, vals[2], vals[3])))


def _pool_fwd(ue, pos):
    s = ue
    sums = []
    for k in (1, 2, 4, 8):
        s = s + _shift_down(s, k)
        sums.append(s[HALO:])
    grp = lax.broadcasted_iota(jnp.int32, (1, D_POOL), 1) // POOL_GW
    wsum = _pool_sel(grp, sums)
    width = _pool_sel(grp, [jnp.float32(w) for w in POOL_WINDOWS])
    cnt = jnp.minimum(pos.astype(F32), width)
    return wsum / cnt - ue[HALO:], cnt, grp


def _conv_fwd(xe, w_ref, nk):
    acc = None
    for k in range(nk):
        term = _shift_down(xe, nk - 1 - k)[HALO:] * w_ref(k)
        acc = term if acc is None else acc + term
    return acc


def _taps(window, w_ref, nk, shift):
    acc = None
    for r in range(min(8, nk)):
        inner = None
        for q in range((nk - 1 - r) // 8 + 1):
            term = window(q) * w_ref(nk - 1 - 8 * q - r)
            inner = term if inner is None else inner + term
        inner = shift(inner, r)
        acc = inner if acc is None else acc + inner
    return acc


def _layernorm(cb):
    mu = jnp.mean(cb, axis=-1, keepdims=True)
    xc = cb - mu
    rstd = lax.rsqrt(jnp.mean(xc * xc, axis=-1, keepdims=True) + EPS)
    return xc * rstd, rstd


def _softplus_neg(lam):
    x = -lam
    e = jnp.exp(-jnp.abs(x))
    u = 1.0 + e
    d = u - 1.0
    log1p = jnp.where(d == 0.0, e, jnp.log(u) * (e / jnp.where(d == 0.0, 1.0, d)))
    return jnp.maximum(x, 0.0) + log1p


def _expm1(x):
    small = x * (1.0 + x * 0.5 * (1.0 + x * (1.0 / 3.0) * (1.0 + x * 0.25)))
    return jnp.where(jnp.abs(x) < 1e-2, small, jnp.exp(x) - 1.0)


def _gelu_parts(x):
    c0 = 0.7978845608028654
    inner = c0 * (x + 0.044715 * x * x * x)
    th = jnp.tanh(inner)
    gelu = 0.5 * x * (1.0 + th)
    dgelu = 0.5 * (1.0 + th) + 0.5 * x * (1.0 - th * th) * c0 * (1.0 + 3.0 * 0.044715 * x * x)
    return gelu, dgelu


def _rg_gates(xc, wa_ref, wx_ref, v512_ref):
    xb = xc.astype(BF16)
    ra = _sigmoid(_dot(xb, wa_ref[...]) + v512_ref[1:2, :])
    ri = _sigmoid(_dot(xb, wx_ref[...]) + v512_ref[2:3, :])
    sp = _softplus_neg(v512_ref[3:4, :])
    log_a = (-RG_C) * ra * sp
    a = jnp.exp(log_a)
    mult = jnp.sqrt(-_expm1(2.0 * log_a))
    return ra, ri, sp, a, mult


def _scan_down(a, b):
    n = a.shape[0]
    row = lax.broadcasted_iota(jnp.int32, (n, 1), 0)
    k = 1
    while k < n:
        ok = row >= k
        b = a * jnp.where(ok, _shift_down(b, k), 0.0) + b
        a = a * jnp.where(ok, _shift_down(a, k), 1.0)
        k *= 2
    return b, a


def _scan_up(c, g):
    n = c.shape[0]
    row = lax.broadcasted_iota(jnp.int32, (n, 1), 0)
    k = 1
    while k < n:
        ok = row < n - k
        g = c * jnp.where(ok, _shift_up(g, k), 0.0) + g
        c = c * jnp.where(ok, _shift_up(c, k), 1.0)
        k *= 2
    return g, c


def _mixer_specs(l, tile_of):
    r = TT // HALO
    return [
        pl.BlockSpec((TT, 1792), lambda i: (tile_of(i), 0)),
        pl.BlockSpec((HALO, 1792), lambda i: (jnp.maximum(tile_of(i) * r - 1, 0), 0)),
        _layer((256, 256), l), _layer((8, 256), l), _layer((32, 256), l), _layer((256, 256), l),
        _layer((512, 512), l), _layer((512, 512), l), _layer((8, 512), l),
    ]


def _mixer_fwd(p, wts, l, comm=None):
    T = p.shape[0]

    def body(p_ref, ph_ref, wpool_ref, v256_ref, dww_ref, wpw_ref, wa_ref, wx_ref, v512_ref, y_ref, hr_ref, cb_ref,
             carry_ref):
        i = pl.program_id(0)

        @pl.when(i == 0)
        def _():
            carry_ref[...] = jnp.zeros_like(carry_ref)

        halo = jnp.where(i == 0, 0.0, ph_ref[...])
        e = jnp.concatenate([halo, p_ref[...]], axis=0)
        pos = i * TT + 1 + lax.broadcasted_iota(jnp.int32, (TT, 1), 0)
        pooled, _, _ = _pool_fwd(e[:, 0:256], pos)
        y_a = _dot(pooled.astype(BF16), wpool_ref[...]) * v256_ref[0:1, :]
        ub = e[:, 256:512] * _sigmoid(e[:, 512:768])
        cb = _conv_fwd(ub, lambda k: dww_ref[k:k + 1, :], CONV_K) + v256_ref[1:2, :]
        cb_ref[...] = cb
        xhat, _ = _layernorm(cb)
        ln = xhat * v256_ref[2:3, :] + v256_ref[3:4, :]
        s = ln * _sigmoid(ln)
        y_b = _dot(s.astype(BF16), wpw_ref[...])
        xc = _conv_fwd(e[:, 1280:1792], lambda k: v512_ref[4 + k:5 + k, :], RG_CONV_K) + v512_ref[0:1, :]
        _, ri, _, a, mult = _rg_gates(xc, wa_ref, wx_ref, v512_ref)
        hloc, acum = _scan_down(a, mult * ri * xc)
        hr = hloc + acum * carry_ref[0:1, :]
        hr_ref[...] = hr
        carry_ref[0:1, :] = hr_ref[TT - 1:TT, :]
        gelu, _ = _gelu_parts(p_ref[:, 768:1280])
        y_ref[:, 0:256] = y_a.astype(BF16)
        y_ref[:, 256:512] = y_b.astype(BF16)
        y_ref[:, 512:1024] = (gelu * hr).astype(BF16)

    return _call(
        body, name=f"mixer_fwd{l}", grid=(T // TT,),
        in_specs=_mixer_specs(l, lambda i: i),
        out_specs=[pl.BlockSpec((TT, 1024), lambda i: (i, 0)), pl.BlockSpec((TT, 512), lambda i: (i, 0)),
                   pl.BlockSpec((TT, 256), lambda i: (i, 0))],
        out_shape=[jax.ShapeDtypeStruct((T, 1024), BF16), jax.ShapeDtypeStruct((T, 512), F32),
                   jax.ShapeDtypeStruct((T, 256), F32)],
        scratch_shapes=[pltpu.VMEM((8, 512), F32)],
        operands=(p, p, *wts), sem=("arbitrary",), comm=comm)


def _mixer_bwd(p, hr, cb, dy, wts, l, comm=None):
    T = p.shape[0]
    nt = T // TT
    tile_of = lambda i: nt - 1 - i

    def body(p_ref, ph_ref, wpool_ref, v256_ref, dww_ref, wpw_ref, wa_ref, wx_ref, v512_ref, hr_ref, hrh_ref, cb_ref,
             dy_ref, dp_ref, g256_ref, g512_ref, gpool_ref, gpw_ref, gwa_ref, gwx_ref,
             q_c, dcb_c, dxc_c, ag_c, ub_s, ed_s):
        i = pl.program_id(0)
        j = nt - 1 - i

        @pl.when(i == 0)
        def _():
            for ref in (g256_ref, g512_ref, gpool_ref, gpw_ref, gwa_ref, gwx_ref, q_c, dcb_c, dxc_c, ag_c):
                ref[...] = jnp.zeros_like(ref)

        halo = jnp.where(j == 0, 0.0, ph_ref[...])
        e = jnp.concatenate([halo, p_ref[...]], axis=0)
        pos = j * TT + 1 + lax.broadcasted_iota(jnp.int32, (TT, 1), 0)
        row = lax.broadcasted_iota(jnp.int32, (TT, 1), 0)

        dy_a = dy_ref[:, 0:256]
        pooled, cnt, grp = _pool_fwd(e[:, 0:256], pos)
        scale = v256_ref[0:1, :]
        mixed = _dot(pooled.astype(BF16), wpool_ref[...])
        g256_ref[34:35, :] += _colsum(dy_a * mixed)
        dmixed = (dy_a * scale).astype(BF16)
        gpool_ref[...] += _dot_tn(pooled.astype(BF16), dmixed)
        dpooled = _dot_nt(dmixed, wpool_ref[...])
        q = dpooled / cnt
        s = jnp.concatenate([q, q_c[...]], axis=0)
        sums = []
        for k in (1, 2, 4, 8):
            s = s + _shift_up(s, k)
            sums.append(s[:TT])
        dp_ref[:, 0:256] = (_pool_sel(grp, sums) - dpooled).astype(BF16)
        q_c[...] = q[0:16]

        dy_b = dy_ref[:, 256:512].astype(BF16)
        sg = _sigmoid(e[:, 512:768])
        ub_s[...] = e[:, 256:512] * sg
        xhat, rstd = _layernorm(cb_ref[...])
        ln_g = v256_ref[2:3, :]
        ln = xhat * ln_g + v256_ref[3:4, :]
        sig = _sigmoid(ln)
        sact = ln * sig
        gpw_ref[...] += _dot_tn(sact.astype(BF16), dy_b)
        dln = _dot_nt(dy_b, wpw_ref[...]) * (sig * (1.0 + ln * (1.0 - sig)))
        g256_ref[32:33, :] += _colsum(dln * xhat)
        g256_ref[33:34, :] += _colsum(dln)
        dxhat = dln * ln_g
        dcb = rstd * (dxhat - jnp.mean(dxhat, axis=-1, keepdims=True)
                      - xhat * jnp.mean(dxhat * xhat, axis=-1, keepdims=True))
        g256_ref[31:32, :] += _colsum(dcb)
        pad8 = jnp.zeros((8, D_CONV), F32)
        dcb_ext = jnp.concatenate([pad8, dcb, pad8], axis=0)
        for r in range(8):
            d_r = _shift_up(dcb_ext, r)[0:TT + 8]
            for q in range((CONV_K - 1 - r) // 8 + 1):
                k = CONV_K - 1 - 8 * q - r
                g256_ref[k:k + 1, :] += _colsum(d_r * ub_s[pl.ds(HALO - 8 - 8 * q, TT + 8), :])
        ed_s[0:TT, :] = dcb
        ed_s[TT:TT + 32, :] = dcb_c[...]
        dub = _taps(lambda q: ed_s[pl.ds(8 * q, TT + 8), :], lambda k: dww_ref[k:k + 1, :], CONV_K, _shift_up)[:TT]
        dcb_c[...] = dcb[0:32]
        sg_t = sg[HALO:]
        v_t = p_ref[:, 256:512]
        dp_ref[:, 256:512] = (dub * sg_t).astype(BF16)
        dp_ref[:, 512:768] = (dub * v_t * sg_t * (1.0 - sg_t)).astype(BF16)

        dy_c = dy_ref[:, 512:1024]
        cxe = e[:, 1280:1792]
        xc = _conv_fwd(cxe, lambda k: v512_ref[4 + k:5 + k, :], RG_CONV_K) + v512_ref[0:1, :]
        ra, ri, sp, a, mult = _rg_gates(xc, wa_ref, wx_ref, v512_ref)
        hrv = hr_ref[...]
        gelu, dgelu = _gelu_parts(p_ref[:, 768:1280])
        dp_ref[:, 768:1280] = (dy_c * hrv * dgelu).astype(BF16)
        dhr = dy_c * gelu
        coef = jnp.where(row < TT - 1, _shift_up(a, 1), 1.0)
        gloc, ccum = _scan_up(coef, dhr)
        gg = gloc + ccum * ag_c[0:1, :]
        ag_c[...] = (a * gg)[0:8]
        hr_before = jnp.where(j == 0, 0.0, hrh_ref[7:8, :])
        hr_prev = jnp.where(row >= 1, _shift_down(hrv, 1), hr_before)
        da = gg * hr_prev
        dmult = gg * ri * xc
        dri = gg * mult * xc
        dxc = gg * mult * ri
        dlog_a = da * a - dmult * (a * a / mult)
        g512_ref[3:4, :] += _colsum(dlog_a * ((-RG_C) * ra))
        dpre_a = dlog_a * ((-RG_C) * sp) * ra * (1.0 - ra)
        dpre_x = dri * ri * (1.0 - ri)
        g512_ref[1:2, :] += _colsum(dpre_a)
        g512_ref[2:3, :] += _colsum(dpre_x)
        xb = xc.astype(BF16)
        dpa_b, dpx_b = dpre_a.astype(BF16), dpre_x.astype(BF16)
        gwa_ref[...] += _dot_tn(xb, dpa_b)
        gwx_ref[...] += _dot_tn(xb, dpx_b)
        dxc = dxc + _dot_nt(dpa_b, wa_ref[...]) + _dot_nt(dpx_b, wx_ref[...])
        g512_ref[0:1, :] += _colsum(dxc)
        for k in range(RG_CONV_K):
            g512_ref[4 + k:5 + k, :] += _colsum(dxc * _shift_down(cxe, RG_CONV_K - 1 - k)[HALO:])
        ex = jnp.concatenate([dxc, dxc_c[...]], axis=0)
        dcx = _taps(lambda q: ex, lambda k: v512_ref[4 + k:5 + k, :], RG_CONV_K, _shift_up)[:TT]
        dxc_c[...] = dxc[0:8]
        dp_ref[:, 1280:1792] = dcx.astype(BF16)

        @pl.when(i == nt - 1)
        def _():
            lam = v512_ref[3:4, :]
            g512_ref[3:4, :] = g512_ref[3:4, :] * (-_sigmoid(-lam))

    in_specs = _mixer_specs(l, tile_of) + [
        pl.BlockSpec((TT, 512), lambda i: (tile_of(i), 0)),
        pl.BlockSpec((8, 512), lambda i: (jnp.maximum(tile_of(i) * (TT // 8) - 1, 0), 0)),
        pl.BlockSpec((TT, 256), lambda i: (tile_of(i), 0)),
        pl.BlockSpec((TT, 1024), lambda i: (tile_of(i), 0)),
    ]
    acc_shapes = [(40, 256), (8, 512), (256, 256), (256, 256), (512, 512), (512, 512)]
    return _call(
        body, name=f"mixer_bwd{l}", grid=(nt,),
        in_specs=in_specs,
        out_specs=[pl.BlockSpec((TT, 1792), lambda i: (tile_of(i), 0))] + [_full(s) for s in acc_shapes],
        out_shape=[jax.ShapeDtypeStruct((T, 1792), BF16)] + [jax.ShapeDtypeStruct(s, F32) for s in acc_shapes],
        scratch_shapes=[pltpu.VMEM((16, 256), F32), pltpu.VMEM((32, 256), F32), pltpu.VMEM((8, 512), F32),
                        pltpu.VMEM((8, 512), F32), pltpu.VMEM((HALO + TT, D_CONV), F32),
                        pltpu.VMEM((TT + 32, D_CONV), F32)],
        operands=(p, p, *wts, hr, hr, cb, dy), sem=("arbitrary",), comm=comm)


def _adam(w, g, m, v):
    m = ADAM_B1 * m + (1.0 - ADAM_B1) * g
    v = ADAM_B2 * v + (1.0 - ADAM_B2) * (g * g)
    m_hat = m / (1.0 - ADAM_B1 ** ADAM_STEP)
    v_hat = v / (1.0 - ADAM_B2 ** ADAM_STEP)
    delta = -ADAM_LR * (m_hat / (jnp.sqrt(v_hat) + ADAM_EPS) + ADAM_WD * w)
    return delta, m, v


def _adamw_big(recvs, w, m, v, tr, name):
    L, R, C = w.shape

    def body(r0_ref, r1_ref, w_ref, m_ref, v_ref, g_ref, d_ref, mo_ref, vo_ref):
        l = pl.program_id(0)

        def total(r_ref):
            g = r_ref[0].astype(F32)
            for s in range(1, N_DEV):
                g = g + r_ref[s].astype(F32)
            g_ref[...] = g

        @pl.when(l == 0)
        def _():
            total(r0_ref)

        @pl.when(l == 1)
        def _():
            total(r1_ref)

        d_ref[...], mo_ref[...], vo_ref[...] = _adam(w_ref[...], g_ref[...], m_ref[...], v_ref[...])

    blk = pl.BlockSpec((None, tr, C), lambda l, i: (l, i, 0))
    return pl.pallas_call(
        body, name=name, grid=(L, R // tr),
        in_specs=[pl.BlockSpec((N_DEV, tr, C), lambda l, i: (0, i * (1 - l), 0)),
                  pl.BlockSpec((N_DEV, tr, C), lambda l, i: (0, i * l, 0)), blk, blk, blk],
        out_specs=[blk] * 4,
        out_shape=[jax.ShapeDtypeStruct((L, R, C), F32)] * 4,
        compiler_params=_cparams("arbitrary", "arbitrary"))(recvs[0], recvs[1], w, m, v)


def _sum_slots(recvs, name):
    L = len(recvs)
    _, R, C = recvs[0].shape

    def body(*refs):
        o_ref = refs[L]
        for l in range(L):
            acc = refs[l][0].astype(F32)
            for s in range(1, N_DEV):
                acc = acc + refs[l][s].astype(F32)
            o_ref[l] = acc

    return pl.pallas_call(body, name=name, out_shape=jax.ShapeDtypeStruct((L, R, C), F32))(*recvs)


def _adamw_plain(gs, ws, ms, vs, name):
    n = len(gs)

    def body(*refs):
        g_r, w_r, m_r, v_r = refs[:n], refs[n:2 * n], refs[2 * n:3 * n], refs[3 * n:4 * n]
        outs = refs[4 * n:]
        for a in range(n):
            d, mo, vo = _adam(w_r[a][...], g_r[a][...], m_r[a][...], v_r[a][...])
            outs[3 * a][...] = d
            outs[3 * a + 1][...] = mo
            outs[3 * a + 2][...] = vo

    shapes = []
    for a in range(n):
        shapes += [jax.ShapeDtypeStruct(ws[a].shape, F32)] * 3
    return pl.pallas_call(body, name=name, out_shape=shapes,
                          compiler_params=pltpu.CompilerParams(vmem_limit_bytes=VMEM_LIMIT))(*gs, *ws, *ms, *vs)


_V256_ROWS = {"convb_dw_b": 31, "convb_ln_g": 32, "convb_ln_b": 33, "pool_scale": 34}
_V512_ROWS = {"rg_conv_b": 0, "rg_b_a": 1, "rg_b_x": 2, "rg_lambda": 3}
_SMALL_REPL = ["mix_norm_g", "mlp_norm_g", "final_norm_g", "pool_w", "pool_scale", "convb_dw_b", "convb_ln_g",
               "convb_ln_b", "rg_conv_b", "rg_w_a", "rg_b_a", "rg_w_x", "rg_b_x", "rg_lambda"]


def _adamw_small(gath1, gath0, gath_tail, ws, ms, vs):
    n = len(_SMALL_REPL)
    n1, n0, nt = len(gath1), len(gath0), len(gath_tail)
    ng = n1 + n0 + nt
    L = DEPTH
    per_layer = [g.shape[1:] for g in gath1[:8]]
    sum_shapes = ([(L,) + s[1:] for s in per_layer[:2]] + [gath1[8].shape[1:]] + [(L,) + s for s in per_layer[2:]]
                  + [gath_tail[1].shape[1:]])

    def body(*refs):
        in1, in0, in_t = refs[:n1], refs[n1:n1 + n0], refs[n1 + n0:ng]
        w_r, m_r, v_r = refs[ng:ng + n], refs[ng + n:ng + 2 * n], refs[ng + 2 * n:ng + 3 * n]
        sums = refs[ng + 3 * n:ng + 3 * n + 10]
        outs = refs[ng + 3 * n + 10:]

        def total(ref):
            acc = ref[0]
            for s in range(1, N_DEV):
                acc = acc + ref[s]
            return acc

        s_mix, s_mlp, s_fin, s_pool, s_256, s_512, s_pw, s_wa, s_wx, s_meta = sums
        s_mix[1:2, :] = total(in1[0])
        s_mix[0:1, :] = total(in_t[0])
        s_mlp[1:2, :] = total(in1[1])
        s_mlp[0:1, :] = total(in0[0])
        s_fin[...] = total(in1[8])
        s_meta[...] = total(in_t[1])
        for k, dst in enumerate((s_pool, s_256, s_512, s_pw, s_wa, s_wx)):
            dst[1] = total(in1[2 + k])
            dst[0] = total(in0[1 + k])

        def grad_of(name, idx):
            if name == "mix_norm_g":
                return s_mix[idx[0]:idx[0] + 1, :]
            if name == "mlp_norm_g":
                return s_mlp[idx[0]:idx[0] + 1, :]
            if name == "final_norm_g":
                return s_fin[...]
            if name in _V256_ROWS:
                r = _V256_ROWS[name]
                return s_256[idx[0], r:r + 1, :]
            if name in _V512_ROWS:
                r = _V512_ROWS[name]
                return s_512[idx[0], r:r + 1, :]
            src = {"pool_w": s_pool, "rg_w_a": s_wa, "rg_w_x": s_wx}[name]
            return src[idx[0], :, idx[1] * 64:(idx[1] + 1) * 64]

        for a, name in enumerate(_SMALL_REPL):
            shape = w_r[a].shape
            if name == "final_norm_g":
                parts = [((), (slice(None), slice(None)))]
            elif len(shape) == 2:
                parts = [((l,), (slice(l, l + 1), slice(None))) for l in range(shape[0])]
            else:
                parts = [((l, h), (l, h)) for l in range(shape[0]) for h in range(shape[1])]
            for idx, sel in parts:
                g = grad_of(name, idx)
                d, mo, vo = _adam(w_r[a][sel], g, m_r[a][sel], v_r[a][sel])
                outs[4 * a][sel] = g
                outs[4 * a + 1][sel] = d
                outs[4 * a + 2][sel] = mo
                outs[4 * a + 3][sel] = vo

    out_shape = [jax.ShapeDtypeStruct(s, F32) for s in sum_shapes]
    for a in range(n):
        out_shape += [jax.ShapeDtypeStruct(ws[a].shape, F32)] * 4
    res = pl.pallas_call(body, name="adamw_small", out_shape=out_shape,
                         compiler_params=pltpu.CompilerParams(vmem_limit_bytes=VMEM_LIMIT))(
                             *gath1, *gath0, *gath_tail, *ws, *ms, *vs)
    return res[:10], res[10:]


_WEIGHTS = ['meta_tokens', 'mix_norm_g', 'w_in', 'pool_w', 'pool_scale', 'convb_dw_w', 'convb_dw_b', 'convb_ln_g',
            'convb_ln_b', 'convb_pw_w', 'rg_conv_w', 'rg_conv_b', 'rg_w_a', 'rg_b_a', 'rg_w_x', 'rg_b_x', 'rg_lambda',
            'w_out', 'mlp_norm_g', 'w_up', 'w_down', 'final_norm_g']


def _block_diag(w):
    L, H, C, _ = w.shape
    eye = jnp.eye(H, dtype=w.dtype)
    return (w[:, :, :, None, :] * eye[None, :, None, :, None]).reshape(L, H * C, H * C)


def _diag_blocks(m, H):
    C = m.shape[0] // H
    return jnp.concatenate([m[h * C:(h + 1) * C, h * C:(h + 1) * C] for h in range(H)], axis=1)


def kernel(x, meta_tokens, mix_norm_g, w_in, pool_w, pool_scale, convb_dw_w, convb_dw_b, convb_ln_g, convb_ln_b, convb_pw_w, rg_conv_w, rg_conv_b, rg_w_a, rg_b_a, rg_w_x, rg_b_x, rg_lambda, w_out, mlp_norm_g, w_up, w_down, final_norm_g, loss_target, m_meta_tokens, m_mix_norm_g, m_w_in, m_pool_w, m_pool_scale, m_convb_dw_w, m_convb_dw_b, m_convb_ln_g, m_convb_ln_b, m_convb_pw_w, m_rg_conv_w, m_rg_conv_b, m_rg_w_a, m_rg_b_a, m_rg_w_x, m_rg_b_x, m_rg_lambda, m_w_out, m_mlp_norm_g, m_w_up, m_w_down, m_final_norm_g, v_meta_tokens, v_mix_norm_g, v_w_in, v_pool_w, v_pool_scale, v_convb_dw_w, v_convb_dw_b, v_convb_ln_g, v_convb_ln_b, v_convb_pw_w, v_rg_conv_w, v_rg_conv_b, v_rg_w_a, v_rg_b_a, v_rg_w_x, v_rg_b_x, v_rg_lambda, v_w_out, v_mlp_norm_g, v_w_up, v_w_down, v_final_norm_g):
    W = dict(meta_tokens=meta_tokens, mix_norm_g=mix_norm_g, w_in=w_in, pool_w=pool_w, pool_scale=pool_scale,
             convb_dw_w=convb_dw_w, convb_dw_b=convb_dw_b, convb_ln_g=convb_ln_g, convb_ln_b=convb_ln_b,
             convb_pw_w=convb_pw_w, rg_conv_w=rg_conv_w, rg_conv_b=rg_conv_b, rg_w_a=rg_w_a, rg_b_a=rg_b_a,
             rg_w_x=rg_w_x, rg_b_x=rg_b_x, rg_lambda=rg_lambda, w_out=w_out, mlp_norm_g=mlp_norm_g, w_up=w_up,
             w_down=w_down, final_norm_g=final_norm_g.reshape(1, -1))
    M = dict(meta_tokens=m_meta_tokens, mix_norm_g=m_mix_norm_g, w_in=m_w_in, pool_w=m_pool_w, pool_scale=m_pool_scale,
             convb_dw_w=m_convb_dw_w, convb_dw_b=m_convb_dw_b, convb_ln_g=m_convb_ln_g, convb_ln_b=m_convb_ln_b,
             convb_pw_w=m_convb_pw_w, rg_conv_w=m_rg_conv_w, rg_conv_b=m_rg_conv_b, rg_w_a=m_rg_w_a, rg_b_a=m_rg_b_a,
             rg_w_x=m_rg_w_x, rg_b_x=m_rg_b_x, rg_lambda=m_rg_lambda, w_out=m_w_out, mlp_norm_g=m_mlp_norm_g,
             w_up=m_w_up, w_down=m_w_down, final_norm_g=m_final_norm_g.reshape(1, -1))
    V = dict(meta_tokens=v_meta_tokens, mix_norm_g=v_mix_norm_g, w_in=v_w_in, pool_w=v_pool_w, pool_scale=v_pool_scale,
             convb_dw_w=v_convb_dw_w, convb_dw_b=v_convb_dw_b, convb_ln_g=v_convb_ln_g, convb_ln_b=v_convb_ln_b,
             convb_pw_w=v_convb_pw_w, rg_conv_w=v_rg_conv_w, rg_conv_b=v_rg_conv_b, rg_w_a=v_rg_w_a, rg_b_a=v_rg_b_a,
             rg_w_x=v_rg_w_x, rg_b_x=v_rg_b_x, rg_lambda=v_rg_lambda, w_out=v_w_out, mlp_norm_g=v_mlp_norm_g,
             w_up=v_w_up, w_down=v_w_down, final_norm_g=v_final_norm_g.reshape(1, -1))

    xs = x[0]
    S, D = xs.shape
    T = S + N_META
    assert T % TT == 0 and D == 1024
    L = DEPTH
    me = 4 * lax.axis_index("x") + 2 * lax.axis_index("y") + lax.axis_index("c")
    c_in, c_ff = w_in.shape[-1], w_up.shape[-1]
    d_in, d_ff = c_in * N_DEV, c_ff * N_DEV
    r_out, r_dn = w_out.shape[1], w_down.shape[1]
    by_slot = lambda ref, s: ref.at[s]
    rows_of = lambda n: (lambda ref, s: ref.at[pl.ds(s * n, n), :])
    cols_of = lambda n: (lambda ref, s: ref.at[:, pl.ds(s * n, n)])

    win_t = jnp.transpose(w_in, (0, 2, 1)).astype(BF16)
    gathered = _allgather([
        (win_t[0], (d_in, D), rows_of(c_in)),
        (meta_tokens, (N_DEV,) + meta_tokens.shape, by_slot),
        (convb_dw_w, (N_DEV,) + convb_dw_w.shape, by_slot),
        (convb_pw_w, (N_DEV,) + convb_pw_w.shape, by_slot),
        (rg_conv_w, (N_DEV,) + rg_conv_w.shape, by_slot),
    ], "gather_first")
    win_f = [gathered[0], None]
    meta_f = jnp.transpose(gathered[1], (1, 0, 2)).reshape(N_META, D)
    dww_f = jnp.transpose(gathered[2], (1, 2, 0, 3)).reshape(L, CONV_K, D_CONV)
    wpw_f = jnp.transpose(gathered[3], (1, 0, 2, 3)).reshape(L, D_CONV, D_CONV).astype(BF16)
    rgw_f = jnp.transpose(gathered[4], (1, 2, 0, 3)).reshape(L, RG_CONV_K, D_RNN)
    wout_b, wup_b, wdown_b = w_out.astype(BF16), w_up.astype(BF16), w_down.astype(BF16)

    def weight_gather(blocks):
        comm = _Comm()
        for blk, shape, place in blocks:
            comm.gather(blk, shape, place)
        return comm

    wpool_bd = _block_diag(pool_w).astype(BF16)
    wa_bd = _block_diag(rg_w_a).astype(BF16)
    wx_bd = _block_diag(rg_w_x).astype(BF16)
    zeros4 = jnp.zeros((L, 4, D_POOL), F32)
    v256 = jnp.concatenate([pool_scale[:, None], convb_dw_b[:, None], convb_ln_g[:, None], convb_ln_b[:, None], zeros4], axis=1)
    dww_p = jnp.concatenate([dww_f, jnp.zeros((L, 1, D_CONV), F32)], axis=1)
    v512 = jnp.concatenate([rg_conv_b[:, None], rg_b_a[:, None], rg_b_x[:, None], rg_lambda[:, None], rgw_f], axis=1)
    mix_w = (wpool_bd, v256, dww_p, wpw_f, wa_bd, wx_bd, v512)

    h = jnp.concatenate([meta_f, xs], axis=0)
    tgt = jnp.concatenate([jnp.zeros((N_META, D), F32), loss_target[0]], axis=0)
    wout_f, wup_f, wdown_f = [None] * L, [None] * L, [None] * L
    saved = []

    (p,), wout_f = _rms_mm(h, mix_norm_g, win_f[0], 0, d_in, "in_proj0",
                           weight_gather([(wout_b[l], (r_out * N_DEV, D), rows_of(r_out)) for l in range(L)]),
                           w_is_nk=True)
    (y, hr, cb), (wup_f[0],) = _mixer_fwd(p, mix_w, 0, weight_gather([(wup_b[0], (D, d_ff), cols_of(c_ff))]))
    (h2, u2), (win_f[1],) = _mm_res(h, y, wout_f[0], y.shape[-1], False, "out_proj0",
                                    weight_gather([(win_t[1], (d_in, D), rows_of(c_in))]), norm=(mlp_norm_g, 0))
    (z,), (wdown_f[0],) = _rms_mm(u2, mlp_norm_g, wup_f[0], 0, 2048, "mlp_up0",
                                  weight_gather([(wdown_b[0], (d_ff, D), rows_of(r_dn))]), out_dtype=BF16)
    (h3, u1), (wup_f[1],) = _mm_res(h2, z, wdown_f[0], 2048, True, "mlp_down0",
                                    weight_gather([(wup_b[1], (D, d_ff), cols_of(c_ff))]), norm=(mix_norm_g, 1))
    saved.append((h, None, p, y, hr, cb, h2, u2, z))
    h = h3
    (p,), _ = _rms_mm(u1, mix_norm_g, win_f[1], 1, d_in, "in_proj1", w_is_nk=True)
    (y, hr, cb), (wdown_f[1],) = _mixer_fwd(p, mix_w, 1, weight_gather([(wdown_b[1], (d_ff, D), rows_of(r_dn))]))
    (h2, u2), _ = _mm_res(h, y, wout_f[1], y.shape[-1], False, "out_proj1", norm=(mlp_norm_g, 1))
    (z,), _ = _rms_mm(u2, mlp_norm_g, wup_f[1], 1, 2048, "mlp_up1", out_dtype=BF16)
    (h3,), _ = _mm_res(h2, z, wdown_f[1], 2048, True, "mlp_down1")
    saved.append((h, u1, p, y, hr, cb, h2, u2, z))

    dh, dhb, g_fin, loss_part = _final_loss(h3, W["final_norm_g"], tgt)
    loss = lax.psum(loss_part[0, 0], ("x", "y", "c"))

    def grad_exchange(pieces):
        comm = _Comm()
        for g, send, shape in pieces:
            comm.add(g, send, comm.add_out((N_DEV,) + shape, BF16), by_slot)
        return comm

    def small_gather(comm, grads):
        for g in grads:
            comm.gather(g, (N_DEV,) + g.shape, by_slot)
        return comm

    recv = {n: [None] * L for n in ("w_in", "w_out", "w_up", "w_down")}
    small1 = gath1 = gath0 = g_win_t1 = None
    for l in reversed(range(L)):
        h0, u1, p, y, hr, cb, h2, u2, z = saved[l]
        (dz,), _ = _bwd_dz(dhb, wdown_f[l], z, 2048, f"mlp_down_bwd{l}")
        (g_wdown,), _ = _mm_tn(z, dhb, True, None, None, l, "a", 512, BF16, f"mlp_down_wgrad{l}")
        comm = grad_exchange([(g_wdown, rows_of(r_dn), (r_dn, D))]) if l == 0 else None
        (dh2, dh2b, g_mlp), got = _bwd_rms(dz, wup_f[l], h2, mlp_norm_g, dh, l, 2048, f"mlp_up_bwd{l}", comm)
        if l == 0:
            recv["w_down"][0] = got[0]
        (g_wup,), _ = _mm_tn(u2, dz, False, None, None, l, "b", 1024, BF16, f"mlp_up_wgrad{l}")
        (dy,), _ = _bwd_plain(dh2b, wout_f[l], f"out_proj_bwd{l}")
        (g_wout,), _ = _mm_tn(y, dh2b, False, None, None, l, "b", D, BF16, f"out_proj_wgrad{l}")
        comm = grad_exchange([(g_wup, cols_of(c_ff), (D, c_ff)), (g_wout, rows_of(r_out), (r_out, D))]
                             + ([(g_wdown, rows_of(r_dn), (r_dn, D))] if l == 1 else
                                [(g_win_t1, rows_of(c_in), (c_in, D))]))
        if l == 0:
            small_gather(comm, small1)
        (dp, g256, g512, gpool, gpw, gwa, gwx), got = _mixer_bwd(p, hr, cb, dy, mix_w, l, comm)
        recv["w_up"][l], recv["w_out"][l] = got[0], got[1]
        if l == 1:
            recv["w_down"][1] = got[2]
        else:
            recv["w_in"][1], gath1 = got[2], got[3:]
        smalls = [g_mlp, _diag_blocks(gpool, 4), g256, g512, gpw, _diag_blocks(gwa, RG_HEADS), _diag_blocks(gwx, RG_HEADS)]
        comm = small_gather(_Comm(), smalls[:5]) if l == 0 else None
        if u1 is None:
            (g_win_t,), got_a = _mm_tn(dp, h0, False, "b", mix_norm_g, l, "a", d_in // 2, BF16, f"in_proj_wgrad{l}", comm)
        else:
            (g_win_t,), got_a = _mm_tn(dp, u1, False, None, None, l, "a", d_in // 2, BF16, f"in_proj_wgrad{l}", comm)
        comm = small_gather(grad_exchange([(g_win_t, rows_of(c_in), (c_in, D))]), smalls[5:]) if l == 0 else None
        outs, got_b = _bwd_rms(dp, win_f[l], h0, mix_norm_g, dh2, l, d_in, f"in_proj_bwd{l}", comm, w_is_kd=True,
                               emit_bf16=l > 0)
        dh, g_mix = outs[0], outs[-1]
        dhb = outs[1] if l > 0 else None
        if l == 0:
            recv["w_in"][0] = got_b[0]
            gath0 = got_a + got_b[1:]
        if l == 1:
            g_win_t1 = g_win_t
            small1 = [g_mix] + smalls + [g_fin]
    grad_x = dh[N_META:][None]

    gath_tail = _comm_only(small_gather(_Comm(), [g_mix, dh[:N_META]]), "exchange_tail")

    out = {}
    for name, tr in (("w_out", r_out), ("w_up", 256), ("w_down", 128)):
        out[name] = _adamw_big(recv[name], W[name], M[name], V[name], tr, f"adamw_{name}")
    g_win = jnp.transpose(_sum_slots(recv["w_in"], "sum_w_in"), (0, 2, 1))
    out["w_in"] = (g_win,) + tuple(_adamw_plain([g_win], [W["w_in"]], [M["w_in"]], [V["w_in"]], "adamw_w_in"))
    sums, small_out = _adamw_small(gath1, gath0, gath_tail, [W[n] for n in _SMALL_REPL], [M[n] for n in _SMALL_REPL],
                                   [V[n] for n in _SMALL_REPL])
    for a, name in enumerate(_SMALL_REPL):
        out[name] = small_out[4 * a:4 * a + 4]
    s_256, s_512, s_pw, s_meta = sums[4], sums[5], sums[6], sums[9]
    g_shard = {
        "meta_tokens": lax.dynamic_slice_in_dim(s_meta, me * 128, 128, axis=1),
        "convb_dw_w": lax.dynamic_slice_in_dim(s_256[:, :CONV_K, :], me * 32, 32, axis=2),
        "convb_pw_w": lax.dynamic_slice_in_dim(s_pw, me * 32, 32, axis=1),
        "rg_conv_w": lax.dynamic_slice_in_dim(s_512[:, 4:8, :], me * 64, 64, axis=2),
    }
    names = list(g_shard)
    res = _adamw_plain([g_shard[n] for n in names], [W[n] for n in names], [M[n] for n in names],
                       [V[n] for n in names], "adamw_small_sharded")
    for a, name in enumerate(names):
        out[name] = (g_shard[name],) + tuple(res[3 * a:3 * a + 3])

    def fix(name, arr):
        return arr.reshape(-1) if name == "final_norm_g" else arr

    return (loss, grad_x,
            *[fix(n, out[n][0]) for n in _WEIGHTS], *[fix(n, out[n][1]) for n in _WEIGHTS],
            *[fix(n, out[n][2]) for n in _WEIGHTS], *[fix(n, out[n][3]) for n in _WEIGHTS])
```

```python
import functools
import operator

import jax
import jax.numpy as jnp
from jax import lax
from jax.experimental import pallas as pl
from jax.experimental.pallas import tpu as pltpu

F32 = jnp.float32
BF16 = jnp.bfloat16

N_DEV = 8
N_META = 16
DEPTH = 2
EPS = 1e-6
POOL_WINDOWS = (2, 4, 8, 16)
POOL_GW = 64
CONV_K = 31
RG_CONV_K = 4
RG_HEADS = 8
RG_HD = 64
RG_C = 8.0
D_POOL = 256
D_CONV = 256
D_RNN = 512

ADAM_LR = 0.001
ADAM_B1 = 0.9
ADAM_B2 = 0.999
ADAM_EPS = 1e-08
ADAM_WD = 0.01
ADAM_STEP = 10

TT = 432
TD = 912
TD_BIG = 2736
HALO = 48
VMEM_LIMIT = 56 * 1024 * 1024

MESH = pl.DeviceIdType.MESH
ANY = pl.BlockSpec(memory_space=pl.ANY)


def _cparams(*sem):
    return pltpu.CompilerParams(dimension_semantics=sem, vmem_limit_bytes=VMEM_LIMIT)


def _full(shape):
    nd = len(shape)
    return pl.BlockSpec(shape, lambda *_: (0,) * nd)


def _layer(shape, l):
    nd = len(shape)
    return pl.BlockSpec((None,) + tuple(shape), lambda *_: (l,) + (0,) * nd)


def _dot(a, b):
    return jnp.dot(a, b, preferred_element_type=F32)


def _dot_nt(a, b):
    return lax.dot_general(a, b, (((1,), (1,)), ((), ())), preferred_element_type=F32)


def _dot_tn(a, b):
    return lax.dot_general(a, b, (((0,), (0,)), ((), ())), preferred_element_type=F32)


def _rms(h):
    r = lax.rsqrt(jnp.mean(h * h, axis=-1, keepdims=True) + EPS)
    return h * r, r


def _rms_bwd(du, xh, r, g):
    dxh = du * g
    return r * (dxh - xh * jnp.mean(dxh * xh, axis=-1, keepdims=True))


def _sigmoid(x):
    return 1.0 / (1.0 + jnp.exp(-x))


def _colsum(x):
    return jnp.sum(x, axis=0, keepdims=True)


def _mesh_pos():
    return lax.axis_index("x"), lax.axis_index("y"), lax.axis_index("c")


def _slot(d):
    return 4 * d[0] + 2 * d[1] + d[2]


class _Comm:
    def __init__(self):
        self.srcs, self.send, self.dst, self.land, self.out_shapes = [], [], [], [], []

    def add_out(self, shape, dtype):
        self.out_shapes.append(jax.ShapeDtypeStruct(shape, dtype))
        return len(self.out_shapes) - 1

    def add(self, src, send, dst, land):
        self.srcs.append(src)
        self.send.append(send)
        self.dst.append(dst)
        self.land.append(land)

    def gather(self, block, out_shape, place):
        self.add(block, lambda ref, s: ref, self.add_out(out_shape, block.dtype), place)

    def sem_shapes(self):
        n = len(self.srcs)
        return [pltpu.SemaphoreType.DMA((7 * n,)), pltpu.SemaphoreType.DMA((7 * n,)), pltpu.SemaphoreType.DMA((n,))]

    def _copies(self, cins, couts, send_sems, recv_sems, local_sems, with_arrivals=True):
        x, y, c = _mesh_pos()
        me = (x, y, c)
        sends, arrivals, local = [], [], []
        for a in range(len(self.srcs)):
            out = couts[self.dst[a]]
            mine = self.land[a](out, _slot(me))
            local.append(pltpu.make_async_copy(self.send[a](cins[a], _slot(me)), mine, local_sems.at[a]))
            for k in range(N_DEV - 1):
                d = k + 1
                peer = (1 - x if (d >> 2) & 1 else x, 1 - y if (d >> 1) & 1 else y, 1 - c if d & 1 else c)
                sems = dict(send_sem=send_sems.at[a * 7 + k], recv_sem=recv_sems.at[a * 7 + k],
                            device_id=peer, device_id_type=MESH)
                sends.append(pltpu.make_async_remote_copy(
                    src_ref=self.send[a](cins[a], _slot(peer)), dst_ref=mine, **sems))
                if with_arrivals:
                    theirs = self.land[a](out, _slot(peer))
                    arrivals.append(pltpu.make_async_remote_copy(src_ref=theirs, dst_ref=theirs, **sems))
        return sends, arrivals, local

    def start(self, *refs):
        sends, _, local = self._copies(*refs, with_arrivals=False)
        for cp in local + sends:
            cp.start()

    def wait(self, *refs):
        sends, arrivals, local = self._copies(*refs)
        for cp in arrivals:
            cp.wait_recv()
        for cp in sends:
            cp.wait_send()
        for cp in local:
            cp.wait()


def _call(body, *, name, grid, in_specs, out_specs, out_shape, operands, scratch_shapes=(), sem=None, comm=None):
    out_specs, out_shape = list(out_specs), list(out_shape)
    if comm is None:
        res = pl.pallas_call(body, name=name, grid=grid, in_specs=list(in_specs), out_specs=out_specs,
                             out_shape=out_shape, scratch_shapes=list(scratch_shapes),
                             compiler_params=_cparams(*sem))(*operands)
        return list(res), []
    n_in, n_out, n_scr = len(in_specs), len(out_specs), len(scratch_shapes)
    nc_in, nc_out = len(comm.srcs), len(comm.out_shapes)

    def hosted(*refs):
        ins, cins = refs[:n_in], refs[n_in:n_in + nc_in]
        o0 = n_in + nc_in
        outs, couts = refs[o0:o0 + n_out], refs[o0 + n_out:o0 + n_out + nc_out]
        s0 = o0 + n_out + nc_out
        scr, sems = refs[s0:s0 + n_scr], refs[s0 + n_scr:]
        ids = [pl.program_id(a) for a in range(len(grid))]
        first = functools.reduce(operator.and_, [i == 0 for i in ids])
        last = functools.reduce(operator.and_, [i == g - 1 for i, g in zip(ids, grid)])

        @pl.when(first)
        def _():
            comm.start(cins, couts, *sems)

        body(*ins, *outs, *scr)

        @pl.when(last)
        def _():
            comm.wait(cins, couts, *sems)

    res = pl.pallas_call(
        hosted, name=name, grid=grid, in_specs=list(in_specs) + [ANY] * nc_in,
        out_specs=out_specs + [ANY] * nc_out, out_shape=out_shape + comm.out_shapes,
        scratch_shapes=list(scratch_shapes) + comm.sem_shapes(),
        compiler_params=_cparams(*(["arbitrary"] * len(grid))))(*operands, *comm.srcs)
    return list(res[:n_out]), list(res[n_out:])


def _comm_only(comm, name):
    def body(*refs):
        n, m = len(comm.srcs), len(comm.out_shapes)
        comm.start(refs[:n], refs[n:n + m], *refs[n + m:])
        comm.wait(refs[:n], refs[n:n + m], *refs[n + m:])

    return list(pl.pallas_call(body, name=name, in_specs=[ANY] * len(comm.srcs), out_specs=[ANY] * len(comm.out_shapes),
                               out_shape=comm.out_shapes, scratch_shapes=comm.sem_shapes())(*comm.srcs))


def _allgather(items, name):
    n = len(items)

    def body(*refs):
        ins, outs = refs[:n], refs[n:2 * n]
        send_sems, recv_sems, local_sems = refs[2 * n:]
        x, y, c = _mesh_pos()
        me, sib = (x, y, c), (x, y, 1 - c)
        chips = [(1 - x, y), (x, 1 - y), (1 - x, 1 - y)]

        def copy(a, k, blk, to, from_input=False):
            region = items[a][2](outs[a], _slot(blk))
            return pltpu.make_async_remote_copy(
                src_ref=ins[a] if from_input else region, dst_ref=region,
                send_sem=send_sems.at[a * 7 + k], recv_sem=recv_sems.at[a * 7 + k],
                device_id=to, device_id_type=MESH)

        local = [pltpu.make_async_copy(ins[a], items[a][2](outs[a], _slot(me)), local_sems.at[a]) for a in range(n)]
        for cp in local:
            cp.start()
        first = []
        for a in range(n):
            first.append(copy(a, 0, me, sib, True))
            for j, chip in enumerate(chips):
                first.append(copy(a, 1 + j, me, (*chip, c), True))
        for cp in first:
            cp.start()
        passed = []
        for j, chip in enumerate(chips):
            for a in range(n):
                copy(a, 1 + j, (*chip, c), me).wait_recv()
                fwd = copy(a, 4 + j, (*chip, c), sib)
                fwd.start()
                passed.append(fwd)
        for a in range(n):
            copy(a, 0, sib, me).wait_recv()
            for j, chip in enumerate(chips):
                copy(a, 4 + j, (*chip, 1 - c), me).wait_recv()
        for cp in first + passed:
            cp.wait_send()
        for cp in local:
            cp.wait()

    return pl.pallas_call(
        body, name=name,
        in_specs=[ANY] * n, out_specs=[ANY] * n,
        out_shape=[jax.ShapeDtypeStruct(it[1], it[0].dtype) for it in items],
        scratch_shapes=[pltpu.SemaphoreType.DMA((7 * n,)), pltpu.SemaphoreType.DMA((7 * n,)),
                        pltpu.SemaphoreType.DMA((n,))],
    )(*[it[0] for it in items])


def _rms_mm(h, g, w, l, ck, name, comm=None, w_is_nk=False, out_dtype=F32):
    T, D = h.shape
    N = w.shape[0] if w_is_nk else w.shape[1]
    normed = h.dtype == BF16

    def body(h_ref, g_ref, w_ref, o_ref):
        if normed:
            u = h_ref[...]
        else:
            u = (_rms(h_ref[...])[0] * g_ref[l:l + 1, :]).astype(BF16)
        res = _dot_nt(u, w_ref[...]) if w_is_nk else _dot(u, w_ref[...])
        o_ref[...] = res.astype(out_dtype)

    w_spec = pl.BlockSpec((ck, D), lambda j, i: (j, 0)) if w_is_nk else pl.BlockSpec((D, ck), lambda j, i: (0, j))
    td = TD
    return _call(
        body, name=name, grid=(N // ck, T // td),
        in_specs=[pl.BlockSpec((td, D), lambda j, i: (i, 0)), _full(g.shape), w_spec],
        out_specs=[pl.BlockSpec((td, ck), lambda j, i: (i, j))],
        out_shape=[jax.ShapeDtypeStruct((T, N), out_dtype)],
        operands=(h, g, w), sem=("parallel", "parallel"), comm=comm)


def _mm_res(res, a, w, ck, relu2, name, comm=None, norm=None):
    T, D = res.shape
    K = a.shape[-1]
    nk = K // ck

    def body(*refs):
        if norm is None:
            r_ref, a_ref, w_ref, o_ref = refs
        else:
            r_ref, a_ref, w_ref, g_ref, o_ref, u_ref = refs
        k = pl.program_id(1)

        @pl.when(k == 0)
        def _():
            o_ref[...] = r_ref[...]

        av = a_ref[...]
        if relu2:
            av = jnp.square(jnp.maximum(av.astype(F32), 0.0))
        o_ref[...] += _dot(av.astype(BF16), w_ref[...])

        if norm is not None:
            @pl.when(k == nk - 1)
            def _():
                u_ref[...] = (_rms(o_ref[...])[0] * g_ref[norm[1]:norm[1] + 1, :]).astype(BF16)

    row = pl.BlockSpec((TD, D), lambda i, k: (i, 0))
    in_specs = [row, pl.BlockSpec((TD, ck), lambda i, k: (i, k)), pl.BlockSpec((ck, D), lambda i, k: (k, 0))]
    operands = (res, a, w)
    out_specs, out_shape = [row], [jax.ShapeDtypeStruct((T, D), F32)]
    if norm is not None:
        in_specs.append(_full(norm[0].shape))
        operands += (norm[0],)
        out_specs.append(row)
        out_shape.append(jax.ShapeDtypeStruct((T, D), BF16))
    return _call(
        body, name=name, grid=(T // TD, nk), in_specs=in_specs, out_specs=out_specs, out_shape=out_shape,
        operands=operands, sem=("parallel", "arbitrary"), comm=comm)


def _final_loss(h, gf, tgt):
    T, D = h.shape

    def body(h_ref, g_ref, t_ref, dh_ref, dhb_ref, dg_ref, loss_ref):
        i = pl.program_id(0)

        @pl.when(i == 0)
        def _():
            dg_ref[...] = jnp.zeros_like(dg_ref)
            loss_ref[...] = jnp.zeros_like(loss_ref)

        g = g_ref[...]
        xh, r = _rms(h_ref[...])
        row = i * TD + lax.broadcasted_iota(jnp.int32, (TD, 1), 0)
        keep = (row >= N_META).astype(F32)
        diff = (xh * g - t_ref[...]) * keep
        part = jnp.sum(jnp.sum(diff * diff, axis=1, keepdims=True), axis=0, keepdims=True)
        loss_ref[...] += (0.5 / D) * part
        dy = diff * (1.0 / D)
        dg_ref[...] += _colsum(dy * xh)
        dh = _rms_bwd(dy, xh, r, g)
        dh_ref[...] = dh
        dhb_ref[...] = dh.astype(BF16)

    row = pl.BlockSpec((TD, D), lambda i: (i, 0))
    return pl.pallas_call(
        body, name="final_loss", grid=(T // TD,),
        in_specs=[row, _full(gf.shape), row],
        out_specs=[row, row, _full((1, D)), _full((8, 128))],
        out_shape=[jax.ShapeDtypeStruct((T, D), F32), jax.ShapeDtypeStruct((T, D), BF16),
                   jax.ShapeDtypeStruct((1, D), F32), jax.ShapeDtypeStruct((8, 128), F32)],
        compiler_params=_cparams("arbitrary"))(h, gf, tgt)


def _bwd_dz(dh, wdown, z, ck, name, comm=None):
    T, D = dh.shape
    F = z.shape[-1]

    def body(d_ref, w_ref, z_ref, o_ref):
        dact = _dot_nt(d_ref[...], w_ref[...])
        o_ref[...] = (dact * (2.0 * jnp.maximum(z_ref[...].astype(F32), 0.0))).astype(BF16)

    return _call(
        body, name=name, grid=(F // ck, T // TD),
        in_specs=[pl.BlockSpec((TD, D), lambda j, i: (i, 0)), pl.BlockSpec((ck, D), lambda j, i: (j, 0)),
                  pl.BlockSpec((TD, ck), lambda j, i: (i, j))],
        out_specs=[pl.BlockSpec((TD, ck), lambda j, i: (i, j))],
        out_shape=[jax.ShapeDtypeStruct((T, F), BF16)],
        operands=(dh, wdown, z), sem=("parallel", "parallel"), comm=comm)


def _bwd_plain(dh, w, name, comm=None):
    T, D = dh.shape
    K = w.shape[0]

    def body(d_ref, w_ref, o_ref):
        o_ref[...] = _dot_nt(d_ref[...], w_ref[...])

    return _call(
        body, name=name, grid=(T // TD,),
        in_specs=[pl.BlockSpec((TD, D), lambda i: (i, 0)), _full((K, D))],
        out_specs=[pl.BlockSpec((TD, K), lambda i: (i, 0))],
        out_shape=[jax.ShapeDtypeStruct((T, K), F32)],
        operands=(dh, w), sem=("parallel",), comm=comm)


def _bwd_rms(dx, w, h, g, dh_in, l, ck, name, comm=None, w_is_kd=False, emit_bf16=True):
    T, D = h.shape
    K = dx.shape[-1]
    nk = K // ck
    w_spec = pl.BlockSpec((ck, D), lambda i, k: (k, 0)) if w_is_kd else pl.BlockSpec((D, ck), lambda i, k: (0, k))

    def body(x_ref, w_ref, h_ref, g_ref, di_ref, *rest):
        dh_ref, dg_ref = rest[0], rest[2 if emit_bf16 else 1]
        i, k = pl.program_id(0), pl.program_id(1)

        @pl.when((i == 0) & (k == 0))
        def _():
            dg_ref[...] = jnp.zeros_like(dg_ref)

        part = _dot(x_ref[...], w_ref[...]) if w_is_kd else _dot_nt(x_ref[...], w_ref[...])

        def finish(du):
            xh, r = _rms(h_ref[...])
            dg_ref[...] += _colsum(du * xh)
            dh = di_ref[...] + _rms_bwd(du, xh, r, g_ref[l:l + 1, :])
            dh_ref[...] = dh
            if emit_bf16:
                rest[1][...] = dh.astype(BF16)

        if nk == 1:
            finish(part)
        else:
            acc_ref = rest[-1]

            @pl.when(k == 0)
            def _():
                acc_ref[...] = part

            @pl.when(k > 0)
            def _():
                acc_ref[...] += part

            @pl.when(k == nk - 1)
            def _():
                finish(acc_ref[...])

    row = pl.BlockSpec((TD, D), lambda i, k: (i, 0))
    out_specs = [row] + ([row] if emit_bf16 else []) + [_full((1, D))]
    out_shape = ([jax.ShapeDtypeStruct((T, D), F32)] + ([jax.ShapeDtypeStruct((T, D), BF16)] if emit_bf16 else [])
                 + [jax.ShapeDtypeStruct((1, D), F32)])
    return _call(
        body, name=name, grid=(T // TD, nk),
        in_specs=[pl.BlockSpec((TD, ck), lambda i, k: (i, k)), w_spec, row, _full(g.shape), row],
        out_specs=out_specs, out_shape=out_shape,
        scratch_shapes=[pltpu.VMEM((TD, D), F32)] if nk > 1 else [],
        operands=(dx, w, h, g, dh_in), sem=("arbitrary", "arbitrary"), comm=comm)


def _mm_tn(a, b, relu2, rms_on, g, l, chunk, ck, out_dtype, name, comm=None):
    T = a.shape[0]
    Ka, Nb = a.shape[1], b.shape[1]
    ca, cb = (ck, Nb) if chunk == "a" else (Ka, ck)
    nj = (Ka // ck) if chunk == "a" else (Nb // ck)
    td = TD_BIG if a.dtype == BF16 and b.dtype == BF16 else TD
    nt = T // td

    def body(*refs):
        if rms_on is not None:
            a_ref, g_ref, b_ref, o_ref, acc_ref = refs
        else:
            a_ref, b_ref, o_ref, acc_ref = refs
        t = pl.program_id(1)

        @pl.when(t == 0)
        def _():
            acc_ref[...] = jnp.zeros_like(acc_ref)

        av, bv = a_ref[...], b_ref[...]
        if relu2:
            av = jnp.square(jnp.maximum(av.astype(F32), 0.0))
        if rms_on == "a":
            av = _rms(av)[0] * g_ref[l:l + 1, :]
        elif rms_on == "b":
            bv = _rms(bv)[0] * g_ref[l:l + 1, :]
        acc_ref[...] += _dot_tn(av.astype(BF16), bv.astype(BF16))

        @pl.when(t == nt - 1)
        def _():
            o_ref[...] = acc_ref[...].astype(out_dtype)

    if chunk == "a":
        a_spec = pl.BlockSpec((td, ca), lambda j, t: (t, j))
        b_spec = pl.BlockSpec((td, cb), lambda j, t: (t, 0))
        o_spec = pl.BlockSpec((ca, cb), lambda j, t: (j, 0))
    else:
        a_spec = pl.BlockSpec((td, ca), lambda j, t: (t, 0))
        b_spec = pl.BlockSpec((td, cb), lambda j, t: (t, j))
        o_spec = pl.BlockSpec((ca, cb), lambda j, t: (0, j))
    ins, specs = [a], [a_spec]
    if rms_on is not None:
        ins.append(g)
        specs.append(_full(g.shape))
    ins.append(b)
    specs.append(b_spec)
    return _call(
        body, name=name, grid=(nj, nt), in_specs=specs, out_specs=[o_spec],
        out_shape=[jax.ShapeDtypeStruct((Ka, Nb), out_dtype)], scratch_shapes=[pltpu.VMEM((ca, cb), F32)],
        operands=tuple(ins), sem=("parallel", "arbitrary"), comm=comm)


def _shift_down(x, k):
    return x if k == 0 else pltpu.roll(x, k, 0)


def _shift_up(x, k):
    return x if k == 0 else pltpu.roll(x, x.shape[0] - k, 0)


def _pool_sel(lane_grp, vals):
    return jnp.where(lane_grp == 0, vals[0], jnp.where(lane_grp == 1, vals[1], jnp.where(lane_grp == 2, vals[2], vals[3])))


def _pool_fwd(ue, pos):
    s = ue
    sums = []
    for k in (1, 2, 4, 8):
        s = s + _shift_down(s, k)
        sums.append(s[HALO:])
    grp = lax.broadcasted_iota(jnp.int32, (1, D_POOL), 1) // POOL_GW
    wsum = _pool_sel(grp, sums)
    width = _pool_sel(grp, [jnp.float32(w) for w in POOL_WINDOWS])
    cnt = jnp.minimum(pos.astype(F32), width)
    return wsum / cnt - ue[HALO:], cnt, grp


def _conv_fwd(xe, w_ref, nk):
    acc = None
    for k in range(nk):
        term = _shift_down(xe, nk - 1 - k)[HALO:] * w_ref(k)
        acc = term if acc is None else acc + term
    return acc


def _taps(window, w_ref, nk, shift):
    acc = None
    for r in range(min(8, nk)):
        inner = None
        for q in range((nk - 1 - r) // 8 + 1):
            term = window(q) * w_ref(nk - 1 - 8 * q - r)
            inner = term if inner is None else inner + term
        inner = shift(inner, r)
        acc = inner if acc is None else acc + inner
    return acc


def _layernorm(cb):
    mu = jnp.mean(cb, axis=-1, keepdims=True)
    xc = cb - mu
    rstd = lax.rsqrt(jnp.mean(xc * xc, axis=-1, keepdims=True) + EPS)
    return xc * rstd, rstd


def _softplus_neg(lam):
    x = -lam
    e = jnp.exp(-jnp.abs(x))
    u = 1.0 + e
    d = u - 1.0
    log1p = jnp.where(d == 0.0, e, jnp.log(u) * (e / jnp.where(d == 0.0, 1.0, d)))
    return jnp.maximum(x, 0.0) + log1p


def _expm1(x):
    small = x * (1.0 + x * 0.5 * (1.0 + x * (1.0 / 3.0) * (1.0 + x * 0.25)))
    return jnp.where(jnp.abs(x) < 1e-2, small, jnp.exp(x) - 1.0)


def _gelu_parts(x):
    c0 = 0.7978845608028654
    inner = c0 * (x + 0.044715 * x * x * x)
    th = jnp.tanh(inner)
    gelu = 0.5 * x * (1.0 + th)
    dgelu = 0.5 * (1.0 + th) + 0.5 * x * (1.0 - th * th) * c0 * (1.0 + 3.0 * 0.044715 * x * x)
    return gelu, dgelu


def _rg_gates(xc, wa_ref, wx_ref, v512_ref):
    xb = xc.astype(BF16)
    ra = _sigmoid(_dot(xb, wa_ref[...]) + v512_ref[1:2, :])
    ri = _sigmoid(_dot(xb, wx_ref[...]) + v512_ref[2:3, :])
    sp = _softplus_neg(v512_ref[3:4, :])
    log_a = (-RG_C) * ra * sp
    a = jnp.exp(log_a)
    mult = jnp.sqrt(-_expm1(2.0 * log_a))
    return ra, ri, sp, a, mult


def _scan_down(a, b):
    n = a.shape[0]
    row = lax.broadcasted_iota(jnp.int32, (n, 1), 0)
    k = 1
    while k < n:
        ok = row >= k
        b = a * jnp.where(ok, _shift_down(b, k), 0.0) + b
        a = a * jnp.where(ok, _shift_down(a, k), 1.0)
        k *= 2
    return b, a


def _scan_up(c, g):
    n = c.shape[0]
    row = lax.broadcasted_iota(jnp.int32, (n, 1), 0)
    k = 1
    while k < n:
        ok = row < n - k
        g = c * jnp.where(ok, _shift_up(g, k), 0.0) + g
        c = c * jnp.where(ok, _shift_up(c, k), 1.0)
        k *= 2
    return g, c


def _mixer_specs(l, tile_of):
    r = TT // HALO
    return [
        pl.BlockSpec((TT, 1792), lambda i: (tile_of(i), 0)),
        pl.BlockSpec((HALO, 1792), lambda i: (jnp.maximum(tile_of(i) * r - 1, 0), 0)),
        _layer((256, 256), l), _layer((8, 256), l), _layer((32, 256), l), _layer((256, 256), l),
        _layer((512, 512), l), _layer((512, 512), l), _layer((8, 512), l),
    ]


def _mixer_fwd(p, wts, l, comm=None):
    T = p.shape[0]

    def body(p_ref, ph_ref, wpool_ref, v256_ref, dww_ref, wpw_ref, wa_ref, wx_ref, v512_ref, y_ref, hr_ref, cb_ref,
             rg_ref, carry_ref):
        i = pl.program_id(0)

        @pl.when(i == 0)
        def _():
            carry_ref[...] = jnp.zeros_like(carry_ref)

        halo = jnp.where(i == 0, 0.0, ph_ref[...])
        e = jnp.concatenate([halo, p_ref[...]], axis=0)
        pos = i * TT + 1 + lax.broadcasted_iota(jnp.int32, (TT, 1), 0)
        pooled, _, _ = _pool_fwd(e[:, 0:256], pos)
        y_a = _dot(pooled.astype(BF16), wpool_ref[...]) * v256_ref[0:1, :]
        ub = e[:, 256:512] * _sigmoid(e[:, 512:768])
        cb = _conv_fwd(ub, lambda k: dww_ref[k:k + 1, :], CONV_K) + v256_ref[1:2, :]
        cb_ref[...] = cb
        xhat, _ = _layernorm(cb)
        ln = xhat * v256_ref[2:3, :] + v256_ref[3:4, :]
        s = ln * _sigmoid(ln)
        y_b = _dot(s.astype(BF16), wpw_ref[...])
        xc = _conv_fwd(e[:, 1280:1792], lambda k: v512_ref[4 + k:5 + k, :], RG_CONV_K) + v512_ref[0:1, :]
        ra, ri, _, a, mult = _rg_gates(xc, wa_ref, wx_ref, v512_ref)
        for n, val in enumerate((xc, ra, ri, a, mult)):
            rg_ref[:, n * D_RNN:(n + 1) * D_RNN] = val
        hloc, acum = _scan_down(a, mult * ri * xc)
        hr = hloc + acum * carry_ref[0:1, :]
        hr_ref[...] = hr
        carry_ref[0:1, :] = hr_ref[TT - 1:TT, :]
        gelu, _ = _gelu_parts(p_ref[:, 768:1280])
        y_ref[:, 0:256] = y_a.astype(BF16)
        y_ref[:, 256:512] = y_b.astype(BF16)
        y_ref[:, 512:1024] = (gelu * hr).astype(BF16)

    return _call(
        body, name=f"mixer_fwd{l}", grid=(T // TT,),
        in_specs=_mixer_specs(l, lambda i: i),
        out_specs=[pl.BlockSpec((TT, 1024), lambda i: (i, 0)), pl.BlockSpec((TT, 512), lambda i: (i, 0)),
                   pl.BlockSpec((TT, 256), lambda i: (i, 0)), pl.BlockSpec((TT, 5 * D_RNN), lambda i: (i, 0))],
        out_shape=[jax.ShapeDtypeStruct((T, 1024), BF16), jax.ShapeDtypeStruct((T, 512), F32),
                   jax.ShapeDtypeStruct((T, 256), F32), jax.ShapeDtypeStruct((T, 5 * D_RNN), F32)],
        scratch_shapes=[pltpu.VMEM((8, 512), F32)],
        operands=(p, p, *wts), sem=("arbitrary",), comm=comm)


def _mixer_bwd(p, hr, cb, rg, dy, wts, l, comm=None):
    T = p.shape[0]
    nt = T // TT
    tile_of = lambda i: nt - 1 - i

    def body(p_ref, ph_ref, wpool_ref, v256_ref, dww_ref, wpw_ref, wa_ref, wx_ref, v512_ref, hr_ref, hrh_ref, cb_ref,
             rg_ref, dy_ref, dp_ref, g256_ref, g512_ref, gpool_ref, gpw_ref, gwa_ref, gwx_ref,
             q_c, dcb_c, dxc_c, ag_c, ub_s, ed_s):
        i = pl.program_id(0)
        j = nt - 1 - i

        @pl.when(i == 0)
        def _():
            for ref in (g256_ref, g512_ref, gpool_ref, gpw_ref, gwa_ref, gwx_ref, q_c, dcb_c, dxc_c, ag_c):
                ref[...] = jnp.zeros_like(ref)

        halo = jnp.where(j == 0, 0.0, ph_ref[...])
        e = jnp.concatenate([halo, p_ref[...]], axis=0)
        pos = j * TT + 1 + lax.broadcasted_iota(jnp.int32, (TT, 1), 0)
        row = lax.broadcasted_iota(jnp.int32, (TT, 1), 0)

        dy_a = dy_ref[:, 0:256]
        pooled, cnt, grp = _pool_fwd(e[:, 0:256], pos)
        scale = v256_ref[0:1, :]
        mixed = _dot(pooled.astype(BF16), wpool_ref[...])
        g256_ref[34:35, :] += _colsum(dy_a * mixed)
        dmixed = (dy_a * scale).astype(BF16)
        gpool_ref[...] += _dot_tn(pooled.astype(BF16), dmixed)
        dpooled = _dot_nt(dmixed, wpool_ref[...])
        q = dpooled / cnt
        s = jnp.concatenate([q, q_c[...]], axis=0)
        sums = []
        for k in (1, 2, 4, 8):
            s = s + _shift_up(s, k)
            sums.append(s[:TT])
        dp_ref[:, 0:256] = (_pool_sel(grp, sums) - dpooled).astype(BF16)
        q_c[...] = q[0:16]

        dy_b = dy_ref[:, 256:512].astype(BF16)
        sg = _sigmoid(e[:, 512:768])
        ub_s[...] = e[:, 256:512] * sg
        xhat, rstd = _layernorm(cb_ref[...])
        ln_g = v256_ref[2:3, :]
        ln = xhat * ln_g + v256_ref[3:4, :]
        sig = _sigmoid(ln)
        sact = ln * sig
        gpw_ref[...] += _dot_tn(sact.astype(BF16), dy_b)
        dln = _dot_nt(dy_b, wpw_ref[...]) * (sig * (1.0 + ln * (1.0 - sig)))
        g256_ref[32:33, :] += _colsum(dln * xhat)
        g256_ref[33:34, :] += _colsum(dln)
        dxhat = dln * ln_g
        dcb = rstd * (dxhat - jnp.mean(dxhat, axis=-1, keepdims=True)
                      - xhat * jnp.mean(dxhat * xhat, axis=-1, keepdims=True))
        g256_ref[31:32, :] += _colsum(dcb)
        pad8 = jnp.zeros((8, D_CONV), F32)
        dcb_ext = jnp.concatenate([pad8, dcb, pad8], axis=0)
        for r in range(8):
            d_r = _shift_up(dcb_ext, r)[0:TT + 8]
            for q in range((CONV_K - 1 - r) // 8 + 1):
                k = CONV_K - 1 - 8 * q - r
                g256_ref[k:k + 1, :] += _colsum(d_r * ub_s[pl.ds(HALO - 8 - 8 * q, TT + 8), :])
        ed_s[0:TT, :] = dcb
        ed_s[TT:TT + 32, :] = dcb_c[...]
        dub = _taps(lambda q: ed_s[pl.ds(8 * q, TT + 8), :], lambda k: dww_ref[k:k + 1, :], CONV_K, _shift_up)[:TT]
        dcb_c[...] = dcb[0:32]
        sg_t = sg[HALO:]
        v_t = p_ref[:, 256:512]
        dp_ref[:, 256:512] = (dub * sg_t).astype(BF16)
        dp_ref[:, 512:768] = (dub * v_t * sg_t * (1.0 - sg_t)).astype(BF16)

        dy_c = dy_ref[:, 512:1024]
        cxe = e[:, 1280:1792]
        xc, ra, ri, a, mult = (rg_ref[:, n * D_RNN:(n + 1) * D_RNN] for n in range(5))
        sp = _softplus_neg(v512_ref[3:4, :])
        hrv = hr_ref[...]
        gelu, dgelu = _gelu_parts(p_ref[:, 768:1280])
        dp_ref[:, 768:1280] = (dy_c * hrv * dgelu).astype(BF16)
        dhr = dy_c * gelu
        coef = jnp.where(row < TT - 1, _shift_up(a, 1), 1.0)
        gloc, ccum = _scan_up(coef, dhr)
        gg = gloc + ccum * ag_c[0:1, :]
        ag_c[...] = (a * gg)[0:8]
        hr_before = jnp.where(j == 0, 0.0, hrh_ref[7:8, :])
        hr_prev = jnp.where(row >= 1, _shift_down(hrv, 1), hr_before)
        da = gg * hr_prev
        dmult = gg * ri * xc
        dri = gg * mult * xc
        dxc = gg * mult * ri
        dlog_a = da * a - dmult * (a * a / mult)
        g512_ref[3:4, :] += _colsum(dlog_a * ((-RG_C) * ra))
        dpre_a = dlog_a * ((-RG_C) * sp) * ra * (1.0 - ra)
        dpre_x = dri * ri * (1.0 - ri)
        g512_ref[1:2, :] += _colsum(dpre_a)
        g512_ref[2:3, :] += _colsum(dpre_x)
        xb = xc.astype(BF16)
        dpa_b, dpx_b = dpre_a.astype(BF16), dpre_x.astype(BF16)
        gwa_ref[...] += _dot_tn(xb, dpa_b)
        gwx_ref[...] += _dot_tn(xb, dpx_b)
        dxc = dxc + _dot_nt(dpa_b, wa_ref[...]) + _dot_nt(dpx_b, wx_ref[...])
        g512_ref[0:1, :] += _colsum(dxc)
        for k in range(RG_CONV_K):
            g512_ref[4 + k:5 + k, :] += _colsum(dxc * _shift_down(cxe, RG_CONV_K - 1 - k)[HALO:])
        ex = jnp.concatenate([dxc, dxc_c[...]], axis=0)
        dcx = _taps(lambda q: ex, lambda k: v512_ref[4 + k:5 + k, :], RG_CONV_K, _shift_up)[:TT]
        dxc_c[...] = dxc[0:8]
        dp_ref[:, 1280:1792] = dcx.astype(BF16)

        @pl.when(i == nt - 1)
        def _():
            lam = v512_ref[3:4, :]
            g512_ref[3:4, :] = g512_ref[3:4, :] * (-_sigmoid(-lam))

    in_specs = _mixer_specs(l, tile_of) + [
        pl.BlockSpec((TT, 512), lambda i: (tile_of(i), 0)),
        pl.BlockSpec((8, 512), lambda i: (jnp.maximum(tile_of(i) * (TT // 8) - 1, 0), 0)),
        pl.BlockSpec((TT, 256), lambda i: (tile_of(i), 0)),
        pl.BlockSpec((TT, 5 * D_RNN), lambda i: (tile_of(i), 0)),
        pl.BlockSpec((TT, 1024), lambda i: (tile_of(i), 0)),
    ]
    acc_shapes = [(40, 256), (8, 512), (256, 256), (256, 256), (512, 512), (512, 512)]
    return _call(
        body, name=f"mixer_bwd{l}", grid=(nt,),
        in_specs=in_specs,
        out_specs=[pl.BlockSpec((TT, 1792), lambda i: (tile_of(i), 0))] + [_full(s) for s in acc_shapes],
        out_shape=[jax.ShapeDtypeStruct((T, 1792), BF16)] + [jax.ShapeDtypeStruct(s, F32) for s in acc_shapes],
        scratch_shapes=[pltpu.VMEM((16, 256), F32), pltpu.VMEM((32, 256), F32), pltpu.VMEM((8, 512), F32),
                        pltpu.VMEM((8, 512), F32), pltpu.VMEM((HALO + TT, D_CONV), F32),
                        pltpu.VMEM((TT + 32, D_CONV), F32)],
        operands=(p, p, *wts, hr, hr, cb, rg, dy), sem=("arbitrary",), comm=comm)


def _adam(w, g, m, v):
    m = ADAM_B1 * m + (1.0 - ADAM_B1) * g
    v = ADAM_B2 * v + (1.0 - ADAM_B2) * (g * g)
    m_hat = m / (1.0 - ADAM_B1 ** ADAM_STEP)
    v_hat = v / (1.0 - ADAM_B2 ** ADAM_STEP)
    delta = -ADAM_LR * (m_hat / (jnp.sqrt(v_hat) + ADAM_EPS) + ADAM_WD * w)
    return delta, m, v


def _adamw_big(recvs, w, m, v, tr, name):
    L, R, C = w.shape

    def body(r0_ref, r1_ref, w_ref, m_ref, v_ref, g_ref, d_ref, mo_ref, vo_ref):
        l = pl.program_id(0)

        def total(r_ref):
            g = r_ref[0].astype(F32)
            for s in range(1, N_DEV):
                g = g + r_ref[s].astype(F32)
            g_ref[...] = g

        @pl.when(l == 0)
        def _():
            total(r0_ref)

        @pl.when(l == 1)
        def _():
            total(r1_ref)

        d_ref[...], mo_ref[...], vo_ref[...] = _adam(w_ref[...], g_ref[...], m_ref[...], v_ref[...])

    blk = pl.BlockSpec((None, tr, C), lambda l, i: (l, i, 0))
    return pl.pallas_call(
        body, name=name, grid=(L, R // tr),
        in_specs=[pl.BlockSpec((N_DEV, tr, C), lambda l, i: (0, i * (1 - l), 0)),
                  pl.BlockSpec((N_DEV, tr, C), lambda l, i: (0, i * l, 0)), blk, blk, blk],
        out_specs=[blk] * 4,
        out_shape=[jax.ShapeDtypeStruct((L, R, C), F32)] * 4,
        compiler_params=_cparams("arbitrary", "arbitrary"))(recvs[0], recvs[1], w, m, v)


def _sum_slots(recvs, name):
    L = len(recvs)
    _, R, C = recvs[0].shape

    def body(*refs):
        o_ref = refs[L]
        for l in range(L):
            acc = refs[l][0].astype(F32)
            for s in range(1, N_DEV):
                acc = acc + refs[l][s].astype(F32)
            o_ref[l] = acc

    return pl.pallas_call(body, name=name, out_shape=jax.ShapeDtypeStruct((L, R, C), F32))(*recvs)


def _adamw_plain(gs, ws, ms, vs, name):
    n = len(gs)

    def body(*refs):
        g_r, w_r, m_r, v_r = refs[:n], refs[n:2 * n], refs[2 * n:3 * n], refs[3 * n:4 * n]
        outs = refs[4 * n:]
        for a in range(n):
            d, mo, vo = _adam(w_r[a][...], g_r[a][...], m_r[a][...], v_r[a][...])
            outs[3 * a][...] = d
            outs[3 * a + 1][...] = mo
            outs[3 * a + 2][...] = vo

    shapes = []
    for a in range(n):
        shapes += [jax.ShapeDtypeStruct(ws[a].shape, F32)] * 3
    return pl.pallas_call(body, name=name, out_shape=shapes,
                          compiler_params=pltpu.CompilerParams(vmem_limit_bytes=VMEM_LIMIT))(*gs, *ws, *ms, *vs)


_V256_ROWS = {"convb_dw_b": 31, "convb_ln_g": 32, "convb_ln_b": 33, "pool_scale": 34}
_V512_ROWS = {"rg_conv_b": 0, "rg_b_a": 1, "rg_b_x": 2, "rg_lambda": 3}
_SMALL_REPL = ["mix_norm_g", "mlp_norm_g", "final_norm_g", "pool_w", "pool_scale", "convb_dw_b", "convb_ln_g",
               "convb_ln_b", "rg_conv_b", "rg_w_a", "rg_b_a", "rg_w_x", "rg_b_x", "rg_lambda"]


def _adamw_small(gath1, gath0, gath_tail, ws, ms, vs):
    n = len(_SMALL_REPL)
    n1, n0, nt = len(gath1), len(gath0), len(gath_tail)
    ng = n1 + n0 + nt
    L = DEPTH
    per_layer = [g.shape[1:] for g in gath1[:8]]
    sum_shapes = ([(L,) + s[1:] for s in per_layer[:2]] + [gath1[8].shape[1:]] + [(L,) + s for s in per_layer[2:]]
                  + [gath_tail[1].shape[1:]])

    def body(*refs):
        in1, in0, in_t = refs[:n1], refs[n1:n1 + n0], refs[n1 + n0:ng]
        w_r, m_r, v_r = refs[ng:ng + n], refs[ng + n:ng + 2 * n], refs[ng + 2 * n:ng + 3 * n]
        sums = refs[ng + 3 * n:ng + 3 * n + 10]
        outs = refs[ng + 3 * n + 10:]

        def total(ref):
            acc = ref[0]
            for s in range(1, N_DEV):
                acc = acc + ref[s]
            return acc

        s_mix, s_mlp, s_fin, s_pool, s_256, s_512, s_pw, s_wa, s_wx, s_meta = sums
        s_mix[1:2, :] = total(in1[0])
        s_mix[0:1, :] = total(in_t[0])
        s_mlp[1:2, :] = total(in1[1])
        s_mlp[0:1, :] = total(in0[0])
        s_fin[...] = total(in1[8])
        s_meta[...] = total(in_t[1])
        for k, dst in enumerate((s_pool, s_256, s_512, s_pw, s_wa, s_wx)):
            dst[1] = total(in1[2 + k])
            dst[0] = total(in0[1 + k])

        def grad_of(name, idx):
            if name == "mix_norm_g":
                return s_mix[idx[0]:idx[0] + 1, :]
            if name == "mlp_norm_g":
                return s_mlp[idx[0]:idx[0] + 1, :]
            if name == "final_norm_g":
                return s_fin[...]
            if name in _V256_ROWS:
                r = _V256_ROWS[name]
                return s_256[idx[0], r:r + 1, :]
            if name in _V512_ROWS:
                r = _V512_ROWS[name]
                return s_512[idx[0], r:r + 1, :]
            src = {"pool_w": s_pool, "rg_w_a": s_wa, "rg_w_x": s_wx}[name]
            return src[idx[0], :, idx[1] * 64:(idx[1] + 1) * 64]

        for a, name in enumerate(_SMALL_REPL):
            shape = w_r[a].shape
            if name == "final_norm_g":
                parts = [((), (slice(None), slice(None)))]
            elif len(shape) == 2:
                parts = [((l,), (slice(l, l + 1), slice(None))) for l in range(shape[0])]
            else:
                parts = [((l, h), (l, h)) for l in range(shape[0]) for h in range(shape[1])]
            for idx, sel in parts:
                g = grad_of(name, idx)
                d, mo, vo = _adam(w_r[a][sel], g, m_r[a][sel], v_r[a][sel])
                outs[4 * a][sel] = g
                outs[4 * a + 1][sel] = d
                outs[4 * a + 2][sel] = mo
                outs[4 * a + 3][sel] = vo

    out_shape = [jax.ShapeDtypeStruct(s, F32) for s in sum_shapes]
    for a in range(n):
        out_shape += [jax.ShapeDtypeStruct(ws[a].shape, F32)] * 4
    res = pl.pallas_call(body, name="adamw_small", out_shape=out_shape,
                         compiler_params=pltpu.CompilerParams(vmem_limit_bytes=VMEM_LIMIT))(
                             *gath1, *gath0, *gath_tail, *ws, *ms, *vs)
    return res[:10], res[10:]


_WEIGHTS = ['meta_tokens', 'mix_norm_g', 'w_in', 'pool_w', 'pool_scale', 'convb_dw_w', 'convb_dw_b', 'convb_ln_g',
            'convb_ln_b', 'convb_pw_w', 'rg_conv_w', 'rg_conv_b', 'rg_w_a', 'rg_b_a', 'rg_w_x', 'rg_b_x', 'rg_lambda',
            'w_out', 'mlp_norm_g', 'w_up', 'w_down', 'final_norm_g']


def _block_diag(w):
    L, H, C, _ = w.shape
    eye = jnp.eye(H, dtype=w.dtype)
    return (w[:, :, :, None, :] * eye[None, :, None, :, None]).reshape(L, H * C, H * C)


def _diag_blocks(m, H):
    C = m.shape[0] // H
    return jnp.concatenate([m[h * C:(h + 1) * C, h * C:(h + 1) * C] for h in range(H)], axis=1)


def kernel(x, meta_tokens, mix_norm_g, w_in, pool_w, pool_scale, convb_dw_w, convb_dw_b, convb_ln_g, convb_ln_b, convb_pw_w, rg_conv_w, rg_conv_b, rg_w_a, rg_b_a, rg_w_x, rg_b_x, rg_lambda, w_out, mlp_norm_g, w_up, w_down, final_norm_g, loss_target, m_meta_tokens, m_mix_norm_g, m_w_in, m_pool_w, m_pool_scale, m_convb_dw_w, m_convb_dw_b, m_convb_ln_g, m_convb_ln_b, m_convb_pw_w, m_rg_conv_w, m_rg_conv_b, m_rg_w_a, m_rg_b_a, m_rg_w_x, m_rg_b_x, m_rg_lambda, m_w_out, m_mlp_norm_g, m_w_up, m_w_down, m_final_norm_g, v_meta_tokens, v_mix_norm_g, v_w_in, v_pool_w, v_pool_scale, v_convb_dw_w, v_convb_dw_b, v_convb_ln_g, v_convb_ln_b, v_convb_pw_w, v_rg_conv_w, v_rg_conv_b, v_rg_w_a, v_rg_b_a, v_rg_w_x, v_rg_b_x, v_rg_lambda, v_w_out, v_mlp_norm_g, v_w_up, v_w_down, v_final_norm_g):
    W = dict(meta_tokens=meta_tokens, mix_norm_g=mix_norm_g, w_in=w_in, pool_w=pool_w, pool_scale=pool_scale,
             convb_dw_w=convb_dw_w, convb_dw_b=convb_dw_b, convb_ln_g=convb_ln_g, convb_ln_b=convb_ln_b,
             convb_pw_w=convb_pw_w, rg_conv_w=rg_conv_w, rg_conv_b=rg_conv_b, rg_w_a=rg_w_a, rg_b_a=rg_b_a,
             rg_w_x=rg_w_x, rg_b_x=rg_b_x, rg_lambda=rg_lambda, w_out=w_out, mlp_norm_g=mlp_norm_g, w_up=w_up,
             w_down=w_down, final_norm_g=final_norm_g.reshape(1, -1))
    M = dict(meta_tokens=m_meta_tokens, mix_norm_g=m_mix_norm_g, w_in=m_w_in, pool_w=m_pool_w, pool_scale=m_pool_scale,
             convb_dw_w=m_convb_dw_w, convb_dw_b=m_convb_dw_b, convb_ln_g=m_convb_ln_g, convb_ln_b=m_convb_ln_b,
             convb_pw_w=m_convb_pw_w, rg_conv_w=m_rg_conv_w, rg_conv_b=m_rg_conv_b, rg_w_a=m_rg_w_a, rg_b_a=m_rg_b_a,
             rg_w_x=m_rg_w_x, rg_b_x=m_rg_b_x, rg_lambda=m_rg_lambda, w_out=m_w_out, mlp_norm_g=m_mlp_norm_g,
             w_up=m_w_up, w_down=m_w_down, final_norm_g=m_final_norm_g.reshape(1, -1))
    V = dict(meta_tokens=v_meta_tokens, mix_norm_g=v_mix_norm_g, w_in=v_w_in, pool_w=v_pool_w, pool_scale=v_pool_scale,
             convb_dw_w=v_convb_dw_w, convb_dw_b=v_convb_dw_b, convb_ln_g=v_convb_ln_g, convb_ln_b=v_convb_ln_b,
             convb_pw_w=v_convb_pw_w, rg_conv_w=v_rg_conv_w, rg_conv_b=v_rg_conv_b, rg_w_a=v_rg_w_a, rg_b_a=v_rg_b_a,
             rg_w_x=v_rg_w_x, rg_b_x=v_rg_b_x, rg_lambda=v_rg_lambda, w_out=v_w_out, mlp_norm_g=v_mlp_norm_g,
             w_up=v_w_up, w_down=v_w_down, final_norm_g=v_final_norm_g.reshape(1, -1))

    xs = x[0]
    S, D = xs.shape
    T = S + N_META
    assert T % TT == 0 and D == 1024
    L = DEPTH
    me = 4 * lax.axis_index("x") + 2 * lax.axis_index("y") + lax.axis_index("c")
    c_in, c_ff = w_in.shape[-1], w_up.shape[-1]
    d_in, d_ff = c_in * N_DEV, c_ff * N_DEV
    r_out, r_dn = w_out.shape[1], w_down.shape[1]
    by_slot = lambda ref, s: ref.at[s]
    rows_of = lambda n: (lambda ref, s: ref.at[pl.ds(s * n, n), :])
    cols_of = lambda n: (lambda ref, s: ref.at[:, pl.ds(s * n, n)])

    win_t = jnp.transpose(w_in, (0, 2, 1)).astype(BF16)
    gathered = _allgather([
        (win_t[0], (d_in, D), rows_of(c_in)),
        (meta_tokens, (N_DEV,) + meta_tokens.shape, by_slot),
        (convb_dw_w, (N_DEV,) + convb_dw_w.shape, by_slot),
        (convb_pw_w, (N_DEV,) + convb_pw_w.shape, by_slot),
        (rg_conv_w, (N_DEV,) + rg_conv_w.shape, by_slot),
    ], "gather_first")
    win_f = [gathered[0], None]
    meta_f = jnp.transpose(gathered[1], (1, 0, 2)).reshape(N_META, D)
    dww_f = jnp.transpose(gathered[2], (1, 2, 0, 3)).reshape(L, CONV_K, D_CONV)
    wpw_f = jnp.transpose(gathered[3], (1, 0, 2, 3)).reshape(L, D_CONV, D_CONV).astype(BF16)
    rgw_f = jnp.transpose(gathered[4], (1, 2, 0, 3)).reshape(L, RG_CONV_K, D_RNN)
    wout_b, wup_b, wdown_b = w_out.astype(BF16), w_up.astype(BF16), w_down.astype(BF16)

    def weight_gather(blocks):
        comm = _Comm()
        for blk, shape, place in blocks:
            comm.gather(blk, shape, place)
        return comm

    wpool_bd = _block_diag(pool_w).astype(BF16)
    wa_bd = _block_diag(rg_w_a).astype(BF16)
    wx_bd = _block_diag(rg_w_x).astype(BF16)
    zeros4 = jnp.zeros((L, 4, D_POOL), F32)
    v256 = jnp.concatenate([pool_scale[:, None], convb_dw_b[:, None], convb_ln_g[:, None], convb_ln_b[:, None], zeros4], axis=1)
    dww_p = jnp.concatenate([dww_f, jnp.zeros((L, 1, D_CONV), F32)], axis=1)
    v512 = jnp.concatenate([rg_conv_b[:, None], rg_b_a[:, None], rg_b_x[:, None], rg_lambda[:, None], rgw_f], axis=1)
    mix_w = (wpool_bd, v256, dww_p, wpw_f, wa_bd, wx_bd, v512)

    h = jnp.concatenate([meta_f, xs], axis=0)
    tgt = jnp.concatenate([jnp.zeros((N_META, D), F32), loss_target[0]], axis=0)
    wout_f, wup_f, wdown_f = [None] * L, [None] * L, [None] * L
    saved = []

    (p,), wout_f = _rms_mm(h, mix_norm_g, win_f[0], 0, d_in, "in_proj0",
                           weight_gather([(wout_b[l], (r_out * N_DEV, D), rows_of(r_out)) for l in range(L)]),
                           w_is_nk=True)
    (y, *kept), (wup_f[0],) = _mixer_fwd(p, mix_w, 0, weight_gather([(wup_b[0], (D, d_ff), cols_of(c_ff))]))
    (h2, u2), (win_f[1],) = _mm_res(h, y, wout_f[0], y.shape[-1], False, "out_proj0",
                                    weight_gather([(win_t[1], (d_in, D), rows_of(c_in))]), norm=(mlp_norm_g, 0))
    (z,), (wdown_f[0],) = _rms_mm(u2, mlp_norm_g, wup_f[0], 0, 2048, "mlp_up0",
                                  weight_gather([(wdown_b[0], (d_ff, D), rows_of(r_dn))]), out_dtype=BF16)
    (h3, u1), (wup_f[1],) = _mm_res(h2, z, wdown_f[0], 2048, True, "mlp_down0",
                                    weight_gather([(wup_b[1], (D, d_ff), cols_of(c_ff))]), norm=(mix_norm_g, 1))
    saved.append((h, None, p, y, kept, h2, u2, z))
    h = h3
    (p,), _ = _rms_mm(u1, mix_norm_g, win_f[1], 1, d_in, "in_proj1", w_is_nk=True)
    (y, *kept), (wdown_f[1],) = _mixer_fwd(p, mix_w, 1, weight_gather([(wdown_b[1], (d_ff, D), rows_of(r_dn))]))
    (h2, u2), _ = _mm_res(h, y, wout_f[1], y.shape[-1], False, "out_proj1", norm=(mlp_norm_g, 1))
    (z,), _ = _rms_mm(u2, mlp_norm_g, wup_f[1], 1, 2048, "mlp_up1", out_dtype=BF16)
    (h3,), _ = _mm_res(h2, z, wdown_f[1], 2048, True, "mlp_down1")
    saved.append((h, u1, p, y, kept, h2, u2, z))

    dh, dhb, g_fin, loss_part = _final_loss(h3, W["final_norm_g"], tgt)
    loss = lax.psum(loss_part[0, 0], ("x", "y", "c"))

    def grad_exchange(pieces):
        comm = _Comm()
        for g, send, shape in pieces:
            comm.add(g, send, comm.add_out((N_DEV,) + shape, BF16), by_slot)
        return comm

    def small_gather(comm, grads):
        for g in grads:
            comm.gather(g, (N_DEV,) + g.shape, by_slot)
        return comm

    recv = {n: [None] * L for n in ("w_in", "w_out", "w_up", "w_down")}
    small1 = gath1 = gath0 = g_win_t1 = None
    for l in reversed(range(L)):
        h0, u1, p, y, kept, h2, u2, z = saved[l]
        (dz,), _ = _bwd_dz(dhb, wdown_f[l], z, 2048, f"mlp_down_bwd{l}")
        (g_wdown,), _ = _mm_tn(z, dhb, True, None, None, l, "a", 512, BF16, f"mlp_down_wgrad{l}")
        comm = grad_exchange([(g_wdown, rows_of(r_dn), (r_dn, D))]) if l == 0 else None
        (dh2, dh2b, g_mlp), got = _bwd_rms(dz, wup_f[l], h2, mlp_norm_g, dh, l, 2048, f"mlp_up_bwd{l}", comm)
        if l == 0:
            recv["w_down"][0] = got[0]
        (g_wup,), _ = _mm_tn(u2, dz, False, None, None, l, "b", 1024, BF16, f"mlp_up_wgrad{l}")
        (dy,), _ = _bwd_plain(dh2b, wout_f[l], f"out_proj_bwd{l}")
        (g_wout,), _ = _mm_tn(y, dh2b, False, None, None, l, "b", D, BF16, f"out_proj_wgrad{l}")
        comm = grad_exchange([(g_wup, cols_of(c_ff), (D, c_ff)), (g_wout, rows_of(r_out), (r_out, D))]
                             + ([(g_wdown, rows_of(r_dn), (r_dn, D))] if l == 1 else
                                [(g_win_t1, rows_of(c_in), (c_in, D))]))
        if l == 0:
            small_gather(comm, small1)
        (dp, g256, g512, gpool, gpw, gwa, gwx), got = _mixer_bwd(p, *kept, dy, mix_w, l, comm)
        recv["w_up"][l], recv["w_out"][l] = got[0], got[1]
        if l == 1:
            recv["w_down"][1] = got[2]
        else:
            recv["w_in"][1], gath1 = got[2], got[3:]
        smalls = [g_mlp, _diag_blocks(gpool, 4), g256, g512, gpw, _diag_blocks(gwa, RG_HEADS), _diag_blocks(gwx, RG_HEADS)]
        comm = small_gather(_Comm(), smalls[:5]) if l == 0 else None
        if u1 is None:
            (g_win_t,), got_a = _mm_tn(dp, h0, False, "b", mix_norm_g, l, "a", d_in // 2, BF16, f"in_proj_wgrad{l}", comm)
        else:
            (g_win_t,), got_a = _mm_tn(dp, u1, False, None, None, l, "a", d_in // 2, BF16, f"in_proj_wgrad{l}", comm)
        comm = small_gather(grad_exchange([(g_win_t, rows_of(c_in), (c_in, D))]), smalls[5:]) if l == 0 else None
        outs, got_b = _bwd_rms(dp, win_f[l], h0, mix_norm_g, dh2, l, d_in, f"in_proj_bwd{l}", comm, w_is_kd=True,
                               emit_bf16=l > 0)
        dh, g_mix = outs[0], outs[-1]
        dhb = outs[1] if l > 0 else None
        if l == 0:
            recv["w_in"][0] = got_b[0]
            gath0 = got_a + got_b[1:]
        if l == 1:
            g_win_t1 = g_win_t
            small1 = [g_mix] + smalls + [g_fin]
    grad_x = dh[N_META:][None]

    gath_tail = _comm_only(small_gather(_Comm(), [g_mix, dh[:N_META]]), "exchange_tail")

    out = {}
    for name, tr in (("w_out", r_out), ("w_up", 256), ("w_down", 128)):
        out[name] = _adamw_big(recv[name], W[name], M[name], V[name], tr, f"adamw_{name}")
    g_win = jnp.transpose(_sum_slots(recv["w_in"], "sum_w_in"), (0, 2, 1))
    out["w_in"] = (g_win,) + tuple(_adamw_plain([g_win], [W["w_in"]], [M["w_in"]], [V["w_in"]], "adamw_w_in"))
    sums, small_out = _adamw_small(gath1, gath0, gath_tail, [W[n] for n in _SMALL_REPL], [M[n] for n in _SMALL_REPL],
                                   [V[n] for n in _SMALL_REPL])
    for a, name in enumerate(_SMALL_REPL):
        out[name] = small_out[4 * a:4 * a + 4]
    s_256, s_512, s_pw, s_meta = sums[4], sums[5], sums[6], sums[9]
    g_shard = {
        "meta_tokens": lax.dynamic_slice_in_dim(s_meta, me * 128, 128, axis=1),
        "convb_dw_w": lax.dynamic_slice_in_dim(s_256[:, :CONV_K, :], me * 32, 32, axis=2),
        "convb_pw_w": lax.dynamic_slice_in_dim(s_pw, me * 32, 32, axis=1),
        "rg_conv_w": lax.dynamic_slice_in_dim(s_512[:, 4:8, :], me * 64, 64, axis=2),
    }
    names = list(g_shard)
    res = _adamw_plain([g_shard[n] for n in names], [W[n] for n in names], [M[n] for n in names],
                       [V[n] for n in names], "adamw_small_sharded")
    for a, name in enumerate(names):
        out[name] = (g_shard[name],) + tuple(res[3 * a:3 * a + 3])

    def fix(name, arr):
        return arr.reshape(-1) if name == "final_norm_g" else arr

    return (loss, grad_x,
            *[fix(n, out[n][0]) for n in _WEIGHTS], *[fix(n, out[n][1]) for n in _WEIGHTS],
            *[fix(n, out[n][2]) for n in _WEIGHTS], *[fix(n, out[n][3]) for n in _WEIGHTS])
```

```python
import functools
import operator

import jax
import jax.numpy as jnp
from jax import lax
from jax.experimental import pallas as pl
from jax.experimental.pallas import tpu as pltpu

F32 = jnp.float32
BF16 = jnp.bfloat16

N_DEV = 8
N_META = 16
DEPTH = 2
EPS = 1e-6
POOL_WINDOWS = (2, 4, 8, 16)
POOL_GW = 64
CONV_K = 31
RG_CONV_K = 4
RG_HEADS = 8
RG_HD = 64
RG_C = 8.0
D_POOL = 256
D_CONV = 256
D_RNN = 512

ADAM_LR = 0.001
ADAM_B1 = 0.9
ADAM_B2 = 0.999
ADAM_EPS = 1e-08
ADAM_WD = 0.01
ADAM_STEP = 10

TT = 432
TD = 912
TD_BIG = 2736
HALO = 48
VMEM_LIMIT = 56 * 1024 * 1024

MESH = pl.DeviceIdType.MESH
ANY = pl.BlockSpec(memory_space=pl.ANY)


def _cparams(*sem):
    return pltpu.CompilerParams(dimension_semantics=sem, vmem_limit_bytes=VMEM_LIMIT)


def _full(shape):
    nd = len(shape)
    return pl.BlockSpec(shape, lambda *_: (0,) * nd)


def _layer(shape, l):
    nd = len(shape)
    return pl.BlockSpec((None,) + tuple(shape), lambda *_: (l,) + (0,) * nd)


def _dot(a, b):
    return jnp.dot(a, b, preferred_element_type=F32)


def _dot_nt(a, b):
    return lax.dot_general(a, b, (((1,), (1,)), ((), ())), preferred_element_type=F32)


def _dot_tn(a, b):
    return lax.dot_general(a, b, (((0,), (0,)), ((), ())), preferred_element_type=F32)


def _rms(h):
    r = lax.rsqrt(jnp.mean(h * h, axis=-1, keepdims=True) + EPS)
    return h * r, r


def _rms_bwd(du, xh, r, g):
    dxh = du * g
    return r * (dxh - xh * jnp.mean(dxh * xh, axis=-1, keepdims=True))


def _sigmoid(x):
    return 1.0 / (1.0 + jnp.exp(-x))


def _colsum(x):
    return jnp.sum(x, axis=0, keepdims=True)


def _mesh_pos():
    return lax.axis_index("x"), lax.axis_index("y"), lax.axis_index("c")


def _slot(d):
    return 4 * d[0] + 2 * d[1] + d[2]


class _Comm:
    def __init__(self):
        self.srcs, self.send, self.dst, self.land, self.out_shapes = [], [], [], [], []

    def add_out(self, shape, dtype):
        self.out_shapes.append(jax.ShapeDtypeStruct(shape, dtype))
        return len(self.out_shapes) - 1

    def add(self, src, send, dst, land):
        self.srcs.append(src)
        self.send.append(send)
        self.dst.append(dst)
        self.land.append(land)

    def gather(self, block, out_shape, place):
        self.add(block, lambda ref, s: ref, self.add_out(out_shape, block.dtype), place)

    def sem_shapes(self):
        n = len(self.srcs)
        return [pltpu.SemaphoreType.DMA((7 * n,)), pltpu.SemaphoreType.DMA((7 * n,)), pltpu.SemaphoreType.DMA((n,))]

    def _copies(self, cins, couts, send_sems, recv_sems, local_sems, with_arrivals=True):
        x, y, c = _mesh_pos()
        me = (x, y, c)
        sends, arrivals, local = [], [], []
        for a in range(len(self.srcs)):
            out = couts[self.dst[a]]
            mine = self.land[a](out, _slot(me))
            local.append(pltpu.make_async_copy(self.send[a](cins[a], _slot(me)), mine, local_sems.at[a]))
            for k in range(N_DEV - 1):
                d = k + 1
                peer = (1 - x if (d >> 2) & 1 else x, 1 - y if (d >> 1) & 1 else y, 1 - c if d & 1 else c)
                sems = dict(send_sem=send_sems.at[a * 7 + k], recv_sem=recv_sems.at[a * 7 + k],
                            device_id=peer, device_id_type=MESH)
                sends.append(pltpu.make_async_remote_copy(
                    src_ref=self.send[a](cins[a], _slot(peer)), dst_ref=mine, **sems))
                if with_arrivals:
                    theirs = self.land[a](out, _slot(peer))
                    arrivals.append(pltpu.make_async_remote_copy(src_ref=theirs, dst_ref=theirs, **sems))
        return sends, arrivals, local

    def start(self, *refs):
        sends, _, local = self._copies(*refs, with_arrivals=False)
        for cp in local + sends:
            cp.start()

    def wait(self, *refs):
        sends, arrivals, local = self._copies(*refs)
        for cp in arrivals:
            cp.wait_recv()
        for cp in sends:
            cp.wait_send()
        for cp in local:
            cp.wait()


def _call(body, *, name, grid, in_specs, out_specs, out_shape, operands, scratch_shapes=(), sem=None, comm=None):
    out_specs, out_shape = list(out_specs), list(out_shape)
    if comm is None:
        res = pl.pallas_call(body, name=name, grid=grid, in_specs=list(in_specs), out_specs=out_specs,
                             out_shape=out_shape, scratch_shapes=list(scratch_shapes),
                             compiler_params=_cparams(*sem))(*operands)
        return list(res), []
    n_in, n_out, n_scr = len(in_specs), len(out_specs), len(scratch_shapes)
    nc_in, nc_out = len(comm.srcs), len(comm.out_shapes)

    def hosted(*refs):
        ins, cins = refs[:n_in], refs[n_in:n_in + nc_in]
        o0 = n_in + nc_in
        outs, couts = refs[o0:o0 + n_out], refs[o0 + n_out:o0 + n_out + nc_out]
        s0 = o0 + n_out + nc_out
        scr, sems = refs[s0:s0 + n_scr], refs[s0 + n_scr:]
        ids = [pl.program_id(a) for a in range(len(grid))]
        first = functools.reduce(operator.and_, [i == 0 for i in ids])
        last = functools.reduce(operator.and_, [i == g - 1 for i, g in zip(ids, grid)])

        @pl.when(first)
        def _():
            comm.start(cins, couts, *sems)

        body(*ins, *outs, *scr)

        @pl.when(last)
        def _():
            comm.wait(cins, couts, *sems)

    res = pl.pallas_call(
        hosted, name=name, grid=grid, in_specs=list(in_specs) + [ANY] * nc_in,
        out_specs=out_specs + [ANY] * nc_out, out_shape=out_shape + comm.out_shapes,
        scratch_shapes=list(scratch_shapes) + comm.sem_shapes(),
        compiler_params=_cparams(*(["arbitrary"] * len(grid))))(*operands, *comm.srcs)
    return list(res[:n_out]), list(res[n_out:])


def _comm_only(comm, name):
    def body(*refs):
        n, m = len(comm.srcs), len(comm.out_shapes)
        comm.start(refs[:n], refs[n:n + m], *refs[n + m:])
        comm.wait(refs[:n], refs[n:n + m], *refs[n + m:])

    return list(pl.pallas_call(body, name=name, in_specs=[ANY] * len(comm.srcs), out_specs=[ANY] * len(comm.out_shapes),
                               out_shape=comm.out_shapes, scratch_shapes=comm.sem_shapes())(*comm.srcs))


def _allgather(items, name, extra_outs=(), fills=()):
    n, nf = len(items), len(fills)

    def body(*refs):
        ins, fill_ins, outs = refs[:n], refs[n:n + nf], refs[n + nf:2 * n + nf + len(extra_outs)]
        send_sems, recv_sems, local_sems, fill_sems = refs[2 * n + nf + len(extra_outs):]
        filling = [pltpu.make_async_copy(fill_ins[f], fills[f][2](outs[fills[f][1]]), fill_sems.at[f]) for f in range(nf)]
        for cp in filling:
            cp.start()
        x, y, c = _mesh_pos()
        me, sib = (x, y, c), (x, y, 1 - c)
        chips = [(1 - x, y), (x, 1 - y), (1 - x, 1 - y)]

        def copy(a, k, blk, to, from_input=False):
            region = items[a][2](outs[a], _slot(blk))
            return pltpu.make_async_remote_copy(
                src_ref=ins[a] if from_input else region, dst_ref=region,
                send_sem=send_sems.at[a * 7 + k], recv_sem=recv_sems.at[a * 7 + k],
                device_id=to, device_id_type=MESH)

        local = [pltpu.make_async_copy(ins[a], items[a][2](outs[a], _slot(me)), local_sems.at[a]) for a in range(n)]
        for cp in local:
            cp.start()
        first = []
        for a in range(n):
            first.append(copy(a, 0, me, sib, True))
            for j, chip in enumerate(chips):
                first.append(copy(a, 1 + j, me, (*chip, c), True))
        for cp in first:
            cp.start()
        passed = []
        for j, chip in enumerate(chips):
            for a in range(n):
                copy(a, 1 + j, (*chip, c), me).wait_recv()
                fwd = copy(a, 4 + j, (*chip, c), sib)
                fwd.start()
                passed.append(fwd)
        for a in range(n):
            copy(a, 0, sib, me).wait_recv()
            for j, chip in enumerate(chips):
                copy(a, 4 + j, (*chip, 1 - c), me).wait_recv()
        for cp in first + passed:
            cp.wait_send()
        for cp in local + filling:
            cp.wait()

    out_shape = [jax.ShapeDtypeStruct(it[1], it[0].dtype) for it in items] + list(extra_outs)
    return pl.pallas_call(
        body, name=name,
        in_specs=[ANY] * (n + nf), out_specs=[ANY] * len(out_shape), out_shape=out_shape,
        scratch_shapes=[pltpu.SemaphoreType.DMA((7 * n,)), pltpu.SemaphoreType.DMA((7 * n,)),
                        pltpu.SemaphoreType.DMA((n,)), pltpu.SemaphoreType.DMA((max(nf, 1),))],
    )(*[it[0] for it in items], *[f[0] for f in fills])


def _rms_mm(h, g, w, l, ck, name, comm=None, w_is_nk=False, out_dtype=F32):
    T, D = h.shape
    N = w.shape[0] if w_is_nk else w.shape[1]
    normed = h.dtype == BF16

    def body(h_ref, g_ref, w_ref, o_ref):
        if normed:
            u = h_ref[...]
        else:
            u = (_rms(h_ref[...])[0] * g_ref[l:l + 1, :]).astype(BF16)
        res = _dot_nt(u, w_ref[...]) if w_is_nk else _dot(u, w_ref[...])
        o_ref[...] = res.astype(out_dtype)

    w_spec = pl.BlockSpec((ck, D), lambda j, i: (j, 0)) if w_is_nk else pl.BlockSpec((D, ck), lambda j, i: (0, j))
    td = TD
    return _call(
        body, name=name, grid=(N // ck, T // td),
        in_specs=[pl.BlockSpec((td, D), lambda j, i: (i, 0)), _full(g.shape), w_spec],
        out_specs=[pl.BlockSpec((td, ck), lambda j, i: (i, j))],
        out_shape=[jax.ShapeDtypeStruct((T, N), out_dtype)],
        operands=(h, g, w), sem=("parallel", "parallel"), comm=comm)


def _mm_res(res, a, w, ck, relu2, name, comm=None, norm=None):
    T, D = res.shape
    K = a.shape[-1]
    nk = K // ck

    def body(*refs):
        if norm is None:
            r_ref, a_ref, w_ref, o_ref = refs
        else:
            r_ref, a_ref, w_ref, g_ref, o_ref, u_ref = refs
        k = pl.program_id(1)

        @pl.when(k == 0)
        def _():
            o_ref[...] = r_ref[...]

        av = a_ref[...]
        if relu2:
            av = jnp.square(jnp.maximum(av.astype(F32), 0.0))
        o_ref[...] += _dot(av.astype(BF16), w_ref[...])

        if norm is not None:
            @pl.when(k == nk - 1)
            def _():
                u_ref[...] = (_rms(o_ref[...])[0] * g_ref[norm[1]:norm[1] + 1, :]).astype(BF16)

    row = pl.BlockSpec((TD, D), lambda i, k: (i, 0))
    in_specs = [row, pl.BlockSpec((TD, ck), lambda i, k: (i, k)), pl.BlockSpec((ck, D), lambda i, k: (k, 0))]
    operands = (res, a, w)
    out_specs, out_shape = [row], [jax.ShapeDtypeStruct((T, D), F32)]
    if norm is not None:
        in_specs.append(_full(norm[0].shape))
        operands += (norm[0],)
        out_specs.append(row)
        out_shape.append(jax.ShapeDtypeStruct((T, D), BF16))
    return _call(
        body, name=name, grid=(T // TD, nk), in_specs=in_specs, out_specs=out_specs, out_shape=out_shape,
        operands=operands, sem=("parallel", "arbitrary"), comm=comm)


def _final_loss(h, gf, tgt):
    T, D = h.shape

    def body(h_ref, g_ref, t_ref, dh_ref, dhb_ref, dg_ref, loss_ref):
        i = pl.program_id(0)

        @pl.when(i == 0)
        def _():
            dg_ref[...] = jnp.zeros_like(dg_ref)
            loss_ref[...] = jnp.zeros_like(loss_ref)

        g = g_ref[...]
        xh, r = _rms(h_ref[...])
        row = i * TD + lax.broadcasted_iota(jnp.int32, (TD, 1), 0)
        keep = (row >= N_META).astype(F32)
        diff = (xh * g - t_ref[...]) * keep
        part = jnp.sum(jnp.sum(diff * diff, axis=1, keepdims=True), axis=0, keepdims=True)
        loss_ref[...] += (0.5 / D) * part
        dy = diff * (1.0 / D)
        dg_ref[...] += _colsum(dy * xh)
        dh = _rms_bwd(dy, xh, r, g)
        dh_ref[...] = dh
        dhb_ref[...] = dh.astype(BF16)

    row = pl.BlockSpec((TD, D), lambda i: (i, 0))
    return pl.pallas_call(
        body, name="final_loss", grid=(T // TD,),
        in_specs=[row, _full(gf.shape), row],
        out_specs=[row, row, _full((1, D)), _full((8, 128))],
        out_shape=[jax.ShapeDtypeStruct((T, D), F32), jax.ShapeDtypeStruct((T, D), BF16),
                   jax.ShapeDtypeStruct((1, D), F32), jax.ShapeDtypeStruct((8, 128), F32)],
        compiler_params=_cparams("arbitrary"))(h, gf, tgt)


def _bwd_dz(dh, wdown, z, ck, name, comm=None):
    T, D = dh.shape
    F = z.shape[-1]

    def body(d_ref, w_ref, z_ref, o_ref):
        dact = _dot_nt(d_ref[...], w_ref[...])
        o_ref[...] = (dact * (2.0 * jnp.maximum(z_ref[...].astype(F32), 0.0))).astype(BF16)

    return _call(
        body, name=name, grid=(F // ck, T // TD),
        in_specs=[pl.BlockSpec((TD, D), lambda j, i: (i, 0)), pl.BlockSpec((ck, D), lambda j, i: (j, 0)),
                  pl.BlockSpec((TD, ck), lambda j, i: (i, j))],
        out_specs=[pl.BlockSpec((TD, ck), lambda j, i: (i, j))],
        out_shape=[jax.ShapeDtypeStruct((T, F), BF16)],
        operands=(dh, wdown, z), sem=("parallel", "parallel"), comm=comm)


def _bwd_plain(dh, w, name, comm=None):
    T, D = dh.shape
    K = w.shape[0]

    def body(d_ref, w_ref, o_ref):
        o_ref[...] = _dot_nt(d_ref[...], w_ref[...])

    return _call(
        body, name=name, grid=(T // TD,),
        in_specs=[pl.BlockSpec((TD, D), lambda i: (i, 0)), _full((K, D))],
        out_specs=[pl.BlockSpec((TD, K), lambda i: (i, 0))],
        out_shape=[jax.ShapeDtypeStruct((T, K), F32)],
        operands=(dh, w), sem=("parallel",), comm=comm)


def _bwd_rms(dx, w, h, g, dh_in, l, ck, name, comm=None, w_is_kd=False, emit_bf16=True):
    T, D = h.shape
    K = dx.shape[-1]
    nk = K // ck
    w_spec = pl.BlockSpec((ck, D), lambda i, k: (k, 0)) if w_is_kd else pl.BlockSpec((D, ck), lambda i, k: (0, k))

    def body(x_ref, w_ref, h_ref, g_ref, di_ref, *rest):
        dh_ref, dg_ref = rest[0], rest[2 if emit_bf16 else 1]
        i, k = pl.program_id(0), pl.program_id(1)

        @pl.when((i == 0) & (k == 0))
        def _():
            dg_ref[...] = jnp.zeros_like(dg_ref)

        part = _dot(x_ref[...], w_ref[...]) if w_is_kd else _dot_nt(x_ref[...], w_ref[...])

        def finish(du):
            xh, r = _rms(h_ref[...])
            dg_ref[...] += _colsum(du * xh)
            dh = di_ref[...] + _rms_bwd(du, xh, r, g_ref[l:l + 1, :])
            dh_ref[...] = dh
            if emit_bf16:
                rest[1][...] = dh.astype(BF16)

        if nk == 1:
            finish(part)
        else:
            acc_ref = rest[-1]

            @pl.when(k == 0)
            def _():
                acc_ref[...] = part

            @pl.when(k > 0)
            def _():
                acc_ref[...] += part

            @pl.when(k == nk - 1)
            def _():
                finish(acc_ref[...])

    row = pl.BlockSpec((TD, D), lambda i, k: (i, 0))
    out_specs = [row] + ([row] if emit_bf16 else []) + [_full((1, D))]
    out_shape = ([jax.ShapeDtypeStruct((T, D), F32)] + ([jax.ShapeDtypeStruct((T, D), BF16)] if emit_bf16 else [])
                 + [jax.ShapeDtypeStruct((1, D), F32)])
    return _call(
        body, name=name, grid=(T // TD, nk),
        in_specs=[pl.BlockSpec((TD, ck), lambda i, k: (i, k)), w_spec, row, _full(g.shape), row],
        out_specs=out_specs, out_shape=out_shape,
        scratch_shapes=[pltpu.VMEM((TD, D), F32)] if nk > 1 else [],
        operands=(dx, w, h, g, dh_in), sem=("arbitrary", "arbitrary"), comm=comm)


def _mm_tn(a, b, relu2, rms_on, g, l, chunk, ck, out_dtype, name, comm=None):
    T = a.shape[0]
    Ka, Nb = a.shape[1], b.shape[1]
    ca, cb = (ck, Nb) if chunk == "a" else (Ka, ck)
    nj = (Ka // ck) if chunk == "a" else (Nb // ck)
    td = TD_BIG if a.dtype == BF16 and b.dtype == BF16 else TD
    nt = T // td

    def body(*refs):
        if rms_on is not None:
            a_ref, g_ref, b_ref, o_ref, acc_ref = refs
        else:
            a_ref, b_ref, o_ref, acc_ref = refs
        t = pl.program_id(1)

        @pl.when(t == 0)
        def _():
            acc_ref[...] = jnp.zeros_like(acc_ref)

        av, bv = a_ref[...], b_ref[...]
        if relu2:
            av = jnp.square(jnp.maximum(av.astype(F32), 0.0))
        if rms_on == "a":
            av = _rms(av)[0] * g_ref[l:l + 1, :]
        elif rms_on == "b":
            bv = _rms(bv)[0] * g_ref[l:l + 1, :]
        acc_ref[...] += _dot_tn(av.astype(BF16), bv.astype(BF16))

        @pl.when(t == nt - 1)
        def _():
            o_ref[...] = acc_ref[...].astype(out_dtype)

    if chunk == "a":
        a_spec = pl.BlockSpec((td, ca), lambda j, t: (t, j))
        b_spec = pl.BlockSpec((td, cb), lambda j, t: (t, 0))
        o_spec = pl.BlockSpec((ca, cb), lambda j, t: (j, 0))
    else:
        a_spec = pl.BlockSpec((td, ca), lambda j, t: (t, 0))
        b_spec = pl.BlockSpec((td, cb), lambda j, t: (t, j))
        o_spec = pl.BlockSpec((ca, cb), lambda j, t: (0, j))
    ins, specs = [a], [a_spec]
    if rms_on is not None:
        ins.append(g)
        specs.append(_full(g.shape))
    ins.append(b)
    specs.append(b_spec)
    return _call(
        body, name=name, grid=(nj, nt), in_specs=specs, out_specs=[o_spec],
        out_shape=[jax.ShapeDtypeStruct((Ka, Nb), out_dtype)], scratch_shapes=[pltpu.VMEM((ca, cb), F32)],
        operands=tuple(ins), sem=("parallel", "arbitrary"), comm=comm)


def _shift_down(x, k):
    return x if k == 0 else pltpu.roll(x, k, 0)


def _shift_up(x, k):
    return x if k == 0 else pltpu.roll(x, x.shape[0] - k, 0)


def _pool_sel(lane_grp, vals):
    return jnp.where(lane_grp == 0, vals[0], jnp.where(lane_grp == 1, vals[1], jnp.where(lane_grp == 2, vals[2], vals[3])))


def _pool_fwd(ue, pos):
    s = ue
    sums = []
    for k in (1, 2, 4, 8):
        s = s + _shift_down(s, k)
        sums.append(s[HALO:])
    grp = lax.broadcasted_iota(jnp.int32, (1, D_POOL), 1) // POOL_GW
    wsum = _pool_sel(grp, sums)
    width = _pool_sel(grp, [jnp.float32(w) for w in POOL_WINDOWS])
    cnt = jnp.minimum(pos.astype(F32), width)
    return wsum / cnt - ue[HALO:], cnt, grp


def _conv_fwd(xe, w_ref, nk):
    acc = None
    for k in range(nk):
        term = _shift_down(xe, nk - 1 - k)[HALO:] * w_ref(k)
        acc = term if acc is None else acc + term
    return acc


def _taps(window, w_ref, nk, shift):
    acc = None
    for r in range(min(8, nk)):
        inner = None
        for q in range((nk - 1 - r) // 8 + 1):
            term = window(q) * w_ref(nk - 1 - 8 * q - r)
            inner = term if inner is None else inner + term
        inner = shift(inner, r)
        acc = inner if acc is None else acc + inner
    return acc


def _layernorm(cb):
    mu = jnp.mean(cb, axis=-1, keepdims=True)
    xc = cb - mu
    rstd = lax.rsqrt(jnp.mean(xc * xc, axis=-1, keepdims=True) + EPS)
    return xc * rstd, rstd


def _softplus_neg(lam):
    x = -lam
    e = jnp.exp(-jnp.abs(x))
    u = 1.0 + e
    d = u - 1.0
    log1p = jnp.where(d == 0.0, e, jnp.log(u) * (e / jnp.where(d == 0.0, 1.0, d)))
    return jnp.maximum(x, 0.0) + log1p


def _expm1(x):
    small = x * (1.0 + x * 0.5 * (1.0 + x * (1.0 / 3.0) * (1.0 + x * 0.25)))
    return jnp.where(jnp.abs(x) < 1e-2, small, jnp.exp(x) - 1.0)


def _gelu_parts(x):
    c0 = 0.7978845608028654
    inner = c0 * (x + 0.044715 * x * x * x)
    th = jnp.tanh(inner)
    gelu = 0.5 * x * (1.0 + th)
    dgelu = 0.5 * (1.0 + th) + 0.5 * x * (1.0 - th * th) * c0 * (1.0 + 3.0 * 0.044715 * x * x)
    return gelu, dgelu


def _rg_gates(xc, wa_ref, wx_ref, v512_ref):
    xb = xc.astype(BF16)
    ra = _sigmoid(_dot(xb, wa_ref[...]) + v512_ref[1:2, :])
    ri = _sigmoid(_dot(xb, wx_ref[...]) + v512_ref[2:3, :])
    sp = _softplus_neg(v512_ref[3:4, :])
    log_a = (-RG_C) * ra * sp
    a = jnp.exp(log_a)
    mult = jnp.sqrt(-_expm1(2.0 * log_a))
    return ra, ri, sp, a, mult


def _scan_down(a, b):
    n = a.shape[0]
    row = lax.broadcasted_iota(jnp.int32, (n, 1), 0)
    k = 1
    while k < n:
        ok = row >= k
        b = a * jnp.where(ok, _shift_down(b, k), 0.0) + b
        a = a * jnp.where(ok, _shift_down(a, k), 1.0)
        k *= 2
    return b, a


def _scan_up(c, g):
    n = c.shape[0]
    row = lax.broadcasted_iota(jnp.int32, (n, 1), 0)
    k = 1
    while k < n:
        ok = row < n - k
        g = c * jnp.where(ok, _shift_up(g, k), 0.0) + g
        c = c * jnp.where(ok, _shift_up(c, k), 1.0)
        k *= 2
    return g, c


def _mixer_specs(l, tile_of):
    r = TT // HALO
    return [
        pl.BlockSpec((TT, 1792), lambda i: (tile_of(i), 0)),
        pl.BlockSpec((HALO, 1792), lambda i: (jnp.maximum(tile_of(i) * r - 1, 0), 0)),
        _layer((256, 256), l), _layer((8, 256), l), _layer((32, 256), l), _layer((256, 256), l),
        _layer((512, 512), l), _layer((512, 512), l), _layer((8, 512), l),
    ]


def _mixer_fwd(p, wts, l, comm=None):
    T = p.shape[0]

    def body(p_ref, ph_ref, wpool_ref, v256_ref, dww_ref, wpw_ref, wa_ref, wx_ref, v512_ref, y_ref, hr_ref, cb_ref,
             rg_ref, carry_ref):
        i = pl.program_id(0)

        @pl.when(i == 0)
        def _():
            carry_ref[...] = jnp.zeros_like(carry_ref)

        halo = jnp.where(i == 0, 0.0, ph_ref[...])
        e = jnp.concatenate([halo, p_ref[...]], axis=0)
        pos = i * TT + 1 + lax.broadcasted_iota(jnp.int32, (TT, 1), 0)
        pooled, _, _ = _pool_fwd(e[:, 0:256], pos)
        y_a = _dot(pooled.astype(BF16), wpool_ref[...]) * v256_ref[0:1, :]
        ub = e[:, 256:512] * _sigmoid(e[:, 512:768])
        cb = _conv_fwd(ub, lambda k: dww_ref[k:k + 1, :], CONV_K) + v256_ref[1:2, :]
        cb_ref[...] = cb
        xhat, _ = _layernorm(cb)
        ln = xhat * v256_ref[2:3, :] + v256_ref[3:4, :]
        s = ln * _sigmoid(ln)
        y_b = _dot(s.astype(BF16), wpw_ref[...])
        xc = _conv_fwd(e[:, 1280:1792], lambda k: v512_ref[4 + k:5 + k, :], RG_CONV_K) + v512_ref[0:1, :]
        ra, ri, _, a, mult = _rg_gates(xc, wa_ref, wx_ref, v512_ref)
        for n, val in enumerate((xc, ra, ri, a, mult)):
            rg_ref[:, n * D_RNN:(n + 1) * D_RNN] = val
        hloc, acum = _scan_down(a, mult * ri * xc)
        hr = hloc + acum * carry_ref[0:1, :]
        hr_ref[...] = hr
        carry_ref[0:1, :] = hr_ref[TT - 1:TT, :]
        gelu, _ = _gelu_parts(p_ref[:, 768:1280])
        y_ref[:, 0:256] = y_a.astype(BF16)
        y_ref[:, 256:512] = y_b.astype(BF16)
        y_ref[:, 512:1024] = (gelu * hr).astype(BF16)

    return _call(
        body, name=f"mixer_fwd{l}", grid=(T // TT,),
        in_specs=_mixer_specs(l, lambda i: i),
        out_specs=[pl.BlockSpec((TT, 1024), lambda i: (i, 0)), pl.BlockSpec((TT, 512), lambda i: (i, 0)),
                   pl.BlockSpec((TT, 256), lambda i: (i, 0)), pl.BlockSpec((TT, 5 * D_RNN), lambda i: (i, 0))],
        out_shape=[jax.ShapeDtypeStruct((T, 1024), BF16), jax.ShapeDtypeStruct((T, 512), F32),
                   jax.ShapeDtypeStruct((T, 256), F32), jax.ShapeDtypeStruct((T, 5 * D_RNN), F32)],
        scratch_shapes=[pltpu.VMEM((8, 512), F32)],
        operands=(p, p, *wts), sem=("arbitrary",), comm=comm)


def _mixer_bwd(p, hr, cb, rg, dy, wts, l, comm=None):
    T = p.shape[0]
    nt = T // TT
    tile_of = lambda i: nt - 1 - i

    def body(p_ref, ph_ref, wpool_ref, v256_ref, dww_ref, wpw_ref, wa_ref, wx_ref, v512_ref, hr_ref, hrh_ref, cb_ref,
             rg_ref, dy_ref, dp_ref, g256_ref, g512_ref, gpool_ref, gpw_ref, gwa_ref, gwx_ref,
             q_c, dcb_c, dxc_c, ag_c, ub_s, ed_s):
        i = pl.program_id(0)
        j = nt - 1 - i

        @pl.when(i == 0)
        def _():
            for ref in (g256_ref, g512_ref, gpool_ref, gpw_ref, gwa_ref, gwx_ref, q_c, dcb_c, dxc_c, ag_c):
                ref[...] = jnp.zeros_like(ref)

        halo = jnp.where(j == 0, 0.0, ph_ref[...])
        e = jnp.concatenate([halo, p_ref[...]], axis=0)
        pos = j * TT + 1 + lax.broadcasted_iota(jnp.int32, (TT, 1), 0)
        row = lax.broadcasted_iota(jnp.int32, (TT, 1), 0)

        dy_a = dy_ref[:, 0:256]
        pooled, cnt, grp = _pool_fwd(e[:, 0:256], pos)
        scale = v256_ref[0:1, :]
        mixed = _dot(pooled.astype(BF16), wpool_ref[...])
        g256_ref[34:35, :] += _colsum(dy_a * mixed)
        dmixed = (dy_a * scale).astype(BF16)
        gpool_ref[...] += _dot_tn(pooled.astype(BF16), dmixed)
        dpooled = _dot_nt(dmixed, wpool_ref[...])
        q = dpooled / cnt
        s = jnp.concatenate([q, q_c[...]], axis=0)
        sums = []
        for k in (1, 2, 4, 8):
            s = s + _shift_up(s, k)
            sums.append(s[:TT])
        dp_ref[:, 0:256] = (_pool_sel(grp, sums) - dpooled).astype(BF16)
        q_c[...] = q[0:16]

        dy_b = dy_ref[:, 256:512].astype(BF16)
        sg = _sigmoid(e[:, 512:768])
        ub_s[...] = e[:, 256:512] * sg
        xhat, rstd = _layernorm(cb_ref[...])
        ln_g = v256_ref[2:3, :]
        ln = xhat * ln_g + v256_ref[3:4, :]
        sig = _sigmoid(ln)
        sact = ln * sig
        gpw_ref[...] += _dot_tn(sact.astype(BF16), dy_b)
        dln = _dot_nt(dy_b, wpw_ref[...]) * (sig * (1.0 + ln * (1.0 - sig)))
        g256_ref[32:33, :] += _colsum(dln * xhat)
        g256_ref[33:34, :] += _colsum(dln)
        dxhat = dln * ln_g
        dcb = rstd * (dxhat - jnp.mean(dxhat, axis=-1, keepdims=True)
                      - xhat * jnp.mean(dxhat * xhat, axis=-1, keepdims=True))
        g256_ref[31:32, :] += _colsum(dcb)
        pad8 = jnp.zeros((8, D_CONV), F32)
        dcb_ext = jnp.concatenate([pad8, dcb, pad8], axis=0)
        for r in range(8):
            d_r = _shift_up(dcb_ext, r)[0:TT + 8]
            for q in range((CONV_K - 1 - r) // 8 + 1):
                k = CONV_K - 1 - 8 * q - r
                g256_ref[k:k + 1, :] += _colsum(d_r * ub_s[pl.ds(HALO - 8 - 8 * q, TT + 8), :])
        ed_s[0:TT, :] = dcb
        ed_s[TT:TT + 32, :] = dcb_c[...]
        dub = _taps(lambda q: ed_s[pl.ds(8 * q, TT + 8), :], lambda k: dww_ref[k:k + 1, :], CONV_K, _shift_up)[:TT]
        dcb_c[...] = dcb[0:32]
        sg_t = sg[HALO:]
        v_t = p_ref[:, 256:512]
        dp_ref[:, 256:512] = (dub * sg_t).astype(BF16)
        dp_ref[:, 512:768] = (dub * v_t * sg_t * (1.0 - sg_t)).astype(BF16)

        dy_c = dy_ref[:, 512:1024]
        cxe = e[:, 1280:1792]
        xc, ra, ri, a, mult = (rg_ref[:, n * D_RNN:(n + 1) * D_RNN] for n in range(5))
        sp = _softplus_neg(v512_ref[3:4, :])
        hrv = hr_ref[...]
        gelu, dgelu = _gelu_parts(p_ref[:, 768:1280])
        dp_ref[:, 768:1280] = (dy_c * hrv * dgelu).astype(BF16)
        dhr = dy_c * gelu
        coef = jnp.where(row < TT - 1, _shift_up(a, 1), 1.0)
        gloc, ccum = _scan_up(coef, dhr)
        gg = gloc + ccum * ag_c[0:1, :]
        ag_c[...] = (a * gg)[0:8]
        hr_before = jnp.where(j == 0, 0.0, hrh_ref[7:8, :])
        hr_prev = jnp.where(row >= 1, _shift_down(hrv, 1), hr_before)
        da = gg * hr_prev
        dmult = gg * ri * xc
        dri = gg * mult * xc
        dxc = gg * mult * ri
        dlog_a = da * a - dmult * (a * a / mult)
        g512_ref[3:4, :] += _colsum(dlog_a * ((-RG_C) * ra))
        dpre_a = dlog_a * ((-RG_C) * sp) * ra * (1.0 - ra)
        dpre_x = dri * ri * (1.0 - ri)
        g512_ref[1:2, :] += _colsum(dpre_a)
        g512_ref[2:3, :] += _colsum(dpre_x)
        xb = xc.astype(BF16)
        dpa_b, dpx_b = dpre_a.astype(BF16), dpre_x.astype(BF16)
        gwa_ref[...] += _dot_tn(xb, dpa_b)
        gwx_ref[...] += _dot_tn(xb, dpx_b)
        dxc = dxc + _dot_nt(dpa_b, wa_ref[...]) + _dot_nt(dpx_b, wx_ref[...])
        g512_ref[0:1, :] += _colsum(dxc)
        for k in range(RG_CONV_K):
            g512_ref[4 + k:5 + k, :] += _colsum(dxc * _shift_down(cxe, RG_CONV_K - 1 - k)[HALO:])
        ex = jnp.concatenate([dxc, dxc_c[...]], axis=0)
        dcx = _taps(lambda q: ex, lambda k: v512_ref[4 + k:5 + k, :], RG_CONV_K, _shift_up)[:TT]
        dxc_c[...] = dxc[0:8]
        dp_ref[:, 1280:1792] = dcx.astype(BF16)

        @pl.when(i == nt - 1)
        def _():
            lam = v512_ref[3:4, :]
            g512_ref[3:4, :] = g512_ref[3:4, :] * (-_sigmoid(-lam))

    in_specs = _mixer_specs(l, tile_of) + [
        pl.BlockSpec((TT, 512), lambda i: (tile_of(i), 0)),
        pl.BlockSpec((8, 512), lambda i: (jnp.maximum(tile_of(i) * (TT // 8) - 1, 0), 0)),
        pl.BlockSpec((TT, 256), lambda i: (tile_of(i), 0)),
        pl.BlockSpec((TT, 5 * D_RNN), lambda i: (tile_of(i), 0)),
        pl.BlockSpec((TT, 1024), lambda i: (tile_of(i), 0)),
    ]
    acc_shapes = [(40, 256), (8, 512), (256, 256), (256, 256), (512, 512), (512, 512)]
    return _call(
        body, name=f"mixer_bwd{l}", grid=(nt,),
        in_specs=in_specs,
        out_specs=[pl.BlockSpec((TT, 1792), lambda i: (tile_of(i), 0))] + [_full(s) for s in acc_shapes],
        out_shape=[jax.ShapeDtypeStruct((T, 1792), BF16)] + [jax.ShapeDtypeStruct(s, F32) for s in acc_shapes],
        scratch_shapes=[pltpu.VMEM((16, 256), F32), pltpu.VMEM((32, 256), F32), pltpu.VMEM((8, 512), F32),
                        pltpu.VMEM((8, 512), F32), pltpu.VMEM((HALO + TT, D_CONV), F32),
                        pltpu.VMEM((TT + 32, D_CONV), F32)],
        operands=(p, p, *wts, hr, hr, cb, rg, dy), sem=("arbitrary",), comm=comm)


def _adam(w, g, m, v):
    m = ADAM_B1 * m + (1.0 - ADAM_B1) * g
    v = ADAM_B2 * v + (1.0 - ADAM_B2) * (g * g)
    m_hat = m / (1.0 - ADAM_B1 ** ADAM_STEP)
    v_hat = v / (1.0 - ADAM_B2 ** ADAM_STEP)
    delta = -ADAM_LR * (m_hat / (jnp.sqrt(v_hat) + ADAM_EPS) + ADAM_WD * w)
    return delta, m, v


def _adamw_big(recvs, w, m, v, tr, name):
    L, R, C = w.shape

    def body(r0_ref, r1_ref, w_ref, m_ref, v_ref, g_ref, d_ref, mo_ref, vo_ref):
        l = pl.program_id(0)

        def total(r_ref):
            g = r_ref[0].astype(F32)
            for s in range(1, N_DEV):
                g = g + r_ref[s].astype(F32)
            g_ref[...] = g

        @pl.when(l == 0)
        def _():
            total(r0_ref)

        @pl.when(l == 1)
        def _():
            total(r1_ref)

        d_ref[...], mo_ref[...], vo_ref[...] = _adam(w_ref[...], g_ref[...], m_ref[...], v_ref[...])

    blk = pl.BlockSpec((None, tr, C), lambda l, i: (l, i, 0))
    return pl.pallas_call(
        body, name=name, grid=(L, R // tr),
        in_specs=[pl.BlockSpec((N_DEV, tr, C), lambda l, i: (0, i * (1 - l), 0)),
                  pl.BlockSpec((N_DEV, tr, C), lambda l, i: (0, i * l, 0)), blk, blk, blk],
        out_specs=[blk] * 4,
        out_shape=[jax.ShapeDtypeStruct((L, R, C), F32)] * 4,
        compiler_params=_cparams("arbitrary", "arbitrary"))(recvs[0], recvs[1], w, m, v)


def _sum_slots(recvs, name):
    L = len(recvs)
    _, R, C = recvs[0].shape

    def body(*refs):
        o_ref = refs[L]
        for l in range(L):
            acc = refs[l][0].astype(F32)
            for s in range(1, N_DEV):
                acc = acc + refs[l][s].astype(F32)
            o_ref[l] = acc

    return pl.pallas_call(body, name=name, out_shape=jax.ShapeDtypeStruct((L, R, C), F32))(*recvs)


def _adamw_plain(gs, ws, ms, vs, name):
    n = len(gs)

    def body(*refs):
        g_r, w_r, m_r, v_r = refs[:n], refs[n:2 * n], refs[2 * n:3 * n], refs[3 * n:4 * n]
        outs = refs[4 * n:]
        for a in range(n):
            d, mo, vo = _adam(w_r[a][...], g_r[a][...], m_r[a][...], v_r[a][...])
            outs[3 * a][...] = d
            outs[3 * a + 1][...] = mo
            outs[3 * a + 2][...] = vo

    shapes = []
    for a in range(n):
        shapes += [jax.ShapeDtypeStruct(ws[a].shape, F32)] * 3
    return pl.pallas_call(body, name=name, out_shape=shapes,
                          compiler_params=pltpu.CompilerParams(vmem_limit_bytes=VMEM_LIMIT))(*gs, *ws, *ms, *vs)


_V256_ROWS = {"convb_dw_b": 31, "convb_ln_g": 32, "convb_ln_b": 33, "pool_scale": 34}
_V512_ROWS = {"rg_conv_b": 0, "rg_b_a": 1, "rg_b_x": 2, "rg_lambda": 3}
_SMALL_REPL = ["mix_norm_g", "mlp_norm_g", "final_norm_g", "pool_w", "pool_scale", "convb_dw_b", "convb_ln_g",
               "convb_ln_b", "rg_conv_b", "rg_w_a", "rg_b_a", "rg_w_x", "rg_b_x", "rg_lambda"]


def _adamw_small(gath1, gath0, gath_tail, ws, ms, vs):
    n = len(_SMALL_REPL)
    n1, n0, nt = len(gath1), len(gath0), len(gath_tail)
    ng = n1 + n0 + nt
    L = DEPTH
    per_layer = [g.shape[1:] for g in gath1[:8]]
    sum_shapes = ([(L,) + s[1:] for s in per_layer[:2]] + [gath1[8].shape[1:]] + [(L,) + s for s in per_layer[2:]]
                  + [gath_tail[1].shape[1:]])

    def body(*refs):
        in1, in0, in_t = refs[:n1], refs[n1:n1 + n0], refs[n1 + n0:ng]
        w_r, m_r, v_r = refs[ng:ng + n], refs[ng + n:ng + 2 * n], refs[ng + 2 * n:ng + 3 * n]
        sums = refs[ng + 3 * n:ng + 3 * n + 10]
        outs = refs[ng + 3 * n + 10:]

        def total(ref):
            acc = ref[0]
            for s in range(1, N_DEV):
                acc = acc + ref[s]
            return acc

        s_mix, s_mlp, s_fin, s_pool, s_256, s_512, s_pw, s_wa, s_wx, s_meta = sums
        s_mix[1:2, :] = total(in1[0])
        s_mix[0:1, :] = total(in_t[0])
        s_mlp[1:2, :] = total(in1[1])
        s_mlp[0:1, :] = total(in0[0])
        s_fin[...] = total(in1[8])
        s_meta[...] = total(in_t[1])
        for k, dst in enumerate((s_pool, s_256, s_512, s_pw, s_wa, s_wx)):
            dst[1] = total(in1[2 + k])
            dst[0] = total(in0[1 + k])

        def grad_of(name, idx):
            if name == "mix_norm_g":
                return s_mix[idx[0]:idx[0] + 1, :]
            if name == "mlp_norm_g":
                return s_mlp[idx[0]:idx[0] + 1, :]
            if name == "final_norm_g":
                return s_fin[...]
            if name in _V256_ROWS:
                r = _V256_ROWS[name]
                return s_256[idx[0], r:r + 1, :]
            if name in _V512_ROWS:
                r = _V512_ROWS[name]
                return s_512[idx[0], r:r + 1, :]
            src = {"pool_w": s_pool, "rg_w_a": s_wa, "rg_w_x": s_wx}[name]
            return src[idx[0], :, idx[1] * 64:(idx[1] + 1) * 64]

        for a, name in enumerate(_SMALL_REPL):
            shape = w_r[a].shape
            if name == "final_norm_g":
                parts = [((), (slice(None), slice(None)))]
            elif len(shape) == 2:
                parts = [((l,), (slice(l, l + 1), slice(None))) for l in range(shape[0])]
            else:
                parts = [((l, h), (l, h)) for l in range(shape[0]) for h in range(shape[1])]
            for idx, sel in parts:
                g = grad_of(name, idx)
                d, mo, vo = _adam(w_r[a][sel], g, m_r[a][sel], v_r[a][sel])
                outs[4 * a][sel] = g
                outs[4 * a + 1][sel] = d
                outs[4 * a + 2][sel] = mo
                outs[4 * a + 3][sel] = vo

    out_shape = [jax.ShapeDtypeStruct(s, F32) for s in sum_shapes]
    for a in range(n):
        out_shape += [jax.ShapeDtypeStruct(ws[a].shape, F32)] * 4
    res = pl.pallas_call(body, name="adamw_small", out_shape=out_shape,
                         compiler_params=pltpu.CompilerParams(vmem_limit_bytes=VMEM_LIMIT))(
                             *gath1, *gath0, *gath_tail, *ws, *ms, *vs)
    return res[:10], res[10:]


_WEIGHTS = ['meta_tokens', 'mix_norm_g', 'w_in', 'pool_w', 'pool_scale', 'convb_dw_w', 'convb_dw_b', 'convb_ln_g',
            'convb_ln_b', 'convb_pw_w', 'rg_conv_w', 'rg_conv_b', 'rg_w_a', 'rg_b_a', 'rg_w_x', 'rg_b_x', 'rg_lambda',
            'w_out', 'mlp_norm_g', 'w_up', 'w_down', 'final_norm_g']


def _block_diag(w):
    L, H, C, _ = w.shape
    eye = jnp.eye(H, dtype=w.dtype)
    return (w[:, :, :, None, :] * eye[None, :, None, :, None]).reshape(L, H * C, H * C)


def _diag_blocks(m, H):
    C = m.shape[0] // H
    return jnp.concatenate([m[h * C:(h + 1) * C, h * C:(h + 1) * C] for h in range(H)], axis=1)


def kernel(x, meta_tokens, mix_norm_g, w_in, pool_w, pool_scale, convb_dw_w, convb_dw_b, convb_ln_g, convb_ln_b, convb_pw_w, rg_conv_w, rg_conv_b, rg_w_a, rg_b_a, rg_w_x, rg_b_x, rg_lambda, w_out, mlp_norm_g, w_up, w_down, final_norm_g, loss_target, m_meta_tokens, m_mix_norm_g, m_w_in, m_pool_w, m_pool_scale, m_convb_dw_w, m_convb_dw_b, m_convb_ln_g, m_convb_ln_b, m_convb_pw_w, m_rg_conv_w, m_rg_conv_b, m_rg_w_a, m_rg_b_a, m_rg_w_x, m_rg_b_x, m_rg_lambda, m_w_out, m_mlp_norm_g, m_w_up, m_w_down, m_final_norm_g, v_meta_tokens, v_mix_norm_g, v_w_in, v_pool_w, v_pool_scale, v_convb_dw_w, v_convb_dw_b, v_convb_ln_g, v_convb_ln_b, v_convb_pw_w, v_rg_conv_w, v_rg_conv_b, v_rg_w_a, v_rg_b_a, v_rg_w_x, v_rg_b_x, v_rg_lambda, v_w_out, v_mlp_norm_g, v_w_up, v_w_down, v_final_norm_g):
    W = dict(meta_tokens=meta_tokens, mix_norm_g=mix_norm_g, w_in=w_in, pool_w=pool_w, pool_scale=pool_scale,
             convb_dw_w=convb_dw_w, convb_dw_b=convb_dw_b, convb_ln_g=convb_ln_g, convb_ln_b=convb_ln_b,
             convb_pw_w=convb_pw_w, rg_conv_w=rg_conv_w, rg_conv_b=rg_conv_b, rg_w_a=rg_w_a, rg_b_a=rg_b_a,
             rg_w_x=rg_w_x, rg_b_x=rg_b_x, rg_lambda=rg_lambda, w_out=w_out, mlp_norm_g=mlp_norm_g, w_up=w_up,
             w_down=w_down, final_norm_g=final_norm_g.reshape(1, -1))
    M = dict(meta_tokens=m_meta_tokens, mix_norm_g=m_mix_norm_g, w_in=m_w_in, pool_w=m_pool_w, pool_scale=m_pool_scale,
             convb_dw_w=m_convb_dw_w, convb_dw_b=m_convb_dw_b, convb_ln_g=m_convb_ln_g, convb_ln_b=m_convb_ln_b,
             convb_pw_w=m_convb_pw_w, rg_conv_w=m_rg_conv_w, rg_conv_b=m_rg_conv_b, rg_w_a=m_rg_w_a, rg_b_a=m_rg_b_a,
             rg_w_x=m_rg_w_x, rg_b_x=m_rg_b_x, rg_lambda=m_rg_lambda, w_out=m_w_out, mlp_norm_g=m_mlp_norm_g,
             w_up=m_w_up, w_down=m_w_down, final_norm_g=m_final_norm_g.reshape(1, -1))
    V = dict(meta_tokens=v_meta_tokens, mix_norm_g=v_mix_norm_g, w_in=v_w_in, pool_w=v_pool_w, pool_scale=v_pool_scale,
             convb_dw_w=v_convb_dw_w, convb_dw_b=v_convb_dw_b, convb_ln_g=v_convb_ln_g, convb_ln_b=v_convb_ln_b,
             convb_pw_w=v_convb_pw_w, rg_conv_w=v_rg_conv_w, rg_conv_b=v_rg_conv_b, rg_w_a=v_rg_w_a, rg_b_a=v_rg_b_a,
             rg_w_x=v_rg_w_x, rg_b_x=v_rg_b_x, rg_lambda=v_rg_lambda, w_out=v_w_out, mlp_norm_g=v_mlp_norm_g,
             w_up=v_w_up, w_down=v_w_down, final_norm_g=v_final_norm_g.reshape(1, -1))

    xs = x[0]
    S, D = xs.shape
    T = S + N_META
    assert T % TT == 0 and D == 1024
    L = DEPTH
    me = 4 * lax.axis_index("x") + 2 * lax.axis_index("y") + lax.axis_index("c")
    c_in, c_ff = w_in.shape[-1], w_up.shape[-1]
    d_in, d_ff = c_in * N_DEV, c_ff * N_DEV
    r_out, r_dn = w_out.shape[1], w_down.shape[1]
    by_slot = lambda ref, s: ref.at[s]
    rows_of = lambda n: (lambda ref, s: ref.at[pl.ds(s * n, n), :])
    cols_of = lambda n: (lambda ref, s: ref.at[:, pl.ds(s * n, n)])

    win_t = jnp.transpose(w_in, (0, 2, 1)).astype(BF16)
    c_meta = meta_tokens.shape[1]
    body_rows = lambda ref: ref.at[pl.ds(N_META, S), :]
    gathered = _allgather([
        (win_t[0], (d_in, D), rows_of(c_in)),
        (meta_tokens, (T, D), lambda ref, s: ref.at[pl.ds(0, N_META), pl.ds(s * c_meta, c_meta)]),
        (convb_dw_w, (N_DEV,) + convb_dw_w.shape, by_slot),
        (convb_pw_w, (N_DEV,) + convb_pw_w.shape, by_slot),
        (rg_conv_w, (N_DEV,) + rg_conv_w.shape, by_slot),
    ], "gather_first", extra_outs=[jax.ShapeDtypeStruct((T, D), F32)],
        fills=[(xs, 1, body_rows), (loss_target[0], 5, body_rows),
               (jnp.zeros((N_META, D), F32), 5, lambda ref: ref.at[pl.ds(0, N_META), :])])
    win_f = [gathered[0], None]
    h, tgt = gathered[1], gathered[5]
    dww_f = jnp.transpose(gathered[2], (1, 2, 0, 3)).reshape(L, CONV_K, D_CONV)
    wpw_f = jnp.transpose(gathered[3], (1, 0, 2, 3)).reshape(L, D_CONV, D_CONV).astype(BF16)
    rgw_f = jnp.transpose(gathered[4], (1, 2, 0, 3)).reshape(L, RG_CONV_K, D_RNN)
    wout_b, wup_b, wdown_b = w_out.astype(BF16), w_up.astype(BF16), w_down.astype(BF16)

    def weight_gather(blocks):
        comm = _Comm()
        for blk, shape, place in blocks:
            comm.gather(blk, shape, place)
        return comm

    wpool_bd = _block_diag(pool_w).astype(BF16)
    wa_bd = _block_diag(rg_w_a).astype(BF16)
    wx_bd = _block_diag(rg_w_x).astype(BF16)
    zeros4 = jnp.zeros((L, 4, D_POOL), F32)
    v256 = jnp.concatenate([pool_scale[:, None], convb_dw_b[:, None], convb_ln_g[:, None], convb_ln_b[:, None], zeros4], axis=1)
    dww_p = jnp.concatenate([dww_f, jnp.zeros((L, 1, D_CONV), F32)], axis=1)
    v512 = jnp.concatenate([rg_conv_b[:, None], rg_b_a[:, None], rg_b_x[:, None], rg_lambda[:, None], rgw_f], axis=1)
    mix_w = (wpool_bd, v256, dww_p, wpw_f, wa_bd, wx_bd, v512)

    wout_f, wup_f, wdown_f = [None] * L, [None] * L, [None] * L
    saved = []

    (p,), wout_f = _rms_mm(h, mix_norm_g, win_f[0], 0, d_in, "in_proj0",
                           weight_gather([(wout_b[l], (r_out * N_DEV, D), rows_of(r_out)) for l in range(L)]),
                           w_is_nk=True)
    (y, *kept), (wup_f[0],) = _mixer_fwd(p, mix_w, 0, weight_gather([(wup_b[0], (D, d_ff), cols_of(c_ff))]))
    (h2, u2), (win_f[1],) = _mm_res(h, y, wout_f[0], y.shape[-1], False, "out_proj0",
                                    weight_gather([(win_t[1], (d_in, D), rows_of(c_in))]), norm=(mlp_norm_g, 0))
    (z,), (wdown_f[0],) = _rms_mm(u2, mlp_norm_g, wup_f[0], 0, 2048, "mlp_up0",
                                  weight_gather([(wdown_b[0], (d_ff, D), rows_of(r_dn))]), out_dtype=BF16)
    (h3, u1), (wup_f[1],) = _mm_res(h2, z, wdown_f[0], 2048, True, "mlp_down0",
                                    weight_gather([(wup_b[1], (D, d_ff), cols_of(c_ff))]), norm=(mix_norm_g, 1))
    saved.append((h, None, p, y, kept, h2, u2, z))
    h = h3
    (p,), _ = _rms_mm(u1, mix_norm_g, win_f[1], 1, d_in, "in_proj1", w_is_nk=True)
    (y, *kept), (wdown_f[1],) = _mixer_fwd(p, mix_w, 1, weight_gather([(wdown_b[1], (d_ff, D), rows_of(r_dn))]))
    (h2, u2), _ = _mm_res(h, y, wout_f[1], y.shape[-1], False, "out_proj1", norm=(mlp_norm_g, 1))
    (z,), _ = _rms_mm(u2, mlp_norm_g, wup_f[1], 1, 2048, "mlp_up1", out_dtype=BF16)
    (h3,), _ = _mm_res(h2, z, wdown_f[1], 2048, True, "mlp_down1")
    saved.append((h, u1, p, y, kept, h2, u2, z))

    dh, dhb, g_fin, loss_part = _final_loss(h3, W["final_norm_g"], tgt)
    loss = lax.psum(loss_part[0, 0], ("x", "y", "c"))

    def grad_exchange(pieces):
        comm = _Comm()
        for g, send, shape in pieces:
            comm.add(g, send, comm.add_out((N_DEV,) + shape, BF16), by_slot)
        return comm

    def small_gather(comm, grads):
        for g in grads:
            comm.gather(g, (N_DEV,) + g.shape, by_slot)
        return comm

    recv = {n: [None] * L for n in ("w_in", "w_out", "w_up", "w_down")}
    small1 = gath1 = gath0 = g_win_t1 = None
    for l in reversed(range(L)):
        h0, u1, p, y, kept, h2, u2, z = saved[l]
        (dz,), _ = _bwd_dz(dhb, wdown_f[l], z, 2048, f"mlp_down_bwd{l}")
        (g_wdown,), _ = _mm_tn(z, dhb, True, None, None, l, "a", 512, BF16, f"mlp_down_wgrad{l}")
        comm = grad_exchange([(g_wdown, rows_of(r_dn), (r_dn, D))]) if l == 0 else None
        (dh2, dh2b, g_mlp), got = _bwd_rms(dz, wup_f[l], h2, mlp_norm_g, dh, l, 2048, f"mlp_up_bwd{l}", comm)
        if l == 0:
            recv["w_down"][0] = got[0]
        (g_wup,), _ = _mm_tn(u2, dz, False, None, None, l, "b", 1024, BF16, f"mlp_up_wgrad{l}")
        (dy,), _ = _bwd_plain(dh2b, wout_f[l], f"out_proj_bwd{l}")
        (g_wout,), _ = _mm_tn(y, dh2b, False, None, None, l, "b", D, BF16, f"out_proj_wgrad{l}")
        comm = grad_exchange([(g_wup, cols_of(c_ff), (D, c_ff)), (g_wout, rows_of(r_out), (r_out, D))]
                             + ([(g_wdown, rows_of(r_dn), (r_dn, D))] if l == 1 else
                                [(g_win_t1, rows_of(c_in), (c_in, D))]))
        if l == 0:
            small_gather(comm, small1)
        (dp, g256, g512, gpool, gpw, gwa, gwx), got = _mixer_bwd(p, *kept, dy, mix_w, l, comm)
        recv["w_up"][l], recv["w_out"][l] = got[0], got[1]
        if l == 1:
            recv["w_down"][1] = got[2]
        else:
            recv["w_in"][1], gath1 = got[2], got[3:]
        smalls = [g_mlp, _diag_blocks(gpool, 4), g256, g512, gpw, _diag_blocks(gwa, RG_HEADS), _diag_blocks(gwx, RG_HEADS)]
        comm = small_gather(_Comm(), smalls[:5]) if l == 0 else None
        if u1 is None:
            (g_win_t,), got_a = _mm_tn(dp, h0, False, "b", mix_norm_g, l, "a", d_in // 2, BF16, f"in_proj_wgrad{l}", comm)
        else:
            (g_win_t,), got_a = _mm_tn(dp, u1, False, None, None, l, "a", d_in // 2, BF16, f"in_proj_wgrad{l}", comm)
        comm = small_gather(grad_exchange([(g_win_t, rows_of(c_in), (c_in, D))]), smalls[5:]) if l == 0 else None
        outs, got_b = _bwd_rms(dp, win_f[l], h0, mix_norm_g, dh2, l, d_in, f"in_proj_bwd{l}", comm, w_is_kd=True,
                               emit_bf16=l > 0)
        dh, g_mix = outs[0], outs[-1]
        dhb = outs[1] if l > 0 else None
        if l == 0:
            recv["w_in"][0] = got_b[0]
            gath0 = got_a + got_b[1:]
        if l == 1:
            g_win_t1 = g_win_t
            small1 = [g_mix] + smalls + [g_fin]
    grad_x = dh[N_META:][None]

    gath_tail = _comm_only(small_gather(_Comm(), [g_mix, dh[:N_META]]), "exchange_tail")

    out = {}
    for name, tr in (("w_out", r_out), ("w_up", 256), ("w_down", 128)):
        out[name] = _adamw_big(recv[name], W[name], M[name], V[name], tr, f"adamw_{name}")
    g_win = jnp.transpose(_sum_slots(recv["w_in"], "sum_w_in"), (0, 2, 1))
    out["w_in"] = (g_win,) + tuple(_adamw_plain([g_win], [W["w_in"]], [M["w_in"]], [V["w_in"]], "adamw_w_in"))
    sums, small_out = _adamw_small(gath1, gath0, gath_tail, [W[n] for n in _SMALL_REPL], [M[n] for n in _SMALL_REPL],
                                   [V[n] for n in _SMALL_REPL])
    for a, name in enumerate(_SMALL_REPL):
        out[name] = small_out[4 * a:4 * a + 4]
    s_256, s_512, s_pw, s_meta = sums[4], sums[5], sums[6], sums[9]
    g_shard = {
        "meta_tokens": lax.dynamic_slice_in_dim(s_meta, me * 128, 128, axis=1),
        "convb_dw_w": lax.dynamic_slice_in_dim(s_256[:, :CONV_K, :], me * 32, 32, axis=2),
        "convb_pw_w": lax.dynamic_slice_in_dim(s_pw, me * 32, 32, axis=1),
        "rg_conv_w": lax.dynamic_slice_in_dim(s_512[:, 4:8, :], me * 64, 64, axis=2),
    }
    names = list(g_shard)
    res = _adamw_plain([g_shard[n] for n in names], [W[n] for n in names], [M[n] for n in names],
                       [V[n] for n in names], "adamw_small_sharded")
    for a, name in enumerate(names):
        out[name] = (g_shard[name],) + tuple(res[3 * a:3 * a + 3])

    def fix(name, arr):
        return arr.reshape(-1) if name == "final_norm_g" else arr

    return (loss, grad_x,
            *[fix(n, out[n][0]) for n in _WEIGHTS], *[fix(n, out[n][1]) for n in _WEIGHTS],
            *[fix(n, out[n][2]) for n in _WEIGHTS], *[fix(n, out[n][3]) for n in _WEIGHTS])
```

```python
import functools
import operator

import jax
import jax.numpy as jnp
from jax import lax
from jax.experimental import pallas as pl
from jax.experimental.pallas import tpu as pltpu

F32 = jnp.float32
BF16 = jnp.bfloat16

N_DEV = 8
N_META = 16
DEPTH = 2
EPS = 1e-6
POOL_WINDOWS = (2, 4, 8, 16)
POOL_GW = 64
CONV_K = 31
RG_CONV_K = 4
RG_HEADS = 8
RG_HD = 64
RG_C = 8.0
D_POOL = 256
D_CONV = 256
D_RNN = 512

ADAM_LR = 0.001
ADAM_B1 = 0.9
ADAM_B2 = 0.999
ADAM_EPS = 1e-08
ADAM_WD = 0.01
ADAM_STEP = 10

TT = 432
TD = 912
TD_BIG = 2736
HALO = 48
VMEM_LIMIT = 56 * 1024 * 1024

MESH = pl.DeviceIdType.MESH
ANY = pl.BlockSpec(memory_space=pl.ANY)


def _cparams(*sem):
    return pltpu.CompilerParams(dimension_semantics=sem, vmem_limit_bytes=VMEM_LIMIT)


def _full(shape):
    nd = len(shape)
    return pl.BlockSpec(shape, lambda *_: (0,) * nd)


def _layer(shape, l):
    nd = len(shape)
    return pl.BlockSpec((None,) + tuple(shape), lambda *_: (l,) + (0,) * nd)


def _dot(a, b):
    return jnp.dot(a, b, preferred_element_type=F32)


def _dot_nt(a, b):
    return lax.dot_general(a, b, (((1,), (1,)), ((), ())), preferred_element_type=F32)


def _dot_tn(a, b):
    return lax.dot_general(a, b, (((0,), (0,)), ((), ())), preferred_element_type=F32)


def _rms(h):
    r = lax.rsqrt(jnp.mean(h * h, axis=-1, keepdims=True) + EPS)
    return h * r, r


def _rms_bwd(du, xh, r, g):
    dxh = du * g
    return r * (dxh - xh * jnp.mean(dxh * xh, axis=-1, keepdims=True))


def _sigmoid(x):
    return 1.0 / (1.0 + jnp.exp(-x))


def _colsum(x):
    return jnp.sum(x, axis=0, keepdims=True)


def _mesh_pos():
    return lax.axis_index("x"), lax.axis_index("y"), lax.axis_index("c")


def _slot(d):
    return 4 * d[0] + 2 * d[1] + d[2]


class _Comm:
    def __init__(self):
        self.srcs, self.send, self.dst, self.land, self.out_shapes = [], [], [], [], []

    def add_out(self, shape, dtype):
        self.out_shapes.append(jax.ShapeDtypeStruct(shape, dtype))
        return len(self.out_shapes) - 1

    def add(self, src, send, dst, land):
        self.srcs.append(src)
        self.send.append(send)
        self.dst.append(dst)
        self.land.append(land)

    def gather(self, block, out_shape, place):
        self.add(block, lambda ref, s: ref, self.add_out(out_shape, block.dtype), place)

    def sem_shapes(self):
        n = len(self.srcs)
        return [pltpu.SemaphoreType.DMA((7 * n,)), pltpu.SemaphoreType.DMA((7 * n,)), pltpu.SemaphoreType.DMA((n,))]

    def _copies(self, cins, couts, send_sems, recv_sems, local_sems, with_arrivals=True):
        x, y, c = _mesh_pos()
        me = (x, y, c)
        sends, arrivals, local = [], [], []
        for a in range(len(self.srcs)):
            out = couts[self.dst[a]]
            mine = self.land[a](out, _slot(me))
            local.append(pltpu.make_async_copy(self.send[a](cins[a], _slot(me)), mine, local_sems.at[a]))
            for k in range(N_DEV - 1):
                d = k + 1
                peer = (1 - x if (d >> 2) & 1 else x, 1 - y if (d >> 1) & 1 else y, 1 - c if d & 1 else c)
                sems = dict(send_sem=send_sems.at[a * 7 + k], recv_sem=recv_sems.at[a * 7 + k],
                            device_id=peer, device_id_type=MESH)
                sends.append(pltpu.make_async_remote_copy(
                    src_ref=self.send[a](cins[a], _slot(peer)), dst_ref=mine, **sems))
                if with_arrivals:
                    theirs = self.land[a](out, _slot(peer))
                    arrivals.append(pltpu.make_async_remote_copy(src_ref=theirs, dst_ref=theirs, **sems))
        return sends, arrivals, local

    def start(self, *refs):
        sends, _, local = self._copies(*refs, with_arrivals=False)
        for cp in local + sends:
            cp.start()

    def wait(self, *refs):
        sends, arrivals, local = self._copies(*refs)
        for cp in arrivals:
            cp.wait_recv()
        for cp in sends:
            cp.wait_send()
        for cp in local:
            cp.wait()


def _call(body, *, name, grid, in_specs, out_specs, out_shape, operands, scratch_shapes=(), sem=None, comm=None):
    out_specs, out_shape = list(out_specs), list(out_shape)
    if comm is None:
        res = pl.pallas_call(body, name=name, grid=grid, in_specs=list(in_specs), out_specs=out_specs,
                             out_shape=out_shape, scratch_shapes=list(scratch_shapes),
                             compiler_params=_cparams(*sem))(*operands)
        return list(res), []
    n_in, n_out, n_scr = len(in_specs), len(out_specs), len(scratch_shapes)
    nc_in, nc_out = len(comm.srcs), len(comm.out_shapes)

    def hosted(*refs):
        ins, cins = refs[:n_in], refs[n_in:n_in + nc_in]
        o0 = n_in + nc_in
        outs, couts = refs[o0:o0 + n_out], refs[o0 + n_out:o0 + n_out + nc_out]
        s0 = o0 + n_out + nc_out
        scr, sems = refs[s0:s0 + n_scr], refs[s0 + n_scr:]
        ids = [pl.program_id(a) for a in range(len(grid))]
        first = functools.reduce(operator.and_, [i == 0 for i in ids])
        last = functools.reduce(operator.and_, [i == g - 1 for i, g in zip(ids, grid)])

        @pl.when(first)
        def _():
            comm.start(cins, couts, *sems)

        body(*ins, *outs, *scr)

        @pl.when(last)
        def _():
            comm.wait(cins, couts, *sems)

    res = pl.pallas_call(
        hosted, name=name, grid=grid, in_specs=list(in_specs) + [ANY] * nc_in,
        out_specs=out_specs + [ANY] * nc_out, out_shape=out_shape + comm.out_shapes,
        scratch_shapes=list(scratch_shapes) + comm.sem_shapes(),
        compiler_params=_cparams(*(["arbitrary"] * len(grid))))(*operands, *comm.srcs)
    return list(res[:n_out]), list(res[n_out:])


def _comm_only(comm, name):
    def body(*refs):
        n, m = len(comm.srcs), len(comm.out_shapes)
        comm.start(refs[:n], refs[n:n + m], *refs[n + m:])
        comm.wait(refs[:n], refs[n:n + m], *refs[n + m:])

    return list(pl.pallas_call(body, name=name, in_specs=[ANY] * len(comm.srcs), out_specs=[ANY] * len(comm.out_shapes),
                               out_shape=comm.out_shapes, scratch_shapes=comm.sem_shapes())(*comm.srcs))


def _allgather(items, name):
    n = len(items)

    def body(*refs):
        ins, outs = refs[:n], refs[n:2 * n]
        send_sems, recv_sems, local_sems = refs[2 * n:]
        x, y, c = _mesh_pos()
        me, sib = (x, y, c), (x, y, 1 - c)
        chips = [(1 - x, y), (x, 1 - y), (1 - x, 1 - y)]

        def copy(a, k, blk, to, from_input=False):
            region = items[a][2](outs[a], _slot(blk))
            return pltpu.make_async_remote_copy(
                src_ref=ins[a] if from_input else region, dst_ref=region,
                send_sem=send_sems.at[a * 7 + k], recv_sem=recv_sems.at[a * 7 + k],
                device_id=to, device_id_type=MESH)

        local = [pltpu.make_async_copy(ins[a], items[a][2](outs[a], _slot(me)), local_sems.at[a]) for a in range(n)]
        for cp in local:
            cp.start()
        first = []
        for a in range(n):
            first.append(copy(a, 0, me, sib, True))
            for j, chip in enumerate(chips):
                first.append(copy(a, 1 + j, me, (*chip, c), True))
        for cp in first:
            cp.start()
        passed = []
        for j, chip in enumerate(chips):
            for a in range(n):
                copy(a, 1 + j, (*chip, c), me).wait_recv()
                fwd = copy(a, 4 + j, (*chip, c), sib)
                fwd.start()
                passed.append(fwd)
        for a in range(n):
            copy(a, 0, sib, me).wait_recv()
            for j, chip in enumerate(chips):
                copy(a, 4 + j, (*chip, 1 - c), me).wait_recv()
        for cp in first + passed:
            cp.wait_send()
        for cp in local:
            cp.wait()

    return pl.pallas_call(
        body, name=name,
        in_specs=[ANY] * n, out_specs=[ANY] * n,
        out_shape=[jax.ShapeDtypeStruct(it[1], it[0].dtype) for it in items],
        scratch_shapes=[pltpu.SemaphoreType.DMA((7 * n,)), pltpu.SemaphoreType.DMA((7 * n,)),
                        pltpu.SemaphoreType.DMA((n,))],
    )(*[it[0] for it in items])


def _rms_mm(h, g, w, l, ck, name, comm=None, w_is_nk=False, out_dtype=F32):
    T, D = h.shape
    N = w.shape[0] if w_is_nk else w.shape[1]
    normed = h.dtype == BF16

    def body(h_ref, g_ref, w_ref, o_ref):
        if normed:
            u = h_ref[...]
        else:
            u = (_rms(h_ref[...])[0] * g_ref[l:l + 1, :]).astype(BF16)
        res = _dot_nt(u, w_ref[...]) if w_is_nk else _dot(u, w_ref[...])
        o_ref[...] = res.astype(out_dtype)

    w_spec = pl.BlockSpec((ck, D), lambda j, i: (j, 0)) if w_is_nk else pl.BlockSpec((D, ck), lambda j, i: (0, j))
    td = TD
    return _call(
        body, name=name, grid=(N // ck, T // td),
        in_specs=[pl.BlockSpec((td, D), lambda j, i: (i, 0)), _full(g.shape), w_spec],
        out_specs=[pl.BlockSpec((td, ck), lambda j, i: (i, j))],
        out_shape=[jax.ShapeDtypeStruct((T, N), out_dtype)],
        operands=(h, g, w), sem=("parallel", "parallel"), comm=comm)


def _mm_res(res, a, w, ck, relu2, name, comm=None, norm=None):
    T, D = res.shape
    K = a.shape[-1]
    nk = K // ck

    def body(*refs):
        if norm is None:
            r_ref, a_ref, w_ref, o_ref = refs
        else:
            r_ref, a_ref, w_ref, g_ref, o_ref, u_ref = refs
        k = pl.program_id(1)

        @pl.when(k == 0)
        def _():
            o_ref[...] = r_ref[...]

        av = a_ref[...]
        if relu2:
            av = jnp.square(jnp.maximum(av.astype(F32), 0.0))
        o_ref[...] += _dot(av.astype(BF16), w_ref[...])

        if norm is not None:
            @pl.when(k == nk - 1)
            def _():
                u_ref[...] = (_rms(o_ref[...])[0] * g_ref[norm[1]:norm[1] + 1, :]).astype(BF16)

    row = pl.BlockSpec((TD, D), lambda i, k: (i, 0))
    in_specs = [row, pl.BlockSpec((TD, ck), lambda i, k: (i, k)), pl.BlockSpec((ck, D), lambda i, k: (k, 0))]
    operands = (res, a, w)
    out_specs, out_shape = [row], [jax.ShapeDtypeStruct((T, D), F32)]
    if norm is not None:
        in_specs.append(_full(norm[0].shape))
        operands += (norm[0],)
        out_specs.append(row)
        out_shape.append(jax.ShapeDtypeStruct((T, D), BF16))
    return _call(
        body, name=name, grid=(T // TD, nk), in_specs=in_specs, out_specs=out_specs, out_shape=out_shape,
        operands=operands, sem=("parallel", "arbitrary"), comm=comm)


def _final_loss(h, gf, tgt):
    T, D = h.shape

    def body(h_ref, g_ref, t_ref, dh_ref, dhb_ref, dg_ref, loss_ref):
        i = pl.program_id(0)

        @pl.when(i == 0)
        def _():
            dg_ref[...] = jnp.zeros_like(dg_ref)
            loss_ref[...] = jnp.zeros_like(loss_ref)

        g = g_ref[...]
        xh, r = _rms(h_ref[...])
        row = i * TD + lax.broadcasted_iota(jnp.int32, (TD, 1), 0)
        keep = (row >= N_META).astype(F32)
        diff = (xh * g - t_ref[...]) * keep
        part = jnp.sum(jnp.sum(diff * diff, axis=1, keepdims=True), axis=0, keepdims=True)
        loss_ref[...] += (0.5 / D) * part
        dy = diff * (1.0 / D)
        dg_ref[...] += _colsum(dy * xh)
        dh = _rms_bwd(dy, xh, r, g)
        dh_ref[...] = dh
        dhb_ref[...] = dh.astype(BF16)

    row = pl.BlockSpec((TD, D), lambda i: (i, 0))
    return pl.pallas_call(
        body, name="final_loss", grid=(T // TD,),
        in_specs=[row, _full(gf.shape), row],
        out_specs=[row, row, _full((1, D)), _full((8, 128))],
        out_shape=[jax.ShapeDtypeStruct((T, D), F32), jax.ShapeDtypeStruct((T, D), BF16),
                   jax.ShapeDtypeStruct((1, D), F32), jax.ShapeDtypeStruct((8, 128), F32)],
        compiler_params=_cparams("arbitrary"))(h, gf, tgt)


def _bwd_dz(dh, wdown, z, ck, name, comm=None):
    T, D = dh.shape
    F = z.shape[-1]

    def body(d_ref, w_ref, z_ref, o_ref):
        dact = _dot_nt(d_ref[...], w_ref[...])
        o_ref[...] = (dact * (2.0 * jnp.maximum(z_ref[...].astype(F32), 0.0))).astype(BF16)

    return _call(
        body, name=name, grid=(F // ck, T // TD),
        in_specs=[pl.BlockSpec((TD, D), lambda j, i: (i, 0)), pl.BlockSpec((ck, D), lambda j, i: (j, 0)),
                  pl.BlockSpec((TD, ck), lambda j, i: (i, j))],
        out_specs=[pl.BlockSpec((TD, ck), lambda j, i: (i, j))],
        out_shape=[jax.ShapeDtypeStruct((T, F), BF16)],
        operands=(dh, wdown, z), sem=("parallel", "parallel"), comm=comm)


def _bwd_plain(dh, w, name, comm=None):
    T, D = dh.shape
    K = w.shape[0]

    def body(d_ref, w_ref, o_ref):
        o_ref[...] = _dot_nt(d_ref[...], w_ref[...])

    return _call(
        body, name=name, grid=(T // TD_BIG,),
        in_specs=[pl.BlockSpec((TD_BIG, D), lambda i: (i, 0)), _full((K, D))],
        out_specs=[pl.BlockSpec((TD_BIG, K), lambda i: (i, 0))],
        out_shape=[jax.ShapeDtypeStruct((T, K), F32)],
        operands=(dh, w), sem=("parallel",), comm=comm)


def _bwd_rms(dx, w, h, g, dh_in, l, ck, name, comm=None, w_is_kd=False, emit_bf16=True):
    T, D = h.shape
    K = dx.shape[-1]
    nk = K // ck
    w_spec = pl.BlockSpec((ck, D), lambda i, k: (k, 0)) if w_is_kd else pl.BlockSpec((D, ck), lambda i, k: (0, k))

    def body(x_ref, w_ref, h_ref, g_ref, di_ref, *rest):
        dh_ref, dg_ref = rest[0], rest[2 if emit_bf16 else 1]
        i, k = pl.program_id(0), pl.program_id(1)

        @pl.when((i == 0) & (k == 0))
        def _():
            dg_ref[...] = jnp.zeros_like(dg_ref)

        part = _dot(x_ref[...], w_ref[...]) if w_is_kd else _dot_nt(x_ref[...], w_ref[...])

        def finish(du):
            xh, r = _rms(h_ref[...])
            dg_ref[...] += _colsum(du * xh)
            dh = di_ref[...] + _rms_bwd(du, xh, r, g_ref[l:l + 1, :])
            dh_ref[...] = dh
            if emit_bf16:
                rest[1][...] = dh.astype(BF16)

        if nk == 1:
            finish(part)
        else:
            acc_ref = rest[-1]

            @pl.when(k == 0)
            def _():
                acc_ref[...] = part

            @pl.when(k > 0)
            def _():
                acc_ref[...] += part

            @pl.when(k == nk - 1)
            def _():
                finish(acc_ref[...])

    row = pl.BlockSpec((TD, D), lambda i, k: (i, 0))
    out_specs = [row] + ([row] if emit_bf16 else []) + [_full((1, D))]
    out_shape = ([jax.ShapeDtypeStruct((T, D), F32)] + ([jax.ShapeDtypeStruct((T, D), BF16)] if emit_bf16 else [])
                 + [jax.ShapeDtypeStruct((1, D), F32)])
    return _call(
        body, name=name, grid=(T // TD, nk),
        in_specs=[pl.BlockSpec((TD, ck), lambda i, k: (i, k)), w_spec, row, _full(g.shape), row],
        out_specs=out_specs, out_shape=out_shape,
        scratch_shapes=[pltpu.VMEM((TD, D), F32)] if nk > 1 else [],
        operands=(dx, w, h, g, dh_in), sem=("arbitrary", "arbitrary"), comm=comm)


def _mm_tn(a, b, relu2, rms_on, g, l, chunk, ck, out_dtype, name, comm=None):
    T = a.shape[0]
    Ka, Nb = a.shape[1], b.shape[1]
    ca, cb = (ck, Nb) if chunk == "a" else (Ka, ck)
    nj = (Ka // ck) if chunk == "a" else (Nb // ck)
    td = TD_BIG if a.dtype == BF16 and b.dtype == BF16 else TD
    nt = T // td

    def body(*refs):
        if rms_on is not None:
            a_ref, g_ref, b_ref, o_ref, acc_ref = refs
        else:
            a_ref, b_ref, o_ref, acc_ref = refs
        t = pl.program_id(1)

        @pl.when(t == 0)
        def _():
            acc_ref[...] = jnp.zeros_like(acc_ref)

        av, bv = a_ref[...], b_ref[...]
        if relu2:
            av = jnp.square(jnp.maximum(av.astype(F32), 0.0))
        if rms_on == "a":
            av = _rms(av)[0] * g_ref[l:l + 1, :]
        elif rms_on == "b":
            bv = _rms(bv)[0] * g_ref[l:l + 1, :]
        acc_ref[...] += _dot_tn(av.astype(BF16), bv.astype(BF16))

        @pl.when(t == nt - 1)
        def _():
            o_ref[...] = acc_ref[...].astype(out_dtype)

    if chunk == "a":
        a_spec = pl.BlockSpec((td, ca), lambda j, t: (t, j))
        b_spec = pl.BlockSpec((td, cb), lambda j, t: (t, 0))
        o_spec = pl.BlockSpec((ca, cb), lambda j, t: (j, 0))
    else:
        a_spec = pl.BlockSpec((td, ca), lambda j, t: (t, 0))
        b_spec = pl.BlockSpec((td, cb), lambda j, t: (t, j))
        o_spec = pl.BlockSpec((ca, cb), lambda j, t: (0, j))
    ins, specs = [a], [a_spec]
    if rms_on is not None:
        ins.append(g)
        specs.append(_full(g.shape))
    ins.append(b)
    specs.append(b_spec)
    return _call(
        body, name=name, grid=(nj, nt), in_specs=specs, out_specs=[o_spec],
        out_shape=[jax.ShapeDtypeStruct((Ka, Nb), out_dtype)], scratch_shapes=[pltpu.VMEM((ca, cb), F32)],
        operands=tuple(ins), sem=("parallel", "arbitrary"), comm=comm)


def _shift_down(x, k):
    return x if k == 0 else pltpu.roll(x, k, 0)


def _shift_up(x, k):
    return x if k == 0 else pltpu.roll(x, x.shape[0] - k, 0)


def _pool_sel(lane_grp, vals):
    return jnp.where(lane_grp == 0, vals[0], jnp.where(lane_grp == 1, vals[1], jnp.where(lane_grp == 2, vals[2], vals[3])))


def _pool_fwd(ue, pos):
    s = ue
    sums = []
    for k in (1, 2, 4, 8):
        s = s + _shift_down(s, k)
        sums.append(s[HALO:])
    grp = lax.broadcasted_iota(jnp.int32, (1, D_POOL), 1) // POOL_GW
    wsum = _pool_sel(grp, sums)
    width = _pool_sel(grp, [jnp.float32(w) for w in POOL_WINDOWS])
    cnt = jnp.minimum(pos.astype(F32), width)
    return wsum / cnt - ue[HALO:], cnt, grp


def _conv_fwd(xe, w_ref, nk):
    acc = None
    for k in range(nk):
        term = _shift_down(xe, nk - 1 - k)[HALO:] * w_ref(k)
        acc = term if acc is None else acc + term
    return acc


def _taps(window, w_ref, nk, shift):
    acc = None
    for r in range(min(8, nk)):
        inner = None
        for q in range((nk - 1 - r) // 8 + 1):
            term = window(q) * w_ref(nk - 1 - 8 * q - r)
            inner = term if inner is None else inner + term
        inner = shift(inner, r)
        acc = inner if acc is None else acc + inner
    return acc


def _layernorm(cb):
    mu = jnp.mean(cb, axis=-1, keepdims=True)
    xc = cb - mu
    rstd = lax.rsqrt(jnp.mean(xc * xc, axis=-1, keepdims=True) + EPS)
    return xc * rstd, rstd


def _softplus_neg(lam):
    x = -lam
    e = jnp.exp(-jnp.abs(x))
    u = 1.0 + e
    d = u - 1.0
    log1p = jnp.where(d == 0.0, e, jnp.log(u) * (e / jnp.where(d == 0.0, 1.0, d)))
    return jnp.maximum(x, 0.0) + log1p


def _expm1(x):
    small = x * (1.0 + x * 0.5 * (1.0 + x * (1.0 / 3.0) * (1.0 + x * 0.25)))
    return jnp.where(jnp.abs(x) < 1e-2, small, jnp.exp(x) - 1.0)


def _gelu_parts(x):
    c0 = 0.7978845608028654
    inner = c0 * (x + 0.044715 * x * x * x)
    th = jnp.tanh(inner)
    gelu = 0.5 * x * (1.0 + th)
    dgelu = 0.5 * (1.0 + th) + 0.5 * x * (1.0 - th * th) * c0 * (1.0 + 3.0 * 0.044715 * x * x)
    return gelu, dgelu


def _rg_gates(xc, wa_ref, wx_ref, v512_ref):
    xb = xc.astype(BF16)
    ra = _sigmoid(_dot(xb, wa_ref[...]) + v512_ref[1:2, :])
    ri = _sigmoid(_dot(xb, wx_ref[...]) + v512_ref[2:3, :])
    sp = _softplus_neg(v512_ref[3:4, :])
    log_a = (-RG_C) * ra * sp
    a = jnp.exp(log_a)
    mult = jnp.sqrt(-_expm1(2.0 * log_a))
    return ra, ri, sp, a, mult


def _scan_down(a, b):
    n = a.shape[0]
    row = lax.broadcasted_iota(jnp.int32, (n, 1), 0)
    k = 1
    while k < n:
        ok = row >= k
        b = a * jnp.where(ok, _shift_down(b, k), 0.0) + b
        a = a * jnp.where(ok, _shift_down(a, k), 1.0)
        k *= 2
    return b, a


def _scan_up(c, g):
    n = c.shape[0]
    row = lax.broadcasted_iota(jnp.int32, (n, 1), 0)
    k = 1
    while k < n:
        ok = row < n - k
        g = c * jnp.where(ok, _shift_up(g, k), 0.0) + g
        c = c * jnp.where(ok, _shift_up(c, k), 1.0)
        k *= 2
    return g, c


def _mixer_specs(l, tile_of):
    r = TT // HALO
    return [
        pl.BlockSpec((TT, 1792), lambda i: (tile_of(i), 0)),
        pl.BlockSpec((HALO, 1792), lambda i: (jnp.maximum(tile_of(i) * r - 1, 0), 0)),
        _layer((256, 256), l), _layer((8, 256), l), _layer((32, 256), l), _layer((256, 256), l),
        _layer((512, 512), l), _layer((512, 512), l), _layer((8, 512), l),
    ]


def _mixer_fwd(p, wts, l, comm=None):
    T = p.shape[0]

    def body(p_ref, ph_ref, wpool_ref, v256_ref, dww_ref, wpw_ref, wa_ref, wx_ref, v512_ref, y_ref, hr_ref, cb_ref,
             rg_ref, carry_ref):
        i = pl.program_id(0)

        @pl.when(i == 0)
        def _():
            carry_ref[...] = jnp.zeros_like(carry_ref)

        halo = jnp.where(i == 0, 0.0, ph_ref[...])
        e = jnp.concatenate([halo, p_ref[...]], axis=0)
        pos = i * TT + 1 + lax.broadcasted_iota(jnp.int32, (TT, 1), 0)
        pooled, _, _ = _pool_fwd(e[:, 0:256], pos)
        y_a = _dot(pooled.astype(BF16), wpool_ref[...]) * v256_ref[0:1, :]
        ub = e[:, 256:512] * _sigmoid(e[:, 512:768])
        cb = _conv_fwd(ub, lambda k: dww_ref[k:k + 1, :], CONV_K) + v256_ref[1:2, :]
        cb_ref[...] = cb
        xhat, _ = _layernorm(cb)
        ln = xhat * v256_ref[2:3, :] + v256_ref[3:4, :]
        s = ln * _sigmoid(ln)
        y_b = _dot(s.astype(BF16), wpw_ref[...])
        xc = _conv_fwd(e[:, 1280:1792], lambda k: v512_ref[4 + k:5 + k, :], RG_CONV_K) + v512_ref[0:1, :]
        ra, ri, _, a, mult = _rg_gates(xc, wa_ref, wx_ref, v512_ref)
        for n, val in enumerate((xc, ra, ri, a, mult)):
            rg_ref[:, n * D_RNN:(n + 1) * D_RNN] = val
        hloc, acum = _scan_down(a, mult * ri * xc)
        hr = hloc + acum * carry_ref[0:1, :]
        hr_ref[...] = hr
        carry_ref[0:1, :] = hr_ref[TT - 1:TT, :]
        gelu, _ = _gelu_parts(p_ref[:, 768:1280])
        y_ref[:, 0:256] = y_a.astype(BF16)
        y_ref[:, 256:512] = y_b.astype(BF16)
        y_ref[:, 512:1024] = (gelu * hr).astype(BF16)

    return _call(
        body, name=f"mixer_fwd{l}", grid=(T // TT,),
        in_specs=_mixer_specs(l, lambda i: i),
        out_specs=[pl.BlockSpec((TT, 1024), lambda i: (i, 0)), pl.BlockSpec((TT, 512), lambda i: (i, 0)),
                   pl.BlockSpec((TT, 256), lambda i: (i, 0)), pl.BlockSpec((TT, 5 * D_RNN), lambda i: (i, 0))],
        out_shape=[jax.ShapeDtypeStruct((T, 1024), BF16), jax.ShapeDtypeStruct((T, 512), F32),
                   jax.ShapeDtypeStruct((T, 256), F32), jax.ShapeDtypeStruct((T, 5 * D_RNN), F32)],
        scratch_shapes=[pltpu.VMEM((8, 512), F32)],
        operands=(p, p, *wts), sem=("arbitrary",), comm=comm)


def _mixer_bwd(p, hr, cb, rg, dy, wts, l, comm=None):
    T = p.shape[0]
    nt = T // TT
    tile_of = lambda i: nt - 1 - i

    def body(p_ref, ph_ref, wpool_ref, v256_ref, dww_ref, wpw_ref, wa_ref, wx_ref, v512_ref, hr_ref, hrh_ref, cb_ref,
             rg_ref, dy_ref, dp_ref, g256_ref, g512_ref, gpool_ref, gpw_ref, gwa_ref, gwx_ref,
             q_c, dcb_c, dxc_c, ag_c, ub_s, ed_s):
        i = pl.program_id(0)
        j = nt - 1 - i

        @pl.when(i == 0)
        def _():
            for ref in (g256_ref, g512_ref, gpool_ref, gpw_ref, gwa_ref, gwx_ref, q_c, dcb_c, dxc_c, ag_c):
                ref[...] = jnp.zeros_like(ref)

        halo = jnp.where(j == 0, 0.0, ph_ref[...])
        e = jnp.concatenate([halo, p_ref[...]], axis=0)
        pos = j * TT + 1 + lax.broadcasted_iota(jnp.int32, (TT, 1), 0)
        row = lax.broadcasted_iota(jnp.int32, (TT, 1), 0)

        dy_a = dy_ref[:, 0:256]
        pooled, cnt, grp = _pool_fwd(e[:, 0:256], pos)
        scale = v256_ref[0:1, :]
        mixed = _dot(pooled.astype(BF16), wpool_ref[...])
        g256_ref[34:35, :] += _colsum(dy_a * mixed)
        dmixed = (dy_a * scale).astype(BF16)
        gpool_ref[...] += _dot_tn(pooled.astype(BF16), dmixed)
        dpooled = _dot_nt(dmixed, wpool_ref[...])
        q = dpooled / cnt
        s = jnp.concatenate([q, q_c[...]], axis=0)
        sums = []
        for k in (1, 2, 4, 8):
            s = s + _shift_up(s, k)
            sums.append(s[:TT])
        dp_ref[:, 0:256] = (_pool_sel(grp, sums) - dpooled).astype(BF16)
        q_c[...] = q[0:16]

        dy_b = dy_ref[:, 256:512].astype(BF16)
        sg = _sigmoid(e[:, 512:768])
        ub_s[...] = e[:, 256:512] * sg
        xhat, rstd = _layernorm(cb_ref[...])
        ln_g = v256_ref[2:3, :]
        ln = xhat * ln_g + v256_ref[3:4, :]
        sig = _sigmoid(ln)
        sact = ln * sig
        gpw_ref[...] += _dot_tn(sact.astype(BF16), dy_b)
        dln = _dot_nt(dy_b, wpw_ref[...]) * (sig * (1.0 + ln * (1.0 - sig)))
        g256_ref[32:33, :] += _colsum(dln * xhat)
        g256_ref[33:34, :] += _colsum(dln)
        dxhat = dln * ln_g
        dcb = rstd * (dxhat - jnp.mean(dxhat, axis=-1, keepdims=True)
                      - xhat * jnp.mean(dxhat * xhat, axis=-1, keepdims=True))
        g256_ref[31:32, :] += _colsum(dcb)
        pad8 = jnp.zeros((8, D_CONV), F32)
        dcb_ext = jnp.concatenate([pad8, dcb, pad8], axis=0)
        for r in range(8):
            d_r = _shift_up(dcb_ext, r)[0:TT + 8]
            for q in range((CONV_K - 1 - r) // 8 + 1):
                k = CONV_K - 1 - 8 * q - r
                g256_ref[k:k + 1, :] += _colsum(d_r * ub_s[pl.ds(HALO - 8 - 8 * q, TT + 8), :])
        ed_s[0:TT, :] = dcb
        ed_s[TT:TT + 32, :] = dcb_c[...]
        dub = _taps(lambda q: ed_s[pl.ds(8 * q, TT + 8), :], lambda k: dww_ref[k:k + 1, :], CONV_K, _shift_up)[:TT]
        dcb_c[...] = dcb[0:32]
        sg_t = sg[HALO:]
        v_t = p_ref[:, 256:512]
        dp_ref[:, 256:512] = (dub * sg_t).astype(BF16)
        dp_ref[:, 512:768] = (dub * v_t * sg_t * (1.0 - sg_t)).astype(BF16)

        dy_c = dy_ref[:, 512:1024]
        cxe = e[:, 1280:1792]
        xc, ra, ri, a, mult = (rg_ref[:, n * D_RNN:(n + 1) * D_RNN] for n in range(5))
        sp = _softplus_neg(v512_ref[3:4, :])
        hrv = hr_ref[...]
        gelu, dgelu = _gelu_parts(p_ref[:, 768:1280])
        dp_ref[:, 768:1280] = (dy_c * hrv * dgelu).astype(BF16)
        dhr = dy_c * gelu
        coef = jnp.where(row < TT - 1, _shift_up(a, 1), 1.0)
        gloc, ccum = _scan_up(coef, dhr)
        gg = gloc + ccum * ag_c[0:1, :]
        ag_c[...] = (a * gg)[0:8]
        hr_before = jnp.where(j == 0, 0.0, hrh_ref[7:8, :])
        hr_prev = jnp.where(row >= 1, _shift_down(hrv, 1), hr_before)
        da = gg * hr_prev
        dmult = gg * ri * xc
        dri = gg * mult * xc
        dxc = gg * mult * ri
        dlog_a = da * a - dmult * (a * a / mult)
        g512_ref[3:4, :] += _colsum(dlog_a * ((-RG_C) * ra))
        dpre_a = dlog_a * ((-RG_C) * sp) * ra * (1.0 - ra)
        dpre_x = dri * ri * (1.0 - ri)
        g512_ref[1:2, :] += _colsum(dpre_a)
        g512_ref[2:3, :] += _colsum(dpre_x)
        xb = xc.astype(BF16)
        dpa_b, dpx_b = dpre_a.astype(BF16), dpre_x.astype(BF16)
        gwa_ref[...] += _dot_tn(xb, dpa_b)
        gwx_ref[...] += _dot_tn(xb, dpx_b)
        dxc = dxc + _dot_nt(dpa_b, wa_ref[...]) + _dot_nt(dpx_b, wx_ref[...])
        g512_ref[0:1, :] += _colsum(dxc)
        for k in range(RG_CONV_K):
            g512_ref[4 + k:5 + k, :] += _colsum(dxc * _shift_down(cxe, RG_CONV_K - 1 - k)[HALO:])
        ex = jnp.concatenate([dxc, dxc_c[...]], axis=0)
        dcx = _taps(lambda q: ex, lambda k: v512_ref[4 + k:5 + k, :], RG_CONV_K, _shift_up)[:TT]
        dxc_c[...] = dxc[0:8]
        dp_ref[:, 1280:1792] = dcx.astype(BF16)

        @pl.when(i == nt - 1)
        def _():
            lam = v512_ref[3:4, :]
            g512_ref[3:4, :] = g512_ref[3:4, :] * (-_sigmoid(-lam))

    in_specs = _mixer_specs(l, tile_of) + [
        pl.BlockSpec((TT, 512), lambda i: (tile_of(i), 0)),
        pl.BlockSpec((8, 512), lambda i: (jnp.maximum(tile_of(i) * (TT // 8) - 1, 0), 0)),
        pl.BlockSpec((TT, 256), lambda i: (tile_of(i), 0)),
        pl.BlockSpec((TT, 5 * D_RNN), lambda i: (tile_of(i), 0)),
        pl.BlockSpec((TT, 1024), lambda i: (tile_of(i), 0)),
    ]
    acc_shapes = [(40, 256), (8, 512), (256, 256), (256, 256), (512, 512), (512, 512)]
    return _call(
        body, name=f"mixer_bwd{l}", grid=(nt,),
        in_specs=in_specs,
        out_specs=[pl.BlockSpec((TT, 1792), lambda i: (tile_of(i), 0))] + [_full(s) for s in acc_shapes],
        out_shape=[jax.ShapeDtypeStruct((T, 1792), BF16)] + [jax.ShapeDtypeStruct(s, F32) for s in acc_shapes],
        scratch_shapes=[pltpu.VMEM((16, 256), F32), pltpu.VMEM((32, 256), F32), pltpu.VMEM((8, 512), F32),
                        pltpu.VMEM((8, 512), F32), pltpu.VMEM((HALO + TT, D_CONV), F32),
                        pltpu.VMEM((TT + 32, D_CONV), F32)],
        operands=(p, p, *wts, hr, hr, cb, rg, dy), sem=("arbitrary",), comm=comm)


def _adam(w, g, m, v):
    m = ADAM_B1 * m + (1.0 - ADAM_B1) * g
    v = ADAM_B2 * v + (1.0 - ADAM_B2) * (g * g)
    m_hat = m / (1.0 - ADAM_B1 ** ADAM_STEP)
    v_hat = v / (1.0 - ADAM_B2 ** ADAM_STEP)
    delta = -ADAM_LR * (m_hat / (jnp.sqrt(v_hat) + ADAM_EPS) + ADAM_WD * w)
    return delta, m, v


def _adamw_big(recvs, w, m, v, tr, name):
    L, R, C = w.shape

    def body(r0_ref, r1_ref, w_ref, m_ref, v_ref, g_ref, d_ref, mo_ref, vo_ref):
        l = pl.program_id(0)

        def total(r_ref):
            g = r_ref[0].astype(F32)
            for s in range(1, N_DEV):
                g = g + r_ref[s].astype(F32)
            g_ref[...] = g

        @pl.when(l == 0)
        def _():
            total(r0_ref)

        @pl.when(l == 1)
        def _():
            total(r1_ref)

        d_ref[...], mo_ref[...], vo_ref[...] = _adam(w_ref[...], g_ref[...], m_ref[...], v_ref[...])

    blk = pl.BlockSpec((None, tr, C), lambda l, i: (l, i, 0))
    return pl.pallas_call(
        body, name=name, grid=(L, R // tr),
        in_specs=[pl.BlockSpec((N_DEV, tr, C), lambda l, i: (0, i * (1 - l), 0)),
                  pl.BlockSpec((N_DEV, tr, C), lambda l, i: (0, i * l, 0)), blk, blk, blk],
        out_specs=[blk] * 4,
        out_shape=[jax.ShapeDtypeStruct((L, R, C), F32)] * 4,
        compiler_params=_cparams("arbitrary", "arbitrary"))(recvs[0], recvs[1], w, m, v)


def _sum_slots(recvs, name):
    L = len(recvs)
    _, R, C = recvs[0].shape

    def body(*refs):
        o_ref = refs[L]
        for l in range(L):
            acc = refs[l][0].astype(F32)
            for s in range(1, N_DEV):
                acc = acc + refs[l][s].astype(F32)
            o_ref[l] = acc

    return pl.pallas_call(body, name=name, out_shape=jax.ShapeDtypeStruct((L, R, C), F32))(*recvs)


def _adamw_plain(gs, ws, ms, vs, name):
    n = len(gs)

    def body(*refs):
        g_r, w_r, m_r, v_r = refs[:n], refs[n:2 * n], refs[2 * n:3 * n], refs[3 * n:4 * n]
        outs = refs[4 * n:]
        for a in range(n):
            d, mo, vo = _adam(w_r[a][...], g_r[a][...], m_r[a][...], v_r[a][...])
            outs[3 * a][...] = d
            outs[3 * a + 1][...] = mo
            outs[3 * a + 2][...] = vo

    shapes = []
    for a in range(n):
        shapes += [jax.ShapeDtypeStruct(ws[a].shape, F32)] * 3
    return pl.pallas_call(body, name=name, out_shape=shapes,
                          compiler_params=pltpu.CompilerParams(vmem_limit_bytes=VMEM_LIMIT))(*gs, *ws, *ms, *vs)


_V256_ROWS = {"convb_dw_b": 31, "convb_ln_g": 32, "convb_ln_b": 33, "pool_scale": 34}
_V512_ROWS = {"rg_conv_b": 0, "rg_b_a": 1, "rg_b_x": 2, "rg_lambda": 3}
_SMALL_REPL = ["mix_norm_g", "mlp_norm_g", "final_norm_g", "pool_w", "pool_scale", "convb_dw_b", "convb_ln_g",
               "convb_ln_b", "rg_conv_b", "rg_w_a", "rg_b_a", "rg_w_x", "rg_b_x", "rg_lambda"]


def _adamw_small(gath1, gath0, gath_tail, ws, ms, vs):
    n = len(_SMALL_REPL)
    n1, n0, nt = len(gath1), len(gath0), len(gath_tail)
    ng = n1 + n0 + nt
    L = DEPTH
    per_layer = [g.shape[1:] for g in gath1[:8]]
    sum_shapes = ([(L,) + s[1:] for s in per_layer[:2]] + [gath1[8].shape[1:]] + [(L,) + s for s in per_layer[2:]]
                  + [gath_tail[1].shape[1:], gath1[9].shape[1:]])

    def body(*refs):
        in1, in0, in_t = refs[:n1], refs[n1:n1 + n0], refs[n1 + n0:ng]
        w_r, m_r, v_r = refs[ng:ng + n], refs[ng + n:ng + 2 * n], refs[ng + 2 * n:ng + 3 * n]
        sums = refs[ng + 3 * n:ng + 3 * n + 10]
        outs = refs[ng + 3 * n + 11:]

        def total(ref):
            acc = ref[0]
            for s in range(1, N_DEV):
                acc = acc + ref[s]
            return acc

        refs[ng + 3 * n + 10][...] = total(in1[9])
        s_mix, s_mlp, s_fin, s_pool, s_256, s_512, s_pw, s_wa, s_wx, s_meta = sums
        s_mix[1:2, :] = total(in1[0])
        s_mix[0:1, :] = total(in_t[0])
        s_mlp[1:2, :] = total(in1[1])
        s_mlp[0:1, :] = total(in0[0])
        s_fin[...] = total(in1[8])
        s_meta[...] = total(in_t[1])
        for k, dst in enumerate((s_pool, s_256, s_512, s_pw, s_wa, s_wx)):
            dst[1] = total(in1[2 + k])
            dst[0] = total(in0[1 + k])

        def grad_of(name, idx):
            if name == "mix_norm_g":
                return s_mix[idx[0]:idx[0] + 1, :]
            if name == "mlp_norm_g":
                return s_mlp[idx[0]:idx[0] + 1, :]
            if name == "final_norm_g":
                return s_fin[...]
            if name in _V256_ROWS:
                r = _V256_ROWS[name]
                return s_256[idx[0], r:r + 1, :]
            if name in _V512_ROWS:
                r = _V512_ROWS[name]
                return s_512[idx[0], r:r + 1, :]
            src = {"pool_w": s_pool, "rg_w_a": s_wa, "rg_w_x": s_wx}[name]
            return src[idx[0], :, idx[1] * 64:(idx[1] + 1) * 64]

        for a, name in enumerate(_SMALL_REPL):
            shape = w_r[a].shape
            if name == "final_norm_g":
                parts = [((), (slice(None), slice(None)))]
            elif len(shape) == 2:
                parts = [((l,), (slice(l, l + 1), slice(None))) for l in range(shape[0])]
            else:
                parts = [((l, h), (l, h)) for l in range(shape[0]) for h in range(shape[1])]
            for idx, sel in parts:
                g = grad_of(name, idx)
                d, mo, vo = _adam(w_r[a][sel], g, m_r[a][sel], v_r[a][sel])
                outs[4 * a][sel] = g
                outs[4 * a + 1][sel] = d
                outs[4 * a + 2][sel] = mo
                outs[4 * a + 3][sel] = vo

    out_shape = [jax.ShapeDtypeStruct(s, F32) for s in sum_shapes]
    for a in range(n):
        out_shape += [jax.ShapeDtypeStruct(ws[a].shape, F32)] * 4
    res = pl.pallas_call(body, name="adamw_small", out_shape=out_shape,
                         compiler_params=pltpu.CompilerParams(vmem_limit_bytes=VMEM_LIMIT))(
                             *gath1, *gath0, *gath_tail, *ws, *ms, *vs)
    return res[:11], res[11:]


_WEIGHTS = ['meta_tokens', 'mix_norm_g', 'w_in', 'pool_w', 'pool_scale', 'convb_dw_w', 'convb_dw_b', 'convb_ln_g',
            'convb_ln_b', 'convb_pw_w', 'rg_conv_w', 'rg_conv_b', 'rg_w_a', 'rg_b_a', 'rg_w_x', 'rg_b_x', 'rg_lambda',
            'w_out', 'mlp_norm_g', 'w_up', 'w_down', 'final_norm_g']


def _block_diag(w):
    L, H, C, _ = w.shape
    eye = jnp.eye(H, dtype=w.dtype)
    return (w[:, :, :, None, :] * eye[None, :, None, :, None]).reshape(L, H * C, H * C)


def _diag_blocks(m, H):
    C = m.shape[0] // H
    return jnp.concatenate([m[h * C:(h + 1) * C, h * C:(h + 1) * C] for h in range(H)], axis=1)


def kernel(x, meta_tokens, mix_norm_g, w_in, pool_w, pool_scale, convb_dw_w, convb_dw_b, convb_ln_g, convb_ln_b, convb_pw_w, rg_conv_w, rg_conv_b, rg_w_a, rg_b_a, rg_w_x, rg_b_x, rg_lambda, w_out, mlp_norm_g, w_up, w_down, final_norm_g, loss_target, m_meta_tokens, m_mix_norm_g, m_w_in, m_pool_w, m_pool_scale, m_convb_dw_w, m_convb_dw_b, m_convb_ln_g, m_convb_ln_b, m_convb_pw_w, m_rg_conv_w, m_rg_conv_b, m_rg_w_a, m_rg_b_a, m_rg_w_x, m_rg_b_x, m_rg_lambda, m_w_out, m_mlp_norm_g, m_w_up, m_w_down, m_final_norm_g, v_meta_tokens, v_mix_norm_g, v_w_in, v_pool_w, v_pool_scale, v_convb_dw_w, v_convb_dw_b, v_convb_ln_g, v_convb_ln_b, v_convb_pw_w, v_rg_conv_w, v_rg_conv_b, v_rg_w_a, v_rg_b_a, v_rg_w_x, v_rg_b_x, v_rg_lambda, v_w_out, v_mlp_norm_g, v_w_up, v_w_down, v_final_norm_g):
    W = dict(meta_tokens=meta_tokens, mix_norm_g=mix_norm_g, w_in=w_in, pool_w=pool_w, pool_scale=pool_scale,
             convb_dw_w=convb_dw_w, convb_dw_b=convb_dw_b, convb_ln_g=convb_ln_g, convb_ln_b=convb_ln_b,
             convb_pw_w=convb_pw_w, rg_conv_w=rg_conv_w, rg_conv_b=rg_conv_b, rg_w_a=rg_w_a, rg_b_a=rg_b_a,
             rg_w_x=rg_w_x, rg_b_x=rg_b_x, rg_lambda=rg_lambda, w_out=w_out, mlp_norm_g=mlp_norm_g, w_up=w_up,
             w_down=w_down, final_norm_g=final_norm_g.reshape(1, -1))
    M = dict(meta_tokens=m_meta_tokens, mix_norm_g=m_mix_norm_g, w_in=m_w_in, pool_w=m_pool_w, pool_scale=m_pool_scale,
             convb_dw_w=m_convb_dw_w, convb_dw_b=m_convb_dw_b, convb_ln_g=m_convb_ln_g, convb_ln_b=m_convb_ln_b,
             convb_pw_w=m_convb_pw_w, rg_conv_w=m_rg_conv_w, rg_conv_b=m_rg_conv_b, rg_w_a=m_rg_w_a, rg_b_a=m_rg_b_a,
             rg_w_x=m_rg_w_x, rg_b_x=m_rg_b_x, rg_lambda=m_rg_lambda, w_out=m_w_out, mlp_norm_g=m_mlp_norm_g,
             w_up=m_w_up, w_down=m_w_down, final_norm_g=m_final_norm_g.reshape(1, -1))
    V = dict(meta_tokens=v_meta_tokens, mix_norm_g=v_mix_norm_g, w_in=v_w_in, pool_w=v_pool_w, pool_scale=v_pool_scale,
             convb_dw_w=v_convb_dw_w, convb_dw_b=v_convb_dw_b, convb_ln_g=v_convb_ln_g, convb_ln_b=v_convb_ln_b,
             convb_pw_w=v_convb_pw_w, rg_conv_w=v_rg_conv_w, rg_conv_b=v_rg_conv_b, rg_w_a=v_rg_w_a, rg_b_a=v_rg_b_a,
             rg_w_x=v_rg_w_x, rg_b_x=v_rg_b_x, rg_lambda=v_rg_lambda, w_out=v_w_out, mlp_norm_g=v_mlp_norm_g,
             w_up=v_w_up, w_down=v_w_down, final_norm_g=v_final_norm_g.reshape(1, -1))

    xs = x[0]
    S, D = xs.shape
    T = S + N_META
    assert T % TT == 0 and D == 1024
    L = DEPTH
    me = 4 * lax.axis_index("x") + 2 * lax.axis_index("y") + lax.axis_index("c")
    c_in, c_ff = w_in.shape[-1], w_up.shape[-1]
    d_in, d_ff = c_in * N_DEV, c_ff * N_DEV
    r_out, r_dn = w_out.shape[1], w_down.shape[1]
    by_slot = lambda ref, s: ref.at[s]
    rows_of = lambda n: (lambda ref, s: ref.at[pl.ds(s * n, n), :])
    cols_of = lambda n: (lambda ref, s: ref.at[:, pl.ds(s * n, n)])

    win_t = jnp.transpose(w_in, (0, 2, 1)).astype(BF16)
    gathered = _allgather([
        (win_t[0], (d_in, D), rows_of(c_in)),
        (meta_tokens, (N_DEV,) + meta_tokens.shape, by_slot),
        (convb_dw_w, (N_DEV,) + convb_dw_w.shape, by_slot),
        (convb_pw_w, (N_DEV,) + convb_pw_w.shape, by_slot),
        (rg_conv_w, (N_DEV,) + rg_conv_w.shape, by_slot),
    ], "gather_first")
    win_f = [gathered[0], None]
    meta_f = jnp.transpose(gathered[1], (1, 0, 2)).reshape(N_META, D)
    dww_f = jnp.transpose(gathered[2], (1, 2, 0, 3)).reshape(L, CONV_K, D_CONV)
    wpw_f = jnp.transpose(gathered[3], (1, 0, 2, 3)).reshape(L, D_CONV, D_CONV).astype(BF16)
    rgw_f = jnp.transpose(gathered[4], (1, 2, 0, 3)).reshape(L, RG_CONV_K, D_RNN)
    wout_b, wup_b, wdown_b = w_out.astype(BF16), w_up.astype(BF16), w_down.astype(BF16)

    def weight_gather(blocks):
        comm = _Comm()
        for blk, shape, place in blocks:
            comm.gather(blk, shape, place)
        return comm

    wpool_bd = _block_diag(pool_w).astype(BF16)
    wa_bd = _block_diag(rg_w_a).astype(BF16)
    wx_bd = _block_diag(rg_w_x).astype(BF16)
    zeros4 = jnp.zeros((L, 4, D_POOL), F32)
    v256 = jnp.concatenate([pool_scale[:, None], convb_dw_b[:, None], convb_ln_g[:, None], convb_ln_b[:, None], zeros4], axis=1)
    dww_p = jnp.concatenate([dww_f, jnp.zeros((L, 1, D_CONV), F32)], axis=1)
    v512 = jnp.concatenate([rg_conv_b[:, None], rg_b_a[:, None], rg_b_x[:, None], rg_lambda[:, None], rgw_f], axis=1)
    mix_w = (wpool_bd, v256, dww_p, wpw_f, wa_bd, wx_bd, v512)

    h = jnp.concatenate([meta_f, xs], axis=0)
    tgt = jnp.concatenate([jnp.zeros((N_META, D), F32), loss_target[0]], axis=0)
    wout_f, wup_f, wdown_f = [None] * L, [None] * L, [None] * L
    saved = []

    (p,), wout_f = _rms_mm(h, mix_norm_g, win_f[0], 0, d_in, "in_proj0",
                           weight_gather([(wout_b[l], (r_out * N_DEV, D), rows_of(r_out)) for l in range(L)]),
                           w_is_nk=True)
    (y, *kept), (wup_f[0],) = _mixer_fwd(p, mix_w, 0, weight_gather([(wup_b[0], (D, d_ff), cols_of(c_ff))]))
    (h2, u2), (win_f[1],) = _mm_res(h, y, wout_f[0], y.shape[-1], False, "out_proj0",
                                    weight_gather([(win_t[1], (d_in, D), rows_of(c_in))]), norm=(mlp_norm_g, 0))
    (z,), (wdown_f[0],) = _rms_mm(u2, mlp_norm_g, wup_f[0], 0, 2048, "mlp_up0",
                                  weight_gather([(wdown_b[0], (d_ff, D), rows_of(r_dn))]), out_dtype=BF16)
    (h3, u1), (wup_f[1],) = _mm_res(h2, z, wdown_f[0], 2048, True, "mlp_down0",
                                    weight_gather([(wup_b[1], (D, d_ff), cols_of(c_ff))]), norm=(mix_norm_g, 1))
    saved.append((h, None, p, y, kept, h2, u2, z))
    h = h3
    (p,), _ = _rms_mm(u1, mix_norm_g, win_f[1], 1, d_in, "in_proj1", w_is_nk=True)
    (y, *kept), (wdown_f[1],) = _mixer_fwd(p, mix_w, 1, weight_gather([(wdown_b[1], (d_ff, D), rows_of(r_dn))]))
    (h2, u2), _ = _mm_res(h, y, wout_f[1], y.shape[-1], False, "out_proj1", norm=(mlp_norm_g, 1))
    (z,), _ = _rms_mm(u2, mlp_norm_g, wup_f[1], 1, 2048, "mlp_up1", out_dtype=BF16)
    (h3,), _ = _mm_res(h2, z, wdown_f[1], 2048, True, "mlp_down1")
    saved.append((h, u1, p, y, kept, h2, u2, z))

    dh, dhb, g_fin, loss_part = _final_loss(h3, W["final_norm_g"], tgt)

    def grad_exchange(pieces):
        comm = _Comm()
        for g, send, shape in pieces:
            comm.add(g, send, comm.add_out((N_DEV,) + shape, BF16), by_slot)
        return comm

    def small_gather(comm, grads):
        for g in grads:
            comm.gather(g, (N_DEV,) + g.shape, by_slot)
        return comm

    recv = {n: [None] * L for n in ("w_in", "w_out", "w_up", "w_down")}
    small1 = gath1 = gath0 = g_win_t1 = None
    for l in reversed(range(L)):
        h0, u1, p, y, kept, h2, u2, z = saved[l]
        (dz,), _ = _bwd_dz(dhb, wdown_f[l], z, 2048, f"mlp_down_bwd{l}")
        (g_wdown,), _ = _mm_tn(z, dhb, True, None, None, l, "a", 512, BF16, f"mlp_down_wgrad{l}")
        comm = grad_exchange([(g_wdown, rows_of(r_dn), (r_dn, D))]) if l == 0 else None
        (dh2, dh2b, g_mlp), got = _bwd_rms(dz, wup_f[l], h2, mlp_norm_g, dh, l, 2048, f"mlp_up_bwd{l}", comm)
        if l == 0:
            recv["w_down"][0] = got[0]
        (g_wup,), _ = _mm_tn(u2, dz, False, None, None, l, "b", 1024, BF16, f"mlp_up_wgrad{l}")
        (dy,), _ = _bwd_plain(dh2b, wout_f[l], f"out_proj_bwd{l}")
        (g_wout,), _ = _mm_tn(y, dh2b, False, None, None, l, "b", D, BF16, f"out_proj_wgrad{l}")
        comm = grad_exchange([(g_wup, cols_of(c_ff), (D, c_ff)), (g_wout, rows_of(r_out), (r_out, D))]
                             + ([(g_wdown, rows_of(r_dn), (r_dn, D))] if l == 1 else
                                [(g_win_t1, rows_of(c_in), (c_in, D))]))
        if l == 0:
            small_gather(comm, small1)
        (dp, g256, g512, gpool, gpw, gwa, gwx), got = _mixer_bwd(p, *kept, dy, mix_w, l, comm)
        recv["w_up"][l], recv["w_out"][l] = got[0], got[1]
        if l == 1:
            recv["w_down"][1] = got[2]
        else:
            recv["w_in"][1], gath1 = got[2], got[3:]
        smalls = [g_mlp, _diag_blocks(gpool, 4), g256, g512, gpw, _diag_blocks(gwa, RG_HEADS), _diag_blocks(gwx, RG_HEADS)]
        comm = small_gather(_Comm(), smalls[:5]) if l == 0 else None
        if u1 is None:
            (g_win_t,), got_a = _mm_tn(dp, h0, False, "b", mix_norm_g, l, "a", d_in // 2, BF16, f"in_proj_wgrad{l}", comm)
        else:
            (g_win_t,), got_a = _mm_tn(dp, u1, False, None, None, l, "a", d_in // 2, BF16, f"in_proj_wgrad{l}", comm)
        comm = small_gather(grad_exchange([(g_win_t, rows_of(c_in), (c_in, D))]), smalls[5:]) if l == 0 else None
        outs, got_b = _bwd_rms(dp, win_f[l], h0, mix_norm_g, dh2, l, d_in, f"in_proj_bwd{l}", comm, w_is_kd=True,
                               emit_bf16=l > 0)
        dh, g_mix = outs[0], outs[-1]
        dhb = outs[1] if l > 0 else None
        if l == 0:
            recv["w_in"][0] = got_b[0]
            gath0 = got_a + got_b[1:]
        if l == 1:
            g_win_t1 = g_win_t
            small1 = [g_mix] + smalls + [g_fin, loss_part]
    grad_x = dh[N_META:][None]

    gath_tail = _comm_only(small_gather(_Comm(), [g_mix, dh[:N_META]]), "exchange_tail")

    out = {}
    for name, tr in (("w_out", r_out), ("w_up", 256), ("w_down", 128)):
        out[name] = _adamw_big(recv[name], W[name], M[name], V[name], tr, f"adamw_{name}")
    g_win = jnp.transpose(_sum_slots(recv["w_in"], "sum_w_in"), (0, 2, 1))
    out["w_in"] = (g_win,) + tuple(_adamw_plain([g_win], [W["w_in"]], [M["w_in"]], [V["w_in"]], "adamw_w_in"))
    sums, small_out = _adamw_small(gath1, gath0, gath_tail, [W[n] for n in _SMALL_REPL], [M[n] for n in _SMALL_REPL],
                                   [V[n] for n in _SMALL_REPL])
    for a, name in enumerate(_SMALL_REPL):
        out[name] = small_out[4 * a:4 * a + 4]
    s_256, s_512, s_pw, s_meta = sums[4], sums[5], sums[6], sums[9]
    loss = sums[10][0, 0]
    g_shard = {
        "meta_tokens": lax.dynamic_slice_in_dim(s_meta, me * 128, 128, axis=1),
        "convb_dw_w": lax.dynamic_slice_in_dim(s_256[:, :CONV_K, :], me * 32, 32, axis=2),
        "convb_pw_w": lax.dynamic_slice_in_dim(s_pw, me * 32, 32, axis=1),
        "rg_conv_w": lax.dynamic_slice_in_dim(s_512[:, 4:8, :], me * 64, 64, axis=2),
    }
    names = list(g_shard)
    res = _adamw_plain([g_shard[n] for n in names], [W[n] for n in names], [M[n] for n in names],
                       [V[n] for n in names], "adamw_small_sharded")
    for a, name in enumerate(names):
        out[name] = (g_shard[name],) + tuple(res[3 * a:3 * a + 3])

    def fix(name, arr):
        return arr.reshape(-1) if name == "final_norm_g" else arr

    return (loss, grad_x,
            *[fix(n, out[n][0]) for n in _WEIGHTS], *[fix(n, out[n][1]) for n in _WEIGHTS],
            *[fix(n, out[n][2]) for n in _WEIGHTS], *[fix(n, out[n][3]) for n in _WEIGHTS])
```

```python
import functools
import operator

import jax
import jax.numpy as jnp
from jax import lax
from jax.experimental import pallas as pl
from jax.experimental.pallas import tpu as pltpu

F32 = jnp.float32
BF16 = jnp.bfloat16

N_DEV = 8
N_META = 16
DEPTH = 2
EPS = 1e-6
POOL_WINDOWS = (2, 4, 8, 16)
POOL_GW = 64
CONV_K = 31
RG_CONV_K = 4
RG_HEADS = 8
RG_HD = 64
RG_C = 8.0
D_POOL = 256
D_CONV = 256
D_RNN = 512

ADAM_LR = 0.001
ADAM_B1 = 0.9
ADAM_B2 = 0.999
ADAM_EPS = 1e-08
ADAM_WD = 0.01
ADAM_STEP = 10

TT = 432
TD = 912
TD_BIG = 2736
HALO = 48
VMEM_LIMIT = 56 * 1024 * 1024

MESH = pl.DeviceIdType.MESH
ANY = pl.BlockSpec(memory_space=pl.ANY)


def _cparams(*sem):
    return pltpu.CompilerParams(dimension_semantics=sem, vmem_limit_bytes=VMEM_LIMIT)


def _full(shape):
    nd = len(shape)
    return pl.BlockSpec(shape, lambda *_: (0,) * nd)


def _layer(shape, l):
    nd = len(shape)
    return pl.BlockSpec((None,) + tuple(shape), lambda *_: (l,) + (0,) * nd)


def _dot(a, b):
    return jnp.dot(a, b, preferred_element_type=F32)


def _dot_nt(a, b):
    return lax.dot_general(a, b, (((1,), (1,)), ((), ())), preferred_element_type=F32)


def _dot_tn(a, b):
    return lax.dot_general(a, b, (((0,), (0,)), ((), ())), preferred_element_type=F32)


def _rms(h):
    r = lax.rsqrt(jnp.mean(h * h, axis=-1, keepdims=True) + EPS)
    return h * r, r


def _rms_bwd(du, xh, r, g):
    dxh = du * g
    return r * (dxh - xh * jnp.mean(dxh * xh, axis=-1, keepdims=True))


def _sigmoid(x):
    return 1.0 / (1.0 + jnp.exp(-x))


def _colsum(x):
    return jnp.sum(x, axis=0, keepdims=True)


def _mesh_pos():
    return lax.axis_index("x"), lax.axis_index("y"), lax.axis_index("c")


def _slot(d):
    return 4 * d[0] + 2 * d[1] + d[2]


class _Comm:
    def __init__(self):
        self.srcs, self.send, self.dst, self.land, self.out_shapes = [], [], [], [], []

    def add_out(self, shape, dtype):
        self.out_shapes.append(jax.ShapeDtypeStruct(shape, dtype))
        return len(self.out_shapes) - 1

    def add(self, src, send, dst, land):
        self.srcs.append(src)
        self.send.append(send)
        self.dst.append(dst)
        self.land.append(land)

    def gather(self, block, out_shape, place):
        self.add(block, lambda ref, s: ref, self.add_out(out_shape, block.dtype), place)

    def sem_shapes(self):
        n = len(self.srcs)
        return [pltpu.SemaphoreType.DMA((7 * n,)), pltpu.SemaphoreType.DMA((7 * n,)), pltpu.SemaphoreType.DMA((n,))]

    def _copies(self, cins, couts, send_sems, recv_sems, local_sems, with_arrivals=True):
        x, y, c = _mesh_pos()
        me = (x, y, c)
        sends, arrivals, local = [], [], []
        for a in range(len(self.srcs)):
            out = couts[self.dst[a]]
            mine = self.land[a](out, _slot(me))
            local.append(pltpu.make_async_copy(self.send[a](cins[a], _slot(me)), mine, local_sems.at[a]))
            for k in range(N_DEV - 1):
                d = k + 1
                peer = (1 - x if (d >> 2) & 1 else x, 1 - y if (d >> 1) & 1 else y, 1 - c if d & 1 else c)
                sems = dict(send_sem=send_sems.at[a * 7 + k], recv_sem=recv_sems.at[a * 7 + k],
                            device_id=peer, device_id_type=MESH)
                sends.append(pltpu.make_async_remote_copy(
                    src_ref=self.send[a](cins[a], _slot(peer)), dst_ref=mine, **sems))
                if with_arrivals:
                    theirs = self.land[a](out, _slot(peer))
                    arrivals.append(pltpu.make_async_remote_copy(src_ref=theirs, dst_ref=theirs, **sems))
        return sends, arrivals, local

    def start(self, *refs):
        sends, _, local = self._copies(*refs, with_arrivals=False)
        for cp in local + sends:
            cp.start()

    def wait(self, *refs):
        sends, arrivals, local = self._copies(*refs)
        for cp in arrivals:
            cp.wait_recv()
        for cp in sends:
            cp.wait_send()
        for cp in local:
            cp.wait()


def _call(body, *, name, grid, in_specs, out_specs, out_shape, operands, scratch_shapes=(), sem=None, comm=None):
    out_specs, out_shape = list(out_specs), list(out_shape)
    if comm is None:
        res = pl.pallas_call(body, name=name, grid=grid, in_specs=list(in_specs), out_specs=out_specs,
                             out_shape=out_shape, scratch_shapes=list(scratch_shapes),
                             compiler_params=_cparams(*sem))(*operands)
        return list(res), []
    n_in, n_out, n_scr = len(in_specs), len(out_specs), len(scratch_shapes)
    nc_in, nc_out = len(comm.srcs), len(comm.out_shapes)

    def hosted(*refs):
        ins, cins = refs[:n_in], refs[n_in:n_in + nc_in]
        o0 = n_in + nc_in
        outs, couts = refs[o0:o0 + n_out], refs[o0 + n_out:o0 + n_out + nc_out]
        s0 = o0 + n_out + nc_out
        scr, sems = refs[s0:s0 + n_scr], refs[s0 + n_scr:]
        ids = [pl.program_id(a) for a in range(len(grid))]
        first = functools.reduce(operator.and_, [i == 0 for i in ids])
        last = functools.reduce(operator.and_, [i == g - 1 for i, g in zip(ids, grid)])

        @pl.when(first)
        def _():
            comm.start(cins, couts, *sems)

        body(*ins, *outs, *scr)

        @pl.when(last)
        def _():
            comm.wait(cins, couts, *sems)

    res = pl.pallas_call(
        hosted, name=name, grid=grid, in_specs=list(in_specs) + [ANY] * nc_in,
        out_specs=out_specs + [ANY] * nc_out, out_shape=out_shape + comm.out_shapes,
        scratch_shapes=list(scratch_shapes) + comm.sem_shapes(),
        compiler_params=_cparams(*(["arbitrary"] * len(grid))))(*operands, *comm.srcs)
    return list(res[:n_out]), list(res[n_out:])


def _comm_only(comm, name):
    def body(*refs):
        n, m = len(comm.srcs), len(comm.out_shapes)
        comm.start(refs[:n], refs[n:n + m], *refs[n + m:])
        comm.wait(refs[:n], refs[n:n + m], *refs[n + m:])

    return list(pl.pallas_call(body, name=name, in_specs=[ANY] * len(comm.srcs), out_specs=[ANY] * len(comm.out_shapes),
                               out_shape=comm.out_shapes, scratch_shapes=comm.sem_shapes())(*comm.srcs))


def _allgather(items, name):
    n = len(items)

    def body(*refs):
        ins, outs = refs[:n], refs[n:2 * n]
        send_sems, recv_sems, local_sems = refs[2 * n:]
        x, y, c = _mesh_pos()
        me, sib = (x, y, c), (x, y, 1 - c)
        chips = [(1 - x, y), (x, 1 - y), (1 - x, 1 - y)]

        def copy(a, k, blk, to, from_input=False):
            region = items[a][2](outs[a], _slot(blk))
            return pltpu.make_async_remote_copy(
                src_ref=ins[a] if from_input else region, dst_ref=region,
                send_sem=send_sems.at[a * 7 + k], recv_sem=recv_sems.at[a * 7 + k],
                device_id=to, device_id_type=MESH)

        local = [pltpu.make_async_copy(ins[a], items[a][2](outs[a], _slot(me)), local_sems.at[a]) for a in range(n)]
        for cp in local:
            cp.start()
        first = []
        for a in range(n):
            first.append(copy(a, 0, me, sib, True))
            for j, chip in enumerate(chips):
                first.append(copy(a, 1 + j, me, (*chip, c), True))
        for cp in first:
            cp.start()
        passed = []
        for j, chip in enumerate(chips):
            for a in range(n):
                copy(a, 1 + j, (*chip, c), me).wait_recv()
                fwd = copy(a, 4 + j, (*chip, c), sib)
                fwd.start()
                passed.append(fwd)
        for a in range(n):
            copy(a, 0, sib, me).wait_recv()
            for j, chip in enumerate(chips):
                copy(a, 4 + j, (*chip, 1 - c), me).wait_recv()
        for cp in first + passed:
            cp.wait_send()
        for cp in local:
            cp.wait()

    return pl.pallas_call(
        body, name=name,
        in_specs=[ANY] * n, out_specs=[ANY] * n,
        out_shape=[jax.ShapeDtypeStruct(it[1], it[0].dtype) for it in items],
        scratch_shapes=[pltpu.SemaphoreType.DMA((7 * n,)), pltpu.SemaphoreType.DMA((7 * n,)),
                        pltpu.SemaphoreType.DMA((n,))],
    )(*[it[0] for it in items])


def _rms_mm(h, g, w, l, ck, name, comm=None, w_is_nk=False, out_dtype=F32):
    T, D = h.shape
    N = w.shape[0] if w_is_nk else w.shape[1]
    normed = h.dtype == BF16

    def body(h_ref, g_ref, w_ref, o_ref):
        if normed:
            u = h_ref[...]
        else:
            u = (_rms(h_ref[...])[0] * g_ref[l:l + 1, :]).astype(BF16)
        res = _dot_nt(u, w_ref[...]) if w_is_nk else _dot(u, w_ref[...])
        o_ref[...] = res.astype(out_dtype)

    w_spec = pl.BlockSpec((ck, D), lambda j, i: (j, 0)) if w_is_nk else pl.BlockSpec((D, ck), lambda j, i: (0, j))
    td = TD
    return _call(
        body, name=name, grid=(N // ck, T // td),
        in_specs=[pl.BlockSpec((td, D), lambda j, i: (i, 0)), _full(g.shape), w_spec],
        out_specs=[pl.BlockSpec((td, ck), lambda j, i: (i, j))],
        out_shape=[jax.ShapeDtypeStruct((T, N), out_dtype)],
        operands=(h, g, w), sem=("parallel", "parallel"), comm=comm)


def _mm_res(res, a, w, ck, relu2, name, comm=None, norm=None):
    T, D = res.shape
    K = a.shape[-1]
    nk = K // ck

    def body(*refs):
        if norm is None:
            r_ref, a_ref, w_ref, o_ref = refs
        else:
            r_ref, a_ref, w_ref, g_ref, o_ref, u_ref = refs
        k = pl.program_id(1)

        @pl.when(k == 0)
        def _():
            o_ref[...] = r_ref[...]

        av = a_ref[...]
        if relu2:
            av = jnp.square(jnp.maximum(av.astype(F32), 0.0))
        o_ref[...] += _dot(av.astype(BF16), w_ref[...])

        if norm is not None:
            @pl.when(k == nk - 1)
            def _():
                u_ref[...] = (_rms(o_ref[...])[0] * g_ref[norm[1]:norm[1] + 1, :]).astype(BF16)

    row = pl.BlockSpec((TD, D), lambda i, k: (i, 0))
    in_specs = [row, pl.BlockSpec((TD, ck), lambda i, k: (i, k)), pl.BlockSpec((ck, D), lambda i, k: (k, 0))]
    operands = (res, a, w)
    out_specs, out_shape = [row], [jax.ShapeDtypeStruct((T, D), F32)]
    if norm is not None:
        in_specs.append(_full(norm[0].shape))
        operands += (norm[0],)
        out_specs.append(row)
        out_shape.append(jax.ShapeDtypeStruct((T, D), BF16))
    return _call(
        body, name=name, grid=(T // TD, nk), in_specs=in_specs, out_specs=out_specs, out_shape=out_shape,
        operands=operands, sem=("parallel", "arbitrary"), comm=comm)


def _final_loss(h, gf, tgt):
    T, D = h.shape

    def body(h_ref, g_ref, t_ref, dh_ref, dhb_ref, dg_ref, loss_ref):
        i = pl.program_id(0)

        @pl.when(i == 0)
        def _():
            dg_ref[...] = jnp.zeros_like(dg_ref)
            loss_ref[...] = jnp.zeros_like(loss_ref)

        g = g_ref[...]
        xh, r = _rms(h_ref[...])
        row = i * TD + lax.broadcasted_iota(jnp.int32, (TD, 1), 0)
        keep = (row >= N_META).astype(F32)
        diff = (xh * g - t_ref[...]) * keep
        part = jnp.sum(jnp.sum(diff * diff, axis=1, keepdims=True), axis=0, keepdims=True)
        loss_ref[...] += (0.5 / D) * part
        dy = diff * (1.0 / D)
        dg_ref[...] += _colsum(dy * xh)
        dh = _rms_bwd(dy, xh, r, g)
        dh_ref[...] = dh
        dhb_ref[...] = dh.astype(BF16)

    row = pl.BlockSpec((TD, D), lambda i: (i, 0))
    return pl.pallas_call(
        body, name="final_loss", grid=(T // TD,),
        in_specs=[row, _full(gf.shape), row],
        out_specs=[row, row, _full((1, D)), _full((8, 128))],
        out_shape=[jax.ShapeDtypeStruct((T, D), F32), jax.ShapeDtypeStruct((T, D), BF16),
                   jax.ShapeDtypeStruct((1, D), F32), jax.ShapeDtypeStruct((8, 128), F32)],
        compiler_params=_cparams("arbitrary"))(h, gf, tgt)


def _bwd_dz(dh, wdown, z, ck, name, comm=None):
    T, D = dh.shape
    F = z.shape[-1]

    def body(d_ref, w_ref, z_ref, o_ref):
        dact = _dot_nt(d_ref[...], w_ref[...])
        o_ref[...] = (dact * (2.0 * jnp.maximum(z_ref[...].astype(F32), 0.0))).astype(BF16)

    return _call(
        body, name=name, grid=(F // ck, T // TD),
        in_specs=[pl.BlockSpec((TD, D), lambda j, i: (i, 0)), pl.BlockSpec((ck, D), lambda j, i: (j, 0)),
                  pl.BlockSpec((TD, ck), lambda j, i: (i, j))],
        out_specs=[pl.BlockSpec((TD, ck), lambda j, i: (i, j))],
        out_shape=[jax.ShapeDtypeStruct((T, F), BF16)],
        operands=(dh, wdown, z), sem=("parallel", "parallel"), comm=comm)


def _bwd_plain(dh, w, name, comm=None):
    T, D = dh.shape
    K = w.shape[0]

    def body(d_ref, w_ref, o_ref):
        o_ref[...] = _dot_nt(d_ref[...], w_ref[...])

    return _call(
        body, name=name, grid=(T // TD_BIG,),
        in_specs=[pl.BlockSpec((TD_BIG, D), lambda i: (i, 0)), _full((K, D))],
        out_specs=[pl.BlockSpec((TD_BIG, K), lambda i: (i, 0))],
        out_shape=[jax.ShapeDtypeStruct((T, K), F32)],
        operands=(dh, w), sem=("parallel",), comm=comm)


def _bwd_rms(dx, w, h, g, dh_in, l, ck, name, comm=None, w_is_kd=False, emit_bf16=True):
    T, D = h.shape
    K = dx.shape[-1]
    nk = K // ck
    mode = dict(pipeline_mode=pl.Buffered(1)) if nk == 1 else {}
    w_spec = (pl.BlockSpec((ck, D), lambda i, k: (k, 0), **mode) if w_is_kd
              else pl.BlockSpec((D, ck), lambda i, k: (0, k), **mode))

    def body(x_ref, w_ref, h_ref, g_ref, di_ref, *rest):
        dh_ref, dg_ref = rest[0], rest[2 if emit_bf16 else 1]
        i, k = pl.program_id(0), pl.program_id(1)

        @pl.when((i == 0) & (k == 0))
        def _():
            dg_ref[...] = jnp.zeros_like(dg_ref)

        part = _dot(x_ref[...], w_ref[...]) if w_is_kd else _dot_nt(x_ref[...], w_ref[...])

        def finish(du):
            xh, r = _rms(h_ref[...])
            dg_ref[...] += _colsum(du * xh)
            dh = di_ref[...] + _rms_bwd(du, xh, r, g_ref[l:l + 1, :])
            dh_ref[...] = dh
            if emit_bf16:
                rest[1][...] = dh.astype(BF16)

        if nk == 1:
            finish(part)
        else:
            acc_ref = rest[-1]

            @pl.when(k == 0)
            def _():
                acc_ref[...] = part

            @pl.when(k > 0)
            def _():
                acc_ref[...] += part

            @pl.when(k == nk - 1)
            def _():
                finish(acc_ref[...])

    td = TT if ck > 2048 else TD
    row = pl.BlockSpec((td, D), lambda i, k: (i, 0))
    out_specs = [row] + ([row] if emit_bf16 else []) + [_full((1, D))]
    out_shape = ([jax.ShapeDtypeStruct((T, D), F32)] + ([jax.ShapeDtypeStruct((T, D), BF16)] if emit_bf16 else [])
                 + [jax.ShapeDtypeStruct((1, D), F32)])
    return _call(
        body, name=name, grid=(T // td, nk),
        in_specs=[pl.BlockSpec((td, ck), lambda i, k: (i, k)), w_spec, row, _full(g.shape), row],
        out_specs=out_specs, out_shape=out_shape,
        scratch_shapes=[pltpu.VMEM((td, D), F32)] if nk > 1 else [],
        operands=(dx, w, h, g, dh_in), sem=("arbitrary", "arbitrary"), comm=comm)


def _mm_tn(a, b, relu2, rms_on, g, l, chunk, ck, out_dtype, name, comm=None):
    T = a.shape[0]
    Ka, Nb = a.shape[1], b.shape[1]
    ca, cb = (ck, Nb) if chunk == "a" else (Ka, ck)
    nj = (Ka // ck) if chunk == "a" else (Nb // ck)
    td = TD_BIG if a.dtype == BF16 and b.dtype == BF16 else TD
    nt = T // td

    def body(*refs):
        if rms_on is not None:
            a_ref, g_ref, b_ref, o_ref, acc_ref = refs
        else:
            a_ref, b_ref, o_ref, acc_ref = refs
        t = pl.program_id(1)

        @pl.when(t == 0)
        def _():
            acc_ref[...] = jnp.zeros_like(acc_ref)

        av, bv = a_ref[...], b_ref[...]
        if relu2:
            av = jnp.square(jnp.maximum(av.astype(F32), 0.0))
        if rms_on == "a":
            av = _rms(av)[0] * g_ref[l:l + 1, :]
        elif rms_on == "b":
            bv = _rms(bv)[0] * g_ref[l:l + 1, :]
        acc_ref[...] += _dot_tn(av.astype(BF16), bv.astype(BF16))

        @pl.when(t == nt - 1)
        def _():
            o_ref[...] = acc_ref[...].astype(out_dtype)

    if chunk == "a":
        a_spec = pl.BlockSpec((td, ca), lambda j, t: (t, j))
        b_spec = pl.BlockSpec((td, cb), lambda j, t: (t, 0))
        o_spec = pl.BlockSpec((ca, cb), lambda j, t: (j, 0))
    else:
        a_spec = pl.BlockSpec((td, ca), lambda j, t: (t, 0))
        b_spec = pl.BlockSpec((td, cb), lambda j, t: (t, j))
        o_spec = pl.BlockSpec((ca, cb), lambda j, t: (0, j))
    ins, specs = [a], [a_spec]
    if rms_on is not None:
        ins.append(g)
        specs.append(_full(g.shape))
    ins.append(b)
    specs.append(b_spec)
    return _call(
        body, name=name, grid=(nj, nt), in_specs=specs, out_specs=[o_spec],
        out_shape=[jax.ShapeDtypeStruct((Ka, Nb), out_dtype)], scratch_shapes=[pltpu.VMEM((ca, cb), F32)],
        operands=tuple(ins), sem=("parallel", "arbitrary"), comm=comm)


def _shift_down(x, k):
    return x if k == 0 else pltpu.roll(x, k, 0)


def _shift_up(x, k):
    return x if k == 0 else pltpu.roll(x, x.shape[0] - k, 0)


def _pool_sel(lane_grp, vals):
    return jnp.where(lane_grp == 0, vals[0], jnp.where(lane_grp == 1, vals[1], jnp.where(lane_grp == 2, vals[2], vals[3])))


def _pool_fwd(ue, pos):
    s = ue
    sums = []
    for k in (1, 2, 4, 8):
        s = s + _shift_down(s, k)
        sums.append(s[HALO:])
    grp = lax.broadcasted_iota(jnp.int32, (1, D_POOL), 1) // POOL_GW
    wsum = _pool_sel(grp, sums)
    width = _pool_sel(grp, [jnp.float32(w) for w in POOL_WINDOWS])
    cnt = jnp.minimum(pos.astype(F32), width)
    return wsum / cnt - ue[HALO:], cnt, grp


def _conv_fwd(xe, w_ref, nk):
    acc = None
    for k in range(nk):
        term = _shift_down(xe, nk - 1 - k)[HALO:] * w_ref(k)
        acc = term if acc is None else acc + term
    return acc


def _taps(window, w_ref, nk, shift):
    acc = None
    for r in range(min(8, nk)):
        inner = None
        for q in range((nk - 1 - r) // 8 + 1):
            term = window(q) * w_ref(nk - 1 - 8 * q - r)
            inner = term if inner is None else inner + term
        inner = shift(inner, r)
        acc = inner if acc is None else acc + inner
    return acc


def _layernorm(cb):
    mu = jnp.mean(cb, axis=-1, keepdims=True)
    xc = cb - mu
    rstd = lax.rsqrt(jnp.mean(xc * xc, axis=-1, keepdims=True) + EPS)
    return xc * rstd, rstd


def _softplus_neg(lam):
    x = -lam
    e = jnp.exp(-jnp.abs(x))
    u = 1.0 + e
    d = u - 1.0
    log1p = jnp.where(d == 0.0, e, jnp.log(u) * (e / jnp.where(d == 0.0, 1.0, d)))
    return jnp.maximum(x, 0.0) + log1p


def _expm1(x):
    small = x * (1.0 + x * 0.5 * (1.0 + x * (1.0 / 3.0) * (1.0 + x * 0.25)))
    return jnp.where(jnp.abs(x) < 1e-2, small, jnp.exp(x) - 1.0)


def _gelu_parts(x):
    c0 = 0.7978845608028654
    inner = c0 * (x + 0.044715 * x * x * x)
    th = jnp.tanh(inner)
    gelu = 0.5 * x * (1.0 + th)
    dgelu = 0.5 * (1.0 + th) + 0.5 * x * (1.0 - th * th) * c0 * (1.0 + 3.0 * 0.044715 * x * x)
    return gelu, dgelu


def _rg_gates(xc, wa_ref, wx_ref, v512_ref):
    xb = xc.astype(BF16)
    ra = _sigmoid(_dot(xb, wa_ref[...]) + v512_ref[1:2, :])
    ri = _sigmoid(_dot(xb, wx_ref[...]) + v512_ref[2:3, :])
    sp = _softplus_neg(v512_ref[3:4, :])
    log_a = (-RG_C) * ra * sp
    a = jnp.exp(log_a)
    mult = jnp.sqrt(-_expm1(2.0 * log_a))
    return ra, ri, sp, a, mult


def _scan_down(a, b):
    n = a.shape[0]
    row = lax.broadcasted_iota(jnp.int32, (n, 1), 0)
    k = 1
    while k < n:
        ok = row >= k
        b = a * jnp.where(ok, _shift_down(b, k), 0.0) + b
        a = a * jnp.where(ok, _shift_down(a, k), 1.0)
        k *= 2
    return b, a


def _scan_up(c, g):
    n = c.shape[0]
    row = lax.broadcasted_iota(jnp.int32, (n, 1), 0)
    k = 1
    while k < n:
        ok = row < n - k
        g = c * jnp.where(ok, _shift_up(g, k), 0.0) + g
        c = c * jnp.where(ok, _shift_up(c, k), 1.0)
        k *= 2
    return g, c


def _mixer_specs(l, tile_of):
    r = TT // HALO
    return [
        pl.BlockSpec((TT, 1792), lambda i: (tile_of(i), 0)),
        pl.BlockSpec((HALO, 1792), lambda i: (jnp.maximum(tile_of(i) * r - 1, 0), 0)),
        _layer((256, 256), l), _layer((8, 256), l), _layer((32, 256), l), _layer((256, 256), l),
        _layer((512, 512), l), _layer((512, 512), l), _layer((8, 512), l),
    ]


def _mixer_fwd(p, wts, l, comm=None):
    T = p.shape[0]

    def body(p_ref, ph_ref, wpool_ref, v256_ref, dww_ref, wpw_ref, wa_ref, wx_ref, v512_ref, y_ref, hr_ref, cb_ref,
             rg_ref, carry_ref):
        i = pl.program_id(0)

        @pl.when(i == 0)
        def _():
            carry_ref[...] = jnp.zeros_like(carry_ref)

        halo = jnp.where(i == 0, 0.0, ph_ref[...])
        e = jnp.concatenate([halo, p_ref[...]], axis=0)
        pos = i * TT + 1 + lax.broadcasted_iota(jnp.int32, (TT, 1), 0)
        pooled, _, _ = _pool_fwd(e[:, 0:256], pos)
        y_a = _dot(pooled.astype(BF16), wpool_ref[...]) * v256_ref[0:1, :]
        ub = e[:, 256:512] * _sigmoid(e[:, 512:768])
        cb = _conv_fwd(ub, lambda k: dww_ref[k:k + 1, :], CONV_K) + v256_ref[1:2, :]
        cb_ref[...] = cb
        xhat, _ = _layernorm(cb)
        ln = xhat * v256_ref[2:3, :] + v256_ref[3:4, :]
        s = ln * _sigmoid(ln)
        y_b = _dot(s.astype(BF16), wpw_ref[...])
        xc = _conv_fwd(e[:, 1280:1792], lambda k: v512_ref[4 + k:5 + k, :], RG_CONV_K) + v512_ref[0:1, :]
        ra, ri, _, a, mult = _rg_gates(xc, wa_ref, wx_ref, v512_ref)
        for n, val in enumerate((xc, ra, ri, a, mult)):
            rg_ref[:, n * D_RNN:(n + 1) * D_RNN] = val
        hloc, acum = _scan_down(a, mult * ri * xc)
        hr = hloc + acum * carry_ref[0:1, :]
        hr_ref[...] = hr
        carry_ref[0:1, :] = hr_ref[TT - 1:TT, :]
        gelu, _ = _gelu_parts(p_ref[:, 768:1280])
        y_ref[:, 0:256] = y_a.astype(BF16)
        y_ref[:, 256:512] = y_b.astype(BF16)
        y_ref[:, 512:1024] = (gelu * hr).astype(BF16)

    return _call(
        body, name=f"mixer_fwd{l}", grid=(T // TT,),
        in_specs=_mixer_specs(l, lambda i: i),
        out_specs=[pl.BlockSpec((TT, 1024), lambda i: (i, 0)), pl.BlockSpec((TT, 512), lambda i: (i, 0)),
                   pl.BlockSpec((TT, 256), lambda i: (i, 0)), pl.BlockSpec((TT, 5 * D_RNN), lambda i: (i, 0))],
        out_shape=[jax.ShapeDtypeStruct((T, 1024), BF16), jax.ShapeDtypeStruct((T, 512), F32),
                   jax.ShapeDtypeStruct((T, 256), F32), jax.ShapeDtypeStruct((T, 5 * D_RNN), F32)],
        scratch_shapes=[pltpu.VMEM((8, 512), F32)],
        operands=(p, p, *wts), sem=("arbitrary",), comm=comm)


def _mixer_bwd(p, hr, cb, rg, dy, wts, l, comm=None):
    T = p.shape[0]
    nt = T // TT
    tile_of = lambda i: nt - 1 - i

    def body(p_ref, ph_ref, wpool_ref, v256_ref, dww_ref, wpw_ref, wa_ref, wx_ref, v512_ref, hr_ref, hrh_ref, cb_ref,
             rg_ref, dy_ref, dp_ref, g256_ref, g512_ref, gpool_ref, gpw_ref, gwa_ref, gwx_ref,
             q_c, dcb_c, dxc_c, ag_c, ub_s, ed_s):
        i = pl.program_id(0)
        j = nt - 1 - i

        @pl.when(i == 0)
        def _():
            for ref in (g256_ref, g512_ref, gpool_ref, gpw_ref, gwa_ref, gwx_ref, q_c, dcb_c, dxc_c, ag_c):
                ref[...] = jnp.zeros_like(ref)

        halo = jnp.where(j == 0, 0.0, ph_ref[...])
        e = jnp.concatenate([halo, p_ref[...]], axis=0)
        pos = j * TT + 1 + lax.broadcasted_iota(jnp.int32, (TT, 1), 0)
        row = lax.broadcasted_iota(jnp.int32, (TT, 1), 0)

        dy_a = dy_ref[:, 0:256]
        pooled, cnt, grp = _pool_fwd(e[:, 0:256], pos)
        scale = v256_ref[0:1, :]
        mixed = _dot(pooled.astype(BF16), wpool_ref[...])
        g256_ref[34:35, :] += _colsum(dy_a * mixed)
        dmixed = (dy_a * scale).astype(BF16)
        gpool_ref[...] += _dot_tn(pooled.astype(BF16), dmixed)
        dpooled = _dot_nt(dmixed, wpool_ref[...])
        q = dpooled / cnt
        s = jnp.concatenate([q, q_c[...]], axis=0)
        sums = []
        for k in (1, 2, 4, 8):
            s = s + _shift_up(s, k)
            sums.append(s[:TT])
        dp_ref[:, 0:256] = (_pool_sel(grp, sums) - dpooled).astype(BF16)
        q_c[...] = q[0:16]

        dy_b = dy_ref[:, 256:512].astype(BF16)
        sg = _sigmoid(e[:, 512:768])
        ub_s[...] = e[:, 256:512] * sg
        xhat, rstd = _layernorm(cb_ref[...])
        ln_g = v256_ref[2:3, :]
        ln = xhat * ln_g + v256_ref[3:4, :]
        sig = _sigmoid(ln)
        sact = ln * sig
        gpw_ref[...] += _dot_tn(sact.astype(BF16), dy_b)
        dln = _dot_nt(dy_b, wpw_ref[...]) * (sig * (1.0 + ln * (1.0 - sig)))
        g256_ref[32:33, :] += _colsum(dln * xhat)
        g256_ref[33:34, :] += _colsum(dln)
        dxhat = dln * ln_g
        dcb = rstd * (dxhat - jnp.mean(dxhat, axis=-1, keepdims=True)
                      - xhat * jnp.mean(dxhat * xhat, axis=-1, keepdims=True))
        g256_ref[31:32, :] += _colsum(dcb)
        pad8 = jnp.zeros((8, D_CONV), F32)
        dcb_ext = jnp.concatenate([pad8, dcb, pad8], axis=0)
        for r in range(8):
            d_r = _shift_up(dcb_ext, r)[0:TT + 8]
            for q in range((CONV_K - 1 - r) // 8 + 1):
                k = CONV_K - 1 - 8 * q - r
                g256_ref[k:k + 1, :] += _colsum(d_r * ub_s[pl.ds(HALO - 8 - 8 * q, TT + 8), :])
        ed_s[0:TT, :] = dcb
        ed_s[TT:TT + 32, :] = dcb_c[...]
        dub = _taps(lambda q: ed_s[pl.ds(8 * q, TT + 8), :], lambda k: dww_ref[k:k + 1, :], CONV_K, _shift_up)[:TT]
        dcb_c[...] = dcb[0:32]
        sg_t = sg[HALO:]
        v_t = p_ref[:, 256:512]
        dp_ref[:, 256:512] = (dub * sg_t).astype(BF16)
        dp_ref[:, 512:768] = (dub * v_t * sg_t * (1.0 - sg_t)).astype(BF16)

        dy_c = dy_ref[:, 512:1024]
        cxe = e[:, 1280:1792]
        xc, ra, ri, a, mult = (rg_ref[:, n * D_RNN:(n + 1) * D_RNN] for n in range(5))
        sp = _softplus_neg(v512_ref[3:4, :])
        hrv = hr_ref[...]
        gelu, dgelu = _gelu_parts(p_ref[:, 768:1280])
        dp_ref[:, 768:1280] = (dy_c * hrv * dgelu).astype(BF16)
        dhr = dy_c * gelu
        coef = jnp.where(row < TT - 1, _shift_up(a, 1), 1.0)
        gloc, ccum = _scan_up(coef, dhr)
        gg = gloc + ccum * ag_c[0:1, :]
        ag_c[...] = (a * gg)[0:8]
        hr_before = jnp.where(j == 0, 0.0, hrh_ref[7:8, :])
        hr_prev = jnp.where(row >= 1, _shift_down(hrv, 1), hr_before)
        da = gg * hr_prev
        dmult = gg * ri * xc
        dri = gg * mult * xc
        dxc = gg * mult * ri
        dlog_a = da * a - dmult * (a * a / mult)
        g512_ref[3:4, :] += _colsum(dlog_a * ((-RG_C) * ra))
        dpre_a = dlog_a * ((-RG_C) * sp) * ra * (1.0 - ra)
        dpre_x = dri * ri * (1.0 - ri)
        g512_ref[1:2, :] += _colsum(dpre_a)
        g512_ref[2:3, :] += _colsum(dpre_x)
        xb = xc.astype(BF16)
        dpa_b, dpx_b = dpre_a.astype(BF16), dpre_x.astype(BF16)
        gwa_ref[...] += _dot_tn(xb, dpa_b)
        gwx_ref[...] += _dot_tn(xb, dpx_b)
        dxc = dxc + _dot_nt(dpa_b, wa_ref[...]) + _dot_nt(dpx_b, wx_ref[...])
        g512_ref[0:1, :] += _colsum(dxc)
        for k in range(RG_CONV_K):
            g512_ref[4 + k:5 + k, :] += _colsum(dxc * _shift_down(cxe, RG_CONV_K - 1 - k)[HALO:])
        ex = jnp.concatenate([dxc, dxc_c[...]], axis=0)
        dcx = _taps(lambda q: ex, lambda k: v512_ref[4 + k:5 + k, :], RG_CONV_K, _shift_up)[:TT]
        dxc_c[...] = dxc[0:8]
        dp_ref[:, 1280:1792] = dcx.astype(BF16)

        @pl.when(i == nt - 1)
        def _():
            lam = v512_ref[3:4, :]
            g512_ref[3:4, :] = g512_ref[3:4, :] * (-_sigmoid(-lam))

    in_specs = _mixer_specs(l, tile_of) + [
        pl.BlockSpec((TT, 512), lambda i: (tile_of(i), 0)),
        pl.BlockSpec((8, 512), lambda i: (jnp.maximum(tile_of(i) * (TT // 8) - 1, 0), 0)),
        pl.BlockSpec((TT, 256), lambda i: (tile_of(i), 0)),
        pl.BlockSpec((TT, 5 * D_RNN), lambda i: (tile_of(i), 0)),
        pl.BlockSpec((TT, 1024), lambda i: (tile_of(i), 0)),
    ]
    acc_shapes = [(40, 256), (8, 512), (256, 256), (256, 256), (512, 512), (512, 512)]
    return _call(
        body, name=f"mixer_bwd{l}", grid=(nt,),
        in_specs=in_specs,
        out_specs=[pl.BlockSpec((TT, 1792), lambda i: (tile_of(i), 0))] + [_full(s) for s in acc_shapes],
        out_shape=[jax.ShapeDtypeStruct((T, 1792), BF16)] + [jax.ShapeDtypeStruct(s, F32) for s in acc_shapes],
        scratch_shapes=[pltpu.VMEM((16, 256), F32), pltpu.VMEM((32, 256), F32), pltpu.VMEM((8, 512), F32),
                        pltpu.VMEM((8, 512), F32), pltpu.VMEM((HALO + TT, D_CONV), F32),
                        pltpu.VMEM((TT + 32, D_CONV), F32)],
        operands=(p, p, *wts, hr, hr, cb, rg, dy), sem=("arbitrary",), comm=comm)


def _adam(w, g, m, v):
    m = ADAM_B1 * m + (1.0 - ADAM_B1) * g
    v = ADAM_B2 * v + (1.0 - ADAM_B2) * (g * g)
    m_hat = m / (1.0 - ADAM_B1 ** ADAM_STEP)
    v_hat = v / (1.0 - ADAM_B2 ** ADAM_STEP)
    delta = -ADAM_LR * (m_hat / (jnp.sqrt(v_hat) + ADAM_EPS) + ADAM_WD * w)
    return delta, m, v


def _adamw_big(recvs, w, m, v, tr, name):
    L, R, C = w.shape

    def body(r0_ref, r1_ref, w_ref, m_ref, v_ref, g_ref, d_ref, mo_ref, vo_ref):
        l = pl.program_id(0)

        def total(r_ref):
            g = r_ref[0].astype(F32)
            for s in range(1, N_DEV):
                g = g + r_ref[s].astype(F32)
            g_ref[...] = g

        @pl.when(l == 0)
        def _():
            total(r0_ref)

        @pl.when(l == 1)
        def _():
            total(r1_ref)

        d_ref[...], mo_ref[...], vo_ref[...] = _adam(w_ref[...], g_ref[...], m_ref[...], v_ref[...])

    blk = pl.BlockSpec((None, tr, C), lambda l, i: (l, i, 0))
    return pl.pallas_call(
        body, name=name, grid=(L, R // tr),
        in_specs=[pl.BlockSpec((N_DEV, tr, C), lambda l, i: (0, i * (1 - l), 0)),
                  pl.BlockSpec((N_DEV, tr, C), lambda l, i: (0, i * l, 0)), blk, blk, blk],
        out_specs=[blk] * 4,
        out_shape=[jax.ShapeDtypeStruct((L, R, C), F32)] * 4,
        compiler_params=_cparams("arbitrary", "arbitrary"))(recvs[0], recvs[1], w, m, v)


def _sum_slots(recvs, name):
    L = len(recvs)
    _, R, C = recvs[0].shape

    def body(*refs):
        o_ref = refs[L]
        for l in range(L):
            acc = refs[l][0].astype(F32)
            for s in range(1, N_DEV):
                acc = acc + refs[l][s].astype(F32)
            o_ref[l] = acc

    return pl.pallas_call(body, name=name, out_shape=jax.ShapeDtypeStruct((L, R, C), F32))(*recvs)


def _adamw_plain(gs, ws, ms, vs, name):
    n = len(gs)

    def body(*refs):
        g_r, w_r, m_r, v_r = refs[:n], refs[n:2 * n], refs[2 * n:3 * n], refs[3 * n:4 * n]
        outs = refs[4 * n:]
        for a in range(n):
            d, mo, vo = _adam(w_r[a][...], g_r[a][...], m_r[a][...], v_r[a][...])
            outs[3 * a][...] = d
            outs[3 * a + 1][...] = mo
            outs[3 * a + 2][...] = vo

    shapes = []
    for a in range(n):
        shapes += [jax.ShapeDtypeStruct(ws[a].shape, F32)] * 3
    return pl.pallas_call(body, name=name, out_shape=shapes,
                          compiler_params=pltpu.CompilerParams(vmem_limit_bytes=VMEM_LIMIT))(*gs, *ws, *ms, *vs)


_V256_ROWS = {"convb_dw_b": 31, "convb_ln_g": 32, "convb_ln_b": 33, "pool_scale": 34}
_V512_ROWS = {"rg_conv_b": 0, "rg_b_a": 1, "rg_b_x": 2, "rg_lambda": 3}
_SMALL_REPL = ["mix_norm_g", "mlp_norm_g", "final_norm_g", "pool_w", "pool_scale", "convb_dw_b", "convb_ln_g",
               "convb_ln_b", "rg_conv_b", "rg_w_a", "rg_b_a", "rg_w_x", "rg_b_x", "rg_lambda"]


def _adamw_small(gath1, gath0, gath_tail, ws, ms, vs):
    n = len(_SMALL_REPL)
    n1, n0, nt = len(gath1), len(gath0), len(gath_tail)
    ng = n1 + n0 + nt
    L = DEPTH
    per_layer = [g.shape[1:] for g in gath1[:8]]
    sum_shapes = ([(L,) + s[1:] for s in per_layer[:2]] + [gath1[8].shape[1:]] + [(L,) + s for s in per_layer[2:]]
                  + [gath_tail[1].shape[1:], gath1[9].shape[1:]])

    def body(*refs):
        in1, in0, in_t = refs[:n1], refs[n1:n1 + n0], refs[n1 + n0:ng]
        w_r, m_r, v_r = refs[ng:ng + n], refs[ng + n:ng + 2 * n], refs[ng + 2 * n:ng + 3 * n]
        sums = refs[ng + 3 * n:ng + 3 * n + 10]
        outs = refs[ng + 3 * n + 11:]

        def total(ref):
            acc = ref[0]
            for s in range(1, N_DEV):
                acc = acc + ref[s]
            return acc

        refs[ng + 3 * n + 10][...] = total(in1[9])
        s_mix, s_mlp, s_fin, s_pool, s_256, s_512, s_pw, s_wa, s_wx, s_meta = sums
        s_mix[1:2, :] = total(in1[0])
        s_mix[0:1, :] = total(in_t[0])
        s_mlp[1:2, :] = total(in1[1])
        s_mlp[0:1, :] = total(in0[0])
        s_fin[...] = total(in1[8])
        s_meta[...] = total(in_t[1])
        for k, dst in enumerate((s_pool, s_256, s_512, s_pw, s_wa, s_wx)):
            dst[1] = total(in1[2 + k])
            dst[0] = total(in0[1 + k])

        def grad_of(name, idx):
            if name == "mix_norm_g":
                return s_mix[idx[0]:idx[0] + 1, :]
            if name == "mlp_norm_g":
                return s_mlp[idx[0]:idx[0] + 1, :]
            if name == "final_norm_g":
                return s_fin[...]
            if name in _V256_ROWS:
                r = _V256_ROWS[name]
                return s_256[idx[0], r:r + 1, :]
            if name in _V512_ROWS:
                r = _V512_ROWS[name]
                return s_512[idx[0], r:r + 1, :]
            src = {"pool_w": s_pool, "rg_w_a": s_wa, "rg_w_x": s_wx}[name]
            return src[idx[0], :, idx[1] * 64:(idx[1] + 1) * 64]

        for a, name in enumerate(_SMALL_REPL):
            shape = w_r[a].shape
            if name == "final_norm_g":
                parts = [((), (slice(None), slice(None)))]
            elif len(shape) == 2:
                parts = [((l,), (slice(l, l + 1), slice(None))) for l in range(shape[0])]
            else:
                parts = [((l, h), (l, h)) for l in range(shape[0]) for h in range(shape[1])]
            for idx, sel in parts:
                g = grad_of(name, idx)
                d, mo, vo = _adam(w_r[a][sel], g, m_r[a][sel], v_r[a][sel])
                outs[4 * a][sel] = g
                outs[4 * a + 1][sel] = d
                outs[4 * a + 2][sel] = mo
                outs[4 * a + 3][sel] = vo

    out_shape = [jax.ShapeDtypeStruct(s, F32) for s in sum_shapes]
    for a in range(n):
        out_shape += [jax.ShapeDtypeStruct(ws[a].shape, F32)] * 4
    res = pl.pallas_call(body, name="adamw_small", out_shape=out_shape,
                         compiler_params=pltpu.CompilerParams(vmem_limit_bytes=VMEM_LIMIT))(
                             *gath1, *gath0, *gath_tail, *ws, *ms, *vs)
    return res[:11], res[11:]


_WEIGHTS = ['meta_tokens', 'mix_norm_g', 'w_in', 'pool_w', 'pool_scale', 'convb_dw_w', 'convb_dw_b', 'convb_ln_g',
            'convb_ln_b', 'convb_pw_w', 'rg_conv_w', 'rg_conv_b', 'rg_w_a', 'rg_b_a', 'rg_w_x', 'rg_b_x', 'rg_lambda',
            'w_out', 'mlp_norm_g', 'w_up', 'w_down', 'final_norm_g']


def _block_diag(w):
    L, H, C, _ = w.shape
    eye = jnp.eye(H, dtype=w.dtype)
    return (w[:, :, :, None, :] * eye[None, :, None, :, None]).reshape(L, H * C, H * C)


def _diag_blocks(m, H):
    C = m.shape[0] // H
    return jnp.concatenate([m[h * C:(h + 1) * C, h * C:(h + 1) * C] for h in range(H)], axis=1)


def kernel(x, meta_tokens, mix_norm_g, w_in, pool_w, pool_scale, convb_dw_w, convb_dw_b, convb_ln_g, convb_ln_b, convb_pw_w, rg_conv_w, rg_conv_b, rg_w_a, rg_b_a, rg_w_x, rg_b_x, rg_lambda, w_out, mlp_norm_g, w_up, w_down, final_norm_g, loss_target, m_meta_tokens, m_mix_norm_g, m_w_in, m_pool_w, m_pool_scale, m_convb_dw_w, m_convb_dw_b, m_convb_ln_g, m_convb_ln_b, m_convb_pw_w, m_rg_conv_w, m_rg_conv_b, m_rg_w_a, m_rg_b_a, m_rg_w_x, m_rg_b_x, m_rg_lambda, m_w_out, m_mlp_norm_g, m_w_up, m_w_down, m_final_norm_g, v_meta_tokens, v_mix_norm_g, v_w_in, v_pool_w, v_pool_scale, v_convb_dw_w, v_convb_dw_b, v_convb_ln_g, v_convb_ln_b, v_convb_pw_w, v_rg_conv_w, v_rg_conv_b, v_rg_w_a, v_rg_b_a, v_rg_w_x, v_rg_b_x, v_rg_lambda, v_w_out, v_mlp_norm_g, v_w_up, v_w_down, v_final_norm_g):
    W = dict(meta_tokens=meta_tokens, mix_norm_g=mix_norm_g, w_in=w_in, pool_w=pool_w, pool_scale=pool_scale,
             convb_dw_w=convb_dw_w, convb_dw_b=convb_dw_b, convb_ln_g=convb_ln_g, convb_ln_b=convb_ln_b,
             convb_pw_w=convb_pw_w, rg_conv_w=rg_conv_w, rg_conv_b=rg_conv_b, rg_w_a=rg_w_a, rg_b_a=rg_b_a,
             rg_w_x=rg_w_x, rg_b_x=rg_b_x, rg_lambda=rg_lambda, w_out=w_out, mlp_norm_g=mlp_norm_g, w_up=w_up,
             w_down=w_down, final_norm_g=final_norm_g.reshape(1, -1))
    M = dict(meta_tokens=m_meta_tokens, mix_norm_g=m_mix_norm_g, w_in=m_w_in, pool_w=m_pool_w, pool_scale=m_pool_scale,
             convb_dw_w=m_convb_dw_w, convb_dw_b=m_convb_dw_b, convb_ln_g=m_convb_ln_g, convb_ln_b=m_convb_ln_b,
             convb_pw_w=m_convb_pw_w, rg_conv_w=m_rg_conv_w, rg_conv_b=m_rg_conv_b, rg_w_a=m_rg_w_a, rg_b_a=m_rg_b_a,
             rg_w_x=m_rg_w_x, rg_b_x=m_rg_b_x, rg_lambda=m_rg_lambda, w_out=m_w_out, mlp_norm_g=m_mlp_norm_g,
             w_up=m_w_up, w_down=m_w_down, final_norm_g=m_final_norm_g.reshape(1, -1))
    V = dict(meta_tokens=v_meta_tokens, mix_norm_g=v_mix_norm_g, w_in=v_w_in, pool_w=v_pool_w, pool_scale=v_pool_scale,
             convb_dw_w=v_convb_dw_w, convb_dw_b=v_convb_dw_b, convb_ln_g=v_convb_ln_g, convb_ln_b=v_convb_ln_b,
             convb_pw_w=v_convb_pw_w, rg_conv_w=v_rg_conv_w, rg_conv_b=v_rg_conv_b, rg_w_a=v_rg_w_a, rg_b_a=v_rg_b_a,
             rg_w_x=v_rg_w_x, rg_b_x=v_rg_b_x, rg_lambda=v_rg_lambda, w_out=v_w_out, mlp_norm_g=v_mlp_norm_g,
             w_up=v_w_up, w_down=v_w_down, final_norm_g=v_final_norm_g.reshape(1, -1))

    xs = x[0]
    S, D = xs.shape
    T = S + N_META
    assert T % TT == 0 and D == 1024
    L = DEPTH
    me = 4 * lax.axis_index("x") + 2 * lax.axis_index("y") + lax.axis_index("c")
    c_in, c_ff = w_in.shape[-1], w_up.shape[-1]
    d_in, d_ff = c_in * N_DEV, c_ff * N_DEV
    r_out, r_dn = w_out.shape[1], w_down.shape[1]
    by_slot = lambda ref, s: ref.at[s]
    rows_of = lambda n: (lambda ref, s: ref.at[pl.ds(s * n, n), :])
    cols_of = lambda n: (lambda ref, s: ref.at[:, pl.ds(s * n, n)])

    win_t = jnp.transpose(w_in, (0, 2, 1)).astype(BF16)
    gathered = _allgather([
        (win_t[0], (d_in, D), rows_of(c_in)),
        (meta_tokens, (N_DEV,) + meta_tokens.shape, by_slot),
        (convb_dw_w, (N_DEV,) + convb_dw_w.shape, by_slot),
        (convb_pw_w, (N_DEV,) + convb_pw_w.shape, by_slot),
        (rg_conv_w, (N_DEV,) + rg_conv_w.shape, by_slot),
    ], "gather_first")
    win_f = [gathered[0], None]
    meta_f = jnp.transpose(gathered[1], (1, 0, 2)).reshape(N_META, D)
    dww_f = jnp.transpose(gathered[2], (1, 2, 0, 3)).reshape(L, CONV_K, D_CONV)
    wpw_f = jnp.transpose(gathered[3], (1, 0, 2, 3)).reshape(L, D_CONV, D_CONV).astype(BF16)
    rgw_f = jnp.transpose(gathered[4], (1, 2, 0, 3)).reshape(L, RG_CONV_K, D_RNN)
    wout_b, wup_b, wdown_b = w_out.astype(BF16), w_up.astype(BF16), w_down.astype(BF16)

    def weight_gather(blocks):
        comm = _Comm()
        for blk, shape, place in blocks:
            comm.gather(blk, shape, place)
        return comm

    wpool_bd = _block_diag(pool_w).astype(BF16)
    wa_bd = _block_diag(rg_w_a).astype(BF16)
    wx_bd = _block_diag(rg_w_x).astype(BF16)
    zeros4 = jnp.zeros((L, 4, D_POOL), F32)
    v256 = jnp.concatenate([pool_scale[:, None], convb_dw_b[:, None], convb_ln_g[:, None], convb_ln_b[:, None], zeros4], axis=1)
    dww_p = jnp.concatenate([dww_f, jnp.zeros((L, 1, D_CONV), F32)], axis=1)
    v512 = jnp.concatenate([rg_conv_b[:, None], rg_b_a[:, None], rg_b_x[:, None], rg_lambda[:, None], rgw_f], axis=1)
    mix_w = (wpool_bd, v256, dww_p, wpw_f, wa_bd, wx_bd, v512)

    h = jnp.concatenate([meta_f, xs], axis=0)
    tgt = jnp.concatenate([jnp.zeros((N_META, D), F32), loss_target[0]], axis=0)
    wout_f, wup_f, wdown_f = [None] * L, [None] * L, [None] * L
    saved = []

    (p,), wout_f = _rms_mm(h, mix_norm_g, win_f[0], 0, d_in, "in_proj0",
                           weight_gather([(wout_b[l], (r_out * N_DEV, D), rows_of(r_out)) for l in range(L)]),
                           w_is_nk=True)
    (y, *kept), (wup_f[0],) = _mixer_fwd(p, mix_w, 0, weight_gather([(wup_b[0], (D, d_ff), cols_of(c_ff))]))
    (h2, u2), (win_f[1],) = _mm_res(h, y, wout_f[0], y.shape[-1], False, "out_proj0",
                                    weight_gather([(win_t[1], (d_in, D), rows_of(c_in))]), norm=(mlp_norm_g, 0))
    (z,), (wdown_f[0],) = _rms_mm(u2, mlp_norm_g, wup_f[0], 0, 2048, "mlp_up0",
                                  weight_gather([(wdown_b[0], (d_ff, D), rows_of(r_dn))]), out_dtype=BF16)
    (h3, u1), (wup_f[1],) = _mm_res(h2, z, wdown_f[0], 2048, True, "mlp_down0",
                                    weight_gather([(wup_b[1], (D, d_ff), cols_of(c_ff))]), norm=(mix_norm_g, 1))
    saved.append((h, None, p, y, kept, h2, u2, z))
    h = h3
    (p,), _ = _rms_mm(u1, mix_norm_g, win_f[1], 1, d_in, "in_proj1", w_is_nk=True)
    (y, *kept), (wdown_f[1],) = _mixer_fwd(p, mix_w, 1, weight_gather([(wdown_b[1], (d_ff, D), rows_of(r_dn))]))
    (h2, u2), _ = _mm_res(h, y, wout_f[1], y.shape[-1], False, "out_proj1", norm=(mlp_norm_g, 1))
    (z,), _ = _rms_mm(u2, mlp_norm_g, wup_f[1], 1, 2048, "mlp_up1", out_dtype=BF16)
    (h3,), _ = _mm_res(h2, z, wdown_f[1], 2048, True, "mlp_down1")
    saved.append((h, u1, p, y, kept, h2, u2, z))

    dh, dhb, g_fin, loss_part = _final_loss(h3, W["final_norm_g"], tgt)

    def grad_exchange(pieces):
        comm = _Comm()
        for g, send, shape in pieces:
            comm.add(g, send, comm.add_out((N_DEV,) + shape, BF16), by_slot)
        return comm

    def small_gather(comm, grads):
        for g in grads:
            comm.gather(g, (N_DEV,) + g.shape, by_slot)
        return comm

    recv = {n: [None] * L for n in ("w_in", "w_out", "w_up", "w_down")}
    small1 = gath1 = gath0 = g_win_t1 = None
    for l in reversed(range(L)):
        h0, u1, p, y, kept, h2, u2, z = saved[l]
        (dz,), _ = _bwd_dz(dhb, wdown_f[l], z, 2048, f"mlp_down_bwd{l}")
        (g_wdown,), _ = _mm_tn(z, dhb, True, None, None, l, "a", 512, BF16, f"mlp_down_wgrad{l}")
        comm = grad_exchange([(g_wdown, rows_of(r_dn), (r_dn, D))]) if l == 0 else None
        (dh2, dh2b, g_mlp), got = _bwd_rms(dz, wup_f[l], h2, mlp_norm_g, dh, l, d_ff, f"mlp_up_bwd{l}", comm)
        if l == 0:
            recv["w_down"][0] = got[0]
        (g_wup,), _ = _mm_tn(u2, dz, False, None, None, l, "b", 1024, BF16, f"mlp_up_wgrad{l}")
        (dy,), _ = _bwd_plain(dh2b, wout_f[l], f"out_proj_bwd{l}")
        (g_wout,), _ = _mm_tn(y, dh2b, False, None, None, l, "b", D, BF16, f"out_proj_wgrad{l}")
        comm = grad_exchange([(g_wup, cols_of(c_ff), (D, c_ff)), (g_wout, rows_of(r_out), (r_out, D))]
                             + ([(g_wdown, rows_of(r_dn), (r_dn, D))] if l == 1 else
                                [(g_win_t1, rows_of(c_in), (c_in, D))]))
        if l == 0:
            small_gather(comm, small1)
        (dp, g256, g512, gpool, gpw, gwa, gwx), got = _mixer_bwd(p, *kept, dy, mix_w, l, comm)
        recv["w_up"][l], recv["w_out"][l] = got[0], got[1]
        if l == 1:
            recv["w_down"][1] = got[2]
        else:
            recv["w_in"][1], gath1 = got[2], got[3:]
        smalls = [g_mlp, _diag_blocks(gpool, 4), g256, g512, gpw, _diag_blocks(gwa, RG_HEADS), _diag_blocks(gwx, RG_HEADS)]
        comm = small_gather(_Comm(), smalls[:5]) if l == 0 else None
        if u1 is None:
            (g_win_t,), got_a = _mm_tn(dp, h0, False, "b", mix_norm_g, l, "a", d_in // 2, BF16, f"in_proj_wgrad{l}", comm)
        else:
            (g_win_t,), got_a = _mm_tn(dp, u1, False, None, None, l, "a", d_in // 2, BF16, f"in_proj_wgrad{l}", comm)
        comm = small_gather(grad_exchange([(g_win_t, rows_of(c_in), (c_in, D))]), smalls[5:]) if l == 0 else None
        outs, got_b = _bwd_rms(dp, win_f[l], h0, mix_norm_g, dh2, l, d_in, f"in_proj_bwd{l}", comm, w_is_kd=True,
                               emit_bf16=l > 0)
        dh, g_mix = outs[0], outs[-1]
        dhb = outs[1] if l > 0 else None
        if l == 0:
            recv["w_in"][0] = got_b[0]
            gath0 = got_a + got_b[1:]
        if l == 1:
            g_win_t1 = g_win_t
            small1 = [g_mix] + smalls + [g_fin, loss_part]
    grad_x = dh[N_META:][None]

    gath_tail = _comm_only(small_gather(_Comm(), [g_mix, dh[:N_META]]), "exchange_tail")

    out = {}
    for name, tr in (("w_out", r_out), ("w_up", 256), ("w_down", 128)):
        out[name] = _adamw_big(recv[name], W[name], M[name], V[name], tr, f"adamw_{name}")
    g_win = jnp.transpose(_sum_slots(recv["w_in"], "sum_w_in"), (0, 2, 1))
    out["w_in"] = (g_win,) + tuple(_adamw_plain([g_win], [W["w_in"]], [M["w_in"]], [V["w_in"]], "adamw_w_in"))
    sums, small_out = _adamw_small(gath1, gath0, gath_tail, [W[n] for n in _SMALL_REPL], [M[n] for n in _SMALL_REPL],
                                   [V[n] for n in _SMALL_REPL])
    for a, name in enumerate(_SMALL_REPL):
        out[name] = small_out[4 * a:4 * a + 4]
    s_256, s_512, s_pw, s_meta = sums[4], sums[5], sums[6], sums[9]
    loss = sums[10][0, 0]
    g_shard = {
        "meta_tokens": lax.dynamic_slice_in_dim(s_meta, me * 128, 128, axis=1),
        "convb_dw_w": lax.dynamic_slice_in_dim(s_256[:, :CONV_K, :], me * 32, 32, axis=2),
        "convb_pw_w": lax.dynamic_slice_in_dim(s_pw, me * 32, 32, axis=1),
        "rg_conv_w": lax.dynamic_slice_in_dim(s_512[:, 4:8, :], me * 64, 64, axis=2),
    }
    names = list(g_shard)
    res = _adamw_plain([g_shard[n] for n in names], [W[n] for n in names], [M[n] for n in names],
                       [V[n] for n in names], "adamw_small_sharded")
    for a, name in enumerate(names):
        out[name] = (g_shard[name],) + tuple(res[3 * a:3 * a + 3])

    def fix(name, arr):
        return arr.reshape(-1) if name == "final_norm_g" else arr

    return (loss, grad_x,
            *[fix(n, out[n][0]) for n in _WEIGHTS], *[fix(n, out[n][1]) for n in _WEIGHTS],
            *[fix(n, out[n][2]) for n in _WEIGHTS], *[fix(n, out[n][3]) for n in _WEIGHTS])
```
